```python
import math
import jax, jax.numpy as jnp
from jax import lax
import numpy as np

D_MODEL = 4096
BATCH = 2
SEQ = 8192
DEPTH = 1

CTX_LEN = 256
GRID_W = 64
HEAD_DIM = 128
MIX_WIDTH = D_MODEL
GDN_HEADS = MIX_WIDTH // (2 * HEAD_DIM)
GDN_WIDTH = GDN_HEADS * HEAD_DIM
GDN_CHUNK = 64
CONV_K = 5
SWA_HEADS = MIX_WIDTH // (2 * HEAD_DIM)
SWA_KV_HEADS = 4
SWA_WIDTH = SWA_HEADS * HEAD_DIM
SWA_KV_WIDTH = SWA_KV_HEADS * HEAD_DIM
WINDOW = 128
Q_BLOCK = 128
ROPE_BASE = 10000.0
N_GROUPS = 8
EXPERTS_PER_GROUP = 8
N_EXPERTS = N_GROUPS * EXPERTS_PER_GROUP
TOP_K = 2
D_EXPERT = D_MODEL // 8
MOE_BLOCK = 128
EPS = 1e-6
NEG_INF = -1e30
IN_SIZES = (3 * GDN_WIDTH, GDN_WIDTH, 2 * GDN_HEADS, 2 * GDN_HEADS, SWA_WIDTH, SWA_KV_WIDTH, SWA_KV_WIDTH)
N_IN = sum(IN_SIZES)
IN_SPLITS = tuple(int(s) for s in np.cumsum(IN_SIZES)[:-1])

kernel_name = "hybrid_gdn_swa_hmoe_prefix_dit_layer"


def rmsnorm(x, g):
    xf = x.astype(jnp.float32)
    y = xf * lax.rsqrt(jnp.mean(xf * xf, axis=-1, keepdims=True) + EPS)
    return (y * g.astype(jnp.float32)).astype(x.dtype)


def l2norm(x):
    return x * lax.rsqrt(jnp.sum(x * x, axis=-1, keepdims=True) + EPS)


def short_conv(x, w):
    C = x.shape[-1]
    y = lax.conv_general_dilated(x, w[:, None, :].astype(x.dtype), window_strides=(1,),
                                 padding=[(CONV_K // 2, CONV_K // 2)],
                                 dimension_numbers=('NWC', 'WIO', 'NWC'), feature_group_count=C)
    return jax.nn.silu(y)


def axial_rope(t, rows, cols):
    half = HEAD_DIM // 2
    inv = ROPE_BASE ** (-jnp.arange(0, half, 2, dtype=jnp.float32) / half)

    def rot(u, pos):
        ang = pos[:, None] * inv
        cos, sin = jnp.cos(ang)[None, :, None, :], jnp.sin(ang)[None, :, None, :]
        u1, u2 = jnp.split(u.astype(jnp.float32), 2, axis=-1)
        return jnp.concatenate([u1 * cos - u2 * sin, u2 * cos + u1 * sin], axis=-1)

    return jnp.concatenate([rot(t[..., :half], rows), rot(t[..., half:], cols)], axis=-1).astype(t.dtype)


def gated_delta_chunked(q, k, v, g, beta, s0):
    B_, L, H, _ = q.shape
    N = L // GDN_CHUNK

    def chunks(t):
        t = t.reshape((B_, N, GDN_CHUNK) + t.shape[2:])
        return jnp.moveaxis(jnp.swapaxes(t, 2, 3), 1, 0)

    qc, kc, vc, gc, bc = map(chunks, (q, k, v, g, beta))
    gcum = jnp.cumsum(gc, axis=-1)
    tri = jnp.tril(jnp.ones((GDN_CHUNK, GDN_CHUNK), bool))
    strict = jnp.tril(jnp.ones((GDN_CHUNK, GDN_CHUNK), jnp.float32), -1)
    diff = gcum[..., :, None] - gcum[..., None, :]
    decay = jnp.where(tri, jnp.exp(jnp.where(tri, diff, 0.0)), 0.0)
    kb = kc * bc[..., None]
    a_strict = jnp.einsum('nbhid,nbhjd->nbhij', kb, kc) * decay * strict
    eye = jnp.eye(GDN_CHUNK, dtype=jnp.float32)
    t_inv = lax.linalg.triangular_solve(eye + a_strict, jnp.broadcast_to(eye, a_strict.shape),
                                        left_side=True, lower=True, unit_diagonal=True)
    u = jnp.einsum('nbhij,nbhjd->nbhid', t_inv, vc * bc[..., None])
    w = jnp.einsum('nbhij,nbhjd->nbhid', t_inv, kb * jnp.exp(gcum)[..., None])
    qk = jnp.einsum('nbhid,nbhjd->nbhij', qc, kc) * decay
    q_dec = qc * jnp.exp(gcum)[..., None]
    k_dec = kc * jnp.exp(gcum[..., -1:] - gcum)[..., None]
    chunk_decay = jnp.exp(gcum[..., -1])

    def step(S, xs):
        u_n, w_n, qk_n, q_n, k_n, a_n = xs
        v_new = u_n - jnp.einsum('bhcd,bhde->bhce', w_n, S)
        o = jnp.einsum('bhcd,bhde->bhce', q_n, S) + jnp.einsum('bhij,bhje->bhie', qk_n, v_new)
        S = S * a_n[..., None, None] + jnp.einsum('bhcd,bhce->bhde', k_n, v_new)
        return S, o

    s_fin, o = lax.scan(step, s0, (u, w, qk, q_dec, k_dec, chunk_decay))
    o = jnp.swapaxes(jnp.moveaxis(o, 0, 1), 2, 3).reshape(B_, L, H, -1)
    return o, s_fin


def gdn_inputs(qkv, b_logit, a_logit, conv_w, a_log, dt_bias):
    B_, L, _ = qkv.shape
    qkv = short_conv(qkv, conv_w).astype(jnp.float32).reshape(B_, L, 3, GDN_HEADS, HEAD_DIM)
    q = l2norm(qkv[:, :, 0]) * HEAD_DIM ** -0.5
    k = l2norm(qkv[:, :, 1])
    v = qkv[:, :, 2]
    beta = jax.nn.sigmoid(b_logit.astype(jnp.float32).reshape(B_, L, 2, GDN_HEADS))
    g = -jnp.exp(a_log.astype(jnp.float32)) * jax.nn.softplus(
        a_logit.astype(jnp.float32).reshape(B_, L, 2, GDN_HEADS) + dt_bias.astype(jnp.float32))
    return q, k, v, g, beta


def gdn_bidir(q, k, v, g, beta, s_fwd0, s_bwd0):
    o_f, s_f = gated_delta_chunked(q, k, v, g[:, :, 0], beta[:, :, 0], s_fwd0)
    rev = lambda t: jnp.flip(t, axis=1)
    o_b, s_b = gated_delta_chunked(rev(q), rev(k), rev(v), rev(g[:, :, 1]), rev(beta[:, :, 1]), s_bwd0)
    return o_f + rev(o_b), s_f, s_b


def gdn_output(o, z, g_on):
    B_, L = o.shape[:2]
    y = rmsnorm(o, g_on).reshape(B_, L, GDN_WIDTH)
    return (y * jax.nn.silu(z.astype(jnp.float32))).astype(z.dtype)


def swa_inputs(q, k, v, g_q, g_k):
    B_, L, _ = q.shape
    q = rmsnorm(q.reshape(B_, L, SWA_HEADS, HEAD_DIM), g_q)
    k = rmsnorm(k.reshape(B_, L, SWA_KV_HEADS, HEAD_DIM), g_k)
    return q, k, v.reshape(B_, L, SWA_KV_HEADS, HEAD_DIM)


def window_attention(q, k, v, k_ctx, v_ctx, sink):
    B_, L, H, D = q.shape
    G = H // SWA_KV_HEADS
    nb = L // Q_BLOCK
    n_ctx = k_ctx.shape[1]
    scale = D ** -0.5
    qb = q.reshape(B_, nb, Q_BLOCK, SWA_KV_HEADS, G, D)

    def band_blocks(t):
        tp = jnp.pad(t, ((0, 0), (Q_BLOCK, Q_BLOCK), (0, 0), (0, 0))).reshape(B_, nb + 2, Q_BLOCK, SWA_KV_HEADS, D)
        return jnp.concatenate([tp[:, :-2], tp[:, 1:-1], tp[:, 2:]], axis=2)

    kw, vw = band_blocks(k), band_blocks(v)
    qi = jnp.arange(Q_BLOCK)[:, None]
    kj = jnp.arange(3 * Q_BLOCK)[None, :]
    band = jnp.abs(kj - Q_BLOCK - qi) <= WINDOW
    kpos = jnp.arange(nb)[:, None] * Q_BLOCK - Q_BLOCK + jnp.arange(3 * Q_BLOCK)[None, :]
    valid = band[None] & ((kpos >= 0) & (kpos < L))[:, None, :]
    s_win = jnp.einsum('bnqkgd,bnjkd->bnkgqj', qb, kw).astype(jnp.float32) * scale
    s_win = jnp.where(valid[None, :, None, None], s_win, NEG_INF)
    s_ctx = jnp.einsum('bnqkgd,bckd->bnkgqc', qb, k_ctx).astype(jnp.float32) * scale
    s_sink = jnp.broadcast_to(sink.astype(jnp.float32).reshape(SWA_KV_HEADS, G, 1, 1), s_ctx.shape[:-1] + (1,))
    p = jax.nn.softmax(jnp.concatenate([s_ctx, s_win, s_sink], axis=-1), axis=-1).astype(v.dtype)
    o = (jnp.einsum('bnkgqc,bckd->bnqkgd', p[..., :n_ctx], v_ctx)
         + jnp.einsum('bnkgqj,bnjkd->bnqkgd', p[..., n_ctx:n_ctx + 3 * Q_BLOCK], vw))
    return o.reshape(B_, L, H * D)


def ctx_attention(q, k, v, sink):
    B_, L, H, D = q.shape
    G = H // SWA_KV_HEADS
    qg = q.reshape(B_, L, SWA_KV_HEADS, G, D)
    s = jnp.einsum('bqkgd,bjkd->bkgqj', qg, k).astype(jnp.float32) * D ** -0.5
    s_sink = jnp.broadcast_to(sink.astype(jnp.float32).reshape(SWA_KV_HEADS, G, 1, 1), s.shape[:-1] + (1,))
    p = jax.nn.softmax(jnp.concatenate([s, s_sink], axis=-1), axis=-1)[..., :-1].astype(v.dtype)
    return jnp.einsum('bkgqj,bjkd->bqkgd', p, v).reshape(B_, L, H * D)


def hier_moe(h, w_rg, b_rg, w_re, b_re, w_gate, w_up, w_down):
    T, D = h.shape
    p_groups = jax.nn.softmax((h @ w_rg + b_rg).astype(jnp.float32), axis=-1)
    p_grp, grp = lax.top_k(p_groups, 1)
    le = (h @ w_re + b_re).astype(jnp.float32).reshape(T, N_GROUPS, EXPERTS_PER_GROUP)
    le = jnp.take_along_axis(le, grp[:, :, None], axis=1)[:, 0]
    p_in, e_in = lax.top_k(jax.nn.softmax(le, axis=-1), TOP_K)
    gate = p_grp * p_in / jnp.sum(p_in, axis=-1, keepdims=True)
    eid = (grp * EXPERTS_PER_GROUP + e_in).astype(jnp.int32)
    A = T * TOP_K
    e_flat = eid.reshape(A)
    tok_flat = jnp.repeat(jnp.arange(T, dtype=jnp.int32), TOP_K)
    order = jnp.argsort(e_flat)
    e_sorted = e_flat[order]
    counts = jax.ops.segment_sum(jnp.ones((A,), jnp.int32), e_flat, num_segments=N_EXPERTS)
    padded = (counts + MOE_BLOCK - 1) // MOE_BLOCK * MOE_BLOCK
    start = jnp.cumsum(counts) - counts
    pend = jnp.cumsum(padded)
    pstart = pend - padded
    dest = pstart[e_sorted] + jnp.arange(A, dtype=jnp.int32) - start[e_sorted]
    n_blocks = (A + MOE_BLOCK - 1) // MOE_BLOCK + N_EXPERTS
    P = n_blocks * MOE_BLOCK
    slot_tok = jnp.zeros((P,), jnp.int32).at[dest].set(tok_flat[order])
    slot_w = jnp.zeros((P,), jnp.float32).at[dest].set(gate.reshape(A)[order])
    blk_expert = jnp.minimum(jnp.searchsorted(pend, jnp.arange(n_blocks, dtype=jnp.int32) * MOE_BLOCK,
                                              side='right'), N_EXPERTS - 1).astype(jnp.int32)

    def expert_block(args):
        tok, e = args
        xb = h[tok]
        return (jax.nn.silu(xb @ w_gate[e]) * (xb @ w_up[e])) @ w_down[e]

    y = lax.map(expert_block, (slot_tok.reshape(n_blocks, MOE_BLOCK), blk_expert))
    y = y.reshape(P, D) * slot_w[:, None].astype(y.dtype)
    return jax.ops.segment_sum(y, slot_tok, num_segments=T)


def setup_inputs(seed: int = 0) -> dict:
    key = jax.random.key(seed)
    ks = jax.random.split(key, 24)
    Lr, D = DEPTH, D_MODEL
    nrm = lambda k, shape, s: jax.random.normal(k, shape, jnp.float32) * s
    dt = jnp.exp(jax.random.uniform(ks[11], (Lr, 2, GDN_HEADS), jnp.float32,
                                    minval=math.log(1e-3), maxval=math.log(1e-1)))
    return {
        "x": nrm(ks[0], (BATCH, SEQ, D), 1.0),
        "c": nrm(ks[1], (BATCH, D), 1.0),
        "ctx": nrm(ks[2], (BATCH, CTX_LEN, D), 1.0),
        "c_ctx": nrm(ks[3], (D,), 1.0),
        "w_ada": nrm(ks[4], (Lr, D, 6 * D), 0.5 * D ** -0.5),
        "b_ada": nrm(ks[5], (Lr, 6 * D), 0.02),
        "g_norm1": 1.0 + nrm(ks[6], (Lr, D), 0.02),
        "g_norm2": 1.0 + nrm(ks[7], (Lr, D), 0.02),
        "w_in": nrm(ks[8], (Lr, D, N_IN), D ** -0.5),
        "conv_qkv": nrm(ks[9], (Lr, CONV_K, 3 * GDN_WIDTH), CONV_K ** -0.5),
        "a_log": jnp.log(jax.random.uniform(ks[10], (Lr, 2, GDN_HEADS), jnp.float32, minval=1.0, maxval=16.0)),
        "dt_bias": dt + jnp.log(-jnp.expm1(-dt)),
        "g_onorm": 1.0 + nrm(ks[12], (Lr, HEAD_DIM), 0.02),
        "g_qnorm": 1.0 + nrm(ks[13], (Lr, HEAD_DIM), 0.02),
        "g_knorm": 1.0 + nrm(ks[14], (Lr, HEAD_DIM), 0.02),
        "sink": nrm(ks[15], (Lr, SWA_HEADS), 0.5),
        "w_out": nrm(ks[16], (Lr, MIX_WIDTH, D), MIX_WIDTH ** -0.5),
        "w_router_grp": nrm(ks[17], (Lr, D, N_GROUPS), D ** -0.5),
        "b_router_grp": nrm(ks[18], (Lr, N_GROUPS), 0.01),
        "w_router_exp": nrm(ks[19], (Lr, D, N_EXPERTS), D ** -0.5),
        "b_router_exp": nrm(ks[20], (Lr, N_EXPERTS), 0.01),
        "w_gate": nrm(ks[21], (Lr, N_EXPERTS, D, D_EXPERT), D ** -0.5),
        "w_up": nrm(ks[22], (Lr, N_EXPERTS, D, D_EXPERT), D ** -0.5),
        "w_down": nrm(ks[23], (Lr, N_EXPERTS, D_EXPERT, D), D_EXPERT ** -0.5),
    }


def reference(x, c, ctx, c_ctx, w_ada, b_ada, g_norm1, g_norm2, w_in, conv_qkv, a_log, dt_bias,
              g_onorm, g_qnorm, g_knorm, sink, w_out, w_router_grp, b_router_grp, w_router_exp,
              b_router_exp, w_gate, w_up, w_down):
    B_, L, D = x.shape
    ROWS = L // GRID_W
    rows = jnp.repeat(jnp.arange(ROWS, dtype=jnp.float32), GRID_W)
    cols = jnp.tile(jnp.arange(GRID_W, dtype=jnp.float32), ROWS)
    for l in range(DEPTH):
        last = l == DEPTH - 1
        mx = jnp.split((jax.nn.silu(c) @ w_ada[l] + b_ada[l])[:, None, :], 6, axis=-1)
        mc = jnp.split(jax.nn.silu(c_ctx) @ w_ada[l] + b_ada[l], 6, axis=-1)
        hx = rmsnorm(x, g_norm1[l]) * (1.0 + mx[1]) + mx[0]
        hc = rmsnorm(ctx, g_norm1[l]) * (1.0 + mc[1]) + mc[0]
        qkv_x, z_x, bl_x, al_x, sq_x, sk_x, sv_x = jnp.split(hx @ w_in[l], IN_SPLITS, axis=-1)
        qkv_c, z_c, bl_c, al_c, sq_c, sk_c, sv_c = jnp.split(hc @ w_in[l], IN_SPLITS, axis=-1)

        zero = jnp.zeros((B_, GDN_HEADS, HEAD_DIM, HEAD_DIM), jnp.float32)
        oa_c, s_fwd, s_bwd = gdn_bidir(*gdn_inputs(qkv_c, bl_c, al_c, conv_qkv[l], a_log[l], dt_bias[l]), zero, zero)
        oa_x, _, _ = gdn_bidir(*gdn_inputs(qkv_x, bl_x, al_x, conv_qkv[l], a_log[l], dt_bias[l]), s_fwd, s_bwd)
        ya_x = gdn_output(oa_x, z_x, g_onorm[l])

        qx, kx, vx = swa_inputs(sq_x, sk_x, sv_x, g_qnorm[l], g_knorm[l])
        qx, kx = axial_rope(qx, rows, cols), axial_rope(kx, rows, cols)
        qc, kc, vc = swa_inputs(sq_c, sk_c, sv_c, g_qnorm[l], g_knorm[l])
        yb_x = window_attention(qx, kx, vx, kc, vc, sink[l])

        x = x + mx[2] * (jnp.concatenate([ya_x, yb_x], axis=-1) @ w_out[l])
        h2x = rmsnorm(x, g_norm2[l]) * (1.0 + mx[4]) + mx[3]
        moe_w = (w_router_grp[l], b_router_grp[l], w_router_exp[l], b_router_exp[l], w_gate[l], w_up[l], w_down[l])
        if last:
            x = x + mx[5] * hier_moe(h2x.reshape(-1, D), *moe_w).reshape(B_, L, D)
        else:
            ya_c = gdn_output(oa_c, z_c, g_onorm[l])
            yb_c = ctx_attention(qc, kc, vc, sink[l])
            ctx = ctx + mc[2] * (jnp.concatenate([ya_c, yb_c], axis=-1) @ w_out[l])
            h2c = rmsnorm(ctx, g_norm2[l]) * (1.0 + mc[4]) + mc[3]
            n_c = B_ * ctx.shape[1]
            y = hier_moe(jnp.concatenate([h2c.reshape(-1, D), h2x.reshape(-1, D)], axis=0), *moe_w)
            ctx = ctx + mc[5] * y[:n_c].reshape(ctx.shape)
            x = x + mx[5] * y[n_c:].reshape(B_, L, D)
    return x
```

```python
import functools
import math

import jax
import jax.numpy as jnp
import numpy as np
from jax import lax
from jax.experimental import pallas as pl
from jax.experimental.pallas import tpu as pltpu

F32 = jnp.float32
BF16 = jnp.bfloat16

D_MODEL = 4096
CTX_LEN = 256
GRID_W = 64
HEAD_DIM = 128
GDN_HEADS = 16
GDN_WIDTH = GDN_HEADS * HEAD_DIM
GDN_CHUNK = 64
CONV_K = 5
SWA_HEADS = 16
SWA_KV_HEADS = 4
SWA_WIDTH = SWA_HEADS * HEAD_DIM
SWA_KV_WIDTH = SWA_KV_HEADS * HEAD_DIM
WINDOW = 128
Q_BLOCK = 128
ROPE_BASE = 10000.0
N_GROUPS = 8
EXPERTS_PER_GROUP = 8
N_EXPERTS = N_GROUPS * EXPERTS_PER_GROUP
TOP_K = 2
D_EXPERT = D_MODEL // 8
EPS = 1e-6
NEG_INF = -1e30

IN_SIZES = (3 * GDN_WIDTH, GDN_WIDTH, 2 * GDN_HEADS, 2 * GDN_HEADS, SWA_WIDTH, SWA_KV_WIDTH, SWA_KV_WIDTH)
IN_OFFS = tuple(int(v) for v in np.cumsum((0,) + IN_SIZES))
N_MAIN = 3 * GDN_WIDTH + GDN_WIDTH + SWA_WIDTH + 2 * SWA_KV_WIDTH
N_SMALL = 128
N_ROUTER = 128
MOD_ROWS = 8

MOE_BM = 256
VMEM_LIMIT = 56 * 1024 * 1024


def _cparams(sem):
    return pltpu.CompilerParams(dimension_semantics=sem, vmem_limit_bytes=VMEM_LIMIT)


def _mod_kernel(c_ref, w_ref, b_ref, o_ref):
    c = c_ref[...]
    a = (c * jax.nn.sigmoid(c)).astype(BF16)
    o_ref[...] = jnp.dot(a, w_ref[...].astype(BF16), preferred_element_type=F32) + b_ref[...]


def _modulation(c_rows, w_ada, b_ada):
    d, n = w_ada.shape
    tn = 512
    return pl.pallas_call(
        _mod_kernel,
        grid=(n // tn,),
        in_specs=[pl.BlockSpec((MOD_ROWS, d), lambda j: (0, 0)),
                  pl.BlockSpec((d, tn), lambda j: (0, j)),
                  pl.BlockSpec((1, tn), lambda j: (0, j))],
        out_specs=pl.BlockSpec((MOD_ROWS, tn), lambda j: (0, j)),
        out_shape=jax.ShapeDtypeStruct((MOD_ROWS, n), F32),
        compiler_params=_cparams(("arbitrary",)),
        name="modulation",
    )(c_rows, w_ada, b_ada.reshape(1, n))


NORM_ROWS = 64


def _norm_mod_rows(x_ref, g_ref, sh_ref, sc_ref, h_ref, tm):
    g = g_ref[...]
    sc = 1.0 + sc_ref[0]
    sh = sh_ref[0]

    def body(r, carry):
        rows = pl.ds(pl.multiple_of(r * NORM_ROWS, NORM_ROWS), NORM_ROWS)
        xf = x_ref[rows, :]
        ms = jnp.mean(xf * xf, axis=-1, keepdims=True)
        y = xf * lax.rsqrt(ms + EPS) * g
        h_ref[rows, :] = (y * sc + sh).astype(h_ref.dtype)
        return carry

    lax.fori_loop(0, tm // NORM_ROWS, body, 0)


def _inproj_kernel(x_ref, g_ref, sh_ref, sc_ref, w_ref, ws_ref, o_ref, os_ref, h_ref, *, tm):
    @pl.when(pl.program_id(1) == 0)
    def _():
        _norm_mod_rows(x_ref, g_ref, sh_ref, sc_ref, h_ref, tm)
        os_ref[...] = jnp.dot(h_ref[...], ws_ref[...], preferred_element_type=F32)

    o_ref[...] = jnp.dot(h_ref[...], w_ref[...], preferred_element_type=F32)


def _in_projection(x2d, g_norm, mod3, mod_row_of_tile, w_main, w_small, tm):
    t, d = x2d.shape
    tn = 1024
    return pl.pallas_call(
        functools.partial(_inproj_kernel, tm=tm),
        grid=(t // tm, N_MAIN // tn),
        in_specs=[pl.BlockSpec((tm, d), lambda i, j: (i, 0)),
                  pl.BlockSpec((1, d), lambda i, j: (0, 0)),
                  pl.BlockSpec((1, 1, d), lambda i, j: (mod_row_of_tile(i) * 6 + 0, 0, 0)),
                  pl.BlockSpec((1, 1, d), lambda i, j: (mod_row_of_tile(i) * 6 + 1, 0, 0)),
                  pl.BlockSpec((d, tn), lambda i, j: (0, j)),
                  pl.BlockSpec((d, N_SMALL), lambda i, j: (0, 0))],
        out_specs=[pl.BlockSpec((tm, tn), lambda i, j: (i, j)),
                   pl.BlockSpec((tm, N_SMALL), lambda i, j: (i, 0))],
        out_shape=[jax.ShapeDtypeStruct((t, N_MAIN), F32),
                   jax.ShapeDtypeStruct((t, N_SMALL), F32)],
        scratch_shapes=[pltpu.VMEM((tm, d), BF16)],
        compiler_params=_cparams(("arbitrary", "arbitrary")),
        name="in_projection",
    )(x2d, g_norm.reshape(1, d), mod3, mod3, w_main, w_small)


def _outproj_kernel(ya_ref, yb_ref, wa_ref, wb_ref, x_ref, gate_ref, o_ref):
    acc = jnp.dot(ya_ref[...], wa_ref[...], preferred_element_type=F32)
    acc = acc + jnp.dot(yb_ref[...], wb_ref[...], preferred_element_type=F32)
    o_ref[...] = x_ref[...] + gate_ref[0] * acc


def _out_projection(ya, yb, wa, wb, x2d, mod3, tiles_per_batch, tm):
    t, d = x2d.shape
    tn = 1024
    nj = d // tn
    ka, kb = ya.shape[1], yb.shape[1]
    return pl.pallas_call(
        _outproj_kernel,
        grid=(t // tm, nj),
        in_specs=[pl.BlockSpec((tm, ka), lambda i, j: (i, 0)),
                  pl.BlockSpec((tm, kb), lambda i, j: (i, 0)),
                  pl.BlockSpec((ka, tn), lambda i, j: (0, j)),
                  pl.BlockSpec((kb, tn), lambda i, j: (0, j)),
                  pl.BlockSpec((tm, tn), lambda i, j: (i, j)),
                  pl.BlockSpec((1, 1, tn), lambda i, j: (((i // tiles_per_batch) * 6 + 2) * nj + j, 0, 0))],
        out_specs=pl.BlockSpec((tm, tn), lambda i, j: (i, j)),
        out_shape=jax.ShapeDtypeStruct((t, d), F32),
        compiler_params=_cparams(("arbitrary", "arbitrary")),
        name="out_projection",
    )(ya, yb, wa, wb, x2d, mod3.reshape(-1, 1, tn))


def _norm2_kernel(x_ref, g_ref, sh_ref, sc_ref, wr_ref, br_ref, h_ref, lg_ref, *, tm):
    _norm_mod_rows(x_ref, g_ref, sh_ref, sc_ref, h_ref, tm)
    lg_ref[...] = jnp.dot(h_ref[...], wr_ref[...], preferred_element_type=F32) + br_ref[...]


def _norm2_router(x2d, g_norm, mod3, tiles_per_batch, w_router, b_router, tm):
    t, d = x2d.shape
    return pl.pallas_call(
        functools.partial(_norm2_kernel, tm=tm),
        grid=(t // tm,),
        in_specs=[pl.BlockSpec((tm, d), lambda i: (i, 0)),
                  pl.BlockSpec((1, d), lambda i: (0, 0)),
                  pl.BlockSpec((1, 1, d), lambda i: ((i // tiles_per_batch) * 6 + 3, 0, 0)),
                  pl.BlockSpec((1, 1, d), lambda i: ((i // tiles_per_batch) * 6 + 4, 0, 0)),
                  pl.BlockSpec((d, N_ROUTER), lambda i: (0, 0)),
                  pl.BlockSpec((1, N_ROUTER), lambda i: (0, 0))],
        out_specs=[pl.BlockSpec((tm, d), lambda i: (i, 0)),
                   pl.BlockSpec((tm, N_ROUTER), lambda i: (i, 0))],
        out_shape=[jax.ShapeDtypeStruct((t, d), BF16),
                   jax.ShapeDtypeStruct((t, N_ROUTER), F32)],
        compiler_params=_cparams(("arbitrary",)),
        name="norm2_router",
    )(x2d, g_norm.reshape(1, d), mod3, mod3, w_router, b_router)


CAST_ROWS = 128


def _cast_rows(src_ref, dst_ref):
    n = src_ref.shape[1]

    def body(r, carry):
        rows = pl.ds(pl.multiple_of(r * CAST_ROWS, CAST_ROWS), CAST_ROWS)
        dst_ref[rows, :] = src_ref[0, rows, :].astype(dst_ref.dtype)
        return carry

    lax.fori_loop(0, n // CAST_ROWS, body, 0)


def _expert_changed(be_ref, i):
    return (i == 0) | (be_ref[i] != be_ref[jnp.maximum(i - 1, 0)])


def _moe_up_kernel(be_ref, nu_ref, x_ref, wg_ref, wu_ref, o_ref, wgb_ref, wub_ref):
    i = pl.program_id(0)

    @pl.when(_expert_changed(be_ref, i))
    def _():
        _cast_rows(wg_ref, wgb_ref)
        _cast_rows(wu_ref, wub_ref)

    @pl.when(i < nu_ref[0])
    def _():
        xb = x_ref[...]
        g = jnp.dot(xb, wgb_ref[...], preferred_element_type=F32)
        u = jnp.dot(xb, wub_ref[...], preferred_element_type=F32)
        o_ref[...] = (g * jax.nn.sigmoid(g) * u).astype(o_ref.dtype)

    @pl.when(i >= nu_ref[0])
    def _():
        o_ref[...] = jnp.zeros_like(o_ref)


def _moe_down_kernel(be_ref, nu_ref, h_ref, wd_ref, sw_ref, o_ref, wdb_ref):
    i = pl.program_id(0)

    @pl.when(_expert_changed(be_ref, i))
    def _():
        _cast_rows(wd_ref, wdb_ref)

    @pl.when(i < nu_ref[0])
    def _():
        y = jnp.dot(h_ref[...], wdb_ref[...], preferred_element_type=F32)
        o_ref[...] = y * sw_ref[...]

    @pl.when(i >= nu_ref[0])
    def _():
        o_ref[...] = jnp.zeros_like(o_ref)


def _moe_experts(xg, blk_expert, n_used, slot_w, w_gate, w_up, w_down):
    p, d = xg.shape
    n_blocks = p // MOE_BM
    de = w_gate.shape[-1]
    hmid = pl.pallas_call(
        _moe_up_kernel,
        grid_spec=pltpu.PrefetchScalarGridSpec(
            num_scalar_prefetch=2,
            grid=(n_blocks,),
            in_specs=[pl.BlockSpec((MOE_BM, d), lambda i, be, nu: (i, 0)),
                      pl.BlockSpec((1, d, de), lambda i, be, nu: (be[i], 0, 0)),
                      pl.BlockSpec((1, d, de), lambda i, be, nu: (be[i], 0, 0))],
            out_specs=pl.BlockSpec((MOE_BM, de), lambda i, be, nu: (i, 0)),
            scratch_shapes=[pltpu.VMEM((d, de), BF16), pltpu.VMEM((d, de), BF16)]),
        out_shape=jax.ShapeDtypeStruct((p, de), BF16),
        compiler_params=_cparams(("arbitrary",)),
        name="moe_gate_up",
    )(blk_expert, n_used, xg, w_gate, w_up)
    return pl.pallas_call(
        _moe_down_kernel,
        grid_spec=pltpu.PrefetchScalarGridSpec(
            num_scalar_prefetch=2,
            grid=(n_blocks,),
            in_specs=[pl.BlockSpec((MOE_BM, de), lambda i, be, nu: (i, 0)),
                      pl.BlockSpec((1, de, d), lambda i, be, nu: (be[i], 0, 0)),
                      pl.BlockSpec((MOE_BM, 1), lambda i, be, nu: (i, 0))],
            out_specs=pl.BlockSpec((MOE_BM, d), lambda i, be, nu: (i, 0)),
            scratch_shapes=[pltpu.VMEM((de, d), BF16)]),
        out_shape=jax.ShapeDtypeStruct((p, d), F32),
        compiler_params=_cparams(("arbitrary",)),
        name="moe_down",
    )(blk_expert, n_used, hmid, w_down, slot_w.reshape(p, 1))


def _route(logits, t):
    lg = logits[:, :N_GROUPS]
    le = logits[:, N_GROUPS:N_GROUPS + N_EXPERTS].reshape(t, N_GROUPS, EXPERTS_PER_GROUP)
    p_grp, grp = lax.top_k(jax.nn.softmax(lg, axis=-1), 1)
    le = jnp.take_along_axis(le, grp[:, :, None], axis=1)[:, 0]
    p_in, e_in = lax.top_k(jax.nn.softmax(le, axis=-1), TOP_K)
    gate = p_grp * p_in / jnp.sum(p_in, axis=-1, keepdims=True)
    eid = (grp * EXPERTS_PER_GROUP + e_in).astype(jnp.int32)
    a = t * TOP_K
    e_flat = eid.reshape(a)
    tok_flat = jnp.repeat(jnp.arange(t, dtype=jnp.int32), TOP_K)
    order = jnp.argsort(e_flat)
    e_sorted = e_flat[order]
    counts = jax.ops.segment_sum(jnp.ones((a,), jnp.int32), e_flat, num_segments=N_EXPERTS)
    padded = (counts + MOE_BM - 1) // MOE_BM * MOE_BM
    start = jnp.cumsum(counts) - counts
    pend = jnp.cumsum(padded)
    pstart = pend - padded
    dest = pstart[e_sorted] + jnp.arange(a, dtype=jnp.int32) - start[e_sorted]
    n_blocks = (a + MOE_BM - 1) // MOE_BM + N_EXPERTS
    p = n_blocks * MOE_BM
    slot_tok = jnp.zeros((p,), jnp.int32).at[dest].set(tok_flat[order])
    slot_w = jnp.zeros((p,), F32).at[dest].set(gate.reshape(a)[order])
    slot_of = jnp.zeros((a,), jnp.int32).at[order].set(dest)
    blk_expert = jnp.minimum(jnp.searchsorted(pend, jnp.arange(n_blocks, dtype=jnp.int32) * MOE_BM,
                                              side='right'), N_EXPERTS - 1).astype(jnp.int32)
    n_used = (pend[-1] // MOE_BM).astype(jnp.int32).reshape(1)
    return slot_tok, slot_w, slot_of, blk_expert, n_used


def _rmsnorm(x, g):
    xf = x.astype(F32)
    y = xf * lax.rsqrt(jnp.mean(xf * xf, axis=-1, keepdims=True) + EPS)
    return (y * g.astype(F32)).astype(x.dtype)


def _l2norm(x):
    return x * lax.rsqrt(jnp.sum(x * x, axis=-1, keepdims=True) + EPS)


def _short_conv(x, w):
    c = x.shape[-1]
    y = lax.conv_general_dilated(x, w[:, None, :].astype(x.dtype), window_strides=(1,),
                                 padding=[(CONV_K // 2, CONV_K // 2)],
                                 dimension_numbers=('NWC', 'WIO', 'NWC'), feature_group_count=c)
    return jax.nn.silu(y)


def _axial_rope(t, rows, cols):
    half = HEAD_DIM // 2
    inv = ROPE_BASE ** (-jnp.arange(0, half, 2, dtype=F32) / half)

    def rot(u, pos):
        ang = pos[:, None] * inv
        cos, sin = jnp.cos(ang)[None, :, None, :], jnp.sin(ang)[None, :, None, :]
        u1, u2 = jnp.split(u.astype(F32), 2, axis=-1)
        return jnp.concatenate([u1 * cos - u2 * sin, u2 * cos + u1 * sin], axis=-1)

    return jnp.concatenate([rot(t[..., :half], rows), rot(t[..., half:], cols)], axis=-1).astype(t.dtype)


def _gated_delta_chunked(q, k, v, g, beta, s0):
    b_, l, h, _ = q.shape
    n = l // GDN_CHUNK

    def chunks(t):
        t = t.reshape((b_, n, GDN_CHUNK) + t.shape[2:])
        return jnp.moveaxis(jnp.swapaxes(t, 2, 3), 1, 0)

    qc, kc, vc, gc, bc = map(chunks, (q, k, v, g, beta))
    gcum = jnp.cumsum(gc, axis=-1)
    tri = jnp.tril(jnp.ones((GDN_CHUNK, GDN_CHUNK), bool))
    strict = jnp.tril(jnp.ones((GDN_CHUNK, GDN_CHUNK), F32), -1)
    diff = gcum[..., :, None] - gcum[..., None, :]
    decay = jnp.where(tri, jnp.exp(jnp.where(tri, diff, 0.0)), 0.0)
    kb = kc * bc[..., None]
    a_strict = jnp.einsum('nbhid,nbhjd->nbhij', kb, kc) * decay * strict
    eye = jnp.eye(GDN_CHUNK, dtype=F32)
    t_inv = lax.linalg.triangular_solve(eye + a_strict, jnp.broadcast_to(eye, a_strict.shape),
                                        left_side=True, lower=True, unit_diagonal=True)
    u = jnp.einsum('nbhij,nbhjd->nbhid', t_inv, vc * bc[..., None])
    w = jnp.einsum('nbhij,nbhjd->nbhid', t_inv, kb * jnp.exp(gcum)[..., None])
    qk = jnp.einsum('nbhid,nbhjd->nbhij', qc, kc) * decay
    q_dec = qc * jnp.exp(gcum)[..., None]
    k_dec = kc * jnp.exp(gcum[..., -1:] - gcum)[..., None]
    chunk_decay = jnp.exp(gcum[..., -1])

    def step(s, xs):
        u_n, w_n, qk_n, q_n, k_n, a_n = xs
        v_new = u_n - jnp.einsum('bhcd,bhde->bhce', w_n, s)
        o = jnp.einsum('bhcd,bhde->bhce', q_n, s) + jnp.einsum('bhij,bhje->bhie', qk_n, v_new)
        s = s * a_n[..., None, None] + jnp.einsum('bhcd,bhce->bhde', k_n, v_new)
        return s, o

    s_fin, o = lax.scan(step, s0, (u, w, qk, q_dec, k_dec, chunk_decay))
    o = jnp.swapaxes(jnp.moveaxis(o, 0, 1), 2, 3).reshape(b_, l, h, -1)
    return o, s_fin


def _gdn_inputs(qkv, b_logit, a_logit, conv_w, a_log, dt_bias):
    b_, l, _ = qkv.shape
    qkv = _short_conv(qkv, conv_w).astype(F32).reshape(b_, l, 3, GDN_HEADS, HEAD_DIM)
    q = _l2norm(qkv[:, :, 0]) * HEAD_DIM ** -0.5
    k = _l2norm(qkv[:, :, 1])
    v = qkv[:, :, 2]
    beta = jax.nn.sigmoid(b_logit.astype(F32).reshape(b_, l, 2, GDN_HEADS))
    g = -jnp.exp(a_log.astype(F32)) * jax.nn.softplus(
        a_logit.astype(F32).reshape(b_, l, 2, GDN_HEADS) + dt_bias.astype(F32))
    return q, k, v, g, beta


def _gdn_bidir(q, k, v, g, beta, s_fwd0, s_bwd0):
    o_f, s_f = _gated_delta_chunked(q, k, v, g[:, :, 0], beta[:, :, 0], s_fwd0)
    rev = lambda t: jnp.flip(t, axis=1)
    o_b, s_b = _gated_delta_chunked(rev(q), rev(k), rev(v), rev(g[:, :, 1]), rev(beta[:, :, 1]), s_bwd0)
    return o_f + rev(o_b), s_f, s_b


def _gdn_output(o, z, g_on):
    b_, l = o.shape[:2]
    y = _rmsnorm(o, g_on).reshape(b_, l, GDN_WIDTH)
    return y * jax.nn.silu(z.astype(F32))


def _swa_inputs(q, k, v, g_q, g_k):
    b_, l, _ = q.shape
    q = _rmsnorm(q.reshape(b_, l, SWA_HEADS, HEAD_DIM), g_q)
    k = _rmsnorm(k.reshape(b_, l, SWA_KV_HEADS, HEAD_DIM), g_k)
    return q, k, v.reshape(b_, l, SWA_KV_HEADS, HEAD_DIM)


def _window_attention(q, k, v, k_ctx, v_ctx, sink):
    b_, l, h, d = q.shape
    g = h // SWA_KV_HEADS
    nb = l // Q_BLOCK
    n_ctx = k_ctx.shape[1]
    scale = d ** -0.5
    qb = q.reshape(b_, nb, Q_BLOCK, SWA_KV_HEADS, g, d)

    def band_blocks(t):
        tp = jnp.pad(t, ((0, 0), (Q_BLOCK, Q_BLOCK), (0, 0), (0, 0))).reshape(b_, nb + 2, Q_BLOCK, SWA_KV_HEADS, d)
        return jnp.concatenate([tp[:, :-2], tp[:, 1:-1], tp[:, 2:]], axis=2)

    kw, vw = band_blocks(k), band_blocks(v)
    qi = jnp.arange(Q_BLOCK)[:, None]
    kj = jnp.arange(3 * Q_BLOCK)[None, :]
    band = jnp.abs(kj - Q_BLOCK - qi) <= WINDOW
    kpos = jnp.arange(nb)[:, None] * Q_BLOCK - Q_BLOCK + jnp.arange(3 * Q_BLOCK)[None, :]
    valid = band[None] & ((kpos >= 0) & (kpos < l))[:, None, :]
    s_win = jnp.einsum('bnqkgd,bnjkd->bnkgqj', qb, kw).astype(F32) * scale
    s_win = jnp.where(valid[None, :, None, None], s_win, NEG_INF)
    s_ctx = jnp.einsum('bnqkgd,bckd->bnkgqc', qb, k_ctx).astype(F32) * scale
    s_sink = jnp.broadcast_to(sink.astype(F32).reshape(SWA_KV_HEADS, g, 1, 1), s_ctx.shape[:-1] + (1,))
    p = jax.nn.softmax(jnp.concatenate([s_ctx, s_win, s_sink], axis=-1), axis=-1).astype(v.dtype)
    o = (jnp.einsum('bnkgqc,bckd->bnqkgd', p[..., :n_ctx], v_ctx)
         + jnp.einsum('bnkgqj,bnjkd->bnqkgd', p[..., n_ctx:n_ctx + 3 * Q_BLOCK], vw))
    return o.reshape(b_, l, h * d)


def _split_main(y):
    o = np.cumsum((0, 3 * GDN_WIDTH, GDN_WIDTH, SWA_WIDTH, SWA_KV_WIDTH, SWA_KV_WIDTH))
    return tuple(y[..., int(o[n]):int(o[n + 1])] for n in range(5))


def kernel(x, c, ctx, c_ctx, w_ada, b_ada, g_norm1, g_norm2, w_in, conv_qkv, a_log, dt_bias, g_onorm, g_qnorm,
           g_knorm, sink, w_out, w_router_grp, b_router_grp, w_router_exp, b_router_exp, w_gate, w_up, w_down):
    b_, l, d = x.shape
    n_ctx = ctx.shape[1]
    t = b_ * l
    assert w_ada.shape[0] == 1 and d == D_MODEL and b_ + 1 <= MOD_ROWS
    rows = jnp.repeat(jnp.arange(l // GRID_W, dtype=F32), GRID_W)
    cols = jnp.tile(jnp.arange(GRID_W, dtype=F32), l // GRID_W)

    wi = w_in[0]
    w_main = jnp.concatenate([wi[:, IN_OFFS[0]:IN_OFFS[2]], wi[:, IN_OFFS[4]:IN_OFFS[7]]], axis=1).astype(BF16)
    w_small = jnp.pad(wi[:, IN_OFFS[2]:IN_OFFS[4]], ((0, 0), (0, N_SMALL - 4 * GDN_HEADS))).astype(BF16)
    wo = w_out[0].astype(BF16)
    w_router = jnp.pad(jnp.concatenate([w_router_grp[0], w_router_exp[0]], axis=1),
                       ((0, 0), (0, N_ROUTER - N_GROUPS - N_EXPERTS))).astype(BF16)
    b_router = jnp.pad(jnp.concatenate([b_router_grp[0], b_router_exp[0]]),
                       (0, N_ROUTER - N_GROUPS - N_EXPERTS)).reshape(1, N_ROUTER)

    c_rows = jnp.zeros((MOD_ROWS, d), F32).at[:b_].set(c).at[b_].set(c_ctx)
    mod = _modulation(c_rows, w_ada[0], b_ada[0])
    mod3 = mod.reshape(MOD_ROWS * 6, 1, d)

    tm = 512
    tpb = l // tm
    yx, sx = _in_projection(x.reshape(t, d), g_norm1[0], mod3, lambda i: i // tpb, w_main, w_small, tm)
    yc, sc = _in_projection(ctx.reshape(b_ * n_ctx, d), g_norm1[0], mod3, lambda i: b_, w_main, w_small, n_ctx)
    qkv_x, z_x, sq_x, sk_x, sv_x = _split_main(yx.reshape(b_, l, N_MAIN))
    qkv_c, _, _, sk_c, sv_c = _split_main(yc.reshape(b_, n_ctx, N_MAIN))
    nh2 = 2 * GDN_HEADS
    bl_x, al_x = sx.reshape(b_, l, N_SMALL)[..., :nh2], sx.reshape(b_, l, N_SMALL)[..., nh2:2 * nh2]
    bl_c, al_c = sc.reshape(b_, n_ctx, N_SMALL)[..., :nh2], sc.reshape(b_, n_ctx, N_SMALL)[..., nh2:2 * nh2]

    zero = jnp.zeros((b_, GDN_HEADS, HEAD_DIM, HEAD_DIM), F32)
    _, s_fwd, s_bwd = _gdn_bidir(*_gdn_inputs(qkv_c, bl_c, al_c, conv_qkv[0], a_log[0], dt_bias[0]), zero, zero)
    oa_x, _, _ = _gdn_bidir(*_gdn_inputs(qkv_x, bl_x, al_x, conv_qkv[0], a_log[0], dt_bias[0]), s_fwd, s_bwd)
    ya_x = _gdn_output(oa_x, z_x, g_onorm[0])

    qx, kx, vx = _swa_inputs(sq_x, sk_x, sv_x, g_qnorm[0], g_knorm[0])
    qx, kx = _axial_rope(qx, rows, cols), _axial_rope(kx, rows, cols)
    kc = _rmsnorm(sk_c.reshape(b_, n_ctx, SWA_KV_HEADS, HEAD_DIM), g_knorm[0])
    vc = sv_c.reshape(b_, n_ctx, SWA_KV_HEADS, HEAD_DIM)
    yb_x = _window_attention(qx, kx, vx, kc, vc, sink[0])

    x1 = _out_projection(ya_x.reshape(t, GDN_WIDTH).astype(BF16), yb_x.reshape(t, SWA_WIDTH).astype(BF16),
                         wo[:GDN_WIDTH], wo[GDN_WIDTH:], x.reshape(t, d), mod3, tpb, tm)

    tm2 = 256
    h2, logits = _norm2_router(x1, g_norm2[0], mod3, l // tm2, w_router, b_router, tm2)
    slot_tok, slot_w, slot_of, blk_expert, n_used = _route(logits, t)
    y = _moe_experts(h2[slot_tok], blk_expert, n_used, slot_w, w_gate[0], w_up[0], w_down[0])
    moe = y[slot_of[0::2]] + y[slot_of[1::2]]
    gate2 = mod[:b_, 5 * d:6 * d][:, None, :]
    return x1.reshape(b_, l, d) + gate2 * moe.reshape(b_, l, d)
```

```python
import functools
import math

import jax
import jax.numpy as jnp
import numpy as np
from jax import lax
from jax.experimental import pallas as pl
from jax.experimental.pallas import tpu as pltpu

F32 = jnp.float32
BF16 = jnp.bfloat16

D_MODEL = 4096
CTX_LEN = 256
GRID_W = 64
HEAD_DIM = 128
GDN_HEADS = 16
GDN_WIDTH = GDN_HEADS * HEAD_DIM
GDN_CHUNK = 64
CONV_K = 5
SWA_HEADS = 16
SWA_KV_HEADS = 4
SWA_WIDTH = SWA_HEADS * HEAD_DIM
SWA_KV_WIDTH = SWA_KV_HEADS * HEAD_DIM
WINDOW = 128
Q_BLOCK = 128
ROPE_BASE = 10000.0
N_GROUPS = 8
EXPERTS_PER_GROUP = 8
N_EXPERTS = N_GROUPS * EXPERTS_PER_GROUP
TOP_K = 2
D_EXPERT = D_MODEL // 8
EPS = 1e-6
NEG_INF = -1e30

IN_SIZES = (3 * GDN_WIDTH, GDN_WIDTH, 2 * GDN_HEADS, 2 * GDN_HEADS, SWA_WIDTH, SWA_KV_WIDTH, SWA_KV_WIDTH)
IN_OFFS = tuple(int(v) for v in np.cumsum((0,) + IN_SIZES))
N_MAIN = 3 * GDN_WIDTH + GDN_WIDTH + SWA_WIDTH + 2 * SWA_KV_WIDTH
N_SMALL = 128
N_ROUTER = 128
MOD_ROWS = 8

MOE_BM = 256
VMEM_LIMIT = 56 * 1024 * 1024


def _cparams(sem):
    return pltpu.CompilerParams(dimension_semantics=sem, vmem_limit_bytes=VMEM_LIMIT)


def _mod_kernel(c_ref, w_ref, b_ref, o_ref):
    c = c_ref[...]
    a = (c * jax.nn.sigmoid(c)).astype(BF16)
    o_ref[...] = jnp.dot(a, w_ref[...].astype(BF16), preferred_element_type=F32) + b_ref[...]


def _modulation(c_rows, w_ada, b_ada):
    d, n = w_ada.shape
    tn = 512
    return pl.pallas_call(
        _mod_kernel,
        grid=(n // tn,),
        in_specs=[pl.BlockSpec((MOD_ROWS, d), lambda j: (0, 0)),
                  pl.BlockSpec((d, tn), lambda j: (0, j)),
                  pl.BlockSpec((1, tn), lambda j: (0, j))],
        out_specs=pl.BlockSpec((MOD_ROWS, tn), lambda j: (0, j)),
        out_shape=jax.ShapeDtypeStruct((MOD_ROWS, n), F32),
        compiler_params=_cparams(("arbitrary",)),
        name="modulation",
    )(c_rows, w_ada, b_ada.reshape(1, n))


NORM_ROWS = 64


def _norm_mod_rows(x_ref, g_ref, sh_ref, sc_ref, h_ref, tm):
    g = g_ref[...]
    sc = 1.0 + sc_ref[0]
    sh = sh_ref[0]

    def body(r, carry):
        rows = pl.ds(pl.multiple_of(r * NORM_ROWS, NORM_ROWS), NORM_ROWS)
        xf = x_ref[rows, :]
        ms = jnp.mean(xf * xf, axis=-1, keepdims=True)
        y = xf * lax.rsqrt(ms + EPS) * g
        h_ref[rows, :] = (y * sc + sh).astype(h_ref.dtype)
        return carry

    lax.fori_loop(0, tm // NORM_ROWS, body, 0)


def _inproj_kernel(x_ref, g_ref, sh_ref, sc_ref, w_ref, ws_ref, o_ref, os_ref, h_ref, *, tm):
    @pl.when(pl.program_id(1) == 0)
    def _():
        _norm_mod_rows(x_ref, g_ref, sh_ref, sc_ref, h_ref, tm)
        os_ref[...] = jnp.dot(h_ref[...], ws_ref[...], preferred_element_type=F32)

    o_ref[...] = jnp.dot(h_ref[...], w_ref[...], preferred_element_type=F32)


def _in_projection(x2d, g_norm, mod3, mod_row_of_tile, w_main, w_small, tm):
    t, d = x2d.shape
    tn = 1024
    return pl.pallas_call(
        functools.partial(_inproj_kernel, tm=tm),
        grid=(t // tm, N_MAIN // tn),
        in_specs=[pl.BlockSpec((tm, d), lambda i, j: (i, 0)),
                  pl.BlockSpec((1, d), lambda i, j: (0, 0)),
                  pl.BlockSpec((1, 1, d), lambda i, j: (mod_row_of_tile(i) * 6 + 0, 0, 0)),
                  pl.BlockSpec((1, 1, d), lambda i, j: (mod_row_of_tile(i) * 6 + 1, 0, 0)),
                  pl.BlockSpec((d, tn), lambda i, j: (0, j)),
                  pl.BlockSpec((d, N_SMALL), lambda i, j: (0, 0))],
        out_specs=[pl.BlockSpec((tm, tn), lambda i, j: (i, j)),
                   pl.BlockSpec((tm, N_SMALL), lambda i, j: (i, 0))],
        out_shape=[jax.ShapeDtypeStruct((t, N_MAIN), F32),
                   jax.ShapeDtypeStruct((t, N_SMALL), F32)],
        scratch_shapes=[pltpu.VMEM((tm, d), BF16)],
        compiler_params=_cparams(("arbitrary", "arbitrary")),
        name="in_projection",
    )(x2d, g_norm.reshape(1, d), mod3, mod3, w_main, w_small)


def _outproj_kernel(ya_ref, yb_ref, wa_ref, wb_ref, x_ref, gate_ref, o_ref):
    acc = jnp.dot(ya_ref[...], wa_ref[...], preferred_element_type=F32)
    acc = acc + jnp.dot(yb_ref[...], wb_ref[...], preferred_element_type=F32)
    o_ref[...] = x_ref[...] + gate_ref[0] * acc


def _out_projection(ya, yb, wa, wb, x2d, mod3, tiles_per_batch, tm):
    t, d = x2d.shape
    tn = 1024
    nj = d // tn
    ka, kb = ya.shape[1], yb.shape[1]
    return pl.pallas_call(
        _outproj_kernel,
        grid=(t // tm, nj),
        in_specs=[pl.BlockSpec((tm, ka), lambda i, j: (i, 0)),
                  pl.BlockSpec((tm, kb), lambda i, j: (i, 0)),
                  pl.BlockSpec((ka, tn), lambda i, j: (0, j)),
                  pl.BlockSpec((kb, tn), lambda i, j: (0, j)),
                  pl.BlockSpec((tm, tn), lambda i, j: (i, j)),
                  pl.BlockSpec((1, 1, tn), lambda i, j: (((i // tiles_per_batch) * 6 + 2) * nj + j, 0, 0))],
        out_specs=pl.BlockSpec((tm, tn), lambda i, j: (i, j)),
        out_shape=jax.ShapeDtypeStruct((t, d), F32),
        compiler_params=_cparams(("arbitrary", "arbitrary")),
        name="out_projection",
    )(ya, yb, wa, wb, x2d, mod3.reshape(-1, 1, tn))


def _norm2_kernel(x_ref, g_ref, sh_ref, sc_ref, wr_ref, br_ref, h_ref, lg_ref, *, tm):
    _norm_mod_rows(x_ref, g_ref, sh_ref, sc_ref, h_ref, tm)
    lg_ref[...] = jnp.dot(h_ref[...], wr_ref[...], preferred_element_type=F32) + br_ref[...]


def _norm2_router(x2d, g_norm, mod3, tiles_per_batch, w_router, b_router, tm):
    t, d = x2d.shape
    return pl.pallas_call(
        functools.partial(_norm2_kernel, tm=tm),
        grid=(t // tm,),
        in_specs=[pl.BlockSpec((tm, d), lambda i: (i, 0)),
                  pl.BlockSpec((1, d), lambda i: (0, 0)),
                  pl.BlockSpec((1, 1, d), lambda i: ((i // tiles_per_batch) * 6 + 3, 0, 0)),
                  pl.BlockSpec((1, 1, d), lambda i: ((i // tiles_per_batch) * 6 + 4, 0, 0)),
                  pl.BlockSpec((d, N_ROUTER), lambda i: (0, 0)),
                  pl.BlockSpec((1, N_ROUTER), lambda i: (0, 0))],
        out_specs=[pl.BlockSpec((tm, d), lambda i: (i, 0)),
                   pl.BlockSpec((tm, N_ROUTER), lambda i: (i, 0))],
        out_shape=[jax.ShapeDtypeStruct((t, d), BF16),
                   jax.ShapeDtypeStruct((t, N_ROUTER), F32)],
        compiler_params=_cparams(("arbitrary",)),
        name="norm2_router",
    )(x2d, g_norm.reshape(1, d), mod3, mod3, w_router, b_router)


CAST_ROWS = 128


def _cast_rows(src_ref, dst_ref):
    n = src_ref.shape[1]

    def body(r, carry):
        rows = pl.ds(pl.multiple_of(r * CAST_ROWS, CAST_ROWS), CAST_ROWS)
        dst_ref[rows, :] = src_ref[0, rows, :].astype(dst_ref.dtype)
        return carry

    lax.fori_loop(0, n // CAST_ROWS, body, 0)


def _expert_changed(be_ref, i):
    return (i == 0) | (be_ref[i] != be_ref[jnp.maximum(i - 1, 0)])


def _moe_up_kernel(be_ref, nu_ref, x_ref, wg_ref, wu_ref, o_ref, wgb_ref, wub_ref):
    i = pl.program_id(0)

    @pl.when(_expert_changed(be_ref, i))
    def _():
        _cast_rows(wg_ref, wgb_ref)
        _cast_rows(wu_ref, wub_ref)

    @pl.when(i < nu_ref[0])
    def _():
        xb = x_ref[...]
        g = jnp.dot(xb, wgb_ref[...], preferred_element_type=F32)
        u = jnp.dot(xb, wub_ref[...], preferred_element_type=F32)
        o_ref[...] = (g * jax.nn.sigmoid(g) * u).astype(o_ref.dtype)

    @pl.when(i >= nu_ref[0])
    def _():
        o_ref[...] = jnp.zeros_like(o_ref)


def _moe_down_kernel(be_ref, nu_ref, h_ref, wd_ref, sw_ref, o_ref, wdb_ref):
    i = pl.program_id(0)

    @pl.when(_expert_changed(be_ref, i))
    def _():
        _cast_rows(wd_ref, wdb_ref)

    @pl.when(i < nu_ref[0])
    def _():
        y = jnp.dot(h_ref[...], wdb_ref[...], preferred_element_type=F32)
        o_ref[...] = y * sw_ref[...]

    @pl.when(i >= nu_ref[0])
    def _():
        o_ref[...] = jnp.zeros_like(o_ref)


def _moe_experts(xg, blk_expert, n_used, slot_w, w_gate, w_up, w_down):
    p, d = xg.shape
    n_blocks = p // MOE_BM
    de = w_gate.shape[-1]
    hmid = pl.pallas_call(
        _moe_up_kernel,
        grid_spec=pltpu.PrefetchScalarGridSpec(
            num_scalar_prefetch=2,
            grid=(n_blocks,),
            in_specs=[pl.BlockSpec((MOE_BM, d), lambda i, be, nu: (i, 0)),
                      pl.BlockSpec((1, d, de), lambda i, be, nu: (be[i], 0, 0)),
                      pl.BlockSpec((1, d, de), lambda i, be, nu: (be[i], 0, 0))],
            out_specs=pl.BlockSpec((MOE_BM, de), lambda i, be, nu: (i, 0)),
            scratch_shapes=[pltpu.VMEM((d, de), BF16), pltpu.VMEM((d, de), BF16)]),
        out_shape=jax.ShapeDtypeStruct((p, de), BF16),
        compiler_params=_cparams(("arbitrary",)),
        name="moe_gate_up",
    )(blk_expert, n_used, xg, w_gate, w_up)
    return pl.pallas_call(
        _moe_down_kernel,
        grid_spec=pltpu.PrefetchScalarGridSpec(
            num_scalar_prefetch=2,
            grid=(n_blocks,),
            in_specs=[pl.BlockSpec((MOE_BM, de), lambda i, be, nu: (i, 0)),
                      pl.BlockSpec((1, de, d), lambda i, be, nu: (be[i], 0, 0)),
                      pl.BlockSpec((MOE_BM, 1), lambda i, be, nu: (i, 0))],
            out_specs=pl.BlockSpec((MOE_BM, d), lambda i, be, nu: (i, 0)),
            scratch_shapes=[pltpu.VMEM((de, d), BF16)]),
        out_shape=jax.ShapeDtypeStruct((p, d), F32),
        compiler_params=_cparams(("arbitrary",)),
        name="moe_down",
    )(blk_expert, n_used, hmid, w_down, slot_w.reshape(p, 1))


def _route(logits, t):
    lg = logits[:, :N_GROUPS]
    le = logits[:, N_GROUPS:N_GROUPS + N_EXPERTS].reshape(t, N_GROUPS, EXPERTS_PER_GROUP)
    p_grp, grp = lax.top_k(jax.nn.softmax(lg, axis=-1), 1)
    le = jnp.take_along_axis(le, grp[:, :, None], axis=1)[:, 0]
    p_in, e_in = lax.top_k(jax.nn.softmax(le, axis=-1), TOP_K)
    gate = p_grp * p_in / jnp.sum(p_in, axis=-1, keepdims=True)
    eid = (grp * EXPERTS_PER_GROUP + e_in).astype(jnp.int32)
    a = t * TOP_K
    e_flat = eid.reshape(a)
    tok_flat = jnp.repeat(jnp.arange(t, dtype=jnp.int32), TOP_K)
    order = jnp.argsort(e_flat)
    e_sorted = e_flat[order]
    counts = jax.ops.segment_sum(jnp.ones((a,), jnp.int32), e_flat, num_segments=N_EXPERTS)
    padded = (counts + MOE_BM - 1) // MOE_BM * MOE_BM
    start = jnp.cumsum(counts) - counts
    pend = jnp.cumsum(padded)
    pstart = pend - padded
    dest = pstart[e_sorted] + jnp.arange(a, dtype=jnp.int32) - start[e_sorted]
    n_blocks = (a + MOE_BM - 1) // MOE_BM + N_EXPERTS
    p = n_blocks * MOE_BM
    slot_tok = jnp.zeros((p,), jnp.int32).at[dest].set(tok_flat[order])
    slot_w = jnp.zeros((p,), F32).at[dest].set(gate.reshape(a)[order])
    slot_of = jnp.zeros((a,), jnp.int32).at[order].set(dest)
    blk_expert = jnp.minimum(jnp.searchsorted(pend, jnp.arange(n_blocks, dtype=jnp.int32) * MOE_BM,
                                              side='right'), N_EXPERTS - 1).astype(jnp.int32)
    n_used = (pend[-1] // MOE_BM).astype(jnp.int32).reshape(1)
    return slot_tok, slot_w, slot_of, blk_expert, n_used


GDN_TB = 256
HALO = 8
N_QKV = 3 * GDN_WIDTH


def _softplus(v):
    return jnp.maximum(v, 0.0) + jnp.log(1.0 + jnp.exp(-jnp.abs(v)))


def _split3_bf16(v):
    hi = v.astype(BF16)
    r1 = v - hi.astype(F32)
    mid = r1.astype(BF16)
    lo = (r1 - mid.astype(F32)).astype(BF16)
    return hi, mid, lo


def _gdn_prep_kernel(cur_ref, prev_ref, next_ref, sm_ref, cw_ref, ea_ref, dt_ref, o_ref, g_ref, ext_ref, *, tb):
    i = pl.program_id(1)
    nblk = pl.num_programs(1)
    ext_ref[pl.ds(0, HALO), :] = jnp.where(i > 0, prev_ref[...], 0.0)
    ext_ref[pl.ds(HALO, tb), :] = cur_ref[...]
    ext_ref[pl.ds(HALO + tb, HALO), :] = jnp.where(i < nblk - 1, next_ref[...], 0.0)

    def conv_cols(kind):
        def body(hh, carry):
            cols = pl.ds(pl.multiple_of((kind * GDN_HEADS + hh) * HEAD_DIM, HEAD_DIM), HEAD_DIM)
            cw = cw_ref[:, cols]
            for r0 in range(0, tb, 64):
                acc = None
                for s in range(CONV_K):
                    term = ext_ref[pl.ds(HALO - CONV_K // 2 + s + r0, 64), cols] * cw[s:s + 1, :]
                    acc = term if acc is None else acc + term
                y = acc * jax.nn.sigmoid(acc)
                if kind < 2:
                    y = y * lax.rsqrt(jnp.sum(y * y, axis=-1, keepdims=True) + EPS)
                if kind == 0:
                    y = y * HEAD_DIM ** -0.5
                o_ref[pl.ds(r0, 64), cols] = y.astype(o_ref.dtype)
            return carry

        lax.fori_loop(0, GDN_HEADS, body, 0)

    conv_cols(0)
    conv_cols(1)
    conv_cols(2)

    s = sm_ref[...]
    beta = jax.nn.sigmoid(s)
    g = -ea_ref[...] * _softplus(s + dt_ref[...])
    r = lax.broadcasted_iota(jnp.int32, (tb, tb), 0)
    c = lax.broadcasted_iota(jnp.int32, (tb, tb), 1)
    same = (r // GDN_CHUNK) == (c // GDN_CHUNK)
    lower = (same & (c <= r)).astype(BF16)
    upper = (same & (c >= r)).astype(BF16)
    parts = _split3_bf16(g)
    cf = sum(jnp.dot(lower, pt, preferred_element_type=F32) for pt in parts)
    cb = sum(jnp.dot(upper, pt, preferred_element_type=F32) for pt in parts)
    col = lax.broadcasted_iota(jnp.int32, s.shape, 1)
    nh = GDN_HEADS
    g_ref[...] = jnp.where(col < 2 * nh, beta, jnp.where(col < 3 * nh, cf, jnp.where(col < 4 * nh, cb, 0.0)))


def _gdn_prep(y_main, small, conv_w8, ea_row, dt_row, b_, l):
    tb = min(GDN_TB, l)
    nblk = l // tb
    hb = tb // HALO
    last = b_ * l // HALO - 1
    return pl.pallas_call(
        functools.partial(_gdn_prep_kernel, tb=tb),
        grid=(b_, nblk),
        in_specs=[pl.BlockSpec((tb, N_QKV), lambda b, i: (b * nblk + i, 0)),
                  pl.BlockSpec((HALO, N_QKV), lambda b, i: (jnp.maximum((b * nblk + i) * hb - 1, 0), 0)),
                  pl.BlockSpec((HALO, N_QKV), lambda b, i: (jnp.minimum((b * nblk + i + 1) * hb, last), 0)),
                  pl.BlockSpec((tb, N_SMALL), lambda b, i: (b * nblk + i, 0)),
                  pl.BlockSpec((HALO, N_QKV), lambda b, i: (0, 0)),
                  pl.BlockSpec((1, N_SMALL), lambda b, i: (0, 0)),
                  pl.BlockSpec((1, N_SMALL), lambda b, i: (0, 0))],
        out_specs=[pl.BlockSpec((tb, N_QKV), lambda b, i: (b * nblk + i, 0)),
                   pl.BlockSpec((tb, N_SMALL), lambda b, i: (b * nblk + i, 0))],
        out_shape=[jax.ShapeDtypeStruct((b_ * l, N_QKV), BF16),
                   jax.ShapeDtypeStruct((b_ * l, N_SMALL), F32)],
        scratch_shapes=[pltpu.VMEM((tb + 2 * HALO, N_QKV), F32)],
        compiler_params=_cparams(("arbitrary", "arbitrary")),
        name="gdn_prep",
    )(y_main, y_main, y_main, small, conv_w8, ea_row, dt_row)


N_PAIRS = GDN_HEADS // 2
PK = 2 * GDN_CHUNK
INV_LEVELS = (2, 4, 8, 16, 32, 64)


def _gdn_masks(reverse):
    i = np.arange(PK)[:, None]
    j = np.arange(PK)[None, :]
    same = (i // GDN_CHUNK) == (j // GDN_CHUNK)
    strict = same & ((j > i) if reverse else (j < i))
    out = []
    for bs in INV_LEVELS:
        out.append(strict & (i // bs == j // bs) & (i // (bs // 2) != j // (bs // 2)))
    incl = same & ((j >= i) if reverse else (j <= i))
    out.append(incl)
    m = np.stack(out).astype(np.float32)
    neg = ((incl.astype(np.float32) - 1.0) * 1e30)[None]
    return jnp.asarray(np.concatenate([m, neg], axis=0))


def _gdn_scan_kernel(*refs, final):
    if final:
        (q_ref, k_ref, v_ref, cp_ref, rp_ref, ap_ref, s0_ref, mk_ref, op_ref, z_ref, gon_ref,
         o_ref, sfin_ref, s_scr) = refs
    else:
        q_ref, k_ref, v_ref, cp_ref, rp_ref, ap_ref, s0_ref, mk_ref, o_ref, sfin_ref, s_scr = refs
    c = pl.program_id(1)

    @pl.when(c == 0)
    def _():
        s_scr[...] = s0_ref[0]

    nl = len(INV_LEVELS)
    incl = mk_ref[nl]
    negm = mk_ref[nl + 1]
    ri = lax.broadcasted_iota(jnp.int32, (PK, 1), 0)
    top = (ri < GDN_CHUNK).astype(F32)
    bot = 1.0 - top
    rr = lax.broadcasted_iota(jnp.int32, (PK, PK), 0)
    cc = lax.broadcasted_iota(jnp.int32, (PK, PK), 1)
    eye = (rr == cc).astype(F32)
    cp = cp_ref[0, 0]
    rp = rp_ref[0, 0]
    ap = ap_ref[0, 0]
    dot = functools.partial(jnp.dot, preferred_element_type=F32)

    pairs = range(N_PAIRS)
    nt = (((1,), (1,)), ((), ()))
    tn = (((0,), (0,)), ((), ()))
    cols = [(pl.ds(2 * p * HEAD_DIM, HEAD_DIM), pl.ds((2 * p + 1) * HEAD_DIM, HEAD_DIM)) for p in pairs]
    pack = lambda ref, p: jnp.concatenate([ref[:, cols[p][0]], ref[:, cols[p][1]]], axis=0)
    gcol = [cp[:, p:p + 1] for p in pairs]
    bcol = [cp[:, N_PAIRS + p:N_PAIRS + p + 1] for p in pairs]
    glcol = [cp[:, 2 * N_PAIRS + p:2 * N_PAIRS + p + 1] for p in pairs]
    kp = [pack(k_ref, p) for p in pairs]
    qp = [pack(q_ref, p) for p in pairs]
    kk = [lax.dot_general(kp[p], kp[p], nt, preferred_element_type=F32) for p in pairs]
    qk = [lax.dot_general(qp[p], kp[p], nt, preferred_element_type=F32) for p in pairs]
    dec = [jnp.exp((gcol[p] - rp[p:p + 1, :]) * incl + negm) for p in pairs]
    a = [kk[p] * dec[p] * bcol[p] for p in pairs]
    qkm = [(qk[p] * dec[p]).astype(BF16) for p in pairs]

    x = [eye - a[p] * mk_ref[0] for p in pairs]
    for lv in range(1, nl):
        xb = [x[p].astype(BF16) for p in pairs]
        po = [dot(xb[p], (a[p] * mk_ref[lv]).astype(BF16)) for p in pairs]
        x = [x[p] - dot(po[p].astype(BF16), xb[p]) for p in pairs]
    tb = [x[p].astype(BF16) for p in pairs]

    egc = [jnp.exp(gcol[p]) for p in pairs]
    kf = [kp[p].astype(F32) for p in pairs]
    u = [dot(tb[p], (pack(v_ref, p).astype(F32) * bcol[p]).astype(BF16)) for p in pairs]
    w = [dot(tb[p], (kf[p] * (bcol[p] * egc[p])).astype(BF16)) for p in pairs]
    qd = [qp[p].astype(F32) * egc[p] for p in pairs]
    kd = [kf[p] * jnp.exp(glcol[p] - gcol[p]) for p in pairs]

    s = [s_scr[p] for p in pairs]
    lhs = [jnp.concatenate([jnp.concatenate([w[p] * top, w[p] * bot], axis=1),
                            jnp.concatenate([qd[p] * top, qd[p] * bot], axis=1)], axis=0).astype(BF16)
           for p in pairs]
    ws = [dot(lhs[p], s[p].astype(BF16)) for p in pairs]
    vnb = [(u[p] - ws[p][:PK]).astype(BF16) for p in pairs]
    o = [ws[p][PK:] + dot(qkm[p], vnb[p]) for p in pairs]
    kbd = [jnp.concatenate([kd[p] * top, kd[p] * bot], axis=1).astype(BF16) for p in pairs]
    kv = [lax.dot_general(kbd[p], vnb[p], tn, preferred_element_type=F32) for p in pairs]
    for p in pairs:
        s_scr[p] = jnp.exp(ap[:, p:p + 1]) * s[p] + kv[p]

    for p in pairs:
        op = o[p]
        if final:
            op = op + pack(op_ref, p)
            zz = pack(z_ref, p)
            op = op * lax.rsqrt(jnp.mean(op * op, axis=-1, keepdims=True) + EPS) * gon_ref[...]
            op = op * (zz * jax.nn.sigmoid(zz))
        o_ref[:, cols[p][0]] = op[:GDN_CHUNK].astype(o_ref.dtype)
        o_ref[:, cols[p][1]] = op[GDN_CHUNK:].astype(o_ref.dtype)

    @pl.when(c == pl.num_programs(1) - 1)
    def _():
        sfin_ref[0] = s_scr[...]


def _gdn_packs(gates, b_, l, reverse):
    nc = l // GDN_CHUNK
    nh = GDN_HEADS
    g4 = gates.reshape(b_, nc, GDN_CHUNK, N_SMALL)
    d = 1 if reverse else 0
    beta = g4[..., d * nh:(d + 1) * nh]
    gc = g4[..., (2 + d) * nh:(3 + d) * nh]
    gl = jnp.broadcast_to(gc[:, :, 0:1] if reverse else gc[:, :, GDN_CHUNK - 1:GDN_CHUNK], gc.shape)

    def rowpack(t):
        return jnp.transpose(t.reshape(b_, nc, GDN_CHUNK, N_PAIRS, 2), (0, 1, 3, 4, 2)).reshape(b_, nc, N_PAIRS, PK)

    rp = rowpack(gc)
    cp = jnp.concatenate([jnp.swapaxes(rowpack(t), 2, 3) for t in (gc, beta, gl)]
                         + [jnp.zeros((b_, nc, PK, N_PAIRS), F32)], axis=-1)
    glh = gl[:, :, 0].reshape(b_, nc, N_PAIRS, 2)
    ap = jnp.swapaxes(jnp.repeat(glh, HEAD_DIM, axis=-1), 2, 3)
    return cp, rp, ap


def _gdn_scan(qkv, gates, s0, b_, l, reverse, final_args=None):
    nc = l // GDN_CHUNK
    cp, rp, ap = _gdn_packs(gates, b_, l, reverse)
    masks = _gdn_masks(reverse)
    final = final_args is not None
    ci = (lambda c: nc - 1 - c) if reverse else (lambda c: c)
    tok = lambda col: pl.BlockSpec((GDN_CHUNK, GDN_WIDTH), lambda b, c: (b * nc + ci(c), col))
    per_chunk = lambda shp: pl.BlockSpec((1, 1) + shp, lambda b, c: (b, ci(c), 0, 0))
    state = pl.BlockSpec((1, N_PAIRS, 2 * HEAD_DIM, HEAD_DIM), lambda b, c: (b, 0, 0, 0))
    in_specs = [tok(0), tok(1), tok(2), per_chunk((PK, 4 * N_PAIRS)), per_chunk((N_PAIRS, PK)),
                per_chunk((2 * HEAD_DIM, N_PAIRS)), state,
                pl.BlockSpec(masks.shape, lambda b, c: (0, 0, 0))]
    args = [qkv, qkv, qkv, cp, rp, ap, s0, masks]
    if final:
        o_prev, y_main, g_on = final_args
        in_specs += [tok(0), tok(N_QKV // GDN_WIDTH), pl.BlockSpec((1, HEAD_DIM), lambda b, c: (0, 0))]
        args += [o_prev, y_main, g_on.reshape(1, HEAD_DIM)]
    return pl.pallas_call(
        functools.partial(_gdn_scan_kernel, final=final),
        grid=(b_, nc),
        in_specs=in_specs,
        out_specs=[tok(0), state],
        out_shape=[jax.ShapeDtypeStruct((b_ * l, GDN_WIDTH), BF16 if final else F32),
                   jax.ShapeDtypeStruct(s0.shape, F32)],
        scratch_shapes=[pltpu.VMEM((N_PAIRS, 2 * HEAD_DIM, HEAD_DIM), F32)],
        compiler_params=_cparams(("arbitrary", "arbitrary")),
        name="gdn_scan_bwd" if reverse else "gdn_scan_fwd",
    )(*args)


def _gdn_mixer(yx, sx, yc, sc, conv_w, a_log, dt_bias, g_on, b_, l, n_ctx):
    nh = GDN_HEADS
    conv_w8 = jnp.pad(conv_w, ((0, HALO - CONV_K), (0, 0)))
    ea_row = jnp.zeros((1, N_SMALL), F32).at[0, 2 * nh:4 * nh].set(jnp.exp(a_log.reshape(-1)))
    dt_row = jnp.zeros((1, N_SMALL), F32).at[0, 2 * nh:4 * nh].set(dt_bias.reshape(-1))
    qkv_c, gates_c = _gdn_prep(yc, sc, conv_w8, ea_row, dt_row, b_, n_ctx)
    qkv_x, gates_x = _gdn_prep(yx, sx, conv_w8, ea_row, dt_row, b_, l)
    zero = jnp.zeros((b_, N_PAIRS, 2 * HEAD_DIM, HEAD_DIM), F32)
    _, s_f = _gdn_scan(qkv_c, gates_c, zero, b_, n_ctx, False)
    _, s_b = _gdn_scan(qkv_c, gates_c, zero, b_, n_ctx, True)
    o_f, _ = _gdn_scan(qkv_x, gates_x, s_f, b_, l, False)
    ya, _ = _gdn_scan(qkv_x, gates_x, s_b, b_, l, True, final_args=(o_f, yx, g_on))
    return ya


SWA_GROUP = SWA_HEADS // SWA_KV_HEADS
ROT = HEAD_DIM // 4


def _rope_tables(l):
    half = HEAD_DIM // 2
    inv = ROPE_BASE ** (-jnp.arange(0, half, 2, dtype=F32) / half)
    pos = jnp.arange(l, dtype=jnp.int32)
    ang_r = (pos // GRID_W).astype(F32)[:, None] * inv
    ang_c = (pos % GRID_W).astype(F32)[:, None] * inv
    zero = jnp.zeros_like(ang_r)
    cos = jnp.concatenate([jnp.cos(ang_r), jnp.cos(ang_r), jnp.cos(ang_c), jnp.cos(ang_c)], axis=1)
    sin_up = jnp.concatenate([-jnp.sin(ang_r), zero, -jnp.sin(ang_c), zero], axis=1)
    sin_dn = jnp.concatenate([zero, jnp.sin(ang_r), zero, jnp.sin(ang_c)], axis=1)
    return cos, sin_up, sin_dn


def _swa_prep_kernel(q_ref, k_ref, v_ref, cos_ref, su_ref, sd_ref, gq_ref, gk_ref, qo_ref, ko_ref, vo_ref):
    cos, su, sd = cos_ref[...], su_ref[...], sd_ref[...]

    def norm_rope(t, g, scale):
        y = t * lax.rsqrt(jnp.mean(t * t, axis=-1, keepdims=True) + EPS) * g
        y = y * cos + pltpu.roll(y, HEAD_DIM - ROT, 1) * su + pltpu.roll(y, ROT, 1) * sd
        return y * scale if scale != 1.0 else y

    for h in range(SWA_HEADS):
        c = pl.ds(h * HEAD_DIM, HEAD_DIM)
        qo_ref[:, c] = norm_rope(q_ref[:, c], gq_ref[...], HEAD_DIM ** -0.5).astype(qo_ref.dtype)
    for h in range(SWA_KV_HEADS):
        c = pl.ds(h * HEAD_DIM, HEAD_DIM)
        ko_ref[:, c] = norm_rope(k_ref[:, c], gk_ref[...], 1.0).astype(ko_ref.dtype)
    vo_ref[...] = v_ref[...].astype(vo_ref.dtype)


def _swa_prep(y_main, tables, g_q, g_k, rows, tm):
    q_blk = (N_QKV + GDN_WIDTH) // SWA_WIDTH
    k_blk = (N_QKV + GDN_WIDTH + SWA_WIDTH) // SWA_KV_WIDTH
    tpb = tables[0].shape[0] // tm
    tab = pl.BlockSpec((tm, HEAD_DIM), lambda i: (i % tpb, 0))
    vec = pl.BlockSpec((1, HEAD_DIM), lambda i: (0, 0))
    return pl.pallas_call(
        _swa_prep_kernel,
        grid=(rows // tm,),
        in_specs=[pl.BlockSpec((tm, SWA_WIDTH), lambda i: (i, q_blk)),
                  pl.BlockSpec((tm, SWA_KV_WIDTH), lambda i: (i, k_blk)),
                  pl.BlockSpec((tm, SWA_KV_WIDTH), lambda i: (i, k_blk + 1)),
                  tab, tab, tab, vec, vec],
        out_specs=[pl.BlockSpec((tm, SWA_WIDTH), lambda i: (i, 0)),
                   pl.BlockSpec((tm, SWA_KV_WIDTH), lambda i: (i, 0)),
                   pl.BlockSpec((tm, SWA_KV_WIDTH), lambda i: (i, 0))],
        out_shape=[jax.ShapeDtypeStruct((rows, SWA_WIDTH), BF16),
                   jax.ShapeDtypeStruct((rows, SWA_KV_WIDTH), BF16),
                   jax.ShapeDtypeStruct((rows, SWA_KV_WIDTH), BF16)],
        compiler_params=_cparams(("arbitrary",)),
        name="swa_prep",
    )(y_main, y_main, y_main, *tables, g_q.reshape(1, HEAD_DIM), g_k.reshape(1, HEAD_DIM))


def _swa_attn_kernel(q_ref, kp_ref, kc_ref, kn_ref, vp_ref, vc_ref, vn_ref, kx_ref, vx_ref, sink_ref, o_ref, *, n_ctx):
    n = pl.program_id(1)
    nb = pl.num_programs(1)
    rows = SWA_GROUP * Q_BLOCK
    nk = n_ctx + 3 * Q_BLOCK
    qi = lax.broadcasted_iota(jnp.int32, (rows, nk), 0) & (Q_BLOCK - 1)
    kj = lax.broadcasted_iota(jnp.int32, (rows, nk), 1) - n_ctx
    lo = jnp.where(n == 0, Q_BLOCK, 0)
    hi = jnp.where(n == nb - 1, 2 * Q_BLOCK, 3 * Q_BLOCK)
    valid = (kj < 0) | ((kj >= qi) & (kj <= qi + 2 * WINDOW) & (kj >= lo) & (kj < hi))
    hsel = lax.broadcasted_iota(jnp.int32, (rows, 1), 0) // Q_BLOCK
    nt = (((1,), (1,)), ((), ()))
    for j in range(SWA_KV_HEADS):
        c = pl.ds(j * HEAD_DIM, HEAD_DIM)
        heads = [j * SWA_GROUP + g for g in range(SWA_GROUP)]
        q = jnp.concatenate([q_ref[:, pl.ds(h * HEAD_DIM, HEAD_DIM)] for h in heads], axis=0)
        k = jnp.concatenate([kx_ref[:, c], kp_ref[:, c], kc_ref[:, c], kn_ref[:, c]], axis=0)
        v = jnp.concatenate([vx_ref[:, c], vp_ref[:, c], vc_ref[:, c], vn_ref[:, c]], axis=0)
        sink = jnp.zeros((rows, 1), F32)
        for g, h in enumerate(heads):
            sink = jnp.where(hsel == g, sink_ref[h:h + 1, 0:1], sink)
        s = jnp.where(valid, lax.dot_general(q, k, nt, preferred_element_type=F32), NEG_INF)
        m = jnp.maximum(jnp.max(s, axis=-1, keepdims=True), sink)
        p = jnp.exp(s - m)
        den = jnp.sum(p, axis=-1, keepdims=True) + jnp.exp(sink - m)
        o = jnp.dot(p.astype(BF16), v, preferred_element_type=F32) / den
        for g, h in enumerate(heads):
            o_ref[:, pl.ds(h * HEAD_DIM, HEAD_DIM)] = o[g * Q_BLOCK:(g + 1) * Q_BLOCK].astype(o_ref.dtype)


def _swa_attention(q, k, v, k_ctx, v_ctx, sink, b_, l, n_ctx):
    nb = l // Q_BLOCK
    blk = lambda w, off: pl.BlockSpec(
        (Q_BLOCK, w), lambda b, n: (b * nb + jnp.clip(n + off, 0, nb - 1), 0))
    ctx = pl.BlockSpec((n_ctx, SWA_KV_WIDTH), lambda b, n: (b, 0))
    kw = SWA_KV_WIDTH
    return pl.pallas_call(
        functools.partial(_swa_attn_kernel, n_ctx=n_ctx),
        grid=(b_, nb),
        in_specs=[blk(SWA_WIDTH, 0), blk(kw, -1), blk(kw, 0), blk(kw, 1), blk(kw, -1), blk(kw, 0), blk(kw, 1),
                  ctx, ctx, pl.BlockSpec((SWA_HEADS, HEAD_DIM), lambda b, n: (0, 0))],
        out_specs=blk(SWA_WIDTH, 0),
        out_shape=jax.ShapeDtypeStruct((b_ * l, SWA_WIDTH), BF16),
        compiler_params=_cparams(("arbitrary", "arbitrary")),
        name="swa_attention",
    )(q, k, k, k, v, v, v, k_ctx, v_ctx, jnp.broadcast_to(sink.astype(F32)[:, None], (SWA_HEADS, HEAD_DIM)))


def _swa_mixer(yx, yc, g_q, g_k, sink, b_, l, n_ctx):
    ones = jnp.ones((n_ctx, HEAD_DIM), F32)
    zeros = jnp.zeros((n_ctx, HEAD_DIM), F32)
    qx, kx, vx = _swa_prep(yx, _rope_tables(l), g_q, g_k, b_ * l, 256)
    _, kc, vc = _swa_prep(yc, (ones, zeros, zeros), g_q, g_k, b_ * n_ctx, n_ctx)
    return _swa_attention(qx, kx, vx, kc, vc, sink, b_, l, n_ctx)


def _rmsnorm(x, g):
    xf = x.astype(F32)
    y = xf * lax.rsqrt(jnp.mean(xf * xf, axis=-1, keepdims=True) + EPS)
    return (y * g.astype(F32)).astype(x.dtype)


def _l2norm(x):
    return x * lax.rsqrt(jnp.sum(x * x, axis=-1, keepdims=True) + EPS)


def _short_conv(x, w):
    c = x.shape[-1]
    y = lax.conv_general_dilated(x, w[:, None, :].astype(x.dtype), window_strides=(1,),
                                 padding=[(CONV_K // 2, CONV_K // 2)],
                                 dimension_numbers=('NWC', 'WIO', 'NWC'), feature_group_count=c)
    return jax.nn.silu(y)


def _axial_rope(t, rows, cols):
    half = HEAD_DIM // 2
    inv = ROPE_BASE ** (-jnp.arange(0, half, 2, dtype=F32) / half)

    def rot(u, pos):
        ang = pos[:, None] * inv
        cos, sin = jnp.cos(ang)[None, :, None, :], jnp.sin(ang)[None, :, None, :]
        u1, u2 = jnp.split(u.astype(F32), 2, axis=-1)
        return jnp.concatenate([u1 * cos - u2 * sin, u2 * cos + u1 * sin], axis=-1)

    return jnp.concatenate([rot(t[..., :half], rows), rot(t[..., half:], cols)], axis=-1).astype(t.dtype)


def _gated_delta_chunked(q, k, v, g, beta, s0):
    b_, l, h, _ = q.shape
    n = l // GDN_CHUNK

    def chunks(t):
        t = t.reshape((b_, n, GDN_CHUNK) + t.shape[2:])
        return jnp.moveaxis(jnp.swapaxes(t, 2, 3), 1, 0)

    qc, kc, vc, gc, bc = map(chunks, (q, k, v, g, beta))
    gcum = jnp.cumsum(gc, axis=-1)
    tri = jnp.tril(jnp.ones((GDN_CHUNK, GDN_CHUNK), bool))
    strict = jnp.tril(jnp.ones((GDN_CHUNK, GDN_CHUNK), F32), -1)
    diff = gcum[..., :, None] - gcum[..., None, :]
    decay = jnp.where(tri, jnp.exp(jnp.where(tri, diff, 0.0)), 0.0)
    kb = kc * bc[..., None]
    a_strict = jnp.einsum('nbhid,nbhjd->nbhij', kb, kc) * decay * strict
    eye = jnp.eye(GDN_CHUNK, dtype=F32)
    t_inv = lax.linalg.triangular_solve(eye + a_strict, jnp.broadcast_to(eye, a_strict.shape),
                                        left_side=True, lower=True, unit_diagonal=True)
    u = jnp.einsum('nbhij,nbhjd->nbhid', t_inv, vc * bc[..., None])
    w = jnp.einsum('nbhij,nbhjd->nbhid', t_inv, kb * jnp.exp(gcum)[..., None])
    qk = jnp.einsum('nbhid,nbhjd->nbhij', qc, kc) * decay
    q_dec = qc * jnp.exp(gcum)[..., None]
    k_dec = kc * jnp.exp(gcum[..., -1:] - gcum)[..., None]
    chunk_decay = jnp.exp(gcum[..., -1])

    def step(s, xs):
        u_n, w_n, qk_n, q_n, k_n, a_n = xs
        v_new = u_n - jnp.einsum('bhcd,bhde->bhce', w_n, s)
        o = jnp.einsum('bhcd,bhde->bhce', q_n, s) + jnp.einsum('bhij,bhje->bhie', qk_n, v_new)
        s = s * a_n[..., None, None] + jnp.einsum('bhcd,bhce->bhde', k_n, v_new)
        return s, o

    s_fin, o = lax.scan(step, s0, (u, w, qk, q_dec, k_dec, chunk_decay))
    o = jnp.swapaxes(jnp.moveaxis(o, 0, 1), 2, 3).reshape(b_, l, h, -1)
    return o, s_fin


def _gdn_inputs(qkv, b_logit, a_logit, conv_w, a_log, dt_bias):
    b_, l, _ = qkv.shape
    qkv = _short_conv(qkv, conv_w).astype(F32).reshape(b_, l, 3, GDN_HEADS, HEAD_DIM)
    q = _l2norm(qkv[:, :, 0]) * HEAD_DIM ** -0.5
    k = _l2norm(qkv[:, :, 1])
    v = qkv[:, :, 2]
    beta = jax.nn.sigmoid(b_logit.astype(F32).reshape(b_, l, 2, GDN_HEADS))
    g = -jnp.exp(a_log.astype(F32)) * jax.nn.softplus(
        a_logit.astype(F32).reshape(b_, l, 2, GDN_HEADS) + dt_bias.astype(F32))
    return q, k, v, g, beta


def _gdn_bidir(q, k, v, g, beta, s_fwd0, s_bwd0):
    o_f, s_f = _gated_delta_chunked(q, k, v, g[:, :, 0], beta[:, :, 0], s_fwd0)
    rev = lambda t: jnp.flip(t, axis=1)
    o_b, s_b = _gated_delta_chunked(rev(q), rev(k), rev(v), rev(g[:, :, 1]), rev(beta[:, :, 1]), s_bwd0)
    return o_f + rev(o_b), s_f, s_b


def _gdn_output(o, z, g_on):
    b_, l = o.shape[:2]
    y = _rmsnorm(o, g_on).reshape(b_, l, GDN_WIDTH)
    return y * jax.nn.silu(z.astype(F32))


def _swa_inputs(q, k, v, g_q, g_k):
    b_, l, _ = q.shape
    q = _rmsnorm(q.reshape(b_, l, SWA_HEADS, HEAD_DIM), g_q)
    k = _rmsnorm(k.reshape(b_, l, SWA_KV_HEADS, HEAD_DIM), g_k)
    return q, k, v.reshape(b_, l, SWA_KV_HEADS, HEAD_DIM)


def _window_attention(q, k, v, k_ctx, v_ctx, sink):
    b_, l, h, d = q.shape
    g = h // SWA_KV_HEADS
    nb = l // Q_BLOCK
    n_ctx = k_ctx.shape[1]
    scale = d ** -0.5
    qb = q.reshape(b_, nb, Q_BLOCK, SWA_KV_HEADS, g, d)

    def band_blocks(t):
        tp = jnp.pad(t, ((0, 0), (Q_BLOCK, Q_BLOCK), (0, 0), (0, 0))).reshape(b_, nb + 2, Q_BLOCK, SWA_KV_HEADS, d)
        return jnp.concatenate([tp[:, :-2], tp[:, 1:-1], tp[:, 2:]], axis=2)

    kw, vw = band_blocks(k), band_blocks(v)
    qi = jnp.arange(Q_BLOCK)[:, None]
    kj = jnp.arange(3 * Q_BLOCK)[None, :]
    band = jnp.abs(kj - Q_BLOCK - qi) <= WINDOW
    kpos = jnp.arange(nb)[:, None] * Q_BLOCK - Q_BLOCK + jnp.arange(3 * Q_BLOCK)[None, :]
    valid = band[None] & ((kpos >= 0) & (kpos < l))[:, None, :]
    s_win = jnp.einsum('bnqkgd,bnjkd->bnkgqj', qb, kw).astype(F32) * scale
    s_win = jnp.where(valid[None, :, None, None], s_win, NEG_INF)
    s_ctx = jnp.einsum('bnqkgd,bckd->bnkgqc', qb, k_ctx).astype(F32) * scale
    s_sink = jnp.broadcast_to(sink.astype(F32).reshape(SWA_KV_HEADS, g, 1, 1), s_ctx.shape[:-1] + (1,))
    p = jax.nn.softmax(jnp.concatenate([s_ctx, s_win, s_sink], axis=-1), axis=-1).astype(v.dtype)
    o = (jnp.einsum('bnkgqc,bckd->bnqkgd', p[..., :n_ctx], v_ctx)
         + jnp.einsum('bnkgqj,bnjkd->bnqkgd', p[..., n_ctx:n_ctx + 3 * Q_BLOCK], vw))
    return o.reshape(b_, l, h * d)


def _split_main(y):
    o = np.cumsum((0, 3 * GDN_WIDTH, GDN_WIDTH, SWA_WIDTH, SWA_KV_WIDTH, SWA_KV_WIDTH))
    return tuple(y[..., int(o[n]):int(o[n + 1])] for n in range(5))


def kernel(x, c, ctx, c_ctx, w_ada, b_ada, g_norm1, g_norm2, w_in, conv_qkv, a_log, dt_bias, g_onorm, g_qnorm,
           g_knorm, sink, w_out, w_router_grp, b_router_grp, w_router_exp, b_router_exp, w_gate, w_up, w_down):
    b_, l, d = x.shape
    n_ctx = ctx.shape[1]
    t = b_ * l
    assert w_ada.shape[0] == 1 and d == D_MODEL and b_ + 1 <= MOD_ROWS
    rows = jnp.repeat(jnp.arange(l // GRID_W, dtype=F32), GRID_W)
    cols = jnp.tile(jnp.arange(GRID_W, dtype=F32), l // GRID_W)

    wi = w_in[0]
    w_main = jnp.concatenate([wi[:, IN_OFFS[0]:IN_OFFS[2]], wi[:, IN_OFFS[4]:IN_OFFS[7]]], axis=1).astype(BF16)
    w_small = jnp.pad(wi[:, IN_OFFS[2]:IN_OFFS[4]], ((0, 0), (0, N_SMALL - 4 * GDN_HEADS))).astype(BF16)
    wo = w_out[0].astype(BF16)
    w_router = jnp.pad(jnp.concatenate([w_router_grp[0], w_router_exp[0]], axis=1),
                       ((0, 0), (0, N_ROUTER - N_GROUPS - N_EXPERTS))).astype(BF16)
    b_router = jnp.pad(jnp.concatenate([b_router_grp[0], b_router_exp[0]]),
                       (0, N_ROUTER - N_GROUPS - N_EXPERTS)).reshape(1, N_ROUTER)

    c_rows = jnp.zeros((MOD_ROWS, d), F32).at[:b_].set(c).at[b_].set(c_ctx)
    mod = _modulation(c_rows, w_ada[0], b_ada[0])
    mod3 = mod.reshape(MOD_ROWS * 6, 1, d)

    tm = 512
    tpb = l // tm
    yx, sx = _in_projection(x.reshape(t, d), g_norm1[0], mod3, lambda i: i // tpb, w_main, w_small, tm)
    yc, sc = _in_projection(ctx.reshape(b_ * n_ctx, d), g_norm1[0], mod3, lambda i: b_, w_main, w_small, n_ctx)
    ya_x = _gdn_mixer(yx, sx, yc, sc, conv_qkv[0], a_log[0], dt_bias[0], g_onorm[0], b_, l, n_ctx)
    yb_x = _swa_mixer(yx, yc, g_qnorm[0], g_knorm[0], sink[0], b_, l, n_ctx)

    x1 = _out_projection(ya_x, yb_x, wo[:GDN_WIDTH], wo[GDN_WIDTH:], x.reshape(t, d), mod3, tpb, tm)

    tm2 = 256
    h2, logits = _norm2_router(x1, g_norm2[0], mod3, l // tm2, w_router, b_router, tm2)
    slot_tok, slot_w, slot_of, blk_expert, n_used = _route(logits, t)
    y = _moe_experts(h2[slot_tok], blk_expert, n_used, slot_w, w_gate[0], w_up[0], w_down[0])
    moe = y[slot_of[0::2]] + y[slot_of[1::2]]
    gate2 = mod[:b_, 5 * d:6 * d][:, None, :]
    return x1.reshape(b_, l, d) + gate2 * moe.reshape(b_, l, d)
```

```python
import functools
import math

import jax
import jax.numpy as jnp
import numpy as np
from jax import lax
from jax.experimental import pallas as pl
from jax.experimental.pallas import tpu as pltpu

F32 = jnp.float32
BF16 = jnp.bfloat16

D_MODEL = 4096
CTX_LEN = 256
GRID_W = 64
HEAD_DIM = 128
GDN_HEADS = 16
GDN_WIDTH = GDN_HEADS * HEAD_DIM
GDN_CHUNK = 64
CONV_K = 5
SWA_HEADS = 16
SWA_KV_HEADS = 4
SWA_WIDTH = SWA_HEADS * HEAD_DIM
SWA_KV_WIDTH = SWA_KV_HEADS * HEAD_DIM
WINDOW = 128
Q_BLOCK = 128
ROPE_BASE = 10000.0
N_GROUPS = 8
EXPERTS_PER_GROUP = 8
N_EXPERTS = N_GROUPS * EXPERTS_PER_GROUP
TOP_K = 2
D_EXPERT = D_MODEL // 8
EPS = 1e-6
NEG_INF = -1e30

IN_SIZES = (3 * GDN_WIDTH, GDN_WIDTH, 2 * GDN_HEADS, 2 * GDN_HEADS, SWA_WIDTH, SWA_KV_WIDTH, SWA_KV_WIDTH)
IN_OFFS = tuple(int(v) for v in np.cumsum((0,) + IN_SIZES))
N_MAIN = 3 * GDN_WIDTH + GDN_WIDTH + SWA_WIDTH + 2 * SWA_KV_WIDTH
N_SMALL = 128
N_ROUTER = 128
MOD_ROWS = 8

MOE_BM = 256
VMEM_LIMIT = 56 * 1024 * 1024


def _cparams(sem):
    return pltpu.CompilerParams(dimension_semantics=sem, vmem_limit_bytes=VMEM_LIMIT)


def _mod_kernel(c_ref, w_ref, b_ref, o_ref):
    c = c_ref[...]
    a = (c * jax.nn.sigmoid(c)).astype(BF16)
    o_ref[...] = jnp.dot(a, w_ref[...].astype(BF16), preferred_element_type=F32) + b_ref[...]


def _modulation(c_rows, w_ada, b_ada):
    d, n = w_ada.shape
    tn = 512
    return pl.pallas_call(
        _mod_kernel,
        grid=(n // tn,),
        in_specs=[pl.BlockSpec((MOD_ROWS, d), lambda j: (0, 0)),
                  pl.BlockSpec((d, tn), lambda j: (0, j)),
                  pl.BlockSpec((1, tn), lambda j: (0, j))],
        out_specs=pl.BlockSpec((MOD_ROWS, tn), lambda j: (0, j)),
        out_shape=jax.ShapeDtypeStruct((MOD_ROWS, n), F32),
        compiler_params=_cparams(("arbitrary",)),
        name="modulation",
    )(c_rows, w_ada, b_ada.reshape(1, n))


NORM_ROWS = 64


def _norm_mod_rows(x_ref, g_ref, sh_ref, sc_ref, h_ref, tm):
    g = g_ref[...]
    sc = 1.0 + sc_ref[0]
    sh = sh_ref[0]

    def body(r, carry):
        rows = pl.ds(pl.multiple_of(r * NORM_ROWS, NORM_ROWS), NORM_ROWS)
        xf = x_ref[rows, :]
        ms = jnp.mean(xf * xf, axis=-1, keepdims=True)
        y = xf * lax.rsqrt(ms + EPS) * g
        h_ref[rows, :] = (y * sc + sh).astype(h_ref.dtype)
        return carry

    lax.fori_loop(0, tm // NORM_ROWS, body, 0)


def _inproj_kernel(x_ref, g_ref, sh_ref, sc_ref, w_ref, ws_ref, o_ref, os_ref, h_ref, *, tm):
    @pl.when(pl.program_id(1) == 0)
    def _():
        _norm_mod_rows(x_ref, g_ref, sh_ref, sc_ref, h_ref, tm)
        os_ref[...] = jnp.dot(h_ref[...], ws_ref[...], preferred_element_type=F32)

    o_ref[...] = jnp.dot(h_ref[...], w_ref[...], preferred_element_type=F32)


def _in_projection(x2d, g_norm, mod3, mod_row_of_tile, w_main, w_small, tm):
    t, d = x2d.shape
    tn = 1024
    return pl.pallas_call(
        functools.partial(_inproj_kernel, tm=tm),
        grid=(t // tm, N_MAIN // tn),
        in_specs=[pl.BlockSpec((tm, d), lambda i, j: (i, 0)),
                  pl.BlockSpec((1, d), lambda i, j: (0, 0)),
                  pl.BlockSpec((1, 1, d), lambda i, j: (mod_row_of_tile(i) * 6 + 0, 0, 0)),
                  pl.BlockSpec((1, 1, d), lambda i, j: (mod_row_of_tile(i) * 6 + 1, 0, 0)),
                  pl.BlockSpec((d, tn), lambda i, j: (0, j)),
                  pl.BlockSpec((d, N_SMALL), lambda i, j: (0, 0))],
        out_specs=[pl.BlockSpec((tm, tn), lambda i, j: (i, j)),
                   pl.BlockSpec((tm, N_SMALL), lambda i, j: (i, 0))],
        out_shape=[jax.ShapeDtypeStruct((t, N_MAIN), F32),
                   jax.ShapeDtypeStruct((t, N_SMALL), F32)],
        scratch_shapes=[pltpu.VMEM((tm, d), BF16)],
        compiler_params=_cparams(("arbitrary", "arbitrary")),
        name="in_projection",
    )(x2d, g_norm.reshape(1, d), mod3, mod3, w_main, w_small)


def _outproj_kernel(ya_ref, yb_ref, wa_ref, wb_ref, x_ref, gate_ref, o_ref):
    acc = jnp.dot(ya_ref[...], wa_ref[...], preferred_element_type=F32)
    acc = acc + jnp.dot(yb_ref[...], wb_ref[...], preferred_element_type=F32)
    o_ref[...] = x_ref[...] + gate_ref[0] * acc


def _out_projection(ya, yb, wa, wb, x2d, mod3, tiles_per_batch, tm):
    t, d = x2d.shape
    tn = 1024
    nj = d // tn
    ka, kb = ya.shape[1], yb.shape[1]
    return pl.pallas_call(
        _outproj_kernel,
        grid=(t // tm, nj),
        in_specs=[pl.BlockSpec((tm, ka), lambda i, j: (i, 0)),
                  pl.BlockSpec((tm, kb), lambda i, j: (i, 0)),
                  pl.BlockSpec((ka, tn), lambda i, j: (0, j)),
                  pl.BlockSpec((kb, tn), lambda i, j: (0, j)),
                  pl.BlockSpec((tm, tn), lambda i, j: (i, j)),
                  pl.BlockSpec((1, 1, tn), lambda i, j: (((i // tiles_per_batch) * 6 + 2) * nj + j, 0, 0))],
        out_specs=pl.BlockSpec((tm, tn), lambda i, j: (i, j)),
        out_shape=jax.ShapeDtypeStruct((t, d), F32),
        compiler_params=_cparams(("arbitrary", "arbitrary")),
        name="out_projection",
    )(ya, yb, wa, wb, x2d, mod3.reshape(-1, 1, tn))


def _route_rows(lg):
    col = lax.broadcasted_iota(jnp.int32, lg.shape, 1)
    first = lambda hit: jnp.min(jnp.where(hit, col, N_ROUTER), axis=-1, keepdims=True)
    gm = col < N_GROUPS
    mg = jnp.max(jnp.where(gm, lg, NEG_INF), axis=-1, keepdims=True)
    grp = first(gm & (lg == mg))
    p_grp = 1.0 / jnp.sum(jnp.where(gm, jnp.exp(lg - mg), 0.0), axis=-1, keepdims=True)
    lo = N_GROUPS + grp * EXPERTS_PER_GROUP
    em = (col >= lo) & (col < lo + EXPERTS_PER_GROUP)
    m1 = jnp.max(jnp.where(em, lg, NEG_INF), axis=-1, keepdims=True)
    i1 = first(em & (lg == m1))
    em2 = em & (col != i1)
    m2 = jnp.max(jnp.where(em2, lg, NEG_INF), axis=-1, keepdims=True)
    i2 = first(em2 & (lg == m2))
    e2 = jnp.exp(m2 - m1)
    g1 = p_grp / (1.0 + e2)
    ids = jnp.where(col == 0, i1 - N_GROUPS, jnp.where(col == 1, i2 - N_GROUPS, 0))
    gates = jnp.where(col == 0, g1, jnp.where(col == 1, g1 * e2, 0.0))
    return ids, gates


def _norm2_kernel(x_ref, g_ref, sh_ref, sc_ref, wr_ref, br_ref, h_ref, id_ref, gt_ref, *, tm):
    _norm_mod_rows(x_ref, g_ref, sh_ref, sc_ref, h_ref, tm)
    lg = jnp.dot(h_ref[...], wr_ref[...], preferred_element_type=F32) + br_ref[...]
    id_ref[...], gt_ref[...] = _route_rows(lg)


def _norm2_router(x2d, g_norm, mod3, tiles_per_batch, w_router, b_router, tm):
    t, d = x2d.shape
    return pl.pallas_call(
        functools.partial(_norm2_kernel, tm=tm),
        grid=(t // tm,),
        in_specs=[pl.BlockSpec((tm, d), lambda i: (i, 0)),
                  pl.BlockSpec((1, d), lambda i: (0, 0)),
                  pl.BlockSpec((1, 1, d), lambda i: ((i // tiles_per_batch) * 6 + 3, 0, 0)),
                  pl.BlockSpec((1, 1, d), lambda i: ((i // tiles_per_batch) * 6 + 4, 0, 0)),
                  pl.BlockSpec((d, N_ROUTER), lambda i: (0, 0)),
                  pl.BlockSpec((1, N_ROUTER), lambda i: (0, 0))],
        out_specs=[pl.BlockSpec((tm, d), lambda i: (i, 0)),
                   pl.BlockSpec((tm, N_ROUTER), lambda i: (i, 0)),
                   pl.BlockSpec((tm, N_ROUTER), lambda i: (i, 0))],
        out_shape=[jax.ShapeDtypeStruct((t, d), BF16),
                   jax.ShapeDtypeStruct((t, N_ROUTER), jnp.int32),
                   jax.ShapeDtypeStruct((t, N_ROUTER), F32)],
        compiler_params=_cparams(("arbitrary",)),
        name="norm2_router",
    )(x2d, g_norm.reshape(1, d), mod3, mod3, w_router, b_router)


CAST_ROWS = 128


def _cast_rows(src_ref, dst_ref):
    n = src_ref.shape[1]

    def body(r, carry):
        rows = pl.ds(pl.multiple_of(r * CAST_ROWS, CAST_ROWS), CAST_ROWS)
        dst_ref[rows, :] = src_ref[0, rows, :].astype(dst_ref.dtype)
        return carry

    lax.fori_loop(0, n // CAST_ROWS, body, 0)


def _expert_changed(be_ref, i):
    return (i == 0) | (be_ref[i] != be_ref[jnp.maximum(i - 1, 0)])


def _moe_up_kernel(be_ref, nu_ref, x_ref, wg_ref, wu_ref, o_ref, wgb_ref, wub_ref):
    i = pl.program_id(0)

    @pl.when(_expert_changed(be_ref, i))
    def _():
        _cast_rows(wg_ref, wgb_ref)
        _cast_rows(wu_ref, wub_ref)

    @pl.when(i < nu_ref[0])
    def _():
        xb = x_ref[...]
        g = jnp.dot(xb, wgb_ref[...], preferred_element_type=F32)
        u = jnp.dot(xb, wub_ref[...], preferred_element_type=F32)
        o_ref[...] = (g * jax.nn.sigmoid(g) * u).astype(o_ref.dtype)

    @pl.when(i >= nu_ref[0])
    def _():
        o_ref[...] = jnp.zeros_like(o_ref)


def _moe_down_kernel(be_ref, nu_ref, h_ref, wd_ref, sw_ref, o_ref, wdb_ref):
    i = pl.program_id(0)

    @pl.when(_expert_changed(be_ref, i))
    def _():
        _cast_rows(wd_ref, wdb_ref)

    @pl.when(i < nu_ref[0])
    def _():
        y = jnp.dot(h_ref[...], wdb_ref[...], preferred_element_type=F32)
        o_ref[...] = y * sw_ref[...]

    @pl.when(i >= nu_ref[0])
    def _():
        o_ref[...] = jnp.zeros_like(o_ref)


def _moe_experts(xg, blk_expert, n_used, slot_w, w_gate, w_up, w_down):
    p, d = xg.shape
    n_blocks = p // MOE_BM
    de = w_gate.shape[-1]
    hmid = pl.pallas_call(
        _moe_up_kernel,
        grid_spec=pltpu.PrefetchScalarGridSpec(
            num_scalar_prefetch=2,
            grid=(n_blocks,),
            in_specs=[pl.BlockSpec((MOE_BM, d), lambda i, be, nu: (i, 0)),
                      pl.BlockSpec((1, d, de), lambda i, be, nu: (be[i], 0, 0)),
                      pl.BlockSpec((1, d, de), lambda i, be, nu: (be[i], 0, 0))],
            out_specs=pl.BlockSpec((MOE_BM, de), lambda i, be, nu: (i, 0)),
            scratch_shapes=[pltpu.VMEM((d, de), BF16), pltpu.VMEM((d, de), BF16)]),
        out_shape=jax.ShapeDtypeStruct((p, de), BF16),
        compiler_params=_cparams(("arbitrary",)),
        name="moe_gate_up",
    )(blk_expert, n_used, xg, w_gate, w_up)
    return pl.pallas_call(
        _moe_down_kernel,
        grid_spec=pltpu.PrefetchScalarGridSpec(
            num_scalar_prefetch=2,
            grid=(n_blocks,),
            in_specs=[pl.BlockSpec((MOE_BM, de), lambda i, be, nu: (i, 0)),
                      pl.BlockSpec((1, de, d), lambda i, be, nu: (be[i], 0, 0)),
                      pl.BlockSpec((MOE_BM, 1), lambda i, be, nu: (i, 0))],
            out_specs=pl.BlockSpec((MOE_BM, d), lambda i, be, nu: (i, 0)),
            scratch_shapes=[pltpu.VMEM((de, d), BF16)]),
        out_shape=jax.ShapeDtypeStruct((p, d), F32),
        compiler_params=_cparams(("arbitrary",)),
        name="moe_down",
    )(blk_expert, n_used, hmid, w_down, slot_w.reshape(p, 1))


def _slots(eid, gate, t):
    a = t * TOP_K
    e_flat = eid.reshape(a)
    tok_flat = jnp.repeat(jnp.arange(t, dtype=jnp.int32), TOP_K)
    order = jnp.argsort(e_flat)
    e_sorted = e_flat[order]
    counts = jax.ops.segment_sum(jnp.ones((a,), jnp.int32), e_flat, num_segments=N_EXPERTS)
    padded = (counts + MOE_BM - 1) // MOE_BM * MOE_BM
    start = jnp.cumsum(counts) - counts
    pend = jnp.cumsum(padded)
    pstart = pend - padded
    dest = pstart[e_sorted] + jnp.arange(a, dtype=jnp.int32) - start[e_sorted]
    n_blocks = (a + MOE_BM - 1) // MOE_BM + N_EXPERTS
    p = n_blocks * MOE_BM
    slot_tok = jnp.zeros((p,), jnp.int32).at[dest].set(tok_flat[order])
    slot_w = jnp.zeros((p,), F32).at[dest].set(gate.reshape(a)[order])
    slot_of = jnp.zeros((a,), jnp.int32).at[order].set(dest)
    blk_expert = jnp.minimum(jnp.searchsorted(pend, jnp.arange(n_blocks, dtype=jnp.int32) * MOE_BM,
                                              side='right'), N_EXPERTS - 1).astype(jnp.int32)
    n_used = (pend[-1] // MOE_BM).astype(jnp.int32).reshape(1)
    return slot_tok, slot_w, slot_of, blk_expert, n_used


GDN_TB = 256
HALO = 8
N_QKV = 3 * GDN_WIDTH


def _softplus(v):
    return jnp.maximum(v, 0.0) + jnp.log(1.0 + jnp.exp(-jnp.abs(v)))


def _split3_bf16(v):
    hi = v.astype(BF16)
    r1 = v - hi.astype(F32)
    mid = r1.astype(BF16)
    lo = (r1 - mid.astype(F32)).astype(BF16)
    return hi, mid, lo


def _gdn_prep_kernel(cur_ref, prev_ref, next_ref, sm_ref, cw_ref, ea_ref, dt_ref, o_ref, g_ref, ext_ref, *, tb):
    i = pl.program_id(1)
    nblk = pl.num_programs(1)
    ext_ref[pl.ds(0, HALO), :] = jnp.where(i > 0, prev_ref[...], 0.0)
    ext_ref[pl.ds(HALO, tb), :] = cur_ref[...]
    ext_ref[pl.ds(HALO + tb, HALO), :] = jnp.where(i < nblk - 1, next_ref[...], 0.0)

    def conv_cols(kind):
        def body(hh, carry):
            cols = pl.ds(pl.multiple_of((kind * GDN_HEADS + hh) * HEAD_DIM, HEAD_DIM), HEAD_DIM)
            cw = cw_ref[:, cols]
            for r0 in range(0, tb, 64):
                acc = None
                for s in range(CONV_K):
                    term = ext_ref[pl.ds(HALO - CONV_K // 2 + s + r0, 64), cols] * cw[s:s + 1, :]
                    acc = term if acc is None else acc + term
                y = acc * jax.nn.sigmoid(acc)
                if kind < 2:
                    y = y * lax.rsqrt(jnp.sum(y * y, axis=-1, keepdims=True) + EPS)
                if kind == 0:
                    y = y * HEAD_DIM ** -0.5
                o_ref[pl.ds(r0, 64), cols] = y.astype(o_ref.dtype)
            return carry

        lax.fori_loop(0, GDN_HEADS, body, 0)

    conv_cols(0)
    conv_cols(1)
    conv_cols(2)

    s = sm_ref[...]
    beta = jax.nn.sigmoid(s)
    g = -ea_ref[...] * _softplus(s + dt_ref[...])
    r = lax.broadcasted_iota(jnp.int32, (tb, tb), 0)
    c = lax.broadcasted_iota(jnp.int32, (tb, tb), 1)
    same = (r // GDN_CHUNK) == (c // GDN_CHUNK)
    lower = (same & (c <= r)).astype(BF16)
    upper = (same & (c >= r)).astype(BF16)
    parts = _split3_bf16(g)
    cf = sum(jnp.dot(lower, pt, preferred_element_type=F32) for pt in parts)
    cb = sum(jnp.dot(upper, pt, preferred_element_type=F32) for pt in parts)
    col = lax.broadcasted_iota(jnp.int32, s.shape, 1)
    nh = GDN_HEADS
    g_ref[...] = jnp.where(col < 2 * nh, beta, jnp.where(col < 3 * nh, cf, jnp.where(col < 4 * nh, cb, 0.0)))


def _gdn_prep(y_main, small, conv_w8, ea_row, dt_row, b_, l):
    tb = min(GDN_TB, l)
    nblk = l // tb
    hb = tb // HALO
    last = b_ * l // HALO - 1
    return pl.pallas_call(
        functools.partial(_gdn_prep_kernel, tb=tb),
        grid=(b_, nblk),
        in_specs=[pl.BlockSpec((tb, N_QKV), lambda b, i: (b * nblk + i, 0)),
                  pl.BlockSpec((HALO, N_QKV), lambda b, i: (jnp.maximum((b * nblk + i) * hb - 1, 0), 0)),
                  pl.BlockSpec((HALO, N_QKV), lambda b, i: (jnp.minimum((b * nblk + i + 1) * hb, last), 0)),
                  pl.BlockSpec((tb, N_SMALL), lambda b, i: (b * nblk + i, 0)),
                  pl.BlockSpec((HALO, N_QKV), lambda b, i: (0, 0)),
                  pl.BlockSpec((1, N_SMALL), lambda b, i: (0, 0)),
                  pl.BlockSpec((1, N_SMALL), lambda b, i: (0, 0))],
        out_specs=[pl.BlockSpec((tb, N_QKV), lambda b, i: (b * nblk + i, 0)),
                   pl.BlockSpec((tb, N_SMALL), lambda b, i: (b * nblk + i, 0))],
        out_shape=[jax.ShapeDtypeStruct((b_ * l, N_QKV), BF16),
                   jax.ShapeDtypeStruct((b_ * l, N_SMALL), F32)],
        scratch_shapes=[pltpu.VMEM((tb + 2 * HALO, N_QKV), F32)],
        compiler_params=_cparams(("arbitrary", "arbitrary")),
        name="gdn_prep",
    )(y_main, y_main, y_main, small, conv_w8, ea_row, dt_row)


N_PAIRS = GDN_HEADS // 2
PK = 2 * GDN_CHUNK
INV_LEVELS = (2, 4, 8, 16, 32, 64)


def _gdn_masks(reverse):
    i = np.arange(PK)[:, None]
    j = np.arange(PK)[None, :]
    same = (i // GDN_CHUNK) == (j // GDN_CHUNK)
    strict = same & ((j > i) if reverse else (j < i))
    out = []
    for bs in INV_LEVELS:
        out.append(strict & (i // bs == j // bs) & (i // (bs // 2) != j // (bs // 2)))
    incl = same & ((j >= i) if reverse else (j <= i))
    out.append(incl)
    m = np.stack(out).astype(np.float32)
    neg = ((incl.astype(np.float32) - 1.0) * 1e30)[None]
    return jnp.asarray(np.concatenate([m, neg], axis=0))


def _gdn_scan_kernel(*refs, final):
    if final:
        (q_ref, k_ref, v_ref, cp_ref, rp_ref, ap_ref, s0_ref, mk_ref, op_ref, z_ref, gon_ref,
         o_ref, sfin_ref, s_scr) = refs
    else:
        q_ref, k_ref, v_ref, cp_ref, rp_ref, ap_ref, s0_ref, mk_ref, o_ref, sfin_ref, s_scr = refs
    c = pl.program_id(1)

    @pl.when(c == 0)
    def _():
        s_scr[...] = s0_ref[0]

    nl = len(INV_LEVELS)
    incl = mk_ref[nl]
    negm = mk_ref[nl + 1]
    ri = lax.broadcasted_iota(jnp.int32, (PK, 1), 0)
    top = (ri < GDN_CHUNK).astype(F32)
    bot = 1.0 - top
    rr = lax.broadcasted_iota(jnp.int32, (PK, PK), 0)
    cc = lax.broadcasted_iota(jnp.int32, (PK, PK), 1)
    eye = (rr == cc).astype(F32)
    cp = cp_ref[0, 0]
    rp = rp_ref[0, 0]
    ap = ap_ref[0, 0]
    dot = functools.partial(jnp.dot, preferred_element_type=F32)

    pairs = range(N_PAIRS)
    nt = (((1,), (1,)), ((), ()))
    tn = (((0,), (0,)), ((), ()))
    cols = [(pl.ds(2 * p * HEAD_DIM, HEAD_DIM), pl.ds((2 * p + 1) * HEAD_DIM, HEAD_DIM)) for p in pairs]
    pack = lambda ref, p: jnp.concatenate([ref[:, cols[p][0]], ref[:, cols[p][1]]], axis=0)
    gcol = [cp[:, p:p + 1] for p in pairs]
    bcol = [cp[:, N_PAIRS + p:N_PAIRS + p + 1] for p in pairs]
    glcol = [cp[:, 2 * N_PAIRS + p:2 * N_PAIRS + p + 1] for p in pairs]
    kp = [pack(k_ref, p) for p in pairs]
    qp = [pack(q_ref, p) for p in pairs]
    kk = [lax.dot_general(kp[p], kp[p], nt, preferred_element_type=F32) for p in pairs]
    qk = [lax.dot_general(qp[p], kp[p], nt, preferred_element_type=F32) for p in pairs]
    dec = [jnp.exp((gcol[p] - rp[p:p + 1, :]) * incl + negm) for p in pairs]
    a = [kk[p] * dec[p] * bcol[p] for p in pairs]
    qkm = [(qk[p] * dec[p]).astype(BF16) for p in pairs]

    x = [eye - a[p] * mk_ref[0] for p in pairs]
    for lv in range(1, nl):
        xb = [x[p].astype(BF16) for p in pairs]
        po = [dot(xb[p], (a[p] * mk_ref[lv]).astype(BF16)) for p in pairs]
        x = [x[p] - dot(po[p].astype(BF16), xb[p]) for p in pairs]
    tb = [x[p].astype(BF16) for p in pairs]

    egc = [jnp.exp(gcol[p]) for p in pairs]
    kf = [kp[p].astype(F32) for p in pairs]
    u = [dot(tb[p], (pack(v_ref, p).astype(F32) * bcol[p]).astype(BF16)) for p in pairs]
    w = [dot(tb[p], (kf[p] * (bcol[p] * egc[p])).astype(BF16)) for p in pairs]
    qd = [qp[p].astype(F32) * egc[p] for p in pairs]
    kd = [kf[p] * jnp.exp(glcol[p] - gcol[p]) for p in pairs]

    s = [s_scr[p] for p in pairs]
    lhs = [jnp.concatenate([jnp.concatenate([w[p] * top, w[p] * bot], axis=1),
                            jnp.concatenate([qd[p] * top, qd[p] * bot], axis=1)], axis=0).astype(BF16)
           for p in pairs]
    ws = [dot(lhs[p], s[p].astype(BF16)) for p in pairs]
    vnb = [(u[p] - ws[p][:PK]).astype(BF16) for p in pairs]
    o = [ws[p][PK:] + dot(qkm[p], vnb[p]) for p in pairs]
    kbd = [jnp.concatenate([kd[p] * top, kd[p] * bot], axis=1).astype(BF16) for p in pairs]
    kv = [lax.dot_general(kbd[p], vnb[p], tn, preferred_element_type=F32) for p in pairs]
    for p in pairs:
        s_scr[p] = jnp.exp(ap[:, p:p + 1]) * s[p] + kv[p]

    for p in pairs:
        op = o[p]
        if final:
            op = op + pack(op_ref, p)
            zz = pack(z_ref, p)
            op = op * lax.rsqrt(jnp.mean(op * op, axis=-1, keepdims=True) + EPS) * gon_ref[...]
            op = op * (zz * jax.nn.sigmoid(zz))
        o_ref[:, cols[p][0]] = op[:GDN_CHUNK].astype(o_ref.dtype)
        o_ref[:, cols[p][1]] = op[GDN_CHUNK:].astype(o_ref.dtype)

    @pl.when(c == pl.num_programs(1) - 1)
    def _():
        sfin_ref[0] = s_scr[...]


def _gdn_packs(gates, b_, l, reverse):
    nc = l // GDN_CHUNK
    nh = GDN_HEADS
    g4 = gates.reshape(b_, nc, GDN_CHUNK, N_SMALL)
    d = 1 if reverse else 0
    beta = g4[..., d * nh:(d + 1) * nh]
    gc = g4[..., (2 + d) * nh:(3 + d) * nh]
    gl = jnp.broadcast_to(gc[:, :, 0:1] if reverse else gc[:, :, GDN_CHUNK - 1:GDN_CHUNK], gc.shape)

    def rowpack(t):
        return jnp.transpose(t.reshape(b_, nc, GDN_CHUNK, N_PAIRS, 2), (0, 1, 3, 4, 2)).reshape(b_, nc, N_PAIRS, PK)

    rp = rowpack(gc)
    cp = jnp.concatenate([jnp.swapaxes(rowpack(t), 2, 3) for t in (gc, beta, gl)]
                         + [jnp.zeros((b_, nc, PK, N_PAIRS), F32)], axis=-1)
    glh = gl[:, :, 0].reshape(b_, nc, N_PAIRS, 2)
    ap = jnp.swapaxes(jnp.repeat(glh, HEAD_DIM, axis=-1), 2, 3)
    return cp, rp, ap


def _gdn_scan(qkv, gates, s0, b_, l, reverse, final_args=None):
    nc = l // GDN_CHUNK
    cp, rp, ap = _gdn_packs(gates, b_, l, reverse)
    masks = _gdn_masks(reverse)
    final = final_args is not None
    ci = (lambda c: nc - 1 - c) if reverse else (lambda c: c)
    tok = lambda col: pl.BlockSpec((GDN_CHUNK, GDN_WIDTH), lambda b, c: (b * nc + ci(c), col))
    per_chunk = lambda shp: pl.BlockSpec((1, 1) + shp, lambda b, c: (b, ci(c), 0, 0))
    state = pl.BlockSpec((1, N_PAIRS, 2 * HEAD_DIM, HEAD_DIM), lambda b, c: (b, 0, 0, 0))
    in_specs = [tok(0), tok(1), tok(2), per_chunk((PK, 4 * N_PAIRS)), per_chunk((N_PAIRS, PK)),
                per_chunk((2 * HEAD_DIM, N_PAIRS)), state,
                pl.BlockSpec(masks.shape, lambda b, c: (0, 0, 0))]
    args = [qkv, qkv, qkv, cp, rp, ap, s0, masks]
    if final:
        o_prev, y_main, g_on = final_args
        in_specs += [tok(0), tok(N_QKV // GDN_WIDTH), pl.BlockSpec((1, HEAD_DIM), lambda b, c: (0, 0))]
        args += [o_prev, y_main, g_on.reshape(1, HEAD_DIM)]
    return pl.pallas_call(
        functools.partial(_gdn_scan_kernel, final=final),
        grid=(b_, nc),
        in_specs=in_specs,
        out_specs=[tok(0), state],
        out_shape=[jax.ShapeDtypeStruct((b_ * l, GDN_WIDTH), BF16 if final else F32),
                   jax.ShapeDtypeStruct(s0.shape, F32)],
        scratch_shapes=[pltpu.VMEM((N_PAIRS, 2 * HEAD_DIM, HEAD_DIM), F32)],
        compiler_params=_cparams(("arbitrary", "arbitrary")),
        name="gdn_scan_bwd" if reverse else "gdn_scan_fwd",
    )(*args)


def _gdn_mixer(yx, sx, yc, sc, conv_w, a_log, dt_bias, g_on, b_, l, n_ctx):
    nh = GDN_HEADS
    conv_w8 = jnp.pad(conv_w, ((0, HALO - CONV_K), (0, 0)))
    ea_row = jnp.zeros((1, N_SMALL), F32).at[0, 2 * nh:4 * nh].set(jnp.exp(a_log.reshape(-1)))
    dt_row = jnp.zeros((1, N_SMALL), F32).at[0, 2 * nh:4 * nh].set(dt_bias.reshape(-1))
    qkv_c, gates_c = _gdn_prep(yc, sc, conv_w8, ea_row, dt_row, b_, n_ctx)
    qkv_x, gates_x = _gdn_prep(yx, sx, conv_w8, ea_row, dt_row, b_, l)
    zero = jnp.zeros((b_, N_PAIRS, 2 * HEAD_DIM, HEAD_DIM), F32)
    _, s_f = _gdn_scan(qkv_c, gates_c, zero, b_, n_ctx, False)
    _, s_b = _gdn_scan(qkv_c, gates_c, zero, b_, n_ctx, True)
    o_f, _ = _gdn_scan(qkv_x, gates_x, s_f, b_, l, False)
    ya, _ = _gdn_scan(qkv_x, gates_x, s_b, b_, l, True, final_args=(o_f, yx, g_on))
    return ya


SWA_GROUP = SWA_HEADS // SWA_KV_HEADS
ROT = HEAD_DIM // 4


def _rope_tables(l):
    half = HEAD_DIM // 2
    inv = ROPE_BASE ** (-jnp.arange(0, half, 2, dtype=F32) / half)
    pos = jnp.arange(l, dtype=jnp.int32)
    ang_r = (pos // GRID_W).astype(F32)[:, None] * inv
    ang_c = (pos % GRID_W).astype(F32)[:, None] * inv
    zero = jnp.zeros_like(ang_r)
    cos = jnp.concatenate([jnp.cos(ang_r), jnp.cos(ang_r), jnp.cos(ang_c), jnp.cos(ang_c)], axis=1)
    sin_up = jnp.concatenate([-jnp.sin(ang_r), zero, -jnp.sin(ang_c), zero], axis=1)
    sin_dn = jnp.concatenate([zero, jnp.sin(ang_r), zero, jnp.sin(ang_c)], axis=1)
    return cos, sin_up, sin_dn


def _swa_prep_kernel(q_ref, k_ref, v_ref, cos_ref, su_ref, sd_ref, gq_ref, gk_ref, qo_ref, ko_ref, vo_ref):
    cos, su, sd = cos_ref[...], su_ref[...], sd_ref[...]

    def norm_rope(t, g, scale):
        y = t * lax.rsqrt(jnp.mean(t * t, axis=-1, keepdims=True) + EPS) * g
        y = y * cos + pltpu.roll(y, HEAD_DIM - ROT, 1) * su + pltpu.roll(y, ROT, 1) * sd
        return y * scale if scale != 1.0 else y

    for h in range(SWA_HEADS):
        c = pl.ds(h * HEAD_DIM, HEAD_DIM)
        qo_ref[:, c] = norm_rope(q_ref[:, c], gq_ref[...], HEAD_DIM ** -0.5).astype(qo_ref.dtype)
    for h in range(SWA_KV_HEADS):
        c = pl.ds(h * HEAD_DIM, HEAD_DIM)
        ko_ref[:, c] = norm_rope(k_ref[:, c], gk_ref[...], 1.0).astype(ko_ref.dtype)
    vo_ref[...] = v_ref[...].astype(vo_ref.dtype)


def _swa_prep(y_main, tables, g_q, g_k, rows, tm):
    q_blk = (N_QKV + GDN_WIDTH) // SWA_WIDTH
    k_blk = (N_QKV + GDN_WIDTH + SWA_WIDTH) // SWA_KV_WIDTH
    tpb = tables[0].shape[0] // tm
    tab = pl.BlockSpec((tm, HEAD_DIM), lambda i: (i % tpb, 0))
    vec = pl.BlockSpec((1, HEAD_DIM), lambda i: (0, 0))
    return pl.pallas_call(
        _swa_prep_kernel,
        grid=(rows // tm,),
        in_specs=[pl.BlockSpec((tm, SWA_WIDTH), lambda i: (i, q_blk)),
                  pl.BlockSpec((tm, SWA_KV_WIDTH), lambda i: (i, k_blk)),
                  pl.BlockSpec((tm, SWA_KV_WIDTH), lambda i: (i, k_blk + 1)),
                  tab, tab, tab, vec, vec],
        out_specs=[pl.BlockSpec((tm, SWA_WIDTH), lambda i: (i, 0)),
                   pl.BlockSpec((tm, SWA_KV_WIDTH), lambda i: (i, 0)),
                   pl.BlockSpec((tm, SWA_KV_WIDTH), lambda i: (i, 0))],
        out_shape=[jax.ShapeDtypeStruct((rows, SWA_WIDTH), BF16),
                   jax.ShapeDtypeStruct((rows, SWA_KV_WIDTH), BF16),
                   jax.ShapeDtypeStruct((rows, SWA_KV_WIDTH), BF16)],
        compiler_params=_cparams(("arbitrary",)),
        name="swa_prep",
    )(y_main, y_main, y_main, *tables, g_q.reshape(1, HEAD_DIM), g_k.reshape(1, HEAD_DIM))


def _swa_attn_kernel(q_ref, kp_ref, kc_ref, kn_ref, vp_ref, vc_ref, vn_ref, kx_ref, vx_ref, sink_ref, o_ref, *, n_ctx):
    n = pl.program_id(1)
    nb = pl.num_programs(1)
    rows = SWA_GROUP * Q_BLOCK
    nk = n_ctx + 3 * Q_BLOCK
    qi = lax.broadcasted_iota(jnp.int32, (rows, nk), 0) & (Q_BLOCK - 1)
    kj = lax.broadcasted_iota(jnp.int32, (rows, nk), 1) - n_ctx
    lo = jnp.where(n == 0, Q_BLOCK, 0)
    hi = jnp.where(n == nb - 1, 2 * Q_BLOCK, 3 * Q_BLOCK)
    valid = (kj < 0) | ((kj >= qi) & (kj <= qi + 2 * WINDOW) & (kj >= lo) & (kj < hi))
    hsel = lax.broadcasted_iota(jnp.int32, (rows, 1), 0) // Q_BLOCK
    nt = (((1,), (1,)), ((), ()))
    for j in range(SWA_KV_HEADS):
        c = pl.ds(j * HEAD_DIM, HEAD_DIM)
        heads = [j * SWA_GROUP + g for g in range(SWA_GROUP)]
        q = jnp.concatenate([q_ref[:, pl.ds(h * HEAD_DIM, HEAD_DIM)] for h in heads], axis=0)
        k = jnp.concatenate([kx_ref[:, c], kp_ref[:, c], kc_ref[:, c], kn_ref[:, c]], axis=0)
        v = jnp.concatenate([vx_ref[:, c], vp_ref[:, c], vc_ref[:, c], vn_ref[:, c]], axis=0)
        sink = jnp.zeros((rows, 1), F32)
        for g, h in enumerate(heads):
            sink = jnp.where(hsel == g, sink_ref[h:h + 1, 0:1], sink)
        s = jnp.where(valid, lax.dot_general(q, k, nt, preferred_element_type=F32), NEG_INF)
        m = jnp.maximum(jnp.max(s, axis=-1, keepdims=True), sink)
        p = jnp.exp(s - m)
        den = jnp.sum(p, axis=-1, keepdims=True) + jnp.exp(sink - m)
        o = jnp.dot(p.astype(BF16), v, preferred_element_type=F32) / den
        for g, h in enumerate(heads):
            o_ref[:, pl.ds(h * HEAD_DIM, HEAD_DIM)] = o[g * Q_BLOCK:(g + 1) * Q_BLOCK].astype(o_ref.dtype)


def _swa_attention(q, k, v, k_ctx, v_ctx, sink, b_, l, n_ctx):
    nb = l // Q_BLOCK
    blk = lambda w, off: pl.BlockSpec(
        (Q_BLOCK, w), lambda b, n: (b * nb + jnp.clip(n + off, 0, nb - 1), 0))
    ctx = pl.BlockSpec((n_ctx, SWA_KV_WIDTH), lambda b, n: (b, 0))
    kw = SWA_KV_WIDTH
    return pl.pallas_call(
        functools.partial(_swa_attn_kernel, n_ctx=n_ctx),
        grid=(b_, nb),
        in_specs=[blk(SWA_WIDTH, 0), blk(kw, -1), blk(kw, 0), blk(kw, 1), blk(kw, -1), blk(kw, 0), blk(kw, 1),
                  ctx, ctx, pl.BlockSpec((SWA_HEADS, HEAD_DIM), lambda b, n: (0, 0))],
        out_specs=blk(SWA_WIDTH, 0),
        out_shape=jax.ShapeDtypeStruct((b_ * l, SWA_WIDTH), BF16),
        compiler_params=_cparams(("arbitrary", "arbitrary")),
        name="swa_attention",
    )(q, k, k, k, v, v, v, k_ctx, v_ctx, jnp.broadcast_to(sink.astype(F32)[:, None], (SWA_HEADS, HEAD_DIM)))


def _swa_mixer(yx, yc, g_q, g_k, sink, b_, l, n_ctx):
    ones = jnp.ones((n_ctx, HEAD_DIM), F32)
    zeros = jnp.zeros((n_ctx, HEAD_DIM), F32)
    qx, kx, vx = _swa_prep(yx, _rope_tables(l), g_q, g_k, b_ * l, 256)
    _, kc, vc = _swa_prep(yc, (ones, zeros, zeros), g_q, g_k, b_ * n_ctx, n_ctx)
    return _swa_attention(qx, kx, vx, kc, vc, sink, b_, l, n_ctx)


def _rmsnorm(x, g):
    xf = x.astype(F32)
    y = xf * lax.rsqrt(jnp.mean(xf * xf, axis=-1, keepdims=True) + EPS)
    return (y * g.astype(F32)).astype(x.dtype)


def _l2norm(x):
    return x * lax.rsqrt(jnp.sum(x * x, axis=-1, keepdims=True) + EPS)


def _short_conv(x, w):
    c = x.shape[-1]
    y = lax.conv_general_dilated(x, w[:, None, :].astype(x.dtype), window_strides=(1,),
                                 padding=[(CONV_K // 2, CONV_K // 2)],
                                 dimension_numbers=('NWC', 'WIO', 'NWC'), feature_group_count=c)
    return jax.nn.silu(y)


def _axial_rope(t, rows, cols):
    half = HEAD_DIM // 2
    inv = ROPE_BASE ** (-jnp.arange(0, half, 2, dtype=F32) / half)

    def rot(u, pos):
        ang = pos[:, None] * inv
        cos, sin = jnp.cos(ang)[None, :, None, :], jnp.sin(ang)[None, :, None, :]
        u1, u2 = jnp.split(u.astype(F32), 2, axis=-1)
        return jnp.concatenate([u1 * cos - u2 * sin, u2 * cos + u1 * sin], axis=-1)

    return jnp.concatenate([rot(t[..., :half], rows), rot(t[..., half:], cols)], axis=-1).astype(t.dtype)


def _gated_delta_chunked(q, k, v, g, beta, s0):
    b_, l, h, _ = q.shape
    n = l // GDN_CHUNK

    def chunks(t):
        t = t.reshape((b_, n, GDN_CHUNK) + t.shape[2:])
        return jnp.moveaxis(jnp.swapaxes(t, 2, 3), 1, 0)

    qc, kc, vc, gc, bc = map(chunks, (q, k, v, g, beta))
    gcum = jnp.cumsum(gc, axis=-1)
    tri = jnp.tril(jnp.ones((GDN_CHUNK, GDN_CHUNK), bool))
    strict = jnp.tril(jnp.ones((GDN_CHUNK, GDN_CHUNK), F32), -1)
    diff = gcum[..., :, None] - gcum[..., None, :]
    decay = jnp.where(tri, jnp.exp(jnp.where(tri, diff, 0.0)), 0.0)
    kb = kc * bc[..., None]
    a_strict = jnp.einsum('nbhid,nbhjd->nbhij', kb, kc) * decay * strict
    eye = jnp.eye(GDN_CHUNK, dtype=F32)
    t_inv = lax.linalg.triangular_solve(eye + a_strict, jnp.broadcast_to(eye, a_strict.shape),
                                        left_side=True, lower=True, unit_diagonal=True)
    u = jnp.einsum('nbhij,nbhjd->nbhid', t_inv, vc * bc[..., None])
    w = jnp.einsum('nbhij,nbhjd->nbhid', t_inv, kb * jnp.exp(gcum)[..., None])
    qk = jnp.einsum('nbhid,nbhjd->nbhij', qc, kc) * decay
    q_dec = qc * jnp.exp(gcum)[..., None]
    k_dec = kc * jnp.exp(gcum[..., -1:] - gcum)[..., None]
    chunk_decay = jnp.exp(gcum[..., -1])

    def step(s, xs):
        u_n, w_n, qk_n, q_n, k_n, a_n = xs
        v_new = u_n - jnp.einsum('bhcd,bhde->bhce', w_n, s)
        o = jnp.einsum('bhcd,bhde->bhce', q_n, s) + jnp.einsum('bhij,bhje->bhie', qk_n, v_new)
        s = s * a_n[..., None, None] + jnp.einsum('bhcd,bhce->bhde', k_n, v_new)
        return s, o

    s_fin, o = lax.scan(step, s0, (u, w, qk, q_dec, k_dec, chunk_decay))
    o = jnp.swapaxes(jnp.moveaxis(o, 0, 1), 2, 3).reshape(b_, l, h, -1)
    return o, s_fin


def _gdn_inputs(qkv, b_logit, a_logit, conv_w, a_log, dt_bias):
    b_, l, _ = qkv.shape
    qkv = _short_conv(qkv, conv_w).astype(F32).reshape(b_, l, 3, GDN_HEADS, HEAD_DIM)
    q = _l2norm(qkv[:, :, 0]) * HEAD_DIM ** -0.5
    k = _l2norm(qkv[:, :, 1])
    v = qkv[:, :, 2]
    beta = jax.nn.sigmoid(b_logit.astype(F32).reshape(b_, l, 2, GDN_HEADS))
    g = -jnp.exp(a_log.astype(F32)) * jax.nn.softplus(
        a_logit.astype(F32).reshape(b_, l, 2, GDN_HEADS) + dt_bias.astype(F32))
    return q, k, v, g, beta


def _gdn_bidir(q, k, v, g, beta, s_fwd0, s_bwd0):
    o_f, s_f = _gated_delta_chunked(q, k, v, g[:, :, 0], beta[:, :, 0], s_fwd0)
    rev = lambda t: jnp.flip(t, axis=1)
    o_b, s_b = _gated_delta_chunked(rev(q), rev(k), rev(v), rev(g[:, :, 1]), rev(beta[:, :, 1]), s_bwd0)
    return o_f + rev(o_b), s_f, s_b


def _gdn_output(o, z, g_on):
    b_, l = o.shape[:2]
    y = _rmsnorm(o, g_on).reshape(b_, l, GDN_WIDTH)
    return y * jax.nn.silu(z.astype(F32))


def _swa_inputs(q, k, v, g_q, g_k):
    b_, l, _ = q.shape
    q = _rmsnorm(q.reshape(b_, l, SWA_HEADS, HEAD_DIM), g_q)
    k = _rmsnorm(k.reshape(b_, l, SWA_KV_HEADS, HEAD_DIM), g_k)
    return q, k, v.reshape(b_, l, SWA_KV_HEADS, HEAD_DIM)


def _window_attention(q, k, v, k_ctx, v_ctx, sink):
    b_, l, h, d = q.shape
    g = h // SWA_KV_HEADS
    nb = l // Q_BLOCK
    n_ctx = k_ctx.shape[1]
    scale = d ** -0.5
    qb = q.reshape(b_, nb, Q_BLOCK, SWA_KV_HEADS, g, d)

    def band_blocks(t):
        tp = jnp.pad(t, ((0, 0), (Q_BLOCK, Q_BLOCK), (0, 0), (0, 0))).reshape(b_, nb + 2, Q_BLOCK, SWA_KV_HEADS, d)
        return jnp.concatenate([tp[:, :-2], tp[:, 1:-1], tp[:, 2:]], axis=2)

    kw, vw = band_blocks(k), band_blocks(v)
    qi = jnp.arange(Q_BLOCK)[:, None]
    kj = jnp.arange(3 * Q_BLOCK)[None, :]
    band = jnp.abs(kj - Q_BLOCK - qi) <= WINDOW
    kpos = jnp.arange(nb)[:, None] * Q_BLOCK - Q_BLOCK + jnp.arange(3 * Q_BLOCK)[None, :]
    valid = band[None] & ((kpos >= 0) & (kpos < l))[:, None, :]
    s_win = jnp.einsum('bnqkgd,bnjkd->bnkgqj', qb, kw).astype(F32) * scale
    s_win = jnp.where(valid[None, :, None, None], s_win, NEG_INF)
    s_ctx = jnp.einsum('bnqkgd,bckd->bnkgqc', qb, k_ctx).astype(F32) * scale
    s_sink = jnp.broadcast_to(sink.astype(F32).reshape(SWA_KV_HEADS, g, 1, 1), s_ctx.shape[:-1] + (1,))
    p = jax.nn.softmax(jnp.concatenate([s_ctx, s_win, s_sink], axis=-1), axis=-1).astype(v.dtype)
    o = (jnp.einsum('bnkgqc,bckd->bnqkgd', p[..., :n_ctx], v_ctx)
         + jnp.einsum('bnkgqj,bnjkd->bnqkgd', p[..., n_ctx:n_ctx + 3 * Q_BLOCK], vw))
    return o.reshape(b_, l, h * d)


def _split_main(y):
    o = np.cumsum((0, 3 * GDN_WIDTH, GDN_WIDTH, SWA_WIDTH, SWA_KV_WIDTH, SWA_KV_WIDTH))
    return tuple(y[..., int(o[n]):int(o[n + 1])] for n in range(5))


def kernel(x, c, ctx, c_ctx, w_ada, b_ada, g_norm1, g_norm2, w_in, conv_qkv, a_log, dt_bias, g_onorm, g_qnorm,
           g_knorm, sink, w_out, w_router_grp, b_router_grp, w_router_exp, b_router_exp, w_gate, w_up, w_down):
    b_, l, d = x.shape
    n_ctx = ctx.shape[1]
    t = b_ * l
    assert w_ada.shape[0] == 1 and d == D_MODEL and b_ + 1 <= MOD_ROWS
    rows = jnp.repeat(jnp.arange(l // GRID_W, dtype=F32), GRID_W)
    cols = jnp.tile(jnp.arange(GRID_W, dtype=F32), l // GRID_W)

    wi = w_in[0]
    w_main = jnp.concatenate([wi[:, IN_OFFS[0]:IN_OFFS[2]], wi[:, IN_OFFS[4]:IN_OFFS[7]]], axis=1).astype(BF16)
    w_small = jnp.pad(wi[:, IN_OFFS[2]:IN_OFFS[4]], ((0, 0), (0, N_SMALL - 4 * GDN_HEADS))).astype(BF16)
    wo = w_out[0].astype(BF16)
    w_router = jnp.pad(jnp.concatenate([w_router_grp[0], w_router_exp[0]], axis=1),
                       ((0, 0), (0, N_ROUTER - N_GROUPS - N_EXPERTS))).astype(BF16)
    b_router = jnp.pad(jnp.concatenate([b_router_grp[0], b_router_exp[0]]),
                       (0, N_ROUTER - N_GROUPS - N_EXPERTS)).reshape(1, N_ROUTER)

    c_rows = jnp.zeros((MOD_ROWS, d), F32).at[:b_].set(c).at[b_].set(c_ctx)
    mod = _modulation(c_rows, w_ada[0], b_ada[0])
    mod3 = mod.reshape(MOD_ROWS * 6, 1, d)

    tm = 512
    tpb = l // tm
    yx, sx = _in_projection(x.reshape(t, d), g_norm1[0], mod3, lambda i: i // tpb, w_main, w_small, tm)
    yc, sc = _in_projection(ctx.reshape(b_ * n_ctx, d), g_norm1[0], mod3, lambda i: b_, w_main, w_small, n_ctx)
    ya_x = _gdn_mixer(yx, sx, yc, sc, conv_qkv[0], a_log[0], dt_bias[0], g_onorm[0], b_, l, n_ctx)
    yb_x = _swa_mixer(yx, yc, g_qnorm[0], g_knorm[0], sink[0], b_, l, n_ctx)

    x1 = _out_projection(ya_x, yb_x, wo[:GDN_WIDTH], wo[GDN_WIDTH:], x.reshape(t, d), mod3, tpb, tm)

    tm2 = 256
    h2, ids, gates = _norm2_router(x1, g_norm2[0], mod3, l // tm2, w_router, b_router, tm2)
    slot_tok, slot_w, slot_of, blk_expert, n_used = _slots(ids[:, :TOP_K], gates[:, :TOP_K], t)
    y = _moe_experts(h2[slot_tok], blk_expert, n_used, slot_w, w_gate[0], w_up[0], w_down[0])
    moe = y[slot_of[0::2]] + y[slot_of[1::2]]
    gate2 = mod[:b_, 5 * d:6 * d][:, None, :]
    return x1.reshape(b_, l, d) + gate2 * moe.reshape(b_, l, d)
```

```python
import functools
import math

import jax
import jax.numpy as jnp
import numpy as np
from jax import lax
from jax.experimental import pallas as pl
from jax.experimental.pallas import tpu as pltpu

F32 = jnp.float32
BF16 = jnp.bfloat16

D_MODEL = 4096
CTX_LEN = 256
GRID_W = 64
HEAD_DIM = 128
GDN_HEADS = 16
GDN_WIDTH = GDN_HEADS * HEAD_DIM
GDN_CHUNK = 64
CONV_K = 5
SWA_HEADS = 16
SWA_KV_HEADS = 4
SWA_WIDTH = SWA_HEADS * HEAD_DIM
SWA_KV_WIDTH = SWA_KV_HEADS * HEAD_DIM
WINDOW = 128
Q_BLOCK = 128
ROPE_BASE = 10000.0
N_GROUPS = 8
EXPERTS_PER_GROUP = 8
N_EXPERTS = N_GROUPS * EXPERTS_PER_GROUP
TOP_K = 2
D_EXPERT = D_MODEL // 8
EPS = 1e-6
NEG_INF = -1e30

IN_SIZES = (3 * GDN_WIDTH, GDN_WIDTH, 2 * GDN_HEADS, 2 * GDN_HEADS, SWA_WIDTH, SWA_KV_WIDTH, SWA_KV_WIDTH)
IN_OFFS = tuple(int(v) for v in np.cumsum((0,) + IN_SIZES))
N_MAIN = 3 * GDN_WIDTH + GDN_WIDTH + SWA_WIDTH + 2 * SWA_KV_WIDTH
N_SMALL = 128
N_ROUTER = 128
MOD_ROWS = 8

MOE_BM = 256
VMEM_LIMIT = 56 * 1024 * 1024


def _cparams(sem):
    return pltpu.CompilerParams(dimension_semantics=sem, vmem_limit_bytes=VMEM_LIMIT)


def _mod_kernel(c_ref, w_ref, b_ref, o_ref):
    c = c_ref[...]
    a = (c * jax.nn.sigmoid(c)).astype(BF16)
    o_ref[...] = jnp.dot(a, w_ref[...].astype(BF16), preferred_element_type=F32) + b_ref[...]


def _modulation(c_rows, w_ada, b_ada):
    d, n = w_ada.shape
    tn = 512
    return pl.pallas_call(
        _mod_kernel,
        grid=(n // tn,),
        in_specs=[pl.BlockSpec((MOD_ROWS, d), lambda j: (0, 0)),
                  pl.BlockSpec((d, tn), lambda j: (0, j)),
                  pl.BlockSpec((1, tn), lambda j: (0, j))],
        out_specs=pl.BlockSpec((MOD_ROWS, tn), lambda j: (0, j)),
        out_shape=jax.ShapeDtypeStruct((MOD_ROWS, n), F32),
        compiler_params=_cparams(("arbitrary",)),
        name="modulation",
    )(c_rows, w_ada, b_ada.reshape(1, n))


NORM_ROWS = 64


def _norm_mod_rows(x_ref, g_ref, sh_ref, sc_ref, h_ref, tm):
    g = g_ref[...]
    sc = 1.0 + sc_ref[0]
    sh = sh_ref[0]

    def body(r, carry):
        rows = pl.ds(pl.multiple_of(r * NORM_ROWS, NORM_ROWS), NORM_ROWS)
        xf = x_ref[rows, :]
        ms = jnp.mean(xf * xf, axis=-1, keepdims=True)
        y = xf * lax.rsqrt(ms + EPS) * g
        h_ref[rows, :] = (y * sc + sh).astype(h_ref.dtype)
        return carry

    lax.fori_loop(0, tm // NORM_ROWS, body, 0)


def _inproj_kernel(x_ref, g_ref, sh_ref, sc_ref, w_ref, ws_ref, o_ref, os_ref, h_ref, *, tm):
    @pl.when(pl.program_id(1) == 0)
    def _():
        _norm_mod_rows(x_ref, g_ref, sh_ref, sc_ref, h_ref, tm)
        os_ref[...] = jnp.dot(h_ref[...], ws_ref[...], preferred_element_type=F32)

    o_ref[...] = jnp.dot(h_ref[...], w_ref[...], preferred_element_type=F32)


def _in_projection(x2d, g_norm, mod3, mod_row_of_tile, w_main, w_small, tm):
    t, d = x2d.shape
    tn = 1024
    return pl.pallas_call(
        functools.partial(_inproj_kernel, tm=tm),
        grid=(t // tm, N_MAIN // tn),
        in_specs=[pl.BlockSpec((tm, d), lambda i, j: (i, 0)),
                  pl.BlockSpec((1, d), lambda i, j: (0, 0)),
                  pl.BlockSpec((1, 1, d), lambda i, j: (mod_row_of_tile(i) * 6 + 0, 0, 0)),
                  pl.BlockSpec((1, 1, d), lambda i, j: (mod_row_of_tile(i) * 6 + 1, 0, 0)),
                  pl.BlockSpec((d, tn), lambda i, j: (0, j)),
                  pl.BlockSpec((d, N_SMALL), lambda i, j: (0, 0))],
        out_specs=[pl.BlockSpec((tm, tn), lambda i, j: (i, j)),
                   pl.BlockSpec((tm, N_SMALL), lambda i, j: (i, 0))],
        out_shape=[jax.ShapeDtypeStruct((t, N_MAIN), F32),
                   jax.ShapeDtypeStruct((t, N_SMALL), F32)],
        scratch_shapes=[pltpu.VMEM((tm, d), BF16)],
        compiler_params=_cparams(("arbitrary", "arbitrary")),
        name="in_projection",
    )(x2d, g_norm.reshape(1, d), mod3, mod3, w_main, w_small)


def _outproj_kernel(ya_ref, yb_ref, wa_ref, wb_ref, x_ref, gate_ref, o_ref):
    acc = jnp.dot(ya_ref[...], wa_ref[...], preferred_element_type=F32)
    acc = acc + jnp.dot(yb_ref[...], wb_ref[...], preferred_element_type=F32)
    o_ref[...] = x_ref[...] + gate_ref[0] * acc


def _out_projection(ya, yb, wa, wb, x2d, mod3, tiles_per_batch, tm):
    t, d = x2d.shape
    tn = 1024
    nj = d // tn
    ka, kb = ya.shape[1], yb.shape[1]
    return pl.pallas_call(
        _outproj_kernel,
        grid=(t // tm, nj),
        in_specs=[pl.BlockSpec((tm, ka), lambda i, j: (i, 0)),
                  pl.BlockSpec((tm, kb), lambda i, j: (i, 0)),
                  pl.BlockSpec((ka, tn), lambda i, j: (0, j)),
                  pl.BlockSpec((kb, tn), lambda i, j: (0, j)),
                  pl.BlockSpec((tm, tn), lambda i, j: (i, j)),
                  pl.BlockSpec((1, 1, tn), lambda i, j: (((i // tiles_per_batch) * 6 + 2) * nj + j, 0, 0))],
        out_specs=pl.BlockSpec((tm, tn), lambda i, j: (i, j)),
        out_shape=jax.ShapeDtypeStruct((t, d), F32),
        compiler_params=_cparams(("arbitrary", "arbitrary")),
        name="out_projection",
    )(ya, yb, wa, wb, x2d, mod3.reshape(-1, 1, tn))


def _route_rows(lg):
    col = lax.broadcasted_iota(jnp.int32, lg.shape, 1)
    first = lambda hit: jnp.min(jnp.where(hit, col, N_ROUTER), axis=-1, keepdims=True)
    gm = col < N_GROUPS
    mg = jnp.max(jnp.where(gm, lg, NEG_INF), axis=-1, keepdims=True)
    grp = first(gm & (lg == mg))
    p_grp = 1.0 / jnp.sum(jnp.where(gm, jnp.exp(lg - mg), 0.0), axis=-1, keepdims=True)
    lo = N_GROUPS + grp * EXPERTS_PER_GROUP
    em = (col >= lo) & (col < lo + EXPERTS_PER_GROUP)
    m1 = jnp.max(jnp.where(em, lg, NEG_INF), axis=-1, keepdims=True)
    i1 = first(em & (lg == m1))
    em2 = em & (col != i1)
    m2 = jnp.max(jnp.where(em2, lg, NEG_INF), axis=-1, keepdims=True)
    i2 = first(em2 & (lg == m2))
    e2 = jnp.exp(m2 - m1)
    g1 = p_grp / (1.0 + e2)
    ids = jnp.where(col == 0, i1 - N_GROUPS, jnp.where(col == 1, i2 - N_GROUPS, 0))
    gates = jnp.where(col == 0, g1, jnp.where(col == 1, g1 * e2, 0.0))
    return ids, gates


LANES = 128
ROW_CH = D_MODEL // LANES
ROW_PITCH = 40


def _store_chunked(dst_ref, row0, vals):
    n = vals.shape[0]
    for j in range(ROW_PITCH):
        piece = vals[:, j * LANES:(j + 1) * LANES] if j < ROW_CH else jnp.zeros((n, LANES), F32)
        dst_ref[pl.ds(row0 * ROW_PITCH + j, n, stride=ROW_PITCH), :] = piece


def _load_chunk(src_ref, row0, n, j):
    return src_ref[pl.ds(row0 * ROW_PITCH + j, n, stride=ROW_PITCH), :]


def _row_copy(src_ref, dst_ref, src_row, dst_row, sem):
    return pltpu.make_async_copy(src_ref.at[pl.ds(pl.multiple_of(src_row * ROW_PITCH, 8), ROW_CH), :],
                                 dst_ref.at[pl.ds(pl.multiple_of(dst_row * ROW_PITCH, 8), ROW_CH), :], sem)


def _norm2_kernel(x_ref, g_ref, sh_ref, sc_ref, wr_ref, br_ref, hc_ref, id_ref, gt_ref, hb_ref, *, tm):
    g = g_ref[...]
    sc = 1.0 + sc_ref[0]
    sh = sh_ref[0]

    def body(r, carry):
        row0 = pl.multiple_of(r * NORM_ROWS, NORM_ROWS)
        xf = x_ref[pl.ds(row0, NORM_ROWS), :]
        ms = jnp.mean(xf * xf, axis=-1, keepdims=True)
        h = xf * lax.rsqrt(ms + EPS) * g * sc + sh
        hb_ref[pl.ds(row0, NORM_ROWS), :] = h.astype(BF16)
        _store_chunked(hc_ref, row0, h)
        return carry

    lax.fori_loop(0, tm // NORM_ROWS, body, 0)
    lg = jnp.dot(hb_ref[...], wr_ref[...], preferred_element_type=F32) + br_ref[...]
    id_ref[...], gt_ref[...] = _route_rows(lg)


def _norm2_router(x2d, g_norm, mod3, tiles_per_batch, w_router, b_router, tm):
    t, d = x2d.shape
    return pl.pallas_call(
        functools.partial(_norm2_kernel, tm=tm),
        grid=(t // tm,),
        in_specs=[pl.BlockSpec((tm, d), lambda i: (i, 0)),
                  pl.BlockSpec((1, d), lambda i: (0, 0)),
                  pl.BlockSpec((1, 1, d), lambda i: ((i // tiles_per_batch) * 6 + 3, 0, 0)),
                  pl.BlockSpec((1, 1, d), lambda i: ((i // tiles_per_batch) * 6 + 4, 0, 0)),
                  pl.BlockSpec((d, N_ROUTER), lambda i: (0, 0)),
                  pl.BlockSpec((1, N_ROUTER), lambda i: (0, 0))],
        out_specs=[pl.BlockSpec((tm * ROW_PITCH, LANES), lambda i: (i, 0)),
                   pl.BlockSpec((tm, N_ROUTER), lambda i: (i, 0)),
                   pl.BlockSpec((tm, N_ROUTER), lambda i: (i, 0))],
        out_shape=[jax.ShapeDtypeStruct((t * ROW_PITCH, LANES), F32),
                   jax.ShapeDtypeStruct((t, N_ROUTER), jnp.int32),
                   jax.ShapeDtypeStruct((t, N_ROUTER), F32)],
        scratch_shapes=[pltpu.VMEM((tm, d), BF16)],
        compiler_params=_cparams(("arbitrary",)),
        name="norm2_router",
    )(x2d, g_norm.reshape(1, d), mod3, mod3, w_router, b_router)


CAST_ROWS = 128


def _cast_rows(src_ref, dst_ref):
    n = src_ref.shape[1]

    def body(r, carry):
        rows = pl.ds(pl.multiple_of(r * CAST_ROWS, CAST_ROWS), CAST_ROWS)
        dst_ref[rows, :] = src_ref[0, rows, :].astype(dst_ref.dtype)
        return carry

    lax.fori_loop(0, n // CAST_ROWS, body, 0)


def _expert_changed(be_ref, i):
    return (i == 0) | (be_ref[i] != be_ref[jnp.maximum(i - 1, 0)])


def _moe_up_kernel(be_ref, nu_ref, st_ref, h_hbm, wg_ref, wu_ref, o_ref, wgb_ref, wub_ref, xg_ref, xb_ref, sem):
    i = pl.program_id(0)
    nu = nu_ref[0]

    def start_gather(blk):
        def body(r, carry):
            _row_copy(h_hbm, xg_ref, st_ref[blk * MOE_BM + r], r, sem).start()
            return carry

        lax.fori_loop(0, MOE_BM, body, 0)

    def wait_gather():
        def body(r, carry):
            _row_copy(h_hbm, xg_ref, 0, r, sem).wait()
            return carry

        lax.fori_loop(0, MOE_BM, body, 0)

    @pl.when(i == 0)
    def _():
        start_gather(0)

    @pl.when(_expert_changed(be_ref, i))
    def _():
        _cast_rows(wg_ref, wgb_ref)
        _cast_rows(wu_ref, wub_ref)

    @pl.when(i < nu)
    def _():
        wait_gather()
        for j in range(ROW_CH):
            xb_ref[:, pl.ds(j * LANES, LANES)] = _load_chunk(xg_ref, 0, MOE_BM, j).astype(BF16)

        @pl.when(i + 1 < nu)
        def _():
            start_gather(i + 1)

        xb = xb_ref[...]
        g = jnp.dot(xb, wgb_ref[...], preferred_element_type=F32)
        u = jnp.dot(xb, wub_ref[...], preferred_element_type=F32)
        o_ref[...] = (g * jax.nn.sigmoid(g) * u).astype(o_ref.dtype)

    @pl.when(i >= nu)
    def _():
        o_ref[...] = jnp.zeros_like(o_ref)


def _moe_down_kernel(be_ref, nu_ref, h_ref, wd_ref, o_ref, wdb_ref):
    i = pl.program_id(0)

    @pl.when(_expert_changed(be_ref, i))
    def _():
        _cast_rows(wd_ref, wdb_ref)

    @pl.when(i < nu_ref[0])
    def _():
        _store_chunked(o_ref, 0, jnp.dot(h_ref[...], wdb_ref[...], preferred_element_type=F32))

    @pl.when(i >= nu_ref[0])
    def _():
        o_ref[...] = jnp.zeros_like(o_ref)


def _moe_experts(h_chunked, slot_tok, blk_expert, n_used, w_gate, w_up, w_down):
    p = slot_tok.shape[0]
    n_blocks = p // MOE_BM
    _, d, de = w_gate.shape
    hmid = pl.pallas_call(
        _moe_up_kernel,
        grid_spec=pltpu.PrefetchScalarGridSpec(
            num_scalar_prefetch=3,
            grid=(n_blocks,),
            in_specs=[pl.BlockSpec(memory_space=pl.ANY),
                      pl.BlockSpec((1, d, de), lambda i, be, nu, st: (be[i], 0, 0)),
                      pl.BlockSpec((1, d, de), lambda i, be, nu, st: (be[i], 0, 0))],
            out_specs=pl.BlockSpec((MOE_BM, de), lambda i, be, nu, st: (i, 0)),
            scratch_shapes=[pltpu.VMEM((d, de), BF16), pltpu.VMEM((d, de), BF16),
                            pltpu.VMEM((MOE_BM * ROW_PITCH, LANES), F32), pltpu.VMEM((MOE_BM, d), BF16),
                            pltpu.SemaphoreType.DMA]),
        out_shape=jax.ShapeDtypeStruct((p, de), BF16),
        compiler_params=_cparams(("arbitrary",)),
        name="moe_gate_up",
    )(blk_expert, n_used, slot_tok, h_chunked, w_gate, w_up)
    return pl.pallas_call(
        _moe_down_kernel,
        grid_spec=pltpu.PrefetchScalarGridSpec(
            num_scalar_prefetch=2,
            grid=(n_blocks,),
            in_specs=[pl.BlockSpec((MOE_BM, de), lambda i, be, nu: (i, 0)),
                      pl.BlockSpec((1, de, d), lambda i, be, nu: (be[i], 0, 0))],
            out_specs=pl.BlockSpec((MOE_BM * ROW_PITCH, LANES), lambda i, be, nu: (i, 0)),
            scratch_shapes=[pltpu.VMEM((de, d), BF16)]),
        out_shape=jax.ShapeDtypeStruct((p * ROW_PITCH, LANES), F32),
        compiler_params=_cparams(("arbitrary",)),
        name="moe_down",
    )(blk_expert, n_used, hmid, w_down)


COMBINE_TM = 256


def _combine_kernel(so_ref, y_hbm, x_ref, gt_ref, g2_ref, o_ref, yb_ref, sem, *, tm):
    i = pl.program_id(0)
    n = pl.num_programs(0)

    def start_gather(blk, slot):
        def body(r, carry):
            for k in range(TOP_K):
                _row_copy(y_hbm, yb_ref.at[slot], so_ref[(blk * tm + r) * TOP_K + k], k * tm + r, sem.at[slot]).start()
            return carry

        lax.fori_loop(0, tm, body, 0)

    def wait_gather(slot):
        def body(r, carry):
            _row_copy(y_hbm, yb_ref.at[slot], 0, r, sem.at[slot]).wait()
            return carry

        lax.fori_loop(0, TOP_K * tm, body, 0)

    @pl.when(i == 0)
    def _():
        start_gather(0, 0)

    @pl.when(i + 1 < n)
    def _():
        start_gather(i + 1, (i + 1) % 2)

    def combine(slot):
        wait_gather(slot)
        g0 = gt_ref[:, 0:1]
        g1 = gt_ref[:, 1:2]
        for j in range(ROW_CH):
            c = pl.ds(j * LANES, LANES)
            y = g0 * _load_chunk(yb_ref.at[slot], 0, tm, j) + g1 * _load_chunk(yb_ref.at[slot], tm, tm, j)
            o_ref[:, c] = x_ref[:, c] + g2_ref[0][:, j * LANES:(j + 1) * LANES] * y

    for slot in range(2):
        pl.when(i % 2 == slot)(functools.partial(combine, slot))


def _moe_combine(y_chunked, slot_of, x2d, gates, mod3, tiles_per_batch):
    t, d = x2d.shape
    tm = COMBINE_TM
    return pl.pallas_call(
        functools.partial(_combine_kernel, tm=tm),
        grid_spec=pltpu.PrefetchScalarGridSpec(
            num_scalar_prefetch=1,
            grid=(t // tm,),
            in_specs=[pl.BlockSpec(memory_space=pl.ANY),
                      pl.BlockSpec((tm, d), lambda i, so: (i, 0)),
                      pl.BlockSpec((tm, N_ROUTER), lambda i, so: (i, 0)),
                      pl.BlockSpec((1, 1, d), lambda i, so: ((i // tiles_per_batch) * 6 + 5, 0, 0))],
            out_specs=pl.BlockSpec((tm, d), lambda i, so: (i, 0)),
            scratch_shapes=[pltpu.VMEM((2, TOP_K * tm * ROW_PITCH, LANES), F32),
                            pltpu.SemaphoreType.DMA((2,))]),
        out_shape=jax.ShapeDtypeStruct((t, d), F32),
        compiler_params=_cparams(("arbitrary",)),
        name="moe_combine",
    )(slot_of, y_chunked, x2d, gates, mod3)


def _slots(eid, t):
    a = t * TOP_K
    e_flat = eid.reshape(a)
    tok_flat = jnp.repeat(jnp.arange(t, dtype=jnp.int32), TOP_K)
    order = jnp.argsort(e_flat)
    e_sorted = e_flat[order]
    counts = jax.ops.segment_sum(jnp.ones((a,), jnp.int32), e_flat, num_segments=N_EXPERTS)
    padded = (counts + MOE_BM - 1) // MOE_BM * MOE_BM
    start = jnp.cumsum(counts) - counts
    pend = jnp.cumsum(padded)
    pstart = pend - padded
    dest = pstart[e_sorted] + jnp.arange(a, dtype=jnp.int32) - start[e_sorted]
    n_blocks = (a + MOE_BM - 1) // MOE_BM + N_EXPERTS
    p = n_blocks * MOE_BM
    slot_tok = jnp.zeros((p,), jnp.int32).at[dest].set(tok_flat[order])
    slot_of = jnp.zeros((a,), jnp.int32).at[order].set(dest)
    blk_expert = jnp.minimum(jnp.searchsorted(pend, jnp.arange(n_blocks, dtype=jnp.int32) * MOE_BM,
                                              side='right'), N_EXPERTS - 1).astype(jnp.int32)
    n_used = (pend[-1] // MOE_BM).astype(jnp.int32).reshape(1)
    return slot_tok, slot_of, blk_expert, n_used


GDN_TB = 256
HALO = 8
N_QKV = 3 * GDN_WIDTH


def _softplus(v):
    return jnp.maximum(v, 0.0) + jnp.log(1.0 + jnp.exp(-jnp.abs(v)))


def _split3_bf16(v):
    hi = v.astype(BF16)
    r1 = v - hi.astype(F32)
    mid = r1.astype(BF16)
    lo = (r1 - mid.astype(F32)).astype(BF16)
    return hi, mid, lo


def _gdn_prep_kernel(cur_ref, prev_ref, next_ref, sm_ref, cw_ref, ea_ref, dt_ref, o_ref, g_ref, ext_ref, *, tb):
    i = pl.program_id(1)
    nblk = pl.num_programs(1)
    ext_ref[pl.ds(0, HALO), :] = jnp.where(i > 0, prev_ref[...], 0.0)
    ext_ref[pl.ds(HALO, tb), :] = cur_ref[...]
    ext_ref[pl.ds(HALO + tb, HALO), :] = jnp.where(i < nblk - 1, next_ref[...], 0.0)

    def conv_cols(kind):
        def body(hh, carry):
            cols = pl.ds(pl.multiple_of((kind * GDN_HEADS + hh) * HEAD_DIM, HEAD_DIM), HEAD_DIM)
            cw = cw_ref[:, cols]
            for r0 in range(0, tb, 64):
                acc = None
                for s in range(CONV_K):
                    term = ext_ref[pl.ds(HALO - CONV_K // 2 + s + r0, 64), cols] * cw[s:s + 1, :]
                    acc = term if acc is None else acc + term
                y = acc * jax.nn.sigmoid(acc)
                if kind < 2:
                    y = y * lax.rsqrt(jnp.sum(y * y, axis=-1, keepdims=True) + EPS)
                if kind == 0:
                    y = y * HEAD_DIM ** -0.5
                o_ref[pl.ds(r0, 64), cols] = y.astype(o_ref.dtype)
            return carry

        lax.fori_loop(0, GDN_HEADS, body, 0)

    conv_cols(0)
    conv_cols(1)
    conv_cols(2)

    s = sm_ref[...]
    beta = jax.nn.sigmoid(s)
    g = -ea_ref[...] * _softplus(s + dt_ref[...])
    r = lax.broadcasted_iota(jnp.int32, (tb, tb), 0)
    c = lax.broadcasted_iota(jnp.int32, (tb, tb), 1)
    same = (r // GDN_CHUNK) == (c // GDN_CHUNK)
    lower = (same & (c <= r)).astype(BF16)
    upper = (same & (c >= r)).astype(BF16)
    parts = _split3_bf16(g)
    cf = sum(jnp.dot(lower, pt, preferred_element_type=F32) for pt in parts)
    cb = sum(jnp.dot(upper, pt, preferred_element_type=F32) for pt in parts)
    col = lax.broadcasted_iota(jnp.int32, s.shape, 1)
    nh = GDN_HEADS
    g_ref[...] = jnp.where(col < 2 * nh, beta, jnp.where(col < 3 * nh, cf, jnp.where(col < 4 * nh, cb, 0.0)))


def _gdn_prep(y_main, small, conv_w8, ea_row, dt_row, b_, l):
    tb = min(GDN_TB, l)
    nblk = l // tb
    hb = tb // HALO
    last = b_ * l // HALO - 1
    return pl.pallas_call(
        functools.partial(_gdn_prep_kernel, tb=tb),
        grid=(b_, nblk),
        in_specs=[pl.BlockSpec((tb, N_QKV), lambda b, i: (b * nblk + i, 0)),
                  pl.BlockSpec((HALO, N_QKV), lambda b, i: (jnp.maximum((b * nblk + i) * hb - 1, 0), 0)),
                  pl.BlockSpec((HALO, N_QKV), lambda b, i: (jnp.minimum((b * nblk + i + 1) * hb, last), 0)),
                  pl.BlockSpec((tb, N_SMALL), lambda b, i: (b * nblk + i, 0)),
                  pl.BlockSpec((HALO, N_QKV), lambda b, i: (0, 0)),
                  pl.BlockSpec((1, N_SMALL), lambda b, i: (0, 0)),
                  pl.BlockSpec((1, N_SMALL), lambda b, i: (0, 0))],
        out_specs=[pl.BlockSpec((tb, N_QKV), lambda b, i: (b * nblk + i, 0)),
                   pl.BlockSpec((tb, N_SMALL), lambda b, i: (b * nblk + i, 0))],
        out_shape=[jax.ShapeDtypeStruct((b_ * l, N_QKV), BF16),
                   jax.ShapeDtypeStruct((b_ * l, N_SMALL), F32)],
        scratch_shapes=[pltpu.VMEM((tb + 2 * HALO, N_QKV), F32)],
        compiler_params=_cparams(("arbitrary", "arbitrary")),
        name="gdn_prep",
    )(y_main, y_main, y_main, small, conv_w8, ea_row, dt_row)


N_PAIRS = GDN_HEADS // 2
PK = 2 * GDN_CHUNK
INV_LEVELS = (2, 4, 8, 16, 32, 64)


def _gdn_masks(reverse):
    i = np.arange(PK)[:, None]
    j = np.arange(PK)[None, :]
    same = (i // GDN_CHUNK) == (j // GDN_CHUNK)
    strict = same & ((j > i) if reverse else (j < i))
    out = []
    for bs in INV_LEVELS:
        out.append(strict & (i // bs == j // bs) & (i // (bs // 2) != j // (bs // 2)))
    incl = same & ((j >= i) if reverse else (j <= i))
    out.append(incl)
    m = np.stack(out).astype(np.float32)
    neg = ((incl.astype(np.float32) - 1.0) * 1e30)[None]
    return jnp.asarray(np.concatenate([m, neg], axis=0))


def _gdn_scan_kernel(*refs, final):
    if final:
        (q_ref, k_ref, v_ref, cp_ref, rp_ref, ap_ref, s0_ref, mk_ref, op_ref, z_ref, gon_ref,
         o_ref, sfin_ref, s_scr) = refs
    else:
        q_ref, k_ref, v_ref, cp_ref, rp_ref, ap_ref, s0_ref, mk_ref, o_ref, sfin_ref, s_scr = refs
    c = pl.program_id(1)

    @pl.when(c == 0)
    def _():
        s_scr[...] = s0_ref[0]

    nl = len(INV_LEVELS)
    incl = mk_ref[nl]
    negm = mk_ref[nl + 1]
    ri = lax.broadcasted_iota(jnp.int32, (PK, 1), 0)
    top = (ri < GDN_CHUNK).astype(F32)
    bot = 1.0 - top
    rr = lax.broadcasted_iota(jnp.int32, (PK, PK), 0)
    cc = lax.broadcasted_iota(jnp.int32, (PK, PK), 1)
    eye = (rr == cc).astype(F32)
    cp = cp_ref[0, 0]
    rp = rp_ref[0, 0]
    ap = ap_ref[0, 0]
    dot = functools.partial(jnp.dot, preferred_element_type=F32)

    pairs = range(N_PAIRS)
    nt = (((1,), (1,)), ((), ()))
    tn = (((0,), (0,)), ((), ()))
    cols = [(pl.ds(2 * p * HEAD_DIM, HEAD_DIM), pl.ds((2 * p + 1) * HEAD_DIM, HEAD_DIM)) for p in pairs]
    pack = lambda ref, p: jnp.concatenate([ref[:, cols[p][0]], ref[:, cols[p][1]]], axis=0)
    gcol = [cp[:, p:p + 1] for p in pairs]
    bcol = [cp[:, N_PAIRS + p:N_PAIRS + p + 1] for p in pairs]
    glcol = [cp[:, 2 * N_PAIRS + p:2 * N_PAIRS + p + 1] for p in pairs]
    kp = [pack(k_ref, p) for p in pairs]
    qp = [pack(q_ref, p) for p in pairs]
    kk = [lax.dot_general(kp[p], kp[p], nt, preferred_element_type=F32) for p in pairs]
    qk = [lax.dot_general(qp[p], kp[p], nt, preferred_element_type=F32) for p in pairs]
    dec = [jnp.exp((gcol[p] - rp[p:p + 1, :]) * incl + negm) for p in pairs]
    a = [kk[p] * dec[p] * bcol[p] for p in pairs]
    qkm = [(qk[p] * dec[p]).astype(BF16) for p in pairs]

    x = [eye - a[p] * mk_ref[0] for p in pairs]
    for lv in range(1, nl):
        xb = [x[p].astype(BF16) for p in pairs]
        po = [dot(xb[p], (a[p] * mk_ref[lv]).astype(BF16)) for p in pairs]
        x = [x[p] - dot(po[p].astype(BF16), xb[p]) for p in pairs]
    tb = [x[p].astype(BF16) for p in pairs]

    egc = [jnp.exp(gcol[p]) for p in pairs]
    kf = [kp[p].astype(F32) for p in pairs]
    u = [dot(tb[p], (pack(v_ref, p).astype(F32) * bcol[p]).astype(BF16)) for p in pairs]
    w = [dot(tb[p], (kf[p] * (bcol[p] * egc[p])).astype(BF16)) for p in pairs]
    qd = [qp[p].astype(F32) * egc[p] for p in pairs]
    kd = [kf[p] * jnp.exp(glcol[p] - gcol[p]) for p in pairs]

    s = [s_scr[p] for p in pairs]
    lhs = [jnp.concatenate([jnp.concatenate([w[p] * top, w[p] * bot], axis=1),
                            jnp.concatenate([qd[p] * top, qd[p] * bot], axis=1)], axis=0).astype(BF16)
           for p in pairs]
    ws = [dot(lhs[p], s[p].astype(BF16)) for p in pairs]
    vnb = [(u[p] - ws[p][:PK]).astype(BF16) for p in pairs]
    o = [ws[p][PK:] + dot(qkm[p], vnb[p]) for p in pairs]
    kbd = [jnp.concatenate([kd[p] * top, kd[p] * bot], axis=1).astype(BF16) for p in pairs]
    kv = [lax.dot_general(kbd[p], vnb[p], tn, preferred_element_type=F32) for p in pairs]
    for p in pairs:
        s_scr[p] = jnp.exp(ap[:, p:p + 1]) * s[p] + kv[p]

    for p in pairs:
        op = o[p]
        if final:
            op = op + pack(op_ref, p)
            zz = pack(z_ref, p)
            op = op * lax.rsqrt(jnp.mean(op * op, axis=-1, keepdims=True) + EPS) * gon_ref[...]
            op = op * (zz * jax.nn.sigmoid(zz))
        o_ref[:, cols[p][0]] = op[:GDN_CHUNK].astype(o_ref.dtype)
        o_ref[:, cols[p][1]] = op[GDN_CHUNK:].astype(o_ref.dtype)

    @pl.when(c == pl.num_programs(1) - 1)
    def _():
        sfin_ref[0] = s_scr[...]


def _gdn_packs(gates, b_, l, reverse):
    nc = l // GDN_CHUNK
    nh = GDN_HEADS
    g4 = gates.reshape(b_, nc, GDN_CHUNK, N_SMALL)
    d = 1 if reverse else 0
    beta = g4[..., d * nh:(d + 1) * nh]
    gc = g4[..., (2 + d) * nh:(3 + d) * nh]
    gl = jnp.broadcast_to(gc[:, :, 0:1] if reverse else gc[:, :, GDN_CHUNK - 1:GDN_CHUNK], gc.shape)

    def rowpack(t):
        return jnp.transpose(t.reshape(b_, nc, GDN_CHUNK, N_PAIRS, 2), (0, 1, 3, 4, 2)).reshape(b_, nc, N_PAIRS, PK)

    rp = rowpack(gc)
    cp = jnp.concatenate([jnp.swapaxes(rowpack(t), 2, 3) for t in (gc, beta, gl)]
                         + [jnp.zeros((b_, nc, PK, N_PAIRS), F32)], axis=-1)
    glh = gl[:, :, 0].reshape(b_, nc, N_PAIRS, 2)
    ap = jnp.swapaxes(jnp.repeat(glh, HEAD_DIM, axis=-1), 2, 3)
    return cp, rp, ap


def _gdn_scan(qkv, gates, s0, b_, l, reverse, final_args=None):
    nc = l // GDN_CHUNK
    cp, rp, ap = _gdn_packs(gates, b_, l, reverse)
    masks = _gdn_masks(reverse)
    final = final_args is not None
    ci = (lambda c: nc - 1 - c) if reverse else (lambda c: c)
    tok = lambda col: pl.BlockSpec((GDN_CHUNK, GDN_WIDTH), lambda b, c: (b * nc + ci(c), col))
    per_chunk = lambda shp: pl.BlockSpec((1, 1) + shp, lambda b, c: (b, ci(c), 0, 0))
    state = pl.BlockSpec((1, N_PAIRS, 2 * HEAD_DIM, HEAD_DIM), lambda b, c: (b, 0, 0, 0))
    in_specs = [tok(0), tok(1), tok(2), per_chunk((PK, 4 * N_PAIRS)), per_chunk((N_PAIRS, PK)),
                per_chunk((2 * HEAD_DIM, N_PAIRS)), state,
                pl.BlockSpec(masks.shape, lambda b, c: (0, 0, 0))]
    args = [qkv, qkv, qkv, cp, rp, ap, s0, masks]
    if final:
        o_prev, y_main, g_on = final_args
        in_specs += [tok(0), tok(N_QKV // GDN_WIDTH), pl.BlockSpec((1, HEAD_DIM), lambda b, c: (0, 0))]
        args += [o_prev, y_main, g_on.reshape(1, HEAD_DIM)]
    return pl.pallas_call(
        functools.partial(_gdn_scan_kernel, final=final),
        grid=(b_, nc),
        in_specs=in_specs,
        out_specs=[tok(0), state],
        out_shape=[jax.ShapeDtypeStruct((b_ * l, GDN_WIDTH), BF16 if final else F32),
                   jax.ShapeDtypeStruct(s0.shape, F32)],
        scratch_shapes=[pltpu.VMEM((N_PAIRS, 2 * HEAD_DIM, HEAD_DIM), F32)],
        compiler_params=_cparams(("arbitrary", "arbitrary")),
        name="gdn_scan_bwd" if reverse else "gdn_scan_fwd",
    )(*args)


def _gdn_mixer(yx, sx, yc, sc, conv_w, a_log, dt_bias, g_on, b_, l, n_ctx):
    nh = GDN_HEADS
    conv_w8 = jnp.pad(conv_w, ((0, HALO - CONV_K), (0, 0)))
    ea_row = jnp.zeros((1, N_SMALL), F32).at[0, 2 * nh:4 * nh].set(jnp.exp(a_log.reshape(-1)))
    dt_row = jnp.zeros((1, N_SMALL), F32).at[0, 2 * nh:4 * nh].set(dt_bias.reshape(-1))
    qkv_c, gates_c = _gdn_prep(yc, sc, conv_w8, ea_row, dt_row, b_, n_ctx)
    qkv_x, gates_x = _gdn_prep(yx, sx, conv_w8, ea_row, dt_row, b_, l)
    zero = jnp.zeros((b_, N_PAIRS, 2 * HEAD_DIM, HEAD_DIM), F32)
    _, s_f = _gdn_scan(qkv_c, gates_c, zero, b_, n_ctx, False)
    _, s_b = _gdn_scan(qkv_c, gates_c, zero, b_, n_ctx, True)
    o_f, _ = _gdn_scan(qkv_x, gates_x, s_f, b_, l, False)
    ya, _ = _gdn_scan(qkv_x, gates_x, s_b, b_, l, True, final_args=(o_f, yx, g_on))
    return ya


SWA_GROUP = SWA_HEADS // SWA_KV_HEADS
ROT = HEAD_DIM // 4


def _rope_tables(l):
    half = HEAD_DIM // 2
    inv = ROPE_BASE ** (-jnp.arange(0, half, 2, dtype=F32) / half)
    pos = jnp.arange(l, dtype=jnp.int32)
    ang_r = (pos // GRID_W).astype(F32)[:, None] * inv
    ang_c = (pos % GRID_W).astype(F32)[:, None] * inv
    zero = jnp.zeros_like(ang_r)
    cos = jnp.concatenate([jnp.cos(ang_r), jnp.cos(ang_r), jnp.cos(ang_c), jnp.cos(ang_c)], axis=1)
    sin_up = jnp.concatenate([-jnp.sin(ang_r), zero, -jnp.sin(ang_c), zero], axis=1)
    sin_dn = jnp.concatenate([zero, jnp.sin(ang_r), zero, jnp.sin(ang_c)], axis=1)
    return cos, sin_up, sin_dn


def _swa_prep_kernel(q_ref, k_ref, v_ref, cos_ref, su_ref, sd_ref, gq_ref, gk_ref, qo_ref, ko_ref, vo_ref):
    cos, su, sd = cos_ref[...], su_ref[...], sd_ref[...]

    def norm_rope(t, g, scale):
        y = t * lax.rsqrt(jnp.mean(t * t, axis=-1, keepdims=True) + EPS) * g
        y = y * cos + pltpu.roll(y, HEAD_DIM - ROT, 1) * su + pltpu.roll(y, ROT, 1) * sd
        return y * scale if scale != 1.0 else y

    for h in range(SWA_HEADS):
        c = pl.ds(h * HEAD_DIM, HEAD_DIM)
        qo_ref[:, c] = norm_rope(q_ref[:, c], gq_ref[...], HEAD_DIM ** -0.5).astype(qo_ref.dtype)
    for h in range(SWA_KV_HEADS):
        c = pl.ds(h * HEAD_DIM, HEAD_DIM)
        ko_ref[:, c] = norm_rope(k_ref[:, c], gk_ref[...], 1.0).astype(ko_ref.dtype)
    vo_ref[...] = v_ref[...].astype(vo_ref.dtype)


def _swa_prep(y_main, tables, g_q, g_k, rows, tm):
    q_blk = (N_QKV + GDN_WIDTH) // SWA_WIDTH
    k_blk = (N_QKV + GDN_WIDTH + SWA_WIDTH) // SWA_KV_WIDTH
    tpb = tables[0].shape[0] // tm
    tab = pl.BlockSpec((tm, HEAD_DIM), lambda i: (i % tpb, 0))
    vec = pl.BlockSpec((1, HEAD_DIM), lambda i: (0, 0))
    return pl.pallas_call(
        _swa_prep_kernel,
        grid=(rows // tm,),
        in_specs=[pl.BlockSpec((tm, SWA_WIDTH), lambda i: (i, q_blk)),
                  pl.BlockSpec((tm, SWA_KV_WIDTH), lambda i: (i, k_blk)),
                  pl.BlockSpec((tm, SWA_KV_WIDTH), lambda i: (i, k_blk + 1)),
                  tab, tab, tab, vec, vec],
        out_specs=[pl.BlockSpec((tm, SWA_WIDTH), lambda i: (i, 0)),
                   pl.BlockSpec((tm, SWA_KV_WIDTH), lambda i: (i, 0)),
                   pl.BlockSpec((tm, SWA_KV_WIDTH), lambda i: (i, 0))],
        out_shape=[jax.ShapeDtypeStruct((rows, SWA_WIDTH), BF16),
                   jax.ShapeDtypeStruct((rows, SWA_KV_WIDTH), BF16),
                   jax.ShapeDtypeStruct((rows, SWA_KV_WIDTH), BF16)],
        compiler_params=_cparams(("arbitrary",)),
        name="swa_prep",
    )(y_main, y_main, y_main, *tables, g_q.reshape(1, HEAD_DIM), g_k.reshape(1, HEAD_DIM))


def _swa_attn_kernel(q_ref, kp_ref, kc_ref, kn_ref, vp_ref, vc_ref, vn_ref, kx_ref, vx_ref, sink_ref, o_ref, *, n_ctx):
    n = pl.program_id(1)
    nb = pl.num_programs(1)
    rows = SWA_GROUP * Q_BLOCK
    nk = n_ctx + 3 * Q_BLOCK
    qi = lax.broadcasted_iota(jnp.int32, (rows, nk), 0) & (Q_BLOCK - 1)
    kj = lax.broadcasted_iota(jnp.int32, (rows, nk), 1) - n_ctx
    lo = jnp.where(n == 0, Q_BLOCK, 0)
    hi = jnp.where(n == nb - 1, 2 * Q_BLOCK, 3 * Q_BLOCK)
    valid = (kj < 0) | ((kj >= qi) & (kj <= qi + 2 * WINDOW) & (kj >= lo) & (kj < hi))
    hsel = lax.broadcasted_iota(jnp.int32, (rows, 1), 0) // Q_BLOCK
    nt = (((1,), (1,)), ((), ()))
    for j in range(SWA_KV_HEADS):
        c = pl.ds(j * HEAD_DIM, HEAD_DIM)
        heads = [j * SWA_GROUP + g for g in range(SWA_GROUP)]
        q = jnp.concatenate([q_ref[:, pl.ds(h * HEAD_DIM, HEAD_DIM)] for h in heads], axis=0)
        k = jnp.concatenate([kx_ref[:, c], kp_ref[:, c], kc_ref[:, c], kn_ref[:, c]], axis=0)
        v = jnp.concatenate([vx_ref[:, c], vp_ref[:, c], vc_ref[:, c], vn_ref[:, c]], axis=0)
        sink = jnp.zeros((rows, 1), F32)
        for g, h in enumerate(heads):
            sink = jnp.where(hsel == g, sink_ref[h:h + 1, 0:1], sink)
        s = jnp.where(valid, lax.dot_general(q, k, nt, preferred_element_type=F32), NEG_INF)
        m = jnp.maximum(jnp.max(s, axis=-1, keepdims=True), sink)
        p = jnp.exp(s - m)
        den = jnp.sum(p, axis=-1, keepdims=True) + jnp.exp(sink - m)
        o = jnp.dot(p.astype(BF16), v, preferred_element_type=F32) / den
        for g, h in enumerate(heads):
            o_ref[:, pl.ds(h * HEAD_DIM, HEAD_DIM)] = o[g * Q_BLOCK:(g + 1) * Q_BLOCK].astype(o_ref.dtype)


def _swa_attention(q, k, v, k_ctx, v_ctx, sink, b_, l, n_ctx):
    nb = l // Q_BLOCK
    blk = lambda w, off: pl.BlockSpec(
        (Q_BLOCK, w), lambda b, n: (b * nb + jnp.clip(n + off, 0, nb - 1), 0))
    ctx = pl.BlockSpec((n_ctx, SWA_KV_WIDTH), lambda b, n: (b, 0))
    kw = SWA_KV_WIDTH
    return pl.pallas_call(
        functools.partial(_swa_attn_kernel, n_ctx=n_ctx),
        grid=(b_, nb),
        in_specs=[blk(SWA_WIDTH, 0), blk(kw, -1), blk(kw, 0), blk(kw, 1), blk(kw, -1), blk(kw, 0), blk(kw, 1),
                  ctx, ctx, pl.BlockSpec((SWA_HEADS, HEAD_DIM), lambda b, n: (0, 0))],
        out_specs=blk(SWA_WIDTH, 0),
        out_shape=jax.ShapeDtypeStruct((b_ * l, SWA_WIDTH), BF16),
        compiler_params=_cparams(("arbitrary", "arbitrary")),
        name="swa_attention",
    )(q, k, k, k, v, v, v, k_ctx, v_ctx, jnp.broadcast_to(sink.astype(F32)[:, None], (SWA_HEADS, HEAD_DIM)))


def _swa_mixer(yx, yc, g_q, g_k, sink, b_, l, n_ctx):
    ones = jnp.ones((n_ctx, HEAD_DIM), F32)
    zeros = jnp.zeros((n_ctx, HEAD_DIM), F32)
    qx, kx, vx = _swa_prep(yx, _rope_tables(l), g_q, g_k, b_ * l, 256)
    _, kc, vc = _swa_prep(yc, (ones, zeros, zeros), g_q, g_k, b_ * n_ctx, n_ctx)
    return _swa_attention(qx, kx, vx, kc, vc, sink, b_, l, n_ctx)


def _rmsnorm(x, g):
    xf = x.astype(F32)
    y = xf * lax.rsqrt(jnp.mean(xf * xf, axis=-1, keepdims=True) + EPS)
    return (y * g.astype(F32)).astype(x.dtype)


def _l2norm(x):
    return x * lax.rsqrt(jnp.sum(x * x, axis=-1, keepdims=True) + EPS)


def _short_conv(x, w):
    c = x.shape[-1]
    y = lax.conv_general_dilated(x, w[:, None, :].astype(x.dtype), window_strides=(1,),
                                 padding=[(CONV_K // 2, CONV_K // 2)],
                                 dimension_numbers=('NWC', 'WIO', 'NWC'), feature_group_count=c)
    return jax.nn.silu(y)


def _axial_rope(t, rows, cols):
    half = HEAD_DIM // 2
    inv = ROPE_BASE ** (-jnp.arange(0, half, 2, dtype=F32) / half)

    def rot(u, pos):
        ang = pos[:, None] * inv
        cos, sin = jnp.cos(ang)[None, :, None, :], jnp.sin(ang)[None, :, None, :]
        u1, u2 = jnp.split(u.astype(F32), 2, axis=-1)
        return jnp.concatenate([u1 * cos - u2 * sin, u2 * cos + u1 * sin], axis=-1)

    return jnp.concatenate([rot(t[..., :half], rows), rot(t[..., half:], cols)], axis=-1).astype(t.dtype)


def _gated_delta_chunked(q, k, v, g, beta, s0):
    b_, l, h, _ = q.shape
    n = l // GDN_CHUNK

    def chunks(t):
        t = t.reshape((b_, n, GDN_CHUNK) + t.shape[2:])
        return jnp.moveaxis(jnp.swapaxes(t, 2, 3), 1, 0)

    qc, kc, vc, gc, bc = map(chunks, (q, k, v, g, beta))
    gcum = jnp.cumsum(gc, axis=-1)
    tri = jnp.tril(jnp.ones((GDN_CHUNK, GDN_CHUNK), bool))
    strict = jnp.tril(jnp.ones((GDN_CHUNK, GDN_CHUNK), F32), -1)
    diff = gcum[..., :, None] - gcum[..., None, :]
    decay = jnp.where(tri, jnp.exp(jnp.where(tri, diff, 0.0)), 0.0)
    kb = kc * bc[..., None]
    a_strict = jnp.einsum('nbhid,nbhjd->nbhij', kb, kc) * decay * strict
    eye = jnp.eye(GDN_CHUNK, dtype=F32)
    t_inv = lax.linalg.triangular_solve(eye + a_strict, jnp.broadcast_to(eye, a_strict.shape),
                                        left_side=True, lower=True, unit_diagonal=True)
    u = jnp.einsum('nbhij,nbhjd->nbhid', t_inv, vc * bc[..., None])
    w = jnp.einsum('nbhij,nbhjd->nbhid', t_inv, kb * jnp.exp(gcum)[..., None])
    qk = jnp.einsum('nbhid,nbhjd->nbhij', qc, kc) * decay
    q_dec = qc * jnp.exp(gcum)[..., None]
    k_dec = kc * jnp.exp(gcum[..., -1:] - gcum)[..., None]
    chunk_decay = jnp.exp(gcum[..., -1])

    def step(s, xs):
        u_n, w_n, qk_n, q_n, k_n, a_n = xs
        v_new = u_n - jnp.einsum('bhcd,bhde->bhce', w_n, s)
        o = jnp.einsum('bhcd,bhde->bhce', q_n, s) + jnp.einsum('bhij,bhje->bhie', qk_n, v_new)
        s = s * a_n[..., None, None] + jnp.einsum('bhcd,bhce->bhde', k_n, v_new)
        return s, o

    s_fin, o = lax.scan(step, s0, (u, w, qk, q_dec, k_dec, chunk_decay))
    o = jnp.swapaxes(jnp.moveaxis(o, 0, 1), 2, 3).reshape(b_, l, h, -1)
    return o, s_fin


def _gdn_inputs(qkv, b_logit, a_logit, conv_w, a_log, dt_bias):
    b_, l, _ = qkv.shape
    qkv = _short_conv(qkv, conv_w).astype(F32).reshape(b_, l, 3, GDN_HEADS, HEAD_DIM)
    q = _l2norm(qkv[:, :, 0]) * HEAD_DIM ** -0.5
    k = _l2norm(qkv[:, :, 1])
    v = qkv[:, :, 2]
    beta = jax.nn.sigmoid(b_logit.astype(F32).reshape(b_, l, 2, GDN_HEADS))
    g = -jnp.exp(a_log.astype(F32)) * jax.nn.softplus(
        a_logit.astype(F32).reshape(b_, l, 2, GDN_HEADS) + dt_bias.astype(F32))
    return q, k, v, g, beta


def _gdn_bidir(q, k, v, g, beta, s_fwd0, s_bwd0):
    o_f, s_f = _gated_delta_chunked(q, k, v, g[:, :, 0], beta[:, :, 0], s_fwd0)
    rev = lambda t: jnp.flip(t, axis=1)
    o_b, s_b = _gated_delta_chunked(rev(q), rev(k), rev(v), rev(g[:, :, 1]), rev(beta[:, :, 1]), s_bwd0)
    return o_f + rev(o_b), s_f, s_b


def _gdn_output(o, z, g_on):
    b_, l = o.shape[:2]
    y = _rmsnorm(o, g_on).reshape(b_, l, GDN_WIDTH)
    return y * jax.nn.silu(z.astype(F32))


def _swa_inputs(q, k, v, g_q, g_k):
    b_, l, _ = q.shape
    q = _rmsnorm(q.reshape(b_, l, SWA_HEADS, HEAD_DIM), g_q)
    k = _rmsnorm(k.reshape(b_, l, SWA_KV_HEADS, HEAD_DIM), g_k)
    return q, k, v.reshape(b_, l, SWA_KV_HEADS, HEAD_DIM)


def _window_attention(q, k, v, k_ctx, v_ctx, sink):
    b_, l, h, d = q.shape
    g = h // SWA_KV_HEADS
    nb = l // Q_BLOCK
    n_ctx = k_ctx.shape[1]
    scale = d ** -0.5
    qb = q.reshape(b_, nb, Q_BLOCK, SWA_KV_HEADS, g, d)

    def band_blocks(t):
        tp = jnp.pad(t, ((0, 0), (Q_BLOCK, Q_BLOCK), (0, 0), (0, 0))).reshape(b_, nb + 2, Q_BLOCK, SWA_KV_HEADS, d)
        return jnp.concatenate([tp[:, :-2], tp[:, 1:-1], tp[:, 2:]], axis=2)

    kw, vw = band_blocks(k), band_blocks(v)
    qi = jnp.arange(Q_BLOCK)[:, None]
    kj = jnp.arange(3 * Q_BLOCK)[None, :]
    band = jnp.abs(kj - Q_BLOCK - qi) <= WINDOW
    kpos = jnp.arange(nb)[:, None] * Q_BLOCK - Q_BLOCK + jnp.arange(3 * Q_BLOCK)[None, :]
    valid = band[None] & ((kpos >= 0) & (kpos < l))[:, None, :]
    s_win = jnp.einsum('bnqkgd,bnjkd->bnkgqj', qb, kw).astype(F32) * scale
    s_win = jnp.where(valid[None, :, None, None], s_win, NEG_INF)
    s_ctx = jnp.einsum('bnqkgd,bckd->bnkgqc', qb, k_ctx).astype(F32) * scale
    s_sink = jnp.broadcast_to(sink.astype(F32).reshape(SWA_KV_HEADS, g, 1, 1), s_ctx.shape[:-1] + (1,))
    p = jax.nn.softmax(jnp.concatenate([s_ctx, s_win, s_sink], axis=-1), axis=-1).astype(v.dtype)
    o = (jnp.einsum('bnkgqc,bckd->bnqkgd', p[..., :n_ctx], v_ctx)
         + jnp.einsum('bnkgqj,bnjkd->bnqkgd', p[..., n_ctx:n_ctx + 3 * Q_BLOCK], vw))
    return o.reshape(b_, l, h * d)


def _split_main(y):
    o = np.cumsum((0, 3 * GDN_WIDTH, GDN_WIDTH, SWA_WIDTH, SWA_KV_WIDTH, SWA_KV_WIDTH))
    return tuple(y[..., int(o[n]):int(o[n + 1])] for n in range(5))


def kernel(x, c, ctx, c_ctx, w_ada, b_ada, g_norm1, g_norm2, w_in, conv_qkv, a_log, dt_bias, g_onorm, g_qnorm,
           g_knorm, sink, w_out, w_router_grp, b_router_grp, w_router_exp, b_router_exp, w_gate, w_up, w_down):
    b_, l, d = x.shape
    n_ctx = ctx.shape[1]
    t = b_ * l
    assert w_ada.shape[0] == 1 and d == D_MODEL and b_ + 1 <= MOD_ROWS
    rows = jnp.repeat(jnp.arange(l // GRID_W, dtype=F32), GRID_W)
    cols = jnp.tile(jnp.arange(GRID_W, dtype=F32), l // GRID_W)

    wi = w_in[0]
    w_main = jnp.concatenate([wi[:, IN_OFFS[0]:IN_OFFS[2]], wi[:, IN_OFFS[4]:IN_OFFS[7]]], axis=1).astype(BF16)
    w_small = jnp.pad(wi[:, IN_OFFS[2]:IN_OFFS[4]], ((0, 0), (0, N_SMALL - 4 * GDN_HEADS))).astype(BF16)
    wo = w_out[0].astype(BF16)
    w_router = jnp.pad(jnp.concatenate([w_router_grp[0], w_router_exp[0]], axis=1),
                       ((0, 0), (0, N_ROUTER - N_GROUPS - N_EXPERTS))).astype(BF16)
    b_router = jnp.pad(jnp.concatenate([b_router_grp[0], b_router_exp[0]]),
                       (0, N_ROUTER - N_GROUPS - N_EXPERTS)).reshape(1, N_ROUTER)

    c_rows = jnp.zeros((MOD_ROWS, d), F32).at[:b_].set(c).at[b_].set(c_ctx)
    mod = _modulation(c_rows, w_ada[0], b_ada[0])
    mod3 = mod.reshape(MOD_ROWS * 6, 1, d)

    tm = 512
    tpb = l // tm
    yx, sx = _in_projection(x.reshape(t, d), g_norm1[0], mod3, lambda i: i // tpb, w_main, w_small, tm)
    yc, sc = _in_projection(ctx.reshape(b_ * n_ctx, d), g_norm1[0], mod3, lambda i: b_, w_main, w_small, n_ctx)
    ya_x = _gdn_mixer(yx, sx, yc, sc, conv_qkv[0], a_log[0], dt_bias[0], g_onorm[0], b_, l, n_ctx)
    yb_x = _swa_mixer(yx, yc, g_qnorm[0], g_knorm[0], sink[0], b_, l, n_ctx)

    x1 = _out_projection(ya_x, yb_x, wo[:GDN_WIDTH], wo[GDN_WIDTH:], x.reshape(t, d), mod3, tpb, tm)

    tm2 = 256
    h2c, ids, gates = _norm2_router(x1, g_norm2[0], mod3, l // tm2, w_router, b_router, tm2)
    slot_tok, slot_of, blk_expert, n_used = _slots(ids[:, :TOP_K], t)
    yc_moe = _moe_experts(h2c, slot_tok, blk_expert, n_used, w_gate[0], w_up[0], w_down[0])
    return _moe_combine(yc_moe, slot_of, x1, gates, mod3, l // COMBINE_TM).reshape(b_, l, d)
```

```python
import functools
import math

import jax
import jax.numpy as jnp
import numpy as np
from jax import lax
from jax.experimental import pallas as pl
from jax.experimental.pallas import tpu as pltpu

F32 = jnp.float32
BF16 = jnp.bfloat16

D_MODEL = 4096
CTX_LEN = 256
GRID_W = 64
HEAD_DIM = 128
GDN_HEADS = 16
GDN_WIDTH = GDN_HEADS * HEAD_DIM
GDN_CHUNK = 64
CONV_K = 5
SWA_HEADS = 16
SWA_KV_HEADS = 4
SWA_WIDTH = SWA_HEADS * HEAD_DIM
SWA_KV_WIDTH = SWA_KV_HEADS * HEAD_DIM
WINDOW = 128
Q_BLOCK = 128
ROPE_BASE = 10000.0
N_GROUPS = 8
EXPERTS_PER_GROUP = 8
N_EXPERTS = N_GROUPS * EXPERTS_PER_GROUP
TOP_K = 2
D_EXPERT = D_MODEL // 8
EPS = 1e-6
NEG_INF = -1e30

IN_SIZES = (3 * GDN_WIDTH, GDN_WIDTH, 2 * GDN_HEADS, 2 * GDN_HEADS, SWA_WIDTH, SWA_KV_WIDTH, SWA_KV_WIDTH)
IN_OFFS = tuple(int(v) for v in np.cumsum((0,) + IN_SIZES))
N_MAIN = 3 * GDN_WIDTH + GDN_WIDTH + SWA_WIDTH + 2 * SWA_KV_WIDTH
N_SMALL = 128
N_ROUTER = 128
MOD_ROWS = 8

MOE_BM = 256
VMEM_LIMIT = 56 * 1024 * 1024


def _cparams(sem):
    return pltpu.CompilerParams(dimension_semantics=sem, vmem_limit_bytes=VMEM_LIMIT)


def _mod_kernel(c_ref, w_ref, b_ref, o_ref):
    c = c_ref[...]
    a = (c * jax.nn.sigmoid(c)).astype(BF16)
    o_ref[...] = jnp.dot(a, w_ref[...].astype(BF16), preferred_element_type=F32) + b_ref[...]


def _modulation(c_rows, w_ada, b_ada):
    d, n = w_ada.shape
    tn = 512
    return pl.pallas_call(
        _mod_kernel,
        grid=(n // tn,),
        in_specs=[pl.BlockSpec((MOD_ROWS, d), lambda j: (0, 0)),
                  pl.BlockSpec((d, tn), lambda j: (0, j)),
                  pl.BlockSpec((1, tn), lambda j: (0, j))],
        out_specs=pl.BlockSpec((MOD_ROWS, tn), lambda j: (0, j)),
        out_shape=jax.ShapeDtypeStruct((MOD_ROWS, n), F32),
        compiler_params=_cparams(("arbitrary",)),
        name="modulation",
    )(c_rows, w_ada, b_ada.reshape(1, n))


NORM_ROWS = 64


def _norm_mod_rows(x_ref, g_ref, sh_ref, sc_ref, h_ref, tm):
    g = g_ref[...]
    sc = 1.0 + sc_ref[0]
    sh = sh_ref[0]

    def body(r, carry):
        rows = pl.ds(pl.multiple_of(r * NORM_ROWS, NORM_ROWS), NORM_ROWS)
        xf = x_ref[rows, :]
        ms = jnp.mean(xf * xf, axis=-1, keepdims=True)
        y = xf * lax.rsqrt(ms + EPS) * g
        h_ref[rows, :] = (y * sc + sh).astype(h_ref.dtype)
        return carry

    lax.fori_loop(0, tm // NORM_ROWS, body, 0)


def _inproj_kernel(x_ref, g_ref, sh_ref, sc_ref, w_ref, ws_ref, o_ref, os_ref, h_ref, *, tm):
    @pl.when(pl.program_id(1) == 0)
    def _():
        _norm_mod_rows(x_ref, g_ref, sh_ref, sc_ref, h_ref, tm)
        os_ref[...] = jnp.dot(h_ref[...], ws_ref[...], preferred_element_type=F32)

    o_ref[...] = jnp.dot(h_ref[...], w_ref[...], preferred_element_type=F32)


def _in_projection(x2d, g_norm, mod3, mod_row_of_tile, w_main, w_small, tm):
    t, d = x2d.shape
    tn = 1024
    return pl.pallas_call(
        functools.partial(_inproj_kernel, tm=tm),
        grid=(t // tm, N_MAIN // tn),
        in_specs=[pl.BlockSpec((tm, d), lambda i, j: (i, 0)),
                  pl.BlockSpec((1, d), lambda i, j: (0, 0)),
                  pl.BlockSpec((1, 1, d), lambda i, j: (mod_row_of_tile(i) * 6 + 0, 0, 0)),
                  pl.BlockSpec((1, 1, d), lambda i, j: (mod_row_of_tile(i) * 6 + 1, 0, 0)),
                  pl.BlockSpec((d, tn), lambda i, j: (0, j)),
                  pl.BlockSpec((d, N_SMALL), lambda i, j: (0, 0))],
        out_specs=[pl.BlockSpec((tm, tn), lambda i, j: (i, j)),
                   pl.BlockSpec((tm, N_SMALL), lambda i, j: (i, 0))],
        out_shape=[jax.ShapeDtypeStruct((t, N_MAIN), F32),
                   jax.ShapeDtypeStruct((t, N_SMALL), F32)],
        scratch_shapes=[pltpu.VMEM((tm, d), BF16)],
        compiler_params=_cparams(("arbitrary", "arbitrary")),
        name="in_projection",
    )(x2d, g_norm.reshape(1, d), mod3, mod3, w_main, w_small)


def _outproj_kernel(ya_ref, yb_ref, wa_ref, wb_ref, x_ref, gate_ref, o_ref):
    acc = jnp.dot(ya_ref[...], wa_ref[...], preferred_element_type=F32)
    acc = acc + jnp.dot(yb_ref[...], wb_ref[...], preferred_element_type=F32)
    o_ref[...] = x_ref[...] + gate_ref[0] * acc


def _out_projection(ya, yb, wa, wb, x2d, mod3, tiles_per_batch, tm):
    t, d = x2d.shape
    tn = 1024
    nj = d // tn
    ka, kb = ya.shape[1], yb.shape[1]
    return pl.pallas_call(
        _outproj_kernel,
        grid=(t // tm, nj),
        in_specs=[pl.BlockSpec((tm, ka), lambda i, j: (i, 0)),
                  pl.BlockSpec((tm, kb), lambda i, j: (i, 0)),
                  pl.BlockSpec((ka, tn), lambda i, j: (0, j)),
                  pl.BlockSpec((kb, tn), lambda i, j: (0, j)),
                  pl.BlockSpec((tm, tn), lambda i, j: (i, j)),
                  pl.BlockSpec((1, 1, tn), lambda i, j: (((i // tiles_per_batch) * 6 + 2) * nj + j, 0, 0))],
        out_specs=pl.BlockSpec((tm, tn), lambda i, j: (i, j)),
        out_shape=jax.ShapeDtypeStruct((t, d), F32),
        compiler_params=_cparams(("arbitrary", "arbitrary")),
        name="out_projection",
    )(ya, yb, wa, wb, x2d, mod3.reshape(-1, 1, tn))


def _route_rows(lg):
    col = lax.broadcasted_iota(jnp.int32, lg.shape, 1)
    first = lambda hit: jnp.min(jnp.where(hit, col, N_ROUTER), axis=-1, keepdims=True)
    gm = col < N_GROUPS
    mg = jnp.max(jnp.where(gm, lg, NEG_INF), axis=-1, keepdims=True)
    grp = first(gm & (lg == mg))
    p_grp = 1.0 / jnp.sum(jnp.where(gm, jnp.exp(lg - mg), 0.0), axis=-1, keepdims=True)
    lo = N_GROUPS + grp * EXPERTS_PER_GROUP
    em = (col >= lo) & (col < lo + EXPERTS_PER_GROUP)
    m1 = jnp.max(jnp.where(em, lg, NEG_INF), axis=-1, keepdims=True)
    i1 = first(em & (lg == m1))
    em2 = em & (col != i1)
    m2 = jnp.max(jnp.where(em2, lg, NEG_INF), axis=-1, keepdims=True)
    i2 = first(em2 & (lg == m2))
    e2 = jnp.exp(m2 - m1)
    g1 = p_grp / (1.0 + e2)
    ids = jnp.where(col == 0, i1 - N_GROUPS, jnp.where(col == 1, i2 - N_GROUPS, 0))
    gates = jnp.where(col == 0, g1, jnp.where(col == 1, g1 * e2, 0.0))
    return ids, gates


LANES = 128
HALF_D = D_MODEL // 2
ROW_CH = HALF_D // LANES
ROW_PITCH = 24
U32 = jnp.uint32
HI_MASK = 0xFFFF0000
DMA_UNROLL = 8


def _pack_pair(lo, hi):
    bits = lambda v: lax.bitcast_convert_type(v.astype(BF16).astype(F32), U32)
    return (bits(hi) & U32(HI_MASK)) | (bits(lo) >> 16)


def _unpack_pair(w):
    return lax.bitcast_convert_type(w << 16, F32), lax.bitcast_convert_type(w & U32(HI_MASK), F32)


def _store_chunked(dst_ref, row0, vals):
    n = vals.shape[0]
    for j in range(ROW_PITCH):
        if j < ROW_CH:
            piece = _pack_pair(vals[:, j * LANES:(j + 1) * LANES], vals[:, HALF_D + j * LANES:HALF_D + (j + 1) * LANES])
        else:
            piece = jnp.zeros((n, LANES), U32)
        dst_ref[pl.ds(row0 * ROW_PITCH + j, n, stride=ROW_PITCH), :] = piece


def _load_chunk(src_ref, row0, n, j):
    return _unpack_pair(src_ref[pl.ds(row0 * ROW_PITCH + j, n, stride=ROW_PITCH), :])


def _row_copy(src_ref, dst_ref, src_row, dst_row, sem):
    return pltpu.make_async_copy(src_ref.at[pl.ds(pl.multiple_of(src_row * ROW_PITCH, 8), ROW_CH), :],
                                 dst_ref.at[pl.ds(pl.multiple_of(dst_row * ROW_PITCH, 8), ROW_CH), :], sem)


def _gather_rows(src_hbm, dst_ref, src_row_of, n, sem, dst_row0=0):
    def body(r, carry):
        _row_copy(src_hbm, dst_ref, src_row_of(r), dst_row0 + r, sem).start()
        return carry

    lax.fori_loop(0, n, body, 0, unroll=DMA_UNROLL)


def _wait_rows(src_hbm, dst_ref, n, sem):
    def body(r, carry):
        _row_copy(src_hbm, dst_ref, 0, r, sem).wait()
        return carry

    lax.fori_loop(0, n, body, 0, unroll=DMA_UNROLL)


def _norm2_kernel(x_ref, g_ref, sh_ref, sc_ref, wr_ref, br_ref, hc_ref, id_ref, gt_ref, hb_ref, *, tm):
    g = g_ref[...]
    sc = 1.0 + sc_ref[0]
    sh = sh_ref[0]

    def body(r, carry):
        row0 = pl.multiple_of(r * NORM_ROWS, NORM_ROWS)
        xf = x_ref[pl.ds(row0, NORM_ROWS), :]
        ms = jnp.mean(xf * xf, axis=-1, keepdims=True)
        h = xf * lax.rsqrt(ms + EPS) * g * sc + sh
        hb_ref[pl.ds(row0, NORM_ROWS), :] = h.astype(BF16)
        _store_chunked(hc_ref, row0, h)
        return carry

    lax.fori_loop(0, tm // NORM_ROWS, body, 0)
    lg = jnp.dot(hb_ref[...], wr_ref[...], preferred_element_type=F32) + br_ref[...]
    id_ref[...], gt_ref[...] = _route_rows(lg)


def _norm2_router(x2d, g_norm, mod3, tiles_per_batch, w_router, b_router, tm):
    t, d = x2d.shape
    return pl.pallas_call(
        functools.partial(_norm2_kernel, tm=tm),
        grid=(t // tm,),
        in_specs=[pl.BlockSpec((tm, d), lambda i: (i, 0)),
                  pl.BlockSpec((1, d), lambda i: (0, 0)),
                  pl.BlockSpec((1, 1, d), lambda i: ((i // tiles_per_batch) * 6 + 3, 0, 0)),
                  pl.BlockSpec((1, 1, d), lambda i: ((i // tiles_per_batch) * 6 + 4, 0, 0)),
                  pl.BlockSpec((d, N_ROUTER), lambda i: (0, 0)),
                  pl.BlockSpec((1, N_ROUTER), lambda i: (0, 0))],
        out_specs=[pl.BlockSpec((tm * ROW_PITCH, LANES), lambda i: (i, 0)),
                   pl.BlockSpec((tm, N_ROUTER), lambda i: (i, 0)),
                   pl.BlockSpec((tm, N_ROUTER), lambda i: (i, 0))],
        out_shape=[jax.ShapeDtypeStruct((t * ROW_PITCH, LANES), U32),
                   jax.ShapeDtypeStruct((t, N_ROUTER), jnp.int32),
                   jax.ShapeDtypeStruct((t, N_ROUTER), F32)],
        scratch_shapes=[pltpu.VMEM((tm, d), BF16)],
        compiler_params=_cparams(("arbitrary",)),
        name="norm2_router",
    )(x2d, g_norm.reshape(1, d), mod3, mod3, w_router, b_router)


CAST_ROWS = 128


def _cast_rows(src_ref, dst_ref):
    n = src_ref.shape[1]

    def body(r, carry):
        rows = pl.ds(pl.multiple_of(r * CAST_ROWS, CAST_ROWS), CAST_ROWS)
        dst_ref[rows, :] = src_ref[0, rows, :].astype(dst_ref.dtype)
        return carry

    lax.fori_loop(0, n // CAST_ROWS, body, 0)


def _expert_changed(be_ref, i):
    return (i == 0) | (be_ref[i] != be_ref[jnp.maximum(i - 1, 0)])


def _moe_up_kernel(be_ref, nu_ref, st_ref, h_hbm, wg_ref, wu_ref, o_ref, wgb_ref, wub_ref, xg_ref, xb_ref, sem):
    i = pl.program_id(0)
    nu = nu_ref[0]

    def start_gather(blk):
        _gather_rows(h_hbm, xg_ref, lambda r: st_ref[blk * MOE_BM + r], MOE_BM, sem)

    @pl.when(i == 0)
    def _():
        start_gather(0)

    @pl.when(_expert_changed(be_ref, i))
    def _():
        _cast_rows(wg_ref, wgb_ref)
        _cast_rows(wu_ref, wub_ref)

    @pl.when(i < nu)
    def _():
        _wait_rows(h_hbm, xg_ref, MOE_BM, sem)
        for j in range(ROW_CH):
            lo, hi = _load_chunk(xg_ref, 0, MOE_BM, j)
            xb_ref[:, pl.ds(j * LANES, LANES)] = lo.astype(BF16)
            xb_ref[:, pl.ds(HALF_D + j * LANES, LANES)] = hi.astype(BF16)

        @pl.when(i + 1 < nu)
        def _():
            start_gather(i + 1)

        xb = xb_ref[...]
        g = jnp.dot(xb, wgb_ref[...], preferred_element_type=F32)
        u = jnp.dot(xb, wub_ref[...], preferred_element_type=F32)
        o_ref[...] = (g * jax.nn.sigmoid(g) * u).astype(o_ref.dtype)

    @pl.when(i >= nu)
    def _():
        o_ref[...] = jnp.zeros_like(o_ref)


def _moe_down_kernel(be_ref, nu_ref, h_ref, wd_ref, o_ref, wdb_ref):
    i = pl.program_id(0)

    @pl.when(_expert_changed(be_ref, i))
    def _():
        _cast_rows(wd_ref, wdb_ref)

    @pl.when(i < nu_ref[0])
    def _():
        _store_chunked(o_ref, 0, jnp.dot(h_ref[...], wdb_ref[...], preferred_element_type=F32))

    @pl.when(i >= nu_ref[0])
    def _():
        o_ref[...] = jnp.zeros_like(o_ref)


def _moe_experts(h_chunked, slot_tok, blk_expert, n_used, w_gate, w_up, w_down):
    p = slot_tok.shape[0]
    n_blocks = p // MOE_BM
    _, d, de = w_gate.shape
    hmid = pl.pallas_call(
        _moe_up_kernel,
        grid_spec=pltpu.PrefetchScalarGridSpec(
            num_scalar_prefetch=3,
            grid=(n_blocks,),
            in_specs=[pl.BlockSpec(memory_space=pl.ANY),
                      pl.BlockSpec((1, d, de), lambda i, be, nu, st: (be[i], 0, 0)),
                      pl.BlockSpec((1, d, de), lambda i, be, nu, st: (be[i], 0, 0))],
            out_specs=pl.BlockSpec((MOE_BM, de), lambda i, be, nu, st: (i, 0)),
            scratch_shapes=[pltpu.VMEM((d, de), BF16), pltpu.VMEM((d, de), BF16),
                            pltpu.VMEM((MOE_BM * ROW_PITCH, LANES), U32), pltpu.VMEM((MOE_BM, d), BF16),
                            pltpu.SemaphoreType.DMA]),
        out_shape=jax.ShapeDtypeStruct((p, de), BF16),
        compiler_params=_cparams(("arbitrary",)),
        name="moe_gate_up",
    )(blk_expert, n_used, slot_tok, h_chunked, w_gate, w_up)
    return pl.pallas_call(
        _moe_down_kernel,
        grid_spec=pltpu.PrefetchScalarGridSpec(
            num_scalar_prefetch=2,
            grid=(n_blocks,),
            in_specs=[pl.BlockSpec((MOE_BM, de), lambda i, be, nu: (i, 0)),
                      pl.BlockSpec((1, de, d), lambda i, be, nu: (be[i], 0, 0))],
            out_specs=pl.BlockSpec((MOE_BM * ROW_PITCH, LANES), lambda i, be, nu: (i, 0)),
            scratch_shapes=[pltpu.VMEM((de, d), BF16)]),
        out_shape=jax.ShapeDtypeStruct((p * ROW_PITCH, LANES), U32),
        compiler_params=_cparams(("arbitrary",)),
        name="moe_down",
    )(blk_expert, n_used, hmid, w_down)


COMBINE_TM = 256


def _combine_kernel(so_ref, y_hbm, x_ref, gt_ref, g2_ref, o_ref, yb_ref, sem, *, tm):
    i = pl.program_id(0)
    n = pl.num_programs(0)

    def start_gather(blk, slot):
        for k in range(TOP_K):
            _gather_rows(y_hbm, yb_ref.at[slot], lambda r: so_ref[(blk * tm + r) * TOP_K + k], tm, sem.at[slot],
                         dst_row0=k * tm)

    @pl.when(i == 0)
    def _():
        start_gather(0, 0)

    @pl.when(i + 1 < n)
    def _():
        start_gather(i + 1, (i + 1) % 2)

    def combine(slot):
        _wait_rows(y_hbm, yb_ref.at[slot], TOP_K * tm, sem.at[slot])
        g0 = gt_ref[:, 0:1]
        g1 = gt_ref[:, 1:2]
        for j in range(ROW_CH):
            lo0, hi0 = _load_chunk(yb_ref.at[slot], 0, tm, j)
            lo1, hi1 = _load_chunk(yb_ref.at[slot], tm, tm, j)
            for off, y in ((j * LANES, g0 * lo0 + g1 * lo1), (HALF_D + j * LANES, g0 * hi0 + g1 * hi1)):
                o_ref[:, pl.ds(off, LANES)] = x_ref[:, pl.ds(off, LANES)] + g2_ref[0][:, off:off + LANES] * y

    for slot in range(2):
        pl.when(i % 2 == slot)(functools.partial(combine, slot))


def _moe_combine(y_chunked, slot_of, x2d, gates, mod3, tiles_per_batch):
    t, d = x2d.shape
    tm = COMBINE_TM
    return pl.pallas_call(
        functools.partial(_combine_kernel, tm=tm),
        grid_spec=pltpu.PrefetchScalarGridSpec(
            num_scalar_prefetch=1,
            grid=(t // tm,),
            in_specs=[pl.BlockSpec(memory_space=pl.ANY),
                      pl.BlockSpec((tm, d), lambda i, so: (i, 0)),
                      pl.BlockSpec((tm, N_ROUTER), lambda i, so: (i, 0)),
                      pl.BlockSpec((1, 1, d), lambda i, so: ((i // tiles_per_batch) * 6 + 5, 0, 0))],
            out_specs=pl.BlockSpec((tm, d), lambda i, so: (i, 0)),
            scratch_shapes=[pltpu.VMEM((2, TOP_K * tm * ROW_PITCH, LANES), U32),
                            pltpu.SemaphoreType.DMA((2,))]),
        out_shape=jax.ShapeDtypeStruct((t, d), F32),
        compiler_params=_cparams(("arbitrary",)),
        name="moe_combine",
    )(slot_of, y_chunked, x2d, gates, mod3)


def _slots(eid, t):
    a = t * TOP_K
    e_flat = eid.reshape(a)
    tok_flat = jnp.repeat(jnp.arange(t, dtype=jnp.int32), TOP_K)
    order = jnp.argsort(e_flat)
    e_sorted = e_flat[order]
    counts = jax.ops.segment_sum(jnp.ones((a,), jnp.int32), e_flat, num_segments=N_EXPERTS)
    padded = (counts + MOE_BM - 1) // MOE_BM * MOE_BM
    start = jnp.cumsum(counts) - counts
    pend = jnp.cumsum(padded)
    pstart = pend - padded
    dest = pstart[e_sorted] + jnp.arange(a, dtype=jnp.int32) - start[e_sorted]
    n_blocks = (a + MOE_BM - 1) // MOE_BM + N_EXPERTS
    p = n_blocks * MOE_BM
    slot_tok = jnp.zeros((p,), jnp.int32).at[dest].set(tok_flat[order])
    slot_of = jnp.zeros((a,), jnp.int32).at[order].set(dest)
    blk_expert = jnp.minimum(jnp.searchsorted(pend, jnp.arange(n_blocks, dtype=jnp.int32) * MOE_BM,
                                              side='right'), N_EXPERTS - 1).astype(jnp.int32)
    n_used = (pend[-1] // MOE_BM).astype(jnp.int32).reshape(1)
    return slot_tok, slot_of, blk_expert, n_used


GDN_TB = 256
HALO = 8
N_QKV = 3 * GDN_WIDTH


def _softplus(v):
    return jnp.maximum(v, 0.0) + jnp.log(1.0 + jnp.exp(-jnp.abs(v)))


def _split3_bf16(v):
    hi = v.astype(BF16)
    r1 = v - hi.astype(F32)
    mid = r1.astype(BF16)
    lo = (r1 - mid.astype(F32)).astype(BF16)
    return hi, mid, lo


def _gdn_prep_kernel(cur_ref, prev_ref, next_ref, sm_ref, cw_ref, ea_ref, dt_ref, o_ref, g_ref, ext_ref, *, tb):
    i = pl.program_id(1)
    nblk = pl.num_programs(1)
    def conv_cols(kind):
        def body(hh, carry):
            cols = pl.ds(pl.multiple_of((kind * GDN_HEADS + hh) * HEAD_DIM, HEAD_DIM), HEAD_DIM)
            cw = cw_ref[:, cols]
            ext_ref[pl.ds(0, HALO), :] = jnp.where(i > 0, prev_ref[:, cols], 0.0)
            ext_ref[pl.ds(HALO, tb), :] = cur_ref[:, cols]
            ext_ref[pl.ds(HALO + tb, HALO), :] = jnp.where(i < nblk - 1, next_ref[:, cols], 0.0)
            for r0 in range(0, tb, 64):
                acc = None
                for s in range(CONV_K):
                    term = ext_ref[pl.ds(HALO - CONV_K // 2 + s + r0, 64), :] * cw[s:s + 1, :]
                    acc = term if acc is None else acc + term
                y = acc * jax.nn.sigmoid(acc)
                if kind < 2:
                    y = y * lax.rsqrt(jnp.sum(y * y, axis=-1, keepdims=True) + EPS)
                if kind == 0:
                    y = y * HEAD_DIM ** -0.5
                o_ref[pl.ds(r0, 64), cols] = y.astype(o_ref.dtype)
            return carry

        lax.fori_loop(0, GDN_HEADS, body, 0)

    conv_cols(0)
    conv_cols(1)
    conv_cols(2)

    s = sm_ref[...]
    beta = jax.nn.sigmoid(s)
    g = -ea_ref[...] * _softplus(s + dt_ref[...])
    r = lax.broadcasted_iota(jnp.int32, (tb, tb), 0)
    c = lax.broadcasted_iota(jnp.int32, (tb, tb), 1)
    same = (r // GDN_CHUNK) == (c // GDN_CHUNK)
    lower = (same & (c <= r)).astype(BF16)
    upper = (same & (c >= r)).astype(BF16)
    parts = _split3_bf16(g)
    cf = sum(jnp.dot(lower, pt, preferred_element_type=F32) for pt in parts)
    cb = sum(jnp.dot(upper, pt, preferred_element_type=F32) for pt in parts)
    col = lax.broadcasted_iota(jnp.int32, s.shape, 1)
    nh = GDN_HEADS
    g_ref[...] = jnp.where(col < 2 * nh, beta, jnp.where(col < 3 * nh, cf, jnp.where(col < 4 * nh, cb, 0.0)))


def _gdn_prep(y_main, small, conv_w8, ea_row, dt_row, b_, l):
    tb = min(GDN_TB, l)
    nblk = l // tb
    hb = tb // HALO
    last = b_ * l // HALO - 1
    return pl.pallas_call(
        functools.partial(_gdn_prep_kernel, tb=tb),
        grid=(b_, nblk),
        in_specs=[pl.BlockSpec((tb, N_QKV), lambda b, i: (b * nblk + i, 0)),
                  pl.BlockSpec((HALO, N_QKV), lambda b, i: (jnp.maximum((b * nblk + i) * hb - 1, 0), 0)),
                  pl.BlockSpec((HALO, N_QKV), lambda b, i: (jnp.minimum((b * nblk + i + 1) * hb, last), 0)),
                  pl.BlockSpec((tb, N_SMALL), lambda b, i: (b * nblk + i, 0)),
                  pl.BlockSpec((HALO, N_QKV), lambda b, i: (0, 0)),
                  pl.BlockSpec((1, N_SMALL), lambda b, i: (0, 0)),
                  pl.BlockSpec((1, N_SMALL), lambda b, i: (0, 0))],
        out_specs=[pl.BlockSpec((tb, N_QKV), lambda b, i: (b * nblk + i, 0)),
                   pl.BlockSpec((tb, N_SMALL), lambda b, i: (b * nblk + i, 0))],
        out_shape=[jax.ShapeDtypeStruct((b_ * l, N_QKV), BF16),
                   jax.ShapeDtypeStruct((b_ * l, N_SMALL), F32)],
        scratch_shapes=[pltpu.VMEM((tb + 2 * HALO, HEAD_DIM), F32)],
        compiler_params=_cparams(("arbitrary", "arbitrary")),
        name="gdn_prep",
    )(y_main, y_main, y_main, small, conv_w8, ea_row, dt_row)


N_PAIRS = GDN_HEADS // 2
PK = 2 * GDN_CHUNK
INV_LEVELS = (2, 4, 8, 16, 32, 64)


def _gdn_masks(reverse):
    i = np.arange(PK)[:, None]
    j = np.arange(PK)[None, :]
    same = (i // GDN_CHUNK) == (j // GDN_CHUNK)
    strict = same & ((j > i) if reverse else (j < i))
    out = []
    for bs in INV_LEVELS:
        out.append(strict & (i // bs == j // bs) & (i // (bs // 2) != j // (bs // 2)))
    incl = same & ((j >= i) if reverse else (j <= i))
    out.append(incl)
    m = np.stack(out).astype(np.float32)
    neg = ((incl.astype(np.float32) - 1.0) * 1e30)[None]
    return jnp.asarray(np.concatenate([m, neg], axis=0))


def _gdn_scan_kernel(*refs, final):
    if final:
        (q_ref, k_ref, v_ref, cp_ref, rp_ref, ap_ref, s0_ref, mk_ref, op_ref, z_ref, gon_ref,
         o_ref, sfin_ref, s_scr) = refs
    else:
        q_ref, k_ref, v_ref, cp_ref, rp_ref, ap_ref, s0_ref, mk_ref, o_ref, sfin_ref, s_scr = refs
    c = pl.program_id(1)

    @pl.when(c == 0)
    def _():
        s_scr[...] = s0_ref[0]

    nl = len(INV_LEVELS)
    incl = mk_ref[nl]
    negm = mk_ref[nl + 1]
    ri = lax.broadcasted_iota(jnp.int32, (PK, 1), 0)
    top = (ri < GDN_CHUNK).astype(F32)
    bot = 1.0 - top
    rr = lax.broadcasted_iota(jnp.int32, (PK, PK), 0)
    cc = lax.broadcasted_iota(jnp.int32, (PK, PK), 1)
    eye = (rr == cc).astype(F32)
    cp = cp_ref[0, 0]
    rp = rp_ref[0, 0]
    ap = ap_ref[0, 0]
    dot = functools.partial(jnp.dot, preferred_element_type=F32)

    pairs = range(N_PAIRS)
    nt = (((1,), (1,)), ((), ()))
    tn = (((0,), (0,)), ((), ()))
    cols = [(pl.ds(2 * p * HEAD_DIM, HEAD_DIM), pl.ds((2 * p + 1) * HEAD_DIM, HEAD_DIM)) for p in pairs]
    pack = lambda ref, p: jnp.concatenate([ref[:, cols[p][0]], ref[:, cols[p][1]]], axis=0)
    gcol = [cp[:, p:p + 1] for p in pairs]
    bcol = [cp[:, N_PAIRS + p:N_PAIRS + p + 1] for p in pairs]
    glcol = [cp[:, 2 * N_PAIRS + p:2 * N_PAIRS + p + 1] for p in pairs]
    kp = [pack(k_ref, p) for p in pairs]
    qp = [pack(q_ref, p) for p in pairs]
    kk = [lax.dot_general(kp[p], kp[p], nt, preferred_element_type=F32) for p in pairs]
    qk = [lax.dot_general(qp[p], kp[p], nt, preferred_element_type=F32) for p in pairs]
    dec = [jnp.exp((gcol[p] - rp[p:p + 1, :]) * incl + negm) for p in pairs]
    a = [kk[p] * dec[p] * bcol[p] for p in pairs]
    qkm = [(qk[p] * dec[p]).astype(BF16) for p in pairs]

    x = [eye - a[p] * mk_ref[0] for p in pairs]
    for lv in range(1, nl):
        xb = [x[p].astype(BF16) for p in pairs]
        po = [dot(xb[p], (a[p] * mk_ref[lv]).astype(BF16)) for p in pairs]
        x = [x[p] - dot(po[p].astype(BF16), xb[p]) for p in pairs]
    tb = [x[p].astype(BF16) for p in pairs]

    egc = [jnp.exp(gcol[p]) for p in pairs]
    kf = [kp[p].astype(F32) for p in pairs]
    u = [dot(tb[p], (pack(v_ref, p).astype(F32) * bcol[p]).astype(BF16)) for p in pairs]
    w = [dot(tb[p], (kf[p] * (bcol[p] * egc[p])).astype(BF16)) for p in pairs]
    qd = [qp[p].astype(F32) * egc[p] for p in pairs]
    kd = [kf[p] * jnp.exp(glcol[p] - gcol[p]) for p in pairs]

    s = [s_scr[p] for p in pairs]
    lhs = [jnp.concatenate([jnp.concatenate([w[p] * top, w[p] * bot], axis=1),
                            jnp.concatenate([qd[p] * top, qd[p] * bot], axis=1)], axis=0).astype(BF16)
           for p in pairs]
    ws = [dot(lhs[p], s[p].astype(BF16)) for p in pairs]
    vnb = [(u[p] - ws[p][:PK]).astype(BF16) for p in pairs]
    o = [ws[p][PK:] + dot(qkm[p], vnb[p]) for p in pairs]
    kbd = [jnp.concatenate([kd[p] * top, kd[p] * bot], axis=1).astype(BF16) for p in pairs]
    kv = [lax.dot_general(kbd[p], vnb[p], tn, preferred_element_type=F32) for p in pairs]
    for p in pairs:
        s_scr[p] = jnp.exp(ap[:, p:p + 1]) * s[p] + kv[p]

    for p in pairs:
        op = o[p]
        if final:
            op = op + pack(op_ref, p)
            zz = pack(z_ref, p)
            op = op * lax.rsqrt(jnp.mean(op * op, axis=-1, keepdims=True) + EPS) * gon_ref[...]
            op = op * (zz * jax.nn.sigmoid(zz))
        o_ref[:, cols[p][0]] = op[:GDN_CHUNK].astype(o_ref.dtype)
        o_ref[:, cols[p][1]] = op[GDN_CHUNK:].astype(o_ref.dtype)

    @pl.when(c == pl.num_programs(1) - 1)
    def _():
        sfin_ref[0] = s_scr[...]


def _gdn_packs(gates, b_, l, reverse):
    nc = l // GDN_CHUNK
    nh = GDN_HEADS
    g4 = gates.reshape(b_, nc, GDN_CHUNK, N_SMALL)
    d = 1 if reverse else 0
    beta = g4[..., d * nh:(d + 1) * nh]
    gc = g4[..., (2 + d) * nh:(3 + d) * nh]
    gl = jnp.broadcast_to(gc[:, :, 0:1] if reverse else gc[:, :, GDN_CHUNK - 1:GDN_CHUNK], gc.shape)

    def rowpack(t):
        return jnp.transpose(t.reshape(b_, nc, GDN_CHUNK, N_PAIRS, 2), (0, 1, 3, 4, 2)).reshape(b_, nc, N_PAIRS, PK)

    rp = rowpack(gc)
    cp = jnp.concatenate([jnp.swapaxes(rowpack(t), 2, 3) for t in (gc, beta, gl)]
                         + [jnp.zeros((b_, nc, PK, N_PAIRS), F32)], axis=-1)
    glh = gl[:, :, 0].reshape(b_, nc, N_PAIRS, 2)
    ap = jnp.swapaxes(jnp.repeat(glh, HEAD_DIM, axis=-1), 2, 3)
    return cp, rp, ap


def _gdn_scan(qkv, gates, s0, b_, l, reverse, final_args=None):
    nc = l // GDN_CHUNK
    cp, rp, ap = _gdn_packs(gates, b_, l, reverse)
    masks = _gdn_masks(reverse)
    final = final_args is not None
    ci = (lambda c: nc - 1 - c) if reverse else (lambda c: c)
    tok = lambda col: pl.BlockSpec((GDN_CHUNK, GDN_WIDTH), lambda b, c: (b * nc + ci(c), col))
    per_chunk = lambda shp: pl.BlockSpec((1, 1) + shp, lambda b, c: (b, ci(c), 0, 0))
    state = pl.BlockSpec((1, N_PAIRS, 2 * HEAD_DIM, HEAD_DIM), lambda b, c: (b, 0, 0, 0))
    in_specs = [tok(0), tok(1), tok(2), per_chunk((PK, 4 * N_PAIRS)), per_chunk((N_PAIRS, PK)),
                per_chunk((2 * HEAD_DIM, N_PAIRS)), state,
                pl.BlockSpec(masks.shape, lambda b, c: (0, 0, 0))]
    args = [qkv, qkv, qkv, cp, rp, ap, s0, masks]
    if final:
        o_prev, y_main, g_on = final_args
        in_specs += [tok(0), tok(N_QKV // GDN_WIDTH), pl.BlockSpec((1, HEAD_DIM), lambda b, c: (0, 0))]
        args += [o_prev, y_main, g_on.reshape(1, HEAD_DIM)]
    return pl.pallas_call(
        functools.partial(_gdn_scan_kernel, final=final),
        grid=(b_, nc),
        in_specs=in_specs,
        out_specs=[tok(0), state],
        out_shape=[jax.ShapeDtypeStruct((b_ * l, GDN_WIDTH), BF16 if final else F32),
                   jax.ShapeDtypeStruct(s0.shape, F32)],
        scratch_shapes=[pltpu.VMEM((N_PAIRS, 2 * HEAD_DIM, HEAD_DIM), F32)],
        compiler_params=_cparams(("arbitrary", "arbitrary")),
        name="gdn_scan_bwd" if reverse else "gdn_scan_fwd",
    )(*args)


def _gdn_mixer(yx, sx, yc, sc, conv_w, a_log, dt_bias, g_on, b_, l, n_ctx):
    nh = GDN_HEADS
    conv_w8 = jnp.pad(conv_w, ((0, HALO - CONV_K), (0, 0)))
    ea_row = jnp.zeros((1, N_SMALL), F32).at[0, 2 * nh:4 * nh].set(jnp.exp(a_log.reshape(-1)))
    dt_row = jnp.zeros((1, N_SMALL), F32).at[0, 2 * nh:4 * nh].set(dt_bias.reshape(-1))
    qkv_c, gates_c = _gdn_prep(yc, sc, conv_w8, ea_row, dt_row, b_, n_ctx)
    qkv_x, gates_x = _gdn_prep(yx, sx, conv_w8, ea_row, dt_row, b_, l)
    zero = jnp.zeros((b_, N_PAIRS, 2 * HEAD_DIM, HEAD_DIM), F32)
    _, s_f = _gdn_scan(qkv_c, gates_c, zero, b_, n_ctx, False)
    _, s_b = _gdn_scan(qkv_c, gates_c, zero, b_, n_ctx, True)
    o_f, _ = _gdn_scan(qkv_x, gates_x, s_f, b_, l, False)
    ya, _ = _gdn_scan(qkv_x, gates_x, s_b, b_, l, True, final_args=(o_f, yx, g_on))
    return ya


SWA_GROUP = SWA_HEADS // SWA_KV_HEADS
ROT = HEAD_DIM // 4


def _rope_tables(l):
    half = HEAD_DIM // 2
    inv = ROPE_BASE ** (-jnp.arange(0, half, 2, dtype=F32) / half)
    pos = jnp.arange(l, dtype=jnp.int32)
    ang_r = (pos // GRID_W).astype(F32)[:, None] * inv
    ang_c = (pos % GRID_W).astype(F32)[:, None] * inv
    zero = jnp.zeros_like(ang_r)
    cos = jnp.concatenate([jnp.cos(ang_r), jnp.cos(ang_r), jnp.cos(ang_c), jnp.cos(ang_c)], axis=1)
    sin_up = jnp.concatenate([-jnp.sin(ang_r), zero, -jnp.sin(ang_c), zero], axis=1)
    sin_dn = jnp.concatenate([zero, jnp.sin(ang_r), zero, jnp.sin(ang_c)], axis=1)
    return cos, sin_up, sin_dn


def _swa_prep_kernel(q_ref, k_ref, v_ref, cos_ref, su_ref, sd_ref, gq_ref, gk_ref, qo_ref, ko_ref, vo_ref):
    cos, su, sd = cos_ref[...], su_ref[...], sd_ref[...]

    def norm_rope(t, g, scale):
        y = t * lax.rsqrt(jnp.mean(t * t, axis=-1, keepdims=True) + EPS) * g
        y = y * cos + pltpu.roll(y, HEAD_DIM - ROT, 1) * su + pltpu.roll(y, ROT, 1) * sd
        return y * scale if scale != 1.0 else y

    for h in range(SWA_HEADS):
        c = pl.ds(h * HEAD_DIM, HEAD_DIM)
        qo_ref[:, c] = norm_rope(q_ref[:, c], gq_ref[...], HEAD_DIM ** -0.5).astype(qo_ref.dtype)
    for h in range(SWA_KV_HEADS):
        c = pl.ds(h * HEAD_DIM, HEAD_DIM)
        ko_ref[:, c] = norm_rope(k_ref[:, c], gk_ref[...], 1.0).astype(ko_ref.dtype)
    vo_ref[...] = v_ref[...].astype(vo_ref.dtype)


def _swa_prep(y_main, tables, g_q, g_k, rows, tm):
    q_blk = (N_QKV + GDN_WIDTH) // SWA_WIDTH
    k_blk = (N_QKV + GDN_WIDTH + SWA_WIDTH) // SWA_KV_WIDTH
    tpb = tables[0].shape[0] // tm
    tab = pl.BlockSpec((tm, HEAD_DIM), lambda i: (i % tpb, 0))
    vec = pl.BlockSpec((1, HEAD_DIM), lambda i: (0, 0))
    return pl.pallas_call(
        _swa_prep_kernel,
        grid=(rows // tm,),
        in_specs=[pl.BlockSpec((tm, SWA_WIDTH), lambda i: (i, q_blk)),
                  pl.BlockSpec((tm, SWA_KV_WIDTH), lambda i: (i, k_blk)),
                  pl.BlockSpec((tm, SWA_KV_WIDTH), lambda i: (i, k_blk + 1)),
                  tab, tab, tab, vec, vec],
        out_specs=[pl.BlockSpec((tm, SWA_WIDTH), lambda i: (i, 0)),
                   pl.BlockSpec((tm, SWA_KV_WIDTH), lambda i: (i, 0)),
                   pl.BlockSpec((tm, SWA_KV_WIDTH), lambda i: (i, 0))],
        out_shape=[jax.ShapeDtypeStruct((rows, SWA_WIDTH), BF16),
                   jax.ShapeDtypeStruct((rows, SWA_KV_WIDTH), BF16),
                   jax.ShapeDtypeStruct((rows, SWA_KV_WIDTH), BF16)],
        compiler_params=_cparams(("arbitrary",)),
        name="swa_prep",
    )(y_main, y_main, y_main, *tables, g_q.reshape(1, HEAD_DIM), g_k.reshape(1, HEAD_DIM))


def _swa_attn_kernel(q_ref, kp_ref, kc_ref, kn_ref, vp_ref, vc_ref, vn_ref, kx_ref, vx_ref, sink_ref, o_ref, *, n_ctx):
    n = pl.program_id(1)
    nb = pl.num_programs(1)
    rows = SWA_GROUP * Q_BLOCK
    nk = n_ctx + 3 * Q_BLOCK
    qi = lax.broadcasted_iota(jnp.int32, (rows, nk), 0) & (Q_BLOCK - 1)
    kj = lax.broadcasted_iota(jnp.int32, (rows, nk), 1) - n_ctx
    lo = jnp.where(n == 0, Q_BLOCK, 0)
    hi = jnp.where(n == nb - 1, 2 * Q_BLOCK, 3 * Q_BLOCK)
    valid = (kj < 0) | ((kj >= qi) & (kj <= qi + 2 * WINDOW) & (kj >= lo) & (kj < hi))
    hsel = lax.broadcasted_iota(jnp.int32, (rows, 1), 0) // Q_BLOCK
    nt = (((1,), (1,)), ((), ()))
    for j in range(SWA_KV_HEADS):
        c = pl.ds(j * HEAD_DIM, HEAD_DIM)
        heads = [j * SWA_GROUP + g for g in range(SWA_GROUP)]
        q = jnp.concatenate([q_ref[:, pl.ds(h * HEAD_DIM, HEAD_DIM)] for h in heads], axis=0)
        k = jnp.concatenate([kx_ref[:, c], kp_ref[:, c], kc_ref[:, c], kn_ref[:, c]], axis=0)
        v = jnp.concatenate([vx_ref[:, c], vp_ref[:, c], vc_ref[:, c], vn_ref[:, c]], axis=0)
        sink = jnp.zeros((rows, 1), F32)
        for g, h in enumerate(heads):
            sink = jnp.where(hsel == g, sink_ref[h:h + 1, 0:1], sink)
        s = jnp.where(valid, lax.dot_general(q, k, nt, preferred_element_type=F32), NEG_INF)
        m = jnp.maximum(jnp.max(s, axis=-1, keepdims=True), sink)
        p = jnp.exp(s - m)
        den = jnp.sum(p, axis=-1, keepdims=True) + jnp.exp(sink - m)
        o = jnp.dot(p.astype(BF16), v, preferred_element_type=F32) / den
        for g, h in enumerate(heads):
            o_ref[:, pl.ds(h * HEAD_DIM, HEAD_DIM)] = o[g * Q_BLOCK:(g + 1) * Q_BLOCK].astype(o_ref.dtype)


def _swa_attention(q, k, v, k_ctx, v_ctx, sink, b_, l, n_ctx):
    nb = l // Q_BLOCK
    blk = lambda w, off: pl.BlockSpec(
        (Q_BLOCK, w), lambda b, n: (b * nb + jnp.clip(n + off, 0, nb - 1), 0))
    ctx = pl.BlockSpec((n_ctx, SWA_KV_WIDTH), lambda b, n: (b, 0))
    kw = SWA_KV_WIDTH
    return pl.pallas_call(
        functools.partial(_swa_attn_kernel, n_ctx=n_ctx),
        grid=(b_, nb),
        in_specs=[blk(SWA_WIDTH, 0), blk(kw, -1), blk(kw, 0), blk(kw, 1), blk(kw, -1), blk(kw, 0), blk(kw, 1),
                  ctx, ctx, pl.BlockSpec((SWA_HEADS, HEAD_DIM), lambda b, n: (0, 0))],
        out_specs=blk(SWA_WIDTH, 0),
        out_shape=jax.ShapeDtypeStruct((b_ * l, SWA_WIDTH), BF16),
        compiler_params=_cparams(("arbitrary", "arbitrary")),
        name="swa_attention",
    )(q, k, k, k, v, v, v, k_ctx, v_ctx, jnp.broadcast_to(sink.astype(F32)[:, None], (SWA_HEADS, HEAD_DIM)))


def _swa_mixer(yx, yc, g_q, g_k, sink, b_, l, n_ctx):
    ones = jnp.ones((n_ctx, HEAD_DIM), F32)
    zeros = jnp.zeros((n_ctx, HEAD_DIM), F32)
    qx, kx, vx = _swa_prep(yx, _rope_tables(l), g_q, g_k, b_ * l, 256)
    _, kc, vc = _swa_prep(yc, (ones, zeros, zeros), g_q, g_k, b_ * n_ctx, n_ctx)
    return _swa_attention(qx, kx, vx, kc, vc, sink, b_, l, n_ctx)


def _rmsnorm(x, g):
    xf = x.astype(F32)
    y = xf * lax.rsqrt(jnp.mean(xf * xf, axis=-1, keepdims=True) + EPS)
    return (y * g.astype(F32)).astype(x.dtype)


def _l2norm(x):
    return x * lax.rsqrt(jnp.sum(x * x, axis=-1, keepdims=True) + EPS)


def _short_conv(x, w):
    c = x.shape[-1]
    y = lax.conv_general_dilated(x, w[:, None, :].astype(x.dtype), window_strides=(1,),
                                 padding=[(CONV_K // 2, CONV_K // 2)],
                                 dimension_numbers=('NWC', 'WIO', 'NWC'), feature_group_count=c)
    return jax.nn.silu(y)


def _axial_rope(t, rows, cols):
    half = HEAD_DIM // 2
    inv = ROPE_BASE ** (-jnp.arange(0, half, 2, dtype=F32) / half)

    def rot(u, pos):
        ang = pos[:, None] * inv
        cos, sin = jnp.cos(ang)[None, :, None, :], jnp.sin(ang)[None, :, None, :]
        u1, u2 = jnp.split(u.astype(F32), 2, axis=-1)
        return jnp.concatenate([u1 * cos - u2 * sin, u2 * cos + u1 * sin], axis=-1)

    return jnp.concatenate([rot(t[..., :half], rows), rot(t[..., half:], cols)], axis=-1).astype(t.dtype)


def _gated_delta_chunked(q, k, v, g, beta, s0):
    b_, l, h, _ = q.shape
    n = l // GDN_CHUNK

    def chunks(t):
        t = t.reshape((b_, n, GDN_CHUNK) + t.shape[2:])
        return jnp.moveaxis(jnp.swapaxes(t, 2, 3), 1, 0)

    qc, kc, vc, gc, bc = map(chunks, (q, k, v, g, beta))
    gcum = jnp.cumsum(gc, axis=-1)
    tri = jnp.tril(jnp.ones((GDN_CHUNK, GDN_CHUNK), bool))
    strict = jnp.tril(jnp.ones((GDN_CHUNK, GDN_CHUNK), F32), -1)
    diff = gcum[..., :, None] - gcum[..., None, :]
    decay = jnp.where(tri, jnp.exp(jnp.where(tri, diff, 0.0)), 0.0)
    kb = kc * bc[..., None]
    a_strict = jnp.einsum('nbhid,nbhjd->nbhij', kb, kc) * decay * strict
    eye = jnp.eye(GDN_CHUNK, dtype=F32)
    t_inv = lax.linalg.triangular_solve(eye + a_strict, jnp.broadcast_to(eye, a_strict.shape),
                                        left_side=True, lower=True, unit_diagonal=True)
    u = jnp.einsum('nbhij,nbhjd->nbhid', t_inv, vc * bc[..., None])
    w = jnp.einsum('nbhij,nbhjd->nbhid', t_inv, kb * jnp.exp(gcum)[..., None])
    qk = jnp.einsum('nbhid,nbhjd->nbhij', qc, kc) * decay
    q_dec = qc * jnp.exp(gcum)[..., None]
    k_dec = kc * jnp.exp(gcum[..., -1:] - gcum)[..., None]
    chunk_decay = jnp.exp(gcum[..., -1])

    def step(s, xs):
        u_n, w_n, qk_n, q_n, k_n, a_n = xs
        v_new = u_n - jnp.einsum('bhcd,bhde->bhce', w_n, s)
        o = jnp.einsum('bhcd,bhde->bhce', q_n, s) + jnp.einsum('bhij,bhje->bhie', qk_n, v_new)
        s = s * a_n[..., None, None] + jnp.einsum('bhcd,bhce->bhde', k_n, v_new)
        return s, o

    s_fin, o = lax.scan(step, s0, (u, w, qk, q_dec, k_dec, chunk_decay))
    o = jnp.swapaxes(jnp.moveaxis(o, 0, 1), 2, 3).reshape(b_, l, h, -1)
    return o, s_fin


def _gdn_inputs(qkv, b_logit, a_logit, conv_w, a_log, dt_bias):
    b_, l, _ = qkv.shape
    qkv = _short_conv(qkv, conv_w).astype(F32).reshape(b_, l, 3, GDN_HEADS, HEAD_DIM)
    q = _l2norm(qkv[:, :, 0]) * HEAD_DIM ** -0.5
    k = _l2norm(qkv[:, :, 1])
    v = qkv[:, :, 2]
    beta = jax.nn.sigmoid(b_logit.astype(F32).reshape(b_, l, 2, GDN_HEADS))
    g = -jnp.exp(a_log.astype(F32)) * jax.nn.softplus(
        a_logit.astype(F32).reshape(b_, l, 2, GDN_HEADS) + dt_bias.astype(F32))
    return q, k, v, g, beta


def _gdn_bidir(q, k, v, g, beta, s_fwd0, s_bwd0):
    o_f, s_f = _gated_delta_chunked(q, k, v, g[:, :, 0], beta[:, :, 0], s_fwd0)
    rev = lambda t: jnp.flip(t, axis=1)
    o_b, s_b = _gated_delta_chunked(rev(q), rev(k), rev(v), rev(g[:, :, 1]), rev(beta[:, :, 1]), s_bwd0)
    return o_f + rev(o_b), s_f, s_b


def _gdn_output(o, z, g_on):
    b_, l = o.shape[:2]
    y = _rmsnorm(o, g_on).reshape(b_, l, GDN_WIDTH)
    return y * jax.nn.silu(z.astype(F32))


def _swa_inputs(q, k, v, g_q, g_k):
    b_, l, _ = q.shape
    q = _rmsnorm(q.reshape(b_, l, SWA_HEADS, HEAD_DIM), g_q)
    k = _rmsnorm(k.reshape(b_, l, SWA_KV_HEADS, HEAD_DIM), g_k)
    return q, k, v.reshape(b_, l, SWA_KV_HEADS, HEAD_DIM)


def _window_attention(q, k, v, k_ctx, v_ctx, sink):
    b_, l, h, d = q.shape
    g = h // SWA_KV_HEADS
    nb = l // Q_BLOCK
    n_ctx = k_ctx.shape[1]
    scale = d ** -0.5
    qb = q.reshape(b_, nb, Q_BLOCK, SWA_KV_HEADS, g, d)

    def band_blocks(t):
        tp = jnp.pad(t, ((0, 0), (Q_BLOCK, Q_BLOCK), (0, 0), (0, 0))).reshape(b_, nb + 2, Q_BLOCK, SWA_KV_HEADS, d)
        return jnp.concatenate([tp[:, :-2], tp[:, 1:-1], tp[:, 2:]], axis=2)

    kw, vw = band_blocks(k), band_blocks(v)
    qi = jnp.arange(Q_BLOCK)[:, None]
    kj = jnp.arange(3 * Q_BLOCK)[None, :]
    band = jnp.abs(kj - Q_BLOCK - qi) <= WINDOW
    kpos = jnp.arange(nb)[:, None] * Q_BLOCK - Q_BLOCK + jnp.arange(3 * Q_BLOCK)[None, :]
    valid = band[None] & ((kpos >= 0) & (kpos < l))[:, None, :]
    s_win = jnp.einsum('bnqkgd,bnjkd->bnkgqj', qb, kw).astype(F32) * scale
    s_win = jnp.where(valid[None, :, None, None], s_win, NEG_INF)
    s_ctx = jnp.einsum('bnqkgd,bckd->bnkgqc', qb, k_ctx).astype(F32) * scale
    s_sink = jnp.broadcast_to(sink.astype(F32).reshape(SWA_KV_HEADS, g, 1, 1), s_ctx.shape[:-1] + (1,))
    p = jax.nn.softmax(jnp.concatenate([s_ctx, s_win, s_sink], axis=-1), axis=-1).astype(v.dtype)
    o = (jnp.einsum('bnkgqc,bckd->bnqkgd', p[..., :n_ctx], v_ctx)
         + jnp.einsum('bnkgqj,bnjkd->bnqkgd', p[..., n_ctx:n_ctx + 3 * Q_BLOCK], vw))
    return o.reshape(b_, l, h * d)


def _split_main(y):
    o = np.cumsum((0, 3 * GDN_WIDTH, GDN_WIDTH, SWA_WIDTH, SWA_KV_WIDTH, SWA_KV_WIDTH))
    return tuple(y[..., int(o[n]):int(o[n + 1])] for n in range(5))


def kernel(x, c, ctx, c_ctx, w_ada, b_ada, g_norm1, g_norm2, w_in, conv_qkv, a_log, dt_bias, g_onorm, g_qnorm,
           g_knorm, sink, w_out, w_router_grp, b_router_grp, w_router_exp, b_router_exp, w_gate, w_up, w_down):
    b_, l, d = x.shape
    n_ctx = ctx.shape[1]
    t = b_ * l
    assert w_ada.shape[0] == 1 and d == D_MODEL and b_ + 1 <= MOD_ROWS
    rows = jnp.repeat(jnp.arange(l // GRID_W, dtype=F32), GRID_W)
    cols = jnp.tile(jnp.arange(GRID_W, dtype=F32), l // GRID_W)

    wi = w_in[0]
    w_main = jnp.concatenate([wi[:, IN_OFFS[0]:IN_OFFS[2]], wi[:, IN_OFFS[4]:IN_OFFS[7]]], axis=1).astype(BF16)
    w_small = jnp.pad(wi[:, IN_OFFS[2]:IN_OFFS[4]], ((0, 0), (0, N_SMALL - 4 * GDN_HEADS))).astype(BF16)
    wo = w_out[0].astype(BF16)
    w_router = jnp.pad(jnp.concatenate([w_router_grp[0], w_router_exp[0]], axis=1),
                       ((0, 0), (0, N_ROUTER - N_GROUPS - N_EXPERTS))).astype(BF16)
    b_router = jnp.pad(jnp.concatenate([b_router_grp[0], b_router_exp[0]]),
                       (0, N_ROUTER - N_GROUPS - N_EXPERTS)).reshape(1, N_ROUTER)

    c_rows = jnp.zeros((MOD_ROWS, d), F32).at[:b_].set(c).at[b_].set(c_ctx)
    mod = _modulation(c_rows, w_ada[0], b_ada[0])
    mod3 = mod.reshape(MOD_ROWS * 6, 1, d)

    tm = 512
    tpb = l // tm
    yx, sx = _in_projection(x.reshape(t, d), g_norm1[0], mod3, lambda i: i // tpb, w_main, w_small, tm)
    yc, sc = _in_projection(ctx.reshape(b_ * n_ctx, d), g_norm1[0], mod3, lambda i: b_, w_main, w_small, n_ctx)
    ya_x = _gdn_mixer(yx, sx, yc, sc, conv_qkv[0], a_log[0], dt_bias[0], g_onorm[0], b_, l, n_ctx)
    yb_x = _swa_mixer(yx, yc, g_qnorm[0], g_knorm[0], sink[0], b_, l, n_ctx)

    x1 = _out_projection(ya_x, yb_x, wo[:GDN_WIDTH], wo[GDN_WIDTH:], x.reshape(t, d), mod3, tpb, tm)

    tm2 = 256
    h2c, ids, gates = _norm2_router(x1, g_norm2[0], mod3, l // tm2, w_router, b_router, tm2)
    slot_tok, slot_of, blk_expert, n_used = _slots(ids[:, :TOP_K], t)
    yc_moe = _moe_experts(h2c, slot_tok, blk_expert, n_used, w_gate[0], w_up[0], w_down[0])
    return _moe_combine(yc_moe, slot_of, x1, gates, mod3, l // COMBINE_TM).reshape(b_, l, d)
```

```python
import functools
import math

import jax
import jax.numpy as jnp
import numpy as np
from jax import lax
from jax.experimental import pallas as pl
from jax.experimental.pallas import tpu as pltpu

F32 = jnp.float32
BF16 = jnp.bfloat16

D_MODEL = 4096
CTX_LEN = 256
GRID_W = 64
HEAD_DIM = 128
GDN_HEADS = 16
GDN_WIDTH = GDN_HEADS * HEAD_DIM
GDN_CHUNK = 64
CONV_K = 5
SWA_HEADS = 16
SWA_KV_HEADS = 4
SWA_WIDTH = SWA_HEADS * HEAD_DIM
SWA_KV_WIDTH = SWA_KV_HEADS * HEAD_DIM
WINDOW = 128
Q_BLOCK = 128
ROPE_BASE = 10000.0
N_GROUPS = 8
EXPERTS_PER_GROUP = 8
N_EXPERTS = N_GROUPS * EXPERTS_PER_GROUP
TOP_K = 2
D_EXPERT = D_MODEL // 8
EPS = 1e-6
NEG_INF = -1e30

IN_SIZES = (3 * GDN_WIDTH, GDN_WIDTH, 2 * GDN_HEADS, 2 * GDN_HEADS, SWA_WIDTH, SWA_KV_WIDTH, SWA_KV_WIDTH)
IN_OFFS = tuple(int(v) for v in np.cumsum((0,) + IN_SIZES))
N_MAIN = 3 * GDN_WIDTH + GDN_WIDTH + SWA_WIDTH + 2 * SWA_KV_WIDTH
N_SMALL = 128
N_ROUTER = 128
MOD_ROWS = 8

MOE_BM = 256
VMEM_LIMIT = 56 * 1024 * 1024


def _cparams(sem):
    return pltpu.CompilerParams(dimension_semantics=sem, vmem_limit_bytes=VMEM_LIMIT)


def _mod_kernel(c_ref, w_ref, b_ref, o_ref):
    c = c_ref[...]
    a = (c * jax.nn.sigmoid(c)).astype(BF16)
    o_ref[...] = jnp.dot(a, w_ref[...].astype(BF16), preferred_element_type=F32) + b_ref[...]


def _modulation(c_rows, w_ada, b_ada):
    d, n = w_ada.shape
    tn = 512
    return pl.pallas_call(
        _mod_kernel,
        grid=(n // tn,),
        in_specs=[pl.BlockSpec((MOD_ROWS, d), lambda j: (0, 0)),
                  pl.BlockSpec((d, tn), lambda j: (0, j)),
                  pl.BlockSpec((1, tn), lambda j: (0, j))],
        out_specs=pl.BlockSpec((MOD_ROWS, tn), lambda j: (0, j)),
        out_shape=jax.ShapeDtypeStruct((MOD_ROWS, n), F32),
        compiler_params=_cparams(("arbitrary",)),
        name="modulation",
    )(c_rows, w_ada, b_ada.reshape(1, n))


NORM_ROWS = 64


def _norm_mod_rows(x_ref, g_ref, sh_ref, sc_ref, h_ref, tm):
    g = g_ref[...]
    sc = 1.0 + sc_ref[0]
    sh = sh_ref[0]

    def body(r, carry):
        rows = pl.ds(pl.multiple_of(r * NORM_ROWS, NORM_ROWS), NORM_ROWS)
        xf = x_ref[rows, :]
        ms = jnp.mean(xf * xf, axis=-1, keepdims=True)
        y = xf * lax.rsqrt(ms + EPS) * g
        h_ref[rows, :] = (y * sc + sh).astype(h_ref.dtype)
        return carry

    lax.fori_loop(0, tm // NORM_ROWS, body, 0)


def _inproj_kernel(x_ref, g_ref, sh_ref, sc_ref, w_ref, ws_ref, o_ref, os_ref, h_ref, *, tm):
    @pl.when(pl.program_id(1) == 0)
    def _():
        _norm_mod_rows(x_ref, g_ref, sh_ref, sc_ref, h_ref, tm)
        os_ref[...] = jnp.dot(h_ref[...], ws_ref[...], preferred_element_type=F32)

    o_ref[...] = jnp.dot(h_ref[...], w_ref[...], preferred_element_type=F32)


def _in_projection(x2d, g_norm, mod3, mod_row_of_tile, w_main, w_small, tm):
    t, d = x2d.shape
    tn = 1024
    return pl.pallas_call(
        functools.partial(_inproj_kernel, tm=tm),
        grid=(t // tm, N_MAIN // tn),
        in_specs=[pl.BlockSpec((tm, d), lambda i, j: (i, 0)),
                  pl.BlockSpec((1, d), lambda i, j: (0, 0)),
                  pl.BlockSpec((1, 1, d), lambda i, j: (mod_row_of_tile(i) * 6 + 0, 0, 0)),
                  pl.BlockSpec((1, 1, d), lambda i, j: (mod_row_of_tile(i) * 6 + 1, 0, 0)),
                  pl.BlockSpec((d, tn), lambda i, j: (0, j)),
                  pl.BlockSpec((d, N_SMALL), lambda i, j: (0, 0))],
        out_specs=[pl.BlockSpec((tm, tn), lambda i, j: (i, j)),
                   pl.BlockSpec((tm, N_SMALL), lambda i, j: (i, 0))],
        out_shape=[jax.ShapeDtypeStruct((t, N_MAIN), F32),
                   jax.ShapeDtypeStruct((t, N_SMALL), F32)],
        scratch_shapes=[pltpu.VMEM((tm, d), BF16)],
        compiler_params=_cparams(("arbitrary", "arbitrary")),
        name="in_projection",
    )(x2d, g_norm.reshape(1, d), mod3, mod3, w_main, w_small)


def _outproj_kernel(ya_ref, yb_ref, wa_ref, wb_ref, x_ref, gate_ref, o_ref):
    acc = jnp.dot(ya_ref[...], wa_ref[...], preferred_element_type=F32)
    acc = acc + jnp.dot(yb_ref[...], wb_ref[...], preferred_element_type=F32)
    o_ref[...] = x_ref[...] + gate_ref[0] * acc


def _out_projection(ya, yb, wa, wb, x2d, mod3, tiles_per_batch, tm):
    t, d = x2d.shape
    tn = 1024
    nj = d // tn
    ka, kb = ya.shape[1], yb.shape[1]
    return pl.pallas_call(
        _outproj_kernel,
        grid=(t // tm, nj),
        in_specs=[pl.BlockSpec((tm, ka), lambda i, j: (i, 0)),
                  pl.BlockSpec((tm, kb), lambda i, j: (i, 0)),
                  pl.BlockSpec((ka, tn), lambda i, j: (0, j)),
                  pl.BlockSpec((kb, tn), lambda i, j: (0, j)),
                  pl.BlockSpec((tm, tn), lambda i, j: (i, j)),
                  pl.BlockSpec((1, 1, tn), lambda i, j: (((i // tiles_per_batch) * 6 + 2) * nj + j, 0, 0))],
        out_specs=pl.BlockSpec((tm, tn), lambda i, j: (i, j)),
        out_shape=jax.ShapeDtypeStruct((t, d), F32),
        compiler_params=_cparams(("arbitrary", "arbitrary")),
        name="out_projection",
    )(ya, yb, wa, wb, x2d, mod3.reshape(-1, 1, tn))


def _route_rows(lg):
    col = lax.broadcasted_iota(jnp.int32, lg.shape, 1)
    first = lambda hit: jnp.min(jnp.where(hit, col, N_ROUTER), axis=-1, keepdims=True)
    gm = col < N_GROUPS
    mg = jnp.max(jnp.where(gm, lg, NEG_INF), axis=-1, keepdims=True)
    grp = first(gm & (lg == mg))
    p_grp = 1.0 / jnp.sum(jnp.where(gm, jnp.exp(lg - mg), 0.0), axis=-1, keepdims=True)
    lo = N_GROUPS + grp * EXPERTS_PER_GROUP
    em = (col >= lo) & (col < lo + EXPERTS_PER_GROUP)
    m1 = jnp.max(jnp.where(em, lg, NEG_INF), axis=-1, keepdims=True)
    i1 = first(em & (lg == m1))
    em2 = em & (col != i1)
    m2 = jnp.max(jnp.where(em2, lg, NEG_INF), axis=-1, keepdims=True)
    i2 = first(em2 & (lg == m2))
    e2 = jnp.exp(m2 - m1)
    g1 = p_grp / (1.0 + e2)
    ids = jnp.where(col == 0, i1 - N_GROUPS, jnp.where(col == 1, i2 - N_GROUPS, 0))
    gates = jnp.where(col == 0, g1, jnp.where(col == 1, g1 * e2, 0.0))
    return ids, gates


LANES = 128
HALF_D = D_MODEL // 2
ROW_CH = HALF_D // LANES
ROW_PITCH = 24
U32 = jnp.uint32
HI_MASK = 0xFFFF0000
DMA_UNROLL = 8


def _pack_pair(lo, hi):
    bits = lambda v: lax.bitcast_convert_type(v.astype(BF16).astype(F32), U32)
    return (bits(hi) & U32(HI_MASK)) | (bits(lo) >> 16)


def _unpack_pair(w):
    return lax.bitcast_convert_type(w << 16, F32), lax.bitcast_convert_type(w & U32(HI_MASK), F32)


def _store_chunked(dst_ref, row0, vals):
    n = vals.shape[0]
    for j in range(ROW_PITCH):
        if j < ROW_CH:
            piece = _pack_pair(vals[:, j * LANES:(j + 1) * LANES], vals[:, HALF_D + j * LANES:HALF_D + (j + 1) * LANES])
        else:
            piece = jnp.zeros((n, LANES), U32)
        dst_ref[pl.ds(row0 * ROW_PITCH + j, n, stride=ROW_PITCH), :] = piece


def _load_chunk(src_ref, row0, n, j):
    return _unpack_pair(src_ref[pl.ds(row0 * ROW_PITCH + j, n, stride=ROW_PITCH), :])


def _row_copy(src_ref, dst_ref, src_row, dst_row, sem):
    return pltpu.make_async_copy(src_ref.at[pl.ds(pl.multiple_of(src_row * ROW_PITCH, 8), ROW_CH), :],
                                 dst_ref.at[pl.ds(pl.multiple_of(dst_row * ROW_PITCH, 8), ROW_CH), :], sem)


def _gather_rows(src_hbm, dst_ref, src_row_of, n, sem, dst_row0=0):
    def body(r, carry):
        _row_copy(src_hbm, dst_ref, src_row_of(r), dst_row0 + r, sem).start()
        return carry

    lax.fori_loop(0, n, body, 0, unroll=DMA_UNROLL)


def _wait_rows(src_hbm, dst_ref, n, sem):
    def body(r, carry):
        _row_copy(src_hbm, dst_ref, 0, r, sem).wait()
        return carry

    lax.fori_loop(0, n, body, 0, unroll=DMA_UNROLL)


def _norm2_kernel(x_ref, g_ref, sh_ref, sc_ref, wr_ref, br_ref, hc_ref, id_ref, gt_ref, hb_ref, *, tm):
    g = g_ref[...]
    sc = 1.0 + sc_ref[0]
    sh = sh_ref[0]

    def body(r, carry):
        row0 = pl.multiple_of(r * NORM_ROWS, NORM_ROWS)
        xf = x_ref[pl.ds(row0, NORM_ROWS), :]
        ms = jnp.mean(xf * xf, axis=-1, keepdims=True)
        h = xf * lax.rsqrt(ms + EPS) * g * sc + sh
        hb_ref[pl.ds(row0, NORM_ROWS), :] = h.astype(BF16)
        _store_chunked(hc_ref, row0, h)
        return carry

    lax.fori_loop(0, tm // NORM_ROWS, body, 0)
    lg = jnp.dot(hb_ref[...], wr_ref[...], preferred_element_type=F32) + br_ref[...]
    id_ref[...], gt_ref[...] = _route_rows(lg)


def _norm2_router(x2d, g_norm, mod3, tiles_per_batch, w_router, b_router, tm):
    t, d = x2d.shape
    return pl.pallas_call(
        functools.partial(_norm2_kernel, tm=tm),
        grid=(t // tm,),
        in_specs=[pl.BlockSpec((tm, d), lambda i: (i, 0)),
                  pl.BlockSpec((1, d), lambda i: (0, 0)),
                  pl.BlockSpec((1, 1, d), lambda i: ((i // tiles_per_batch) * 6 + 3, 0, 0)),
                  pl.BlockSpec((1, 1, d), lambda i: ((i // tiles_per_batch) * 6 + 4, 0, 0)),
                  pl.BlockSpec((d, N_ROUTER), lambda i: (0, 0)),
                  pl.BlockSpec((1, N_ROUTER), lambda i: (0, 0))],
        out_specs=[pl.BlockSpec((tm * ROW_PITCH, LANES), lambda i: (i, 0)),
                   pl.BlockSpec((tm, N_ROUTER), lambda i: (i, 0)),
                   pl.BlockSpec((tm, N_ROUTER), lambda i: (i, 0))],
        out_shape=[jax.ShapeDtypeStruct((t * ROW_PITCH, LANES), U32),
                   jax.ShapeDtypeStruct((t, N_ROUTER), jnp.int32),
                   jax.ShapeDtypeStruct((t, N_ROUTER), F32)],
        scratch_shapes=[pltpu.VMEM((tm, d), BF16)],
        compiler_params=_cparams(("arbitrary",)),
        name="norm2_router",
    )(x2d, g_norm.reshape(1, d), mod3, mod3, w_router, b_router)


CAST_ROWS = 128


def _cast_rows(src_ref, dst_ref):
    n = src_ref.shape[1]

    def body(r, carry):
        rows = pl.ds(pl.multiple_of(r * CAST_ROWS, CAST_ROWS), CAST_ROWS)
        dst_ref[rows, :] = src_ref[0, rows, :].astype(dst_ref.dtype)
        return carry

    lax.fori_loop(0, n // CAST_ROWS, body, 0)


def _expert_changed(be_ref, i):
    return (i == 0) | (be_ref[i] != be_ref[jnp.maximum(i - 1, 0)])


def _moe_up_kernel(be_ref, nu_ref, tk_ref, bf_ref, br_ref, h_hbm, wg_ref, wu_ref, o_ref, wgb_ref, wub_ref, xg_ref,
                   xb_ref, sem):
    i = pl.program_id(0)
    nu = nu_ref[0]
    pad_entry = tk_ref.shape[0] - 8

    def start_gather(blk, slot):
        first, rows = bf_ref[blk], br_ref[blk]
        token_of = lambda r: tk_ref[jnp.where(r < rows, first + r, pad_entry)]
        _gather_rows(h_hbm, xg_ref.at[slot], token_of, MOE_BM, sem.at[slot])

    @pl.when(i == 0)
    def _():
        start_gather(0, 0)

    @pl.when(i + 1 < nu)
    def _():
        start_gather(i + 1, (i + 1) % 2)

    @pl.when(_expert_changed(be_ref, i))
    def _():
        _cast_rows(wg_ref, wgb_ref)
        _cast_rows(wu_ref, wub_ref)

    def unpack_rows(slot):
        _wait_rows(h_hbm, xg_ref.at[slot], MOE_BM, sem.at[slot])
        for j in range(ROW_CH):
            lo, hi = _load_chunk(xg_ref.at[slot], 0, MOE_BM, j)
            xb_ref[:, pl.ds(j * LANES, LANES)] = lo.astype(BF16)
            xb_ref[:, pl.ds(HALF_D + j * LANES, LANES)] = hi.astype(BF16)

    @pl.when(i < nu)
    def _():
        for slot in range(2):
            pl.when(i % 2 == slot)(functools.partial(unpack_rows, slot))
        xb = xb_ref[...]
        g = jnp.dot(xb, wgb_ref[...], preferred_element_type=F32)
        u = jnp.dot(xb, wub_ref[...], preferred_element_type=F32)
        o_ref[...] = (g * jax.nn.sigmoid(g) * u).astype(o_ref.dtype)

    @pl.when(i >= nu)
    def _():
        o_ref[...] = jnp.zeros_like(o_ref)


def _moe_down_kernel(be_ref, nu_ref, h_ref, wd_ref, o_ref, wdb_ref):
    i = pl.program_id(0)

    @pl.when(_expert_changed(be_ref, i))
    def _():
        _cast_rows(wd_ref, wdb_ref)

    @pl.when(i < nu_ref[0])
    def _():
        _store_chunked(o_ref, 0, jnp.dot(h_ref[...], wdb_ref[...], preferred_element_type=F32))

    @pl.when(i >= nu_ref[0])
    def _():
        o_ref[...] = jnp.zeros_like(o_ref)


def _moe_experts(h_chunked, tok_sorted, blk_first, blk_rows, blk_expert, n_used, w_gate, w_up, w_down):
    n_blocks = blk_expert.shape[0]
    p = n_blocks * MOE_BM
    _, d, de = w_gate.shape
    hmid = pl.pallas_call(
        _moe_up_kernel,
        grid_spec=pltpu.PrefetchScalarGridSpec(
            num_scalar_prefetch=5,
            grid=(n_blocks,),
            in_specs=[pl.BlockSpec(memory_space=pl.ANY),
                      pl.BlockSpec((1, d, de), lambda i, be, *_: (be[i], 0, 0)),
                      pl.BlockSpec((1, d, de), lambda i, be, *_: (be[i], 0, 0))],
            out_specs=pl.BlockSpec((MOE_BM, de), lambda i, *_: (i, 0)),
            scratch_shapes=[pltpu.VMEM((d, de), BF16), pltpu.VMEM((d, de), BF16),
                            pltpu.VMEM((2, MOE_BM * ROW_PITCH, LANES), U32), pltpu.VMEM((MOE_BM, d), BF16),
                            pltpu.SemaphoreType.DMA((2,))]),
        out_shape=jax.ShapeDtypeStruct((p, de), BF16),
        compiler_params=_cparams(("arbitrary",)),
        name="moe_gate_up",
    )(blk_expert, n_used, tok_sorted, blk_first, blk_rows, h_chunked, w_gate, w_up)
    return pl.pallas_call(
        _moe_down_kernel,
        grid_spec=pltpu.PrefetchScalarGridSpec(
            num_scalar_prefetch=2,
            grid=(n_blocks,),
            in_specs=[pl.BlockSpec((MOE_BM, de), lambda i, be, nu: (i, 0)),
                      pl.BlockSpec((1, de, d), lambda i, be, nu: (be[i], 0, 0))],
            out_specs=pl.BlockSpec((MOE_BM * ROW_PITCH, LANES), lambda i, be, nu: (i, 0)),
            scratch_shapes=[pltpu.VMEM((de, d), BF16)]),
        out_shape=jax.ShapeDtypeStruct((p * ROW_PITCH, LANES), U32),
        compiler_params=_cparams(("arbitrary",)),
        name="moe_down",
    )(blk_expert, n_used, hmid, w_down)


COMBINE_TM = 256


def _combine_kernel(so_ref, y_hbm, x_ref, gt_ref, g2_ref, o_ref, yb_ref, sem, *, tm):
    i = pl.program_id(0)
    n = pl.num_programs(0)

    def start_gather(blk, slot):
        for k in range(TOP_K):
            _gather_rows(y_hbm, yb_ref.at[slot], lambda r: so_ref[(blk * tm + r) * TOP_K + k], tm, sem.at[slot],
                         dst_row0=k * tm)

    @pl.when(i == 0)
    def _():
        start_gather(0, 0)

    @pl.when(i + 1 < n)
    def _():
        start_gather(i + 1, (i + 1) % 2)

    def combine(slot):
        _wait_rows(y_hbm, yb_ref.at[slot], TOP_K * tm, sem.at[slot])
        g0 = gt_ref[:, 0:1]
        g1 = gt_ref[:, 1:2]
        for j in range(ROW_CH):
            lo0, hi0 = _load_chunk(yb_ref.at[slot], 0, tm, j)
            lo1, hi1 = _load_chunk(yb_ref.at[slot], tm, tm, j)
            for off, y in ((j * LANES, g0 * lo0 + g1 * lo1), (HALF_D + j * LANES, g0 * hi0 + g1 * hi1)):
                o_ref[:, pl.ds(off, LANES)] = x_ref[:, pl.ds(off, LANES)] + g2_ref[0][:, off:off + LANES] * y

    for slot in range(2):
        pl.when(i % 2 == slot)(functools.partial(combine, slot))


def _moe_combine(y_chunked, slot_of, x2d, gates, mod3, tiles_per_batch):
    t, d = x2d.shape
    tm = COMBINE_TM
    return pl.pallas_call(
        functools.partial(_combine_kernel, tm=tm),
        grid_spec=pltpu.PrefetchScalarGridSpec(
            num_scalar_prefetch=1,
            grid=(t // tm,),
            in_specs=[pl.BlockSpec(memory_space=pl.ANY),
                      pl.BlockSpec((tm, d), lambda i, so: (i, 0)),
                      pl.BlockSpec((tm, N_ROUTER), lambda i, so: (i, 0)),
                      pl.BlockSpec((1, 1, d), lambda i, so: ((i // tiles_per_batch) * 6 + 5, 0, 0))],
            out_specs=pl.BlockSpec((tm, d), lambda i, so: (i, 0)),
            scratch_shapes=[pltpu.VMEM((2, TOP_K * tm * ROW_PITCH, LANES), U32),
                            pltpu.SemaphoreType.DMA((2,))]),
        out_shape=jax.ShapeDtypeStruct((t, d), F32),
        compiler_params=_cparams(("arbitrary",)),
        name="moe_combine",
    )(slot_of, y_chunked, x2d, gates, mod3)


def _slots(eid, t):
    a = t * TOP_K
    iota = jnp.arange(a, dtype=jnp.int32)
    e_sorted, order = lax.sort((eid.reshape(a), iota), num_keys=1)
    experts = jnp.arange(N_EXPERTS, dtype=jnp.int32)
    start = jnp.searchsorted(e_sorted, experts, side='left').astype(jnp.int32)
    counts = jnp.searchsorted(e_sorted, experts, side='right').astype(jnp.int32) - start
    padded = (counts + MOE_BM - 1) // MOE_BM * MOE_BM
    pend = jnp.cumsum(padded)
    pstart = pend - padded
    dest = pstart[e_sorted] + iota - start[e_sorted]
    slot_of = lax.sort((order, dest), num_keys=1)[1]
    n_blocks = (a + MOE_BM - 1) // MOE_BM + N_EXPERTS
    blk = jnp.arange(n_blocks, dtype=jnp.int32) * MOE_BM
    blk_expert = jnp.minimum(jnp.searchsorted(pend, blk, side='right'), N_EXPERTS - 1).astype(jnp.int32)
    off = blk - pstart[blk_expert]
    blk_first = start[blk_expert] + off
    blk_rows = jnp.clip(counts[blk_expert] - off, 0, MOE_BM)
    tok_sorted = jnp.concatenate([order // TOP_K, jnp.zeros((8,), jnp.int32)])
    n_used = (pend[-1] // MOE_BM).astype(jnp.int32).reshape(1)
    return tok_sorted, blk_first, blk_rows, slot_of, blk_expert, n_used


GDN_TB = 256
HALO = 8
N_QKV = 3 * GDN_WIDTH


def _softplus(v):
    return jnp.maximum(v, 0.0) + jnp.log(1.0 + jnp.exp(-jnp.abs(v)))


def _split3_bf16(v):
    hi = v.astype(BF16)
    r1 = v - hi.astype(F32)
    mid = r1.astype(BF16)
    lo = (r1 - mid.astype(F32)).astype(BF16)
    return hi, mid, lo


def _gdn_prep_kernel(cur_ref, prev_ref, next_ref, sm_ref, cw_ref, ea_ref, dt_ref, o_ref, g_ref, ext_ref, *, tb):
    i = pl.program_id(1)
    nblk = pl.num_programs(1)
    def conv_cols(kind):
        def body(hh, carry):
            cols = pl.ds(pl.multiple_of((kind * GDN_HEADS + hh) * HEAD_DIM, HEAD_DIM), HEAD_DIM)
            cw = cw_ref[:, cols]
            ext_ref[pl.ds(0, HALO), :] = jnp.where(i > 0, prev_ref[:, cols], 0.0)
            ext_ref[pl.ds(HALO, tb), :] = cur_ref[:, cols]
            ext_ref[pl.ds(HALO + tb, HALO), :] = jnp.where(i < nblk - 1, next_ref[:, cols], 0.0)
            for r0 in range(0, tb, 64):
                acc = None
                for s in range(CONV_K):
                    term = ext_ref[pl.ds(HALO - CONV_K // 2 + s + r0, 64), :] * cw[s:s + 1, :]
                    acc = term if acc is None else acc + term
                y = acc * jax.nn.sigmoid(acc)
                if kind < 2:
                    y = y * lax.rsqrt(jnp.sum(y * y, axis=-1, keepdims=True) + EPS)
                if kind == 0:
                    y = y * HEAD_DIM ** -0.5
                o_ref[pl.ds(r0, 64), cols] = y.astype(o_ref.dtype)
            return carry

        lax.fori_loop(0, GDN_HEADS, body, 0)

    conv_cols(0)
    conv_cols(1)
    conv_cols(2)

    s = sm_ref[...]
    beta = jax.nn.sigmoid(s)
    g = -ea_ref[...] * _softplus(s + dt_ref[...])
    r = lax.broadcasted_iota(jnp.int32, (tb, tb), 0)
    c = lax.broadcasted_iota(jnp.int32, (tb, tb), 1)
    same = (r // GDN_CHUNK) == (c // GDN_CHUNK)
    lower = (same & (c <= r)).astype(BF16)
    upper = (same & (c >= r)).astype(BF16)
    parts = _split3_bf16(g)
    cf = sum(jnp.dot(lower, pt, preferred_element_type=F32) for pt in parts)
    cb = sum(jnp.dot(upper, pt, preferred_element_type=F32) for pt in parts)
    col = lax.broadcasted_iota(jnp.int32, s.shape, 1)
    nh = GDN_HEADS
    g_ref[...] = jnp.where(col < 2 * nh, beta, jnp.where(col < 3 * nh, cf, jnp.where(col < 4 * nh, cb, 0.0)))


def _gdn_prep(y_main, small, conv_w8, ea_row, dt_row, b_, l):
    tb = min(GDN_TB, l)
    nblk = l // tb
    hb = tb // HALO
    last = b_ * l // HALO - 1
    return pl.pallas_call(
        functools.partial(_gdn_prep_kernel, tb=tb),
        grid=(b_, nblk),
        in_specs=[pl.BlockSpec((tb, N_QKV), lambda b, i: (b * nblk + i, 0)),
                  pl.BlockSpec((HALO, N_QKV), lambda b, i: (jnp.maximum((b * nblk + i) * hb - 1, 0), 0)),
                  pl.BlockSpec((HALO, N_QKV), lambda b, i: (jnp.minimum((b * nblk + i + 1) * hb, last), 0)),
                  pl.BlockSpec((tb, N_SMALL), lambda b, i: (b * nblk + i, 0)),
                  pl.BlockSpec((HALO, N_QKV), lambda b, i: (0, 0)),
                  pl.BlockSpec((1, N_SMALL), lambda b, i: (0, 0)),
                  pl.BlockSpec((1, N_SMALL), lambda b, i: (0, 0))],
        out_specs=[pl.BlockSpec((tb, N_QKV), lambda b, i: (b * nblk + i, 0)),
                   pl.BlockSpec((tb, N_SMALL), lambda b, i: (b * nblk + i, 0))],
        out_shape=[jax.ShapeDtypeStruct((b_ * l, N_QKV), BF16),
                   jax.ShapeDtypeStruct((b_ * l, N_SMALL), F32)],
        scratch_shapes=[pltpu.VMEM((tb + 2 * HALO, HEAD_DIM), F32)],
        compiler_params=_cparams(("arbitrary", "arbitrary")),
        name="gdn_prep",
    )(y_main, y_main, y_main, small, conv_w8, ea_row, dt_row)


N_PAIRS = GDN_HEADS // 2
PK = 2 * GDN_CHUNK
INV_LEVELS = (2, 4, 8, 16, 32, 64)


def _gdn_masks(reverse):
    i = np.arange(PK)[:, None]
    j = np.arange(PK)[None, :]
    same = (i // GDN_CHUNK) == (j // GDN_CHUNK)
    strict = same & ((j > i) if reverse else (j < i))
    out = []
    for bs in INV_LEVELS:
        out.append(strict & (i // bs == j // bs) & (i // (bs // 2) != j // (bs // 2)))
    incl = same & ((j >= i) if reverse else (j <= i))
    out.append(incl)
    m = np.stack(out).astype(np.float32)
    neg = ((incl.astype(np.float32) - 1.0) * 1e30)[None]
    return jnp.asarray(np.concatenate([m, neg], axis=0))


def _gdn_scan_kernel(qf_ref, kf_ref, vf_ref, qb_ref, kb_ref, vb_ref, cpf_ref, rpf_ref, apf_ref, cpb_ref, rpb_ref,
                     apb_ref, s0_ref, mk_ref, of_ref, ob_ref, sfin_ref, s_scr):
    c = pl.program_id(1)

    @pl.when(c == 0)
    def _():
        s_scr[...] = s0_ref[0]

    nl = len(INV_LEVELS)
    ri = lax.broadcasted_iota(jnp.int32, (PK, 1), 0)
    top = (ri < GDN_CHUNK).astype(F32)
    bot = 1.0 - top
    rr = lax.broadcasted_iota(jnp.int32, (PK, PK), 0)
    cc = lax.broadcasted_iota(jnp.int32, (PK, PK), 1)
    eye = (rr == cc).astype(F32)
    dot = functools.partial(jnp.dot, preferred_element_type=F32)
    nt = (((1,), (1,)), ((), ()))
    tn = (((0,), (0,)), ((), ()))

    q_refs, k_refs, v_refs, o_refs = (qf_ref, qb_ref), (kf_ref, kb_ref), (vf_ref, vb_ref), (of_ref, ob_ref)
    cps = (cpf_ref[0, 0], cpb_ref[0, 0])
    rps = (rpf_ref[0, 0], rpb_ref[0, 0])
    aps = (apf_ref[0, 0], apb_ref[0, 0])
    units = [(d, p) for p in range(N_PAIRS) for d in range(2)]
    per_unit = lambda f: [f(i, d, p) for i, (d, p) in enumerate(units)]
    cols = lambda p: (pl.ds(2 * p * HEAD_DIM, HEAD_DIM), pl.ds((2 * p + 1) * HEAD_DIM, HEAD_DIM))
    pack = lambda ref, p: jnp.concatenate([ref[:, cols(p)[0]], ref[:, cols(p)[1]]], axis=0)
    mask = lambda d, n: mk_ref[d, n]

    gcol = per_unit(lambda i, d, p: cps[d][:, p:p + 1])
    bcol = per_unit(lambda i, d, p: cps[d][:, N_PAIRS + p:N_PAIRS + p + 1])
    glcol = per_unit(lambda i, d, p: cps[d][:, 2 * N_PAIRS + p:2 * N_PAIRS + p + 1])
    kp = per_unit(lambda i, d, p: pack(k_refs[d], p))
    qp = per_unit(lambda i, d, p: pack(q_refs[d], p))
    kk = per_unit(lambda i, d, p: lax.dot_general(kp[i], kp[i], nt, preferred_element_type=F32))
    qk = per_unit(lambda i, d, p: lax.dot_general(qp[i], kp[i], nt, preferred_element_type=F32))
    dec = per_unit(lambda i, d, p: jnp.exp((gcol[i] - rps[d][p:p + 1, :]) * mask(d, nl) + mask(d, nl + 1)))
    a = per_unit(lambda i, d, p: kk[i] * dec[i] * bcol[i])
    qkm = per_unit(lambda i, d, p: (qk[i] * dec[i]).astype(BF16))

    x = per_unit(lambda i, d, p: eye - a[i] * mask(d, 0))
    for lv in range(1, nl):
        xb = per_unit(lambda i, d, p: x[i].astype(BF16))
        po = per_unit(lambda i, d, p: dot(xb[i], (a[i] * mask(d, lv)).astype(BF16)))
        x = per_unit(lambda i, d, p: x[i] - dot(po[i].astype(BF16), xb[i]))
    tb = per_unit(lambda i, d, p: x[i].astype(BF16))

    egc = per_unit(lambda i, d, p: jnp.exp(gcol[i]))
    kf = per_unit(lambda i, d, p: kp[i].astype(F32))
    u = per_unit(lambda i, d, p: dot(tb[i], (pack(v_refs[d], p).astype(F32) * bcol[i]).astype(BF16)))
    w = per_unit(lambda i, d, p: dot(tb[i], (kf[i] * (bcol[i] * egc[i])).astype(BF16)))
    qd = per_unit(lambda i, d, p: qp[i].astype(F32) * egc[i])
    kd = per_unit(lambda i, d, p: kf[i] * jnp.exp(glcol[i] - gcol[i]))

    s = per_unit(lambda i, d, p: s_scr[d, p])
    lhs = per_unit(lambda i, d, p: jnp.concatenate(
        [jnp.concatenate([w[i] * top, w[i] * bot], axis=1),
         jnp.concatenate([qd[i] * top, qd[i] * bot], axis=1)], axis=0).astype(BF16))
    ws = per_unit(lambda i, d, p: dot(lhs[i], s[i].astype(BF16)))
    vnb = per_unit(lambda i, d, p: (u[i] - ws[i][:PK]).astype(BF16))
    o = per_unit(lambda i, d, p: ws[i][PK:] + dot(qkm[i], vnb[i]))
    kbd = per_unit(lambda i, d, p: jnp.concatenate([kd[i] * top, kd[i] * bot], axis=1).astype(BF16))
    kv = per_unit(lambda i, d, p: lax.dot_general(kbd[i], vnb[i], tn, preferred_element_type=F32))
    for i, (d, p) in enumerate(units):
        s_scr[d, p] = jnp.exp(aps[d][:, p:p + 1]) * s[i] + kv[i]
    for i, (d, p) in enumerate(units):
        o_refs[d][:, cols(p)[0]] = o[i][:GDN_CHUNK].astype(o_refs[d].dtype)
        o_refs[d][:, cols(p)[1]] = o[i][GDN_CHUNK:].astype(o_refs[d].dtype)

    @pl.when(c == pl.num_programs(1) - 1)
    def _():
        sfin_ref[0] = s_scr[...]


def _gdn_out_kernel(of_ref, ob_ref, z_ref, gon_ref, o_ref):
    for h in range(GDN_HEADS):
        c = pl.ds(h * HEAD_DIM, HEAD_DIM)
        o = of_ref[:, c].astype(F32) + ob_ref[:, c].astype(F32)
        z = z_ref[:, c]
        y = o * lax.rsqrt(jnp.mean(o * o, axis=-1, keepdims=True) + EPS) * gon_ref[...]
        o_ref[:, c] = (y * (z * jax.nn.sigmoid(z))).astype(o_ref.dtype)


def _gdn_out(o_f, o_b, y_main, g_on, tm):
    t = o_f.shape[0]
    blk = lambda col: pl.BlockSpec((tm, GDN_WIDTH), lambda i: (i, col))
    return pl.pallas_call(
        _gdn_out_kernel,
        grid=(t // tm,),
        in_specs=[blk(0), blk(0), blk(N_QKV // GDN_WIDTH), pl.BlockSpec((1, HEAD_DIM), lambda i: (0, 0))],
        out_specs=blk(0),
        out_shape=jax.ShapeDtypeStruct((t, GDN_WIDTH), BF16),
        compiler_params=_cparams(("arbitrary",)),
        name="gdn_out",
    )(o_f, o_b, y_main, g_on.reshape(1, HEAD_DIM))


def _gdn_packs(gates, b_, l, reverse):
    nc = l // GDN_CHUNK
    nh = GDN_HEADS
    g4 = gates.reshape(b_, nc, GDN_CHUNK, N_SMALL)
    d = 1 if reverse else 0
    beta = g4[..., d * nh:(d + 1) * nh]
    gc = g4[..., (2 + d) * nh:(3 + d) * nh]
    gl = jnp.broadcast_to(gc[:, :, 0:1] if reverse else gc[:, :, GDN_CHUNK - 1:GDN_CHUNK], gc.shape)

    def rowpack(t):
        return jnp.transpose(t.reshape(b_, nc, GDN_CHUNK, N_PAIRS, 2), (0, 1, 3, 4, 2)).reshape(b_, nc, N_PAIRS, PK)

    rp = rowpack(gc)
    cp = jnp.concatenate([jnp.swapaxes(rowpack(t), 2, 3) for t in (gc, beta, gl)]
                         + [jnp.zeros((b_, nc, PK, N_PAIRS), F32)], axis=-1)
    glh = gl[:, :, 0].reshape(b_, nc, N_PAIRS, 2)
    ap = jnp.swapaxes(jnp.repeat(glh, HEAD_DIM, axis=-1), 2, 3)
    return cp, rp, ap


def _gdn_scan(qkv, gates, s0, b_, l):
    nc = l // GDN_CHUNK
    packs = [_gdn_packs(gates, b_, l, reverse) for reverse in (False, True)]
    masks = jnp.stack([_gdn_masks(False), _gdn_masks(True)])
    chunk_of = (lambda c: c, lambda c: nc - 1 - c)
    tok = lambda d, col: pl.BlockSpec((GDN_CHUNK, GDN_WIDTH), lambda b, c: (b * nc + chunk_of[d](c), col))
    per_chunk = lambda d, shp: pl.BlockSpec((1, 1) + shp, lambda b, c: (b, chunk_of[d](c), 0, 0))
    gate_specs = lambda d: [per_chunk(d, (PK, 4 * N_PAIRS)), per_chunk(d, (N_PAIRS, PK)),
                            per_chunk(d, (2 * HEAD_DIM, N_PAIRS))]
    state = pl.BlockSpec((1, 2, N_PAIRS, 2 * HEAD_DIM, HEAD_DIM), lambda b, c: (b, 0, 0, 0, 0))
    return pl.pallas_call(
        _gdn_scan_kernel,
        grid=(b_, nc),
        in_specs=[tok(0, 0), tok(0, 1), tok(0, 2), tok(1, 0), tok(1, 1), tok(1, 2)] + gate_specs(0) + gate_specs(1)
        + [state, pl.BlockSpec(masks.shape, lambda b, c: (0, 0, 0, 0))],
        out_specs=[tok(0, 0), tok(1, 0), state],
        out_shape=[jax.ShapeDtypeStruct((b_ * l, GDN_WIDTH), BF16),
                   jax.ShapeDtypeStruct((b_ * l, GDN_WIDTH), BF16),
                   jax.ShapeDtypeStruct(s0.shape, F32)],
        scratch_shapes=[pltpu.VMEM((2, N_PAIRS, 2 * HEAD_DIM, HEAD_DIM), F32)],
        compiler_params=_cparams(("arbitrary", "arbitrary")),
        name="gdn_scan",
    )(qkv, qkv, qkv, qkv, qkv, qkv, *packs[0], *packs[1], s0, masks)


def _gdn_mixer(yx, sx, yc, sc, conv_w, a_log, dt_bias, g_on, b_, l, n_ctx):
    nh = GDN_HEADS
    conv_w8 = jnp.pad(conv_w, ((0, HALO - CONV_K), (0, 0)))
    ea_row = jnp.zeros((1, N_SMALL), F32).at[0, 2 * nh:4 * nh].set(jnp.exp(a_log.reshape(-1)))
    dt_row = jnp.zeros((1, N_SMALL), F32).at[0, 2 * nh:4 * nh].set(dt_bias.reshape(-1))
    qkv_c, gates_c = _gdn_prep(yc, sc, conv_w8, ea_row, dt_row, b_, n_ctx)
    qkv_x, gates_x = _gdn_prep(yx, sx, conv_w8, ea_row, dt_row, b_, l)
    zero = jnp.zeros((b_, 2, N_PAIRS, 2 * HEAD_DIM, HEAD_DIM), F32)
    _, _, s_ctx = _gdn_scan(qkv_c, gates_c, zero, b_, n_ctx)
    o_f, o_b, _ = _gdn_scan(qkv_x, gates_x, s_ctx, b_, l)
    return _gdn_out(o_f, o_b, yx, g_on, 256)


SWA_GROUP = SWA_HEADS // SWA_KV_HEADS
ROT = HEAD_DIM // 4


def _rope_tables(l):
    half = HEAD_DIM // 2
    inv = ROPE_BASE ** (-jnp.arange(0, half, 2, dtype=F32) / half)
    pos = jnp.arange(l, dtype=jnp.int32)
    ang_r = (pos // GRID_W).astype(F32)[:, None] * inv
    ang_c = (pos % GRID_W).astype(F32)[:, None] * inv
    zero = jnp.zeros_like(ang_r)
    cos = jnp.concatenate([jnp.cos(ang_r), jnp.cos(ang_r), jnp.cos(ang_c), jnp.cos(ang_c)], axis=1)
    sin_up = jnp.concatenate([-jnp.sin(ang_r), zero, -jnp.sin(ang_c), zero], axis=1)
    sin_dn = jnp.concatenate([zero, jnp.sin(ang_r), zero, jnp.sin(ang_c)], axis=1)
    return cos, sin_up, sin_dn


def _swa_prep_kernel(q_ref, k_ref, v_ref, cos_ref, su_ref, sd_ref, gq_ref, gk_ref, qo_ref, ko_ref, vo_ref):
    cos, su, sd = cos_ref[...], su_ref[...], sd_ref[...]

    def norm_rope(t, g, scale):
        y = t * lax.rsqrt(jnp.mean(t * t, axis=-1, keepdims=True) + EPS) * g
        y = y * cos + pltpu.roll(y, HEAD_DIM - ROT, 1) * su + pltpu.roll(y, ROT, 1) * sd
        return y * scale if scale != 1.0 else y

    for h in range(SWA_HEADS):
        c = pl.ds(h * HEAD_DIM, HEAD_DIM)
        qo_ref[:, c] = norm_rope(q_ref[:, c], gq_ref[...], HEAD_DIM ** -0.5).astype(qo_ref.dtype)
    for h in range(SWA_KV_HEADS):
        c = pl.ds(h * HEAD_DIM, HEAD_DIM)
        ko_ref[:, c] = norm_rope(k_ref[:, c], gk_ref[...], 1.0).astype(ko_ref.dtype)
    vo_ref[...] = v_ref[...].astype(vo_ref.dtype)


def _swa_prep(y_main, tables, g_q, g_k, rows, tm):
    q_blk = (N_QKV + GDN_WIDTH) // SWA_WIDTH
    k_blk = (N_QKV + GDN_WIDTH + SWA_WIDTH) // SWA_KV_WIDTH
    tpb = tables[0].shape[0] // tm
    tab = pl.BlockSpec((tm, HEAD_DIM), lambda i: (i % tpb, 0))
    vec = pl.BlockSpec((1, HEAD_DIM), lambda i: (0, 0))
    return pl.pallas_call(
        _swa_prep_kernel,
        grid=(rows // tm,),
        in_specs=[pl.BlockSpec((tm, SWA_WIDTH), lambda i: (i, q_blk)),
                  pl.BlockSpec((tm, SWA_KV_WIDTH), lambda i: (i, k_blk)),
                  pl.BlockSpec((tm, SWA_KV_WIDTH), lambda i: (i, k_blk + 1)),
                  tab, tab, tab, vec, vec],
        out_specs=[pl.BlockSpec((tm, SWA_WIDTH), lambda i: (i, 0)),
                   pl.BlockSpec((tm, SWA_KV_WIDTH), lambda i: (i, 0)),
                   pl.BlockSpec((tm, SWA_KV_WIDTH), lambda i: (i, 0))],
        out_shape=[jax.ShapeDtypeStruct((rows, SWA_WIDTH), BF16),
                   jax.ShapeDtypeStruct((rows, SWA_KV_WIDTH), BF16),
                   jax.ShapeDtypeStruct((rows, SWA_KV_WIDTH), BF16)],
        compiler_params=_cparams(("arbitrary",)),
        name="swa_prep",
    )(y_main, y_main, y_main, *tables, g_q.reshape(1, HEAD_DIM), g_k.reshape(1, HEAD_DIM))


def _swa_attn_kernel(q_ref, kp_ref, kc_ref, kn_ref, vp_ref, vc_ref, vn_ref, kx_ref, vx_ref, sink_ref, o_ref, *, n_ctx):
    n = pl.program_id(1)
    nb = pl.num_programs(1)
    rows = SWA_GROUP * Q_BLOCK
    nk = n_ctx + 3 * Q_BLOCK
    qi = lax.broadcasted_iota(jnp.int32, (rows, nk), 0) & (Q_BLOCK - 1)
    kj = lax.broadcasted_iota(jnp.int32, (rows, nk), 1) - n_ctx
    lo = jnp.where(n == 0, Q_BLOCK, 0)
    hi = jnp.where(n == nb - 1, 2 * Q_BLOCK, 3 * Q_BLOCK)
    valid = (kj < 0) | ((kj >= qi) & (kj <= qi + 2 * WINDOW) & (kj >= lo) & (kj < hi))
    hsel = lax.broadcasted_iota(jnp.int32, (rows, 1), 0) // Q_BLOCK
    nt = (((1,), (1,)), ((), ()))
    for j in range(SWA_KV_HEADS):
        c = pl.ds(j * HEAD_DIM, HEAD_DIM)
        heads = [j * SWA_GROUP + g for g in range(SWA_GROUP)]
        q = jnp.concatenate([q_ref[:, pl.ds(h * HEAD_DIM, HEAD_DIM)] for h in heads], axis=0)
        k = jnp.concatenate([kx_ref[:, c], kp_ref[:, c], kc_ref[:, c], kn_ref[:, c]], axis=0)
        v = jnp.concatenate([vx_ref[:, c], vp_ref[:, c], vc_ref[:, c], vn_ref[:, c]], axis=0)
        sink = jnp.zeros((rows, 1), F32)
        for g, h in enumerate(heads):
            sink = jnp.where(hsel == g, sink_ref[h:h + 1, 0:1], sink)
        s = jnp.where(valid, lax.dot_general(q, k, nt, preferred_element_type=F32), NEG_INF)
        m = jnp.maximum(jnp.max(s, axis=-1, keepdims=True), sink)
        p = jnp.exp(s - m)
        den = jnp.sum(p, axis=-1, keepdims=True) + jnp.exp(sink - m)
        o = jnp.dot(p.astype(BF16), v, preferred_element_type=F32) / den
        for g, h in enumerate(heads):
            o_ref[:, pl.ds(h * HEAD_DIM, HEAD_DIM)] = o[g * Q_BLOCK:(g + 1) * Q_BLOCK].astype(o_ref.dtype)


def _swa_attention(q, k, v, k_ctx, v_ctx, sink, b_, l, n_ctx):
    nb = l // Q_BLOCK
    blk = lambda w, off: pl.BlockSpec(
        (Q_BLOCK, w), lambda b, n: (b * nb + jnp.clip(n + off, 0, nb - 1), 0))
    ctx = pl.BlockSpec((n_ctx, SWA_KV_WIDTH), lambda b, n: (b, 0))
    kw = SWA_KV_WIDTH
    return pl.pallas_call(
        functools.partial(_swa_attn_kernel, n_ctx=n_ctx),
        grid=(b_, nb),
        in_specs=[blk(SWA_WIDTH, 0), blk(kw, -1), blk(kw, 0), blk(kw, 1), blk(kw, -1), blk(kw, 0), blk(kw, 1),
                  ctx, ctx, pl.BlockSpec((SWA_HEADS, HEAD_DIM), lambda b, n: (0, 0))],
        out_specs=blk(SWA_WIDTH, 0),
        out_shape=jax.ShapeDtypeStruct((b_ * l, SWA_WIDTH), BF16),
        compiler_params=_cparams(("arbitrary", "arbitrary")),
        name="swa_attention",
    )(q, k, k, k, v, v, v, k_ctx, v_ctx, jnp.broadcast_to(sink.astype(F32)[:, None], (SWA_HEADS, HEAD_DIM)))


def _swa_mixer(yx, yc, g_q, g_k, sink, b_, l, n_ctx):
    ones = jnp.ones((n_ctx, HEAD_DIM), F32)
    zeros = jnp.zeros((n_ctx, HEAD_DIM), F32)
    qx, kx, vx = _swa_prep(yx, _rope_tables(l), g_q, g_k, b_ * l, 256)
    _, kc, vc = _swa_prep(yc, (ones, zeros, zeros), g_q, g_k, b_ * n_ctx, n_ctx)
    return _swa_attention(qx, kx, vx, kc, vc, sink, b_, l, n_ctx)


def _rmsnorm(x, g):
    xf = x.astype(F32)
    y = xf * lax.rsqrt(jnp.mean(xf * xf, axis=-1, keepdims=True) + EPS)
    return (y * g.astype(F32)).astype(x.dtype)


def _l2norm(x):
    return x * lax.rsqrt(jnp.sum(x * x, axis=-1, keepdims=True) + EPS)


def _short_conv(x, w):
    c = x.shape[-1]
    y = lax.conv_general_dilated(x, w[:, None, :].astype(x.dtype), window_strides=(1,),
                                 padding=[(CONV_K // 2, CONV_K // 2)],
                                 dimension_numbers=('NWC', 'WIO', 'NWC'), feature_group_count=c)
    return jax.nn.silu(y)


def _axial_rope(t, rows, cols):
    half = HEAD_DIM // 2
    inv = ROPE_BASE ** (-jnp.arange(0, half, 2, dtype=F32) / half)

    def rot(u, pos):
        ang = pos[:, None] * inv
        cos, sin = jnp.cos(ang)[None, :, None, :], jnp.sin(ang)[None, :, None, :]
        u1, u2 = jnp.split(u.astype(F32), 2, axis=-1)
        return jnp.concatenate([u1 * cos - u2 * sin, u2 * cos + u1 * sin], axis=-1)

    return jnp.concatenate([rot(t[..., :half], rows), rot(t[..., half:], cols)], axis=-1).astype(t.dtype)


def _gated_delta_chunked(q, k, v, g, beta, s0):
    b_, l, h, _ = q.shape
    n = l // GDN_CHUNK

    def chunks(t):
        t = t.reshape((b_, n, GDN_CHUNK) + t.shape[2:])
        return jnp.moveaxis(jnp.swapaxes(t, 2, 3), 1, 0)

    qc, kc, vc, gc, bc = map(chunks, (q, k, v, g, beta))
    gcum = jnp.cumsum(gc, axis=-1)
    tri = jnp.tril(jnp.ones((GDN_CHUNK, GDN_CHUNK), bool))
    strict = jnp.tril(jnp.ones((GDN_CHUNK, GDN_CHUNK), F32), -1)
    diff = gcum[..., :, None] - gcum[..., None, :]
    decay = jnp.where(tri, jnp.exp(jnp.where(tri, diff, 0.0)), 0.0)
    kb = kc * bc[..., None]
    a_strict = jnp.einsum('nbhid,nbhjd->nbhij', kb, kc) * decay * strict
    eye = jnp.eye(GDN_CHUNK, dtype=F32)
    t_inv = lax.linalg.triangular_solve(eye + a_strict, jnp.broadcast_to(eye, a_strict.shape),
                                        left_side=True, lower=True, unit_diagonal=True)
    u = jnp.einsum('nbhij,nbhjd->nbhid', t_inv, vc * bc[..., None])
    w = jnp.einsum('nbhij,nbhjd->nbhid', t_inv, kb * jnp.exp(gcum)[..., None])
    qk = jnp.einsum('nbhid,nbhjd->nbhij', qc, kc) * decay
    q_dec = qc * jnp.exp(gcum)[..., None]
    k_dec = kc * jnp.exp(gcum[..., -1:] - gcum)[..., None]
    chunk_decay = jnp.exp(gcum[..., -1])

    def step(s, xs):
        u_n, w_n, qk_n, q_n, k_n, a_n = xs
        v_new = u_n - jnp.einsum('bhcd,bhde->bhce', w_n, s)
        o = jnp.einsum('bhcd,bhde->bhce', q_n, s) + jnp.einsum('bhij,bhje->bhie', qk_n, v_new)
        s = s * a_n[..., None, None] + jnp.einsum('bhcd,bhce->bhde', k_n, v_new)
        return s, o

    s_fin, o = lax.scan(step, s0, (u, w, qk, q_dec, k_dec, chunk_decay))
    o = jnp.swapaxes(jnp.moveaxis(o, 0, 1), 2, 3).reshape(b_, l, h, -1)
    return o, s_fin


def _gdn_inputs(qkv, b_logit, a_logit, conv_w, a_log, dt_bias):
    b_, l, _ = qkv.shape
    qkv = _short_conv(qkv, conv_w).astype(F32).reshape(b_, l, 3, GDN_HEADS, HEAD_DIM)
    q = _l2norm(qkv[:, :, 0]) * HEAD_DIM ** -0.5
    k = _l2norm(qkv[:, :, 1])
    v = qkv[:, :, 2]
    beta = jax.nn.sigmoid(b_logit.astype(F32).reshape(b_, l, 2, GDN_HEADS))
    g = -jnp.exp(a_log.astype(F32)) * jax.nn.softplus(
        a_logit.astype(F32).reshape(b_, l, 2, GDN_HEADS) + dt_bias.astype(F32))
    return q, k, v, g, beta


def _gdn_bidir(q, k, v, g, beta, s_fwd0, s_bwd0):
    o_f, s_f = _gated_delta_chunked(q, k, v, g[:, :, 0], beta[:, :, 0], s_fwd0)
    rev = lambda t: jnp.flip(t, axis=1)
    o_b, s_b = _gated_delta_chunked(rev(q), rev(k), rev(v), rev(g[:, :, 1]), rev(beta[:, :, 1]), s_bwd0)
    return o_f + rev(o_b), s_f, s_b


def _gdn_output(o, z, g_on):
    b_, l = o.shape[:2]
    y = _rmsnorm(o, g_on).reshape(b_, l, GDN_WIDTH)
    return y * jax.nn.silu(z.astype(F32))


def _swa_inputs(q, k, v, g_q, g_k):
    b_, l, _ = q.shape
    q = _rmsnorm(q.reshape(b_, l, SWA_HEADS, HEAD_DIM), g_q)
    k = _rmsnorm(k.reshape(b_, l, SWA_KV_HEADS, HEAD_DIM), g_k)
    return q, k, v.reshape(b_, l, SWA_KV_HEADS, HEAD_DIM)


def _window_attention(q, k, v, k_ctx, v_ctx, sink):
    b_, l, h, d = q.shape
    g = h // SWA_KV_HEADS
    nb = l // Q_BLOCK
    n_ctx = k_ctx.shape[1]
    scale = d ** -0.5
    qb = q.reshape(b_, nb, Q_BLOCK, SWA_KV_HEADS, g, d)

    def band_blocks(t):
        tp = jnp.pad(t, ((0, 0), (Q_BLOCK, Q_BLOCK), (0, 0), (0, 0))).reshape(b_, nb + 2, Q_BLOCK, SWA_KV_HEADS, d)
        return jnp.concatenate([tp[:, :-2], tp[:, 1:-1], tp[:, 2:]], axis=2)

    kw, vw = band_blocks(k), band_blocks(v)
    qi = jnp.arange(Q_BLOCK)[:, None]
    kj = jnp.arange(3 * Q_BLOCK)[None, :]
    band = jnp.abs(kj - Q_BLOCK - qi) <= WINDOW
    kpos = jnp.arange(nb)[:, None] * Q_BLOCK - Q_BLOCK + jnp.arange(3 * Q_BLOCK)[None, :]
    valid = band[None] & ((kpos >= 0) & (kpos < l))[:, None, :]
    s_win = jnp.einsum('bnqkgd,bnjkd->bnkgqj', qb, kw).astype(F32) * scale
    s_win = jnp.where(valid[None, :, None, None], s_win, NEG_INF)
    s_ctx = jnp.einsum('bnqkgd,bckd->bnkgqc', qb, k_ctx).astype(F32) * scale
    s_sink = jnp.broadcast_to(sink.astype(F32).reshape(SWA_KV_HEADS, g, 1, 1), s_ctx.shape[:-1] + (1,))
    p = jax.nn.softmax(jnp.concatenate([s_ctx, s_win, s_sink], axis=-1), axis=-1).astype(v.dtype)
    o = (jnp.einsum('bnkgqc,bckd->bnqkgd', p[..., :n_ctx], v_ctx)
         + jnp.einsum('bnkgqj,bnjkd->bnqkgd', p[..., n_ctx:n_ctx + 3 * Q_BLOCK], vw))
    return o.reshape(b_, l, h * d)


def _split_main(y):
    o = np.cumsum((0, 3 * GDN_WIDTH, GDN_WIDTH, SWA_WIDTH, SWA_KV_WIDTH, SWA_KV_WIDTH))
    return tuple(y[..., int(o[n]):int(o[n + 1])] for n in range(5))


def kernel(x, c, ctx, c_ctx, w_ada, b_ada, g_norm1, g_norm2, w_in, conv_qkv, a_log, dt_bias, g_onorm, g_qnorm,
           g_knorm, sink, w_out, w_router_grp, b_router_grp, w_router_exp, b_router_exp, w_gate, w_up, w_down):
    b_, l, d = x.shape
    n_ctx = ctx.shape[1]
    t = b_ * l
    assert w_ada.shape[0] == 1 and d == D_MODEL and b_ + 1 <= MOD_ROWS
    rows = jnp.repeat(jnp.arange(l // GRID_W, dtype=F32), GRID_W)
    cols = jnp.tile(jnp.arange(GRID_W, dtype=F32), l // GRID_W)

    wi = w_in[0]
    w_main = jnp.concatenate([wi[:, IN_OFFS[0]:IN_OFFS[2]], wi[:, IN_OFFS[4]:IN_OFFS[7]]], axis=1).astype(BF16)
    w_small = jnp.pad(wi[:, IN_OFFS[2]:IN_OFFS[4]], ((0, 0), (0, N_SMALL - 4 * GDN_HEADS))).astype(BF16)
    wo = w_out[0].astype(BF16)
    w_router = jnp.pad(jnp.concatenate([w_router_grp[0], w_router_exp[0]], axis=1),
                       ((0, 0), (0, N_ROUTER - N_GROUPS - N_EXPERTS))).astype(BF16)
    b_router = jnp.pad(jnp.concatenate([b_router_grp[0], b_router_exp[0]]),
                       (0, N_ROUTER - N_GROUPS - N_EXPERTS)).reshape(1, N_ROUTER)

    c_rows = jnp.zeros((MOD_ROWS, d), F32).at[:b_].set(c).at[b_].set(c_ctx)
    mod = _modulation(c_rows, w_ada[0], b_ada[0])
    mod3 = mod.reshape(MOD_ROWS * 6, 1, d)

    tm = 512
    tpb = l // tm
    yx, sx = _in_projection(x.reshape(t, d), g_norm1[0], mod3, lambda i: i // tpb, w_main, w_small, tm)
    yc, sc = _in_projection(ctx.reshape(b_ * n_ctx, d), g_norm1[0], mod3, lambda i: b_, w_main, w_small, n_ctx)
    ya_x = _gdn_mixer(yx, sx, yc, sc, conv_qkv[0], a_log[0], dt_bias[0], g_onorm[0], b_, l, n_ctx)
    yb_x = _swa_mixer(yx, yc, g_qnorm[0], g_knorm[0], sink[0], b_, l, n_ctx)

    x1 = _out_projection(ya_x, yb_x, wo[:GDN_WIDTH], wo[GDN_WIDTH:], x.reshape(t, d), mod3, tpb, tm)

    tm2 = 256
    h2c, ids, gates = _norm2_router(x1, g_norm2[0], mod3, l // tm2, w_router, b_router, tm2)
    tok_sorted, blk_first, blk_rows, slot_of, blk_expert, n_used = _slots(ids[:, :TOP_K], t)
    yc_moe = _moe_experts(h2c, tok_sorted, blk_first, blk_rows, blk_expert, n_used, w_gate[0], w_up[0], w_down[0])
    return _moe_combine(yc_moe, slot_of, x1, gates, mod3, l // COMBINE_TM).reshape(b_, l, d)
```

```python
import functools
import math

import jax
import jax.numpy as jnp
import numpy as np
from jax import lax
from jax.experimental import pallas as pl
from jax.experimental.pallas import tpu as pltpu

F32 = jnp.float32
BF16 = jnp.bfloat16

D_MODEL = 4096
CTX_LEN = 256
GRID_W = 64
HEAD_DIM = 128
GDN_HEADS = 16
GDN_WIDTH = GDN_HEADS * HEAD_DIM
GDN_CHUNK = 64
CONV_K = 5
SWA_HEADS = 16
SWA_KV_HEADS = 4
SWA_WIDTH = SWA_HEADS * HEAD_DIM
SWA_KV_WIDTH = SWA_KV_HEADS * HEAD_DIM
WINDOW = 128
Q_BLOCK = 128
ROPE_BASE = 10000.0
N_GROUPS = 8
EXPERTS_PER_GROUP = 8
N_EXPERTS = N_GROUPS * EXPERTS_PER_GROUP
TOP_K = 2
D_EXPERT = D_MODEL // 8
EPS = 1e-6
NEG_INF = -1e30

IN_SIZES = (3 * GDN_WIDTH, GDN_WIDTH, 2 * GDN_HEADS, 2 * GDN_HEADS, SWA_WIDTH, SWA_KV_WIDTH, SWA_KV_WIDTH)
IN_OFFS = tuple(int(v) for v in np.cumsum((0,) + IN_SIZES))
N_MAIN = 3 * GDN_WIDTH + GDN_WIDTH + SWA_WIDTH + 2 * SWA_KV_WIDTH
N_SMALL = 128
N_ROUTER = 128
MOD_ROWS = 8

MOE_BM = 256
VMEM_LIMIT = 56 * 1024 * 1024


def _cparams(sem):
    return pltpu.CompilerParams(dimension_semantics=sem, vmem_limit_bytes=VMEM_LIMIT)


def _mod_kernel(c_ref, w_ref, b_ref, o_ref):
    c = c_ref[...]
    a = (c * jax.nn.sigmoid(c)).astype(BF16)
    o_ref[...] = jnp.dot(a, w_ref[...].astype(BF16), preferred_element_type=F32) + b_ref[...]


def _modulation(c_rows, w_ada, b_ada):
    d, n = w_ada.shape
    tn = 512
    return pl.pallas_call(
        _mod_kernel,
        grid=(n // tn,),
        in_specs=[pl.BlockSpec((MOD_ROWS, d), lambda j: (0, 0)),
                  pl.BlockSpec((d, tn), lambda j: (0, j)),
                  pl.BlockSpec((1, tn), lambda j: (0, j))],
        out_specs=pl.BlockSpec((MOD_ROWS, tn), lambda j: (0, j)),
        out_shape=jax.ShapeDtypeStruct((MOD_ROWS, n), F32),
        compiler_params=_cparams(("arbitrary",)),
        name="modulation",
    )(c_rows, w_ada, b_ada.reshape(1, n))


NORM_ROWS = 64


def _norm_mod_rows(x_ref, g_ref, sh_ref, sc_ref, h_ref, tm):
    g = g_ref[...]
    sc = 1.0 + sc_ref[0]
    sh = sh_ref[0]

    def body(r, carry):
        rows = pl.ds(pl.multiple_of(r * NORM_ROWS, NORM_ROWS), NORM_ROWS)
        xf = x_ref[rows, :]
        ms = jnp.mean(xf * xf, axis=-1, keepdims=True)
        y = xf * lax.rsqrt(ms + EPS) * g
        h_ref[rows, :] = (y * sc + sh).astype(h_ref.dtype)
        return carry

    lax.fori_loop(0, tm // NORM_ROWS, body, 0)


def _inproj_kernel(x_ref, g_ref, sh_ref, sc_ref, w_ref, ws_ref, o_ref, os_ref, h_ref, *, tm):
    @pl.when(pl.program_id(1) == 0)
    def _():
        _norm_mod_rows(x_ref, g_ref, sh_ref, sc_ref, h_ref, tm)
        os_ref[...] = jnp.dot(h_ref[...], ws_ref[...], preferred_element_type=F32)

    o_ref[...] = jnp.dot(h_ref[...], w_ref[...], preferred_element_type=F32)


def _in_projection(x2d, g_norm, mod3, mod_row_of_tile, w_main, w_small, tm):
    t, d = x2d.shape
    tn = 1024
    return pl.pallas_call(
        functools.partial(_inproj_kernel, tm=tm),
        grid=(t // tm, N_MAIN // tn),
        in_specs=[pl.BlockSpec((tm, d), lambda i, j: (i, 0)),
                  pl.BlockSpec((1, d), lambda i, j: (0, 0)),
                  pl.BlockSpec((1, 1, d), lambda i, j: (mod_row_of_tile(i) * 6 + 0, 0, 0)),
                  pl.BlockSpec((1, 1, d), lambda i, j: (mod_row_of_tile(i) * 6 + 1, 0, 0)),
                  pl.BlockSpec((d, tn), lambda i, j: (0, j)),
                  pl.BlockSpec((d, N_SMALL), lambda i, j: (0, 0))],
        out_specs=[pl.BlockSpec((tm, tn), lambda i, j: (i, j)),
                   pl.BlockSpec((tm, N_SMALL), lambda i, j: (i, 0))],
        out_shape=[jax.ShapeDtypeStruct((t, N_MAIN), F32),
                   jax.ShapeDtypeStruct((t, N_SMALL), F32)],
        scratch_shapes=[pltpu.VMEM((tm, d), BF16)],
        compiler_params=_cparams(("arbitrary", "arbitrary")),
        name="in_projection",
    )(x2d, g_norm.reshape(1, d), mod3, mod3, w_main, w_small)


def _outproj_kernel(ya_ref, yb_ref, wa_ref, wb_ref, x_ref, gate_ref, o_ref):
    acc = jnp.dot(ya_ref[...], wa_ref[...], preferred_element_type=F32)
    acc = acc + jnp.dot(yb_ref[...], wb_ref[...], preferred_element_type=F32)
    o_ref[...] = x_ref[...] + gate_ref[0] * acc


def _out_projection(ya, yb, wa, wb, x2d, mod3, tiles_per_batch, tm):
    t, d = x2d.shape
    tn = 1024
    nj = d // tn
    ka, kb = ya.shape[1], yb.shape[1]
    return pl.pallas_call(
        _outproj_kernel,
        grid=(t // tm, nj),
        in_specs=[pl.BlockSpec((tm, ka), lambda i, j: (i, 0)),
                  pl.BlockSpec((tm, kb), lambda i, j: (i, 0)),
                  pl.BlockSpec((ka, tn), lambda i, j: (0, j)),
                  pl.BlockSpec((kb, tn), lambda i, j: (0, j)),
                  pl.BlockSpec((tm, tn), lambda i, j: (i, j)),
                  pl.BlockSpec((1, 1, tn), lambda i, j: (((i // tiles_per_batch) * 6 + 2) * nj + j, 0, 0))],
        out_specs=pl.BlockSpec((tm, tn), lambda i, j: (i, j)),
        out_shape=jax.ShapeDtypeStruct((t, d), F32),
        compiler_params=_cparams(("arbitrary", "arbitrary")),
        name="out_projection",
    )(ya, yb, wa, wb, x2d, mod3.reshape(-1, 1, tn))


def _route_rows(lg):
    col = lax.broadcasted_iota(jnp.int32, lg.shape, 1)
    first = lambda hit: jnp.min(jnp.where(hit, col, N_ROUTER), axis=-1, keepdims=True)
    gm = col < N_GROUPS
    mg = jnp.max(jnp.where(gm, lg, NEG_INF), axis=-1, keepdims=True)
    grp = first(gm & (lg == mg))
    p_grp = 1.0 / jnp.sum(jnp.where(gm, jnp.exp(lg - mg), 0.0), axis=-1, keepdims=True)
    lo = N_GROUPS + grp * EXPERTS_PER_GROUP
    em = (col >= lo) & (col < lo + EXPERTS_PER_GROUP)
    m1 = jnp.max(jnp.where(em, lg, NEG_INF), axis=-1, keepdims=True)
    i1 = first(em & (lg == m1))
    em2 = em & (col != i1)
    m2 = jnp.max(jnp.where(em2, lg, NEG_INF), axis=-1, keepdims=True)
    i2 = first(em2 & (lg == m2))
    e2 = jnp.exp(m2 - m1)
    g1 = p_grp / (1.0 + e2)
    ids = jnp.where(col == 0, i1 - N_GROUPS, jnp.where(col == 1, i2 - N_GROUPS, 0))
    gates = jnp.where(col == 0, g1, jnp.where(col == 1, g1 * e2, 0.0))
    return ids, gates


LANES = 128
HALF_D = D_MODEL // 2
ROW_CH = HALF_D // LANES
ROW_PITCH = 24
U32 = jnp.uint32
HI_MASK = 0xFFFF0000
DMA_UNROLL = 8


def _pack_pair(lo, hi):
    bits = lambda v: lax.bitcast_convert_type(v.astype(BF16).astype(F32), U32)
    return (bits(hi) & U32(HI_MASK)) | (bits(lo) >> 16)


def _unpack_pair(w):
    return lax.bitcast_convert_type(w << 16, F32), lax.bitcast_convert_type(w & U32(HI_MASK), F32)


def _store_chunked(dst_ref, row0, vals):
    n = vals.shape[0]
    for j in range(ROW_PITCH):
        if j < ROW_CH:
            piece = _pack_pair(vals[:, j * LANES:(j + 1) * LANES], vals[:, HALF_D + j * LANES:HALF_D + (j + 1) * LANES])
        else:
            piece = jnp.zeros((n, LANES), U32)
        dst_ref[pl.ds(row0 * ROW_PITCH + j, n, stride=ROW_PITCH), :] = piece


def _load_chunk(src_ref, row0, n, j):
    return _unpack_pair(src_ref[pl.ds(row0 * ROW_PITCH + j, n, stride=ROW_PITCH), :])


def _row_copy(src_ref, dst_ref, src_row, dst_row, sem):
    return pltpu.make_async_copy(src_ref.at[pl.ds(pl.multiple_of(src_row * ROW_PITCH, 8), ROW_CH), :],
                                 dst_ref.at[pl.ds(pl.multiple_of(dst_row * ROW_PITCH, 8), ROW_CH), :], sem)


def _gather_rows(src_hbm, dst_ref, src_row_of, n, sem, dst_row0=0):
    def body(r, carry):
        _row_copy(src_hbm, dst_ref, src_row_of(r), dst_row0 + r, sem).start()
        return carry

    lax.fori_loop(0, n, body, 0, unroll=DMA_UNROLL)


def _wait_rows(src_hbm, dst_ref, n, sem):
    def body(r, carry):
        _row_copy(src_hbm, dst_ref, 0, r, sem).wait()
        return carry

    lax.fori_loop(0, n, body, 0, unroll=DMA_UNROLL)


def _norm2_kernel(x_ref, g_ref, sh_ref, sc_ref, wr_ref, br_ref, hc_ref, id_ref, gt_ref, hb_ref, *, tm):
    g = g_ref[...]
    sc = 1.0 + sc_ref[0]
    sh = sh_ref[0]

    def body(r, carry):
        row0 = pl.multiple_of(r * NORM_ROWS, NORM_ROWS)
        xf = x_ref[pl.ds(row0, NORM_ROWS), :]
        ms = jnp.mean(xf * xf, axis=-1, keepdims=True)
        h = xf * lax.rsqrt(ms + EPS) * g * sc + sh
        hb_ref[pl.ds(row0, NORM_ROWS), :] = h.astype(BF16)
        _store_chunked(hc_ref, row0, h)
        return carry

    lax.fori_loop(0, tm // NORM_ROWS, body, 0)
    lg = jnp.dot(hb_ref[...], wr_ref[...], preferred_element_type=F32) + br_ref[...]
    id_ref[...], gt_ref[...] = _route_rows(lg)


def _norm2_router(x2d, g_norm, mod3, tiles_per_batch, w_router, b_router, tm):
    t, d = x2d.shape
    return pl.pallas_call(
        functools.partial(_norm2_kernel, tm=tm),
        grid=(t // tm,),
        in_specs=[pl.BlockSpec((tm, d), lambda i: (i, 0)),
                  pl.BlockSpec((1, d), lambda i: (0, 0)),
                  pl.BlockSpec((1, 1, d), lambda i: ((i // tiles_per_batch) * 6 + 3, 0, 0)),
                  pl.BlockSpec((1, 1, d), lambda i: ((i // tiles_per_batch) * 6 + 4, 0, 0)),
                  pl.BlockSpec((d, N_ROUTER), lambda i: (0, 0)),
                  pl.BlockSpec((1, N_ROUTER), lambda i: (0, 0))],
        out_specs=[pl.BlockSpec((tm * ROW_PITCH, LANES), lambda i: (i, 0)),
                   pl.BlockSpec((tm, N_ROUTER), lambda i: (i, 0)),
                   pl.BlockSpec((tm, N_ROUTER), lambda i: (i, 0))],
        out_shape=[jax.ShapeDtypeStruct((t * ROW_PITCH, LANES), U32),
                   jax.ShapeDtypeStruct((t, N_ROUTER), jnp.int32),
                   jax.ShapeDtypeStruct((t, N_ROUTER), F32)],
        scratch_shapes=[pltpu.VMEM((tm, d), BF16)],
        compiler_params=_cparams(("arbitrary",)),
        name="norm2_router",
    )(x2d, g_norm.reshape(1, d), mod3, mod3, w_router, b_router)


CAST_ROWS = 128
WEIGHT_DMA_PRIORITY = 1


def _cast_rows(src_ref, dst_ref):
    n = src_ref.shape[0]

    def body(r, carry):
        rows = pl.ds(pl.multiple_of(r * CAST_ROWS, CAST_ROWS), CAST_ROWS)
        dst_ref[rows, :] = src_ref[rows, :].astype(dst_ref.dtype)
        return carry

    lax.fori_loop(0, n // CAST_ROWS, body, 0)


def _expert_weights(i, nu, sched, w_hbms, w_bufs, w_bf16s, sem):
    be_ref, sl_ref, nx_ref, nv_ref = sched

    def copies(e, slot):
        return [pltpu.make_async_copy(w.at[e], buf.at[slot], sem.at[slot, k])
                for k, (w, buf) in enumerate(zip(w_hbms, w_bufs))]

    @pl.when(i == 0)
    def _():
        for cp in copies(be_ref[0], 0):
            cp.start(priority=WEIGHT_DMA_PRIORITY)

    first_block = (i == 0) | (be_ref[i] != be_ref[jnp.maximum(i - 1, 0)])

    @pl.when(first_block & (i < nu))
    def _():
        slot = sl_ref[i]
        for cp in copies(be_ref[i], slot):
            cp.wait()
        for buf, dst in zip(w_bufs, w_bf16s):
            _cast_rows(buf.at[slot], dst)

        @pl.when(nv_ref[i] == 1)
        def _():
            for cp in copies(nx_ref[i], 1 - slot):
                cp.start(priority=WEIGHT_DMA_PRIORITY)


def _moe_up_kernel(be_ref, nu_ref, sl_ref, nx_ref, nv_ref, tk_ref, bf_ref, br_ref, h_hbm, wg_hbm, wu_hbm, o_ref,
                   wg_buf, wu_buf, wgb_ref, wub_ref, xg_ref, xb_ref, sem, wsem):
    i = pl.program_id(0)
    nu = nu_ref[0]
    pad_entry = tk_ref.shape[0] - 8

    def start_gather(blk, slot):
        first, rows = bf_ref[blk], br_ref[blk]
        token_of = lambda r: tk_ref[jnp.where(r < rows, first + r, pad_entry)]
        _gather_rows(h_hbm, xg_ref.at[slot], token_of, MOE_BM, sem.at[slot])

    @pl.when(i == 0)
    def _():
        start_gather(0, 0)

    @pl.when(i + 1 < nu)
    def _():
        start_gather(i + 1, (i + 1) % 2)

    _expert_weights(i, nu, (be_ref, sl_ref, nx_ref, nv_ref), (wg_hbm, wu_hbm), (wg_buf, wu_buf), (wgb_ref, wub_ref),
                    wsem)

    def unpack_rows(slot):
        _wait_rows(h_hbm, xg_ref.at[slot], MOE_BM, sem.at[slot])
        for j in range(ROW_CH):
            lo, hi = _load_chunk(xg_ref.at[slot], 0, MOE_BM, j)
            xb_ref[:, pl.ds(j * LANES, LANES)] = lo.astype(BF16)
            xb_ref[:, pl.ds(HALF_D + j * LANES, LANES)] = hi.astype(BF16)

    @pl.when(i < nu)
    def _():
        for slot in range(2):
            pl.when(i % 2 == slot)(functools.partial(unpack_rows, slot))
        xb = xb_ref[...]
        g = jnp.dot(xb, wgb_ref[...], preferred_element_type=F32)
        u = jnp.dot(xb, wub_ref[...], preferred_element_type=F32)
        o_ref[...] = (g * jax.nn.sigmoid(g) * u).astype(o_ref.dtype)

    @pl.when(i >= nu)
    def _():
        o_ref[...] = jnp.zeros_like(o_ref)


def _moe_down_kernel(be_ref, nu_ref, sl_ref, nx_ref, nv_ref, h_ref, wd_hbm, o_ref, wd_buf, wdb_ref, wsem):
    i = pl.program_id(0)
    _expert_weights(i, nu_ref[0], (be_ref, sl_ref, nx_ref, nv_ref), (wd_hbm,), (wd_buf,), (wdb_ref,), wsem)

    @pl.when(i < nu_ref[0])
    def _():
        _store_chunked(o_ref, 0, jnp.dot(h_ref[...], wdb_ref[...], preferred_element_type=F32))

    @pl.when(i >= nu_ref[0])
    def _():
        o_ref[...] = jnp.zeros_like(o_ref)


def _moe_experts(h_chunked, tok_sorted, blk_first, blk_rows, blk_expert, n_used, w_gate, w_up, w_down):
    n_blocks = blk_expert.shape[0]
    p = n_blocks * MOE_BM
    _, d, de = w_gate.shape
    run = jnp.cumsum(jnp.concatenate([jnp.zeros((1,), jnp.int32),
                                      (blk_expert[1:] != blk_expert[:-1]).astype(jnp.int32)]))
    run_end = jnp.searchsorted(blk_expert, blk_expert, side='right').astype(jnp.int32)
    sched = (blk_expert, n_used, run % 2, blk_expert[jnp.minimum(run_end, n_blocks - 1)],
             (run_end < n_used[0]).astype(jnp.int32))
    any_spec = pl.BlockSpec(memory_space=pl.ANY)
    hmid = pl.pallas_call(
        _moe_up_kernel,
        grid_spec=pltpu.PrefetchScalarGridSpec(
            num_scalar_prefetch=8,
            grid=(n_blocks,),
            in_specs=[any_spec, any_spec, any_spec],
            out_specs=pl.BlockSpec((MOE_BM, de), lambda i, *_: (i, 0)),
            scratch_shapes=[pltpu.VMEM((2, d, de), F32), pltpu.VMEM((2, d, de), F32),
                            pltpu.VMEM((d, de), BF16), pltpu.VMEM((d, de), BF16),
                            pltpu.VMEM((2, MOE_BM * ROW_PITCH, LANES), U32), pltpu.VMEM((MOE_BM, d), BF16),
                            pltpu.SemaphoreType.DMA((2,)), pltpu.SemaphoreType.DMA((2, 2))]),
        out_shape=jax.ShapeDtypeStruct((p, de), BF16),
        compiler_params=_cparams(("arbitrary",)),
        name="moe_gate_up",
    )(*sched, tok_sorted, blk_first, blk_rows, h_chunked, w_gate, w_up)
    return pl.pallas_call(
        _moe_down_kernel,
        grid_spec=pltpu.PrefetchScalarGridSpec(
            num_scalar_prefetch=5,
            grid=(n_blocks,),
            in_specs=[pl.BlockSpec((MOE_BM, de), lambda i, *_: (i, 0)), any_spec],
            out_specs=pl.BlockSpec((MOE_BM * ROW_PITCH, LANES), lambda i, *_: (i, 0)),
            scratch_shapes=[pltpu.VMEM((2, de, d), F32), pltpu.VMEM((de, d), BF16),
                            pltpu.SemaphoreType.DMA((2, 1))]),
        out_shape=jax.ShapeDtypeStruct((p * ROW_PITCH, LANES), U32),
        compiler_params=_cparams(("arbitrary",)),
        name="moe_down",
    )(*sched, hmid, w_down)


COMBINE_TM = 256


def _combine_kernel(so_ref, y_hbm, x_ref, gt_ref, g2_ref, o_ref, yb_ref, sem, *, tm):
    i = pl.program_id(0)
    n = pl.num_programs(0)

    def start_gather(blk, slot):
        for k in range(TOP_K):
            _gather_rows(y_hbm, yb_ref.at[slot], lambda r: so_ref[(blk * tm + r) * TOP_K + k], tm, sem.at[slot],
                         dst_row0=k * tm)

    @pl.when(i == 0)
    def _():
        start_gather(0, 0)

    @pl.when(i + 1 < n)
    def _():
        start_gather(i + 1, (i + 1) % 2)

    def combine(slot):
        _wait_rows(y_hbm, yb_ref.at[slot], TOP_K * tm, sem.at[slot])
        g0 = gt_ref[:, 0:1]
        g1 = gt_ref[:, 1:2]
        for j in range(ROW_CH):
            lo0, hi0 = _load_chunk(yb_ref.at[slot], 0, tm, j)
            lo1, hi1 = _load_chunk(yb_ref.at[slot], tm, tm, j)
            for off, y in ((j * LANES, g0 * lo0 + g1 * lo1), (HALF_D + j * LANES, g0 * hi0 + g1 * hi1)):
                o_ref[:, pl.ds(off, LANES)] = x_ref[:, pl.ds(off, LANES)] + g2_ref[0][:, off:off + LANES] * y

    for slot in range(2):
        pl.when(i % 2 == slot)(functools.partial(combine, slot))


def _moe_combine(y_chunked, slot_of, x2d, gates, mod3, tiles_per_batch):
    t, d = x2d.shape
    tm = COMBINE_TM
    return pl.pallas_call(
        functools.partial(_combine_kernel, tm=tm),
        grid_spec=pltpu.PrefetchScalarGridSpec(
            num_scalar_prefetch=1,
            grid=(t // tm,),
            in_specs=[pl.BlockSpec(memory_space=pl.ANY),
                      pl.BlockSpec((tm, d), lambda i, so: (i, 0)),
                      pl.BlockSpec((tm, N_ROUTER), lambda i, so: (i, 0)),
                      pl.BlockSpec((1, 1, d), lambda i, so: ((i // tiles_per_batch) * 6 + 5, 0, 0))],
            out_specs=pl.BlockSpec((tm, d), lambda i, so: (i, 0)),
            scratch_shapes=[pltpu.VMEM((2, TOP_K * tm * ROW_PITCH, LANES), U32),
                            pltpu.SemaphoreType.DMA((2,))]),
        out_shape=jax.ShapeDtypeStruct((t, d), F32),
        compiler_params=_cparams(("arbitrary",)),
        name="moe_combine",
    )(slot_of, y_chunked, x2d, gates, mod3)


def _slots(eid, t):
    a = t * TOP_K
    iota = jnp.arange(a, dtype=jnp.int32)
    e_sorted, order = lax.sort((eid.reshape(a), iota), num_keys=1)
    experts = jnp.arange(N_EXPERTS, dtype=jnp.int32)
    start = jnp.searchsorted(e_sorted, experts, side='left').astype(jnp.int32)
    counts = jnp.searchsorted(e_sorted, experts, side='right').astype(jnp.int32) - start
    padded = (counts + MOE_BM - 1) // MOE_BM * MOE_BM
    pend = jnp.cumsum(padded)
    pstart = pend - padded
    dest = pstart[e_sorted] + iota - start[e_sorted]
    slot_of = lax.sort((order, dest), num_keys=1)[1]
    n_blocks = (a + MOE_BM - 1) // MOE_BM + N_EXPERTS
    blk = jnp.arange(n_blocks, dtype=jnp.int32) * MOE_BM
    blk_expert = jnp.minimum(jnp.searchsorted(pend, blk, side='right'), N_EXPERTS - 1).astype(jnp.int32)
    off = blk - pstart[blk_expert]
    blk_first = start[blk_expert] + off
    blk_rows = jnp.clip(counts[blk_expert] - off, 0, MOE_BM)
    tok_sorted = jnp.concatenate([order // TOP_K, jnp.zeros((8,), jnp.int32)])
    n_used = (pend[-1] // MOE_BM).astype(jnp.int32).reshape(1)
    return tok_sorted, blk_first, blk_rows, slot_of, blk_expert, n_used


GDN_TB = 256
HALO = 8
N_QKV = 3 * GDN_WIDTH


def _softplus(v):
    return jnp.maximum(v, 0.0) + jnp.log(1.0 + jnp.exp(-jnp.abs(v)))


def _split3_bf16(v):
    hi = v.astype(BF16)
    r1 = v - hi.astype(F32)
    mid = r1.astype(BF16)
    lo = (r1 - mid.astype(F32)).astype(BF16)
    return hi, mid, lo


def _gdn_prep_kernel(cur_ref, prev_ref, next_ref, sm_ref, cw_ref, ea_ref, dt_ref, o_ref, g_ref, ext_ref, *, tb):
    i = pl.program_id(1)
    nblk = pl.num_programs(1)
    def conv_cols(kind):
        def body(hh, carry):
            cols = pl.ds(pl.multiple_of((kind * GDN_HEADS + hh) * HEAD_DIM, HEAD_DIM), HEAD_DIM)
            cw = cw_ref[:, cols]
            ext_ref[pl.ds(0, HALO), :] = jnp.where(i > 0, prev_ref[:, cols], 0.0)
            ext_ref[pl.ds(HALO, tb), :] = cur_ref[:, cols]
            ext_ref[pl.ds(HALO + tb, HALO), :] = jnp.where(i < nblk - 1, next_ref[:, cols], 0.0)
            for r0 in range(0, tb, 64):
                acc = None
                for s in range(CONV_K):
                    term = ext_ref[pl.ds(HALO - CONV_K // 2 + s + r0, 64), :] * cw[s:s + 1, :]
                    acc = term if acc is None else acc + term
                y = acc * jax.nn.sigmoid(acc)
                if kind < 2:
                    y = y * lax.rsqrt(jnp.sum(y * y, axis=-1, keepdims=True) + EPS)
                if kind == 0:
                    y = y * HEAD_DIM ** -0.5
                o_ref[pl.ds(r0, 64), cols] = y.astype(o_ref.dtype)
            return carry

        lax.fori_loop(0, GDN_HEADS, body, 0)

    conv_cols(0)
    conv_cols(1)
    conv_cols(2)

    s = sm_ref[...]
    beta = jax.nn.sigmoid(s)
    g = -ea_ref[...] * _softplus(s + dt_ref[...])
    r = lax.broadcasted_iota(jnp.int32, (tb, tb), 0)
    c = lax.broadcasted_iota(jnp.int32, (tb, tb), 1)
    same = (r // GDN_CHUNK) == (c // GDN_CHUNK)
    lower = (same & (c <= r)).astype(BF16)
    upper = (same & (c >= r)).astype(BF16)
    parts = _split3_bf16(g)
    cf = sum(jnp.dot(lower, pt, preferred_element_type=F32) for pt in parts)
    cb = sum(jnp.dot(upper, pt, preferred_element_type=F32) for pt in parts)
    col = lax.broadcasted_iota(jnp.int32, s.shape, 1)
    nh = GDN_HEADS
    g_ref[...] = jnp.where(col < 2 * nh, beta, jnp.where(col < 3 * nh, cf, jnp.where(col < 4 * nh, cb, 0.0)))


def _gdn_prep(y_main, small, conv_w8, ea_row, dt_row, b_, l):
    tb = min(GDN_TB, l)
    nblk = l // tb
    hb = tb // HALO
    last = b_ * l // HALO - 1
    return pl.pallas_call(
        functools.partial(_gdn_prep_kernel, tb=tb),
        grid=(b_, nblk),
        in_specs=[pl.BlockSpec((tb, N_QKV), lambda b, i: (b * nblk + i, 0)),
                  pl.BlockSpec((HALO, N_QKV), lambda b, i: (jnp.maximum((b * nblk + i) * hb - 1, 0), 0)),
                  pl.BlockSpec((HALO, N_QKV), lambda b, i: (jnp.minimum((b * nblk + i + 1) * hb, last), 0)),
                  pl.BlockSpec((tb, N_SMALL), lambda b, i: (b * nblk + i, 0)),
                  pl.BlockSpec((HALO, N_QKV), lambda b, i: (0, 0)),
                  pl.BlockSpec((1, N_SMALL), lambda b, i: (0, 0)),
                  pl.BlockSpec((1, N_SMALL), lambda b, i: (0, 0))],
        out_specs=[pl.BlockSpec((tb, N_QKV), lambda b, i: (b * nblk + i, 0)),
                   pl.BlockSpec((tb, N_SMALL), lambda b, i: (b * nblk + i, 0))],
        out_shape=[jax.ShapeDtypeStruct((b_ * l, N_QKV), BF16),
                   jax.ShapeDtypeStruct((b_ * l, N_SMALL), F32)],
        scratch_shapes=[pltpu.VMEM((tb + 2 * HALO, HEAD_DIM), F32)],
        compiler_params=_cparams(("arbitrary", "arbitrary")),
        name="gdn_prep",
    )(y_main, y_main, y_main, small, conv_w8, ea_row, dt_row)


N_PAIRS = GDN_HEADS // 2
PK = 2 * GDN_CHUNK
INV_LEVELS = (2, 4, 8, 16, 32, 64)


def _gdn_masks(reverse):
    i = np.arange(PK)[:, None]
    j = np.arange(PK)[None, :]
    same = (i // GDN_CHUNK) == (j // GDN_CHUNK)
    strict = same & ((j > i) if reverse else (j < i))
    out = []
    for bs in INV_LEVELS:
        out.append(strict & (i // bs == j // bs) & (i // (bs // 2) != j // (bs // 2)))
    incl = same & ((j >= i) if reverse else (j <= i))
    out.append(incl)
    m = np.stack(out).astype(np.float32)
    neg = ((incl.astype(np.float32) - 1.0) * 1e30)[None]
    return jnp.asarray(np.concatenate([m, neg], axis=0))


def _gdn_scan_kernel(qf_ref, kf_ref, vf_ref, qb_ref, kb_ref, vb_ref, cpf_ref, rpf_ref, apf_ref, cpb_ref, rpb_ref,
                     apb_ref, s0_ref, mk_ref, of_ref, ob_ref, sfin_ref, s_scr):
    c = pl.program_id(1)

    @pl.when(c == 0)
    def _():
        s_scr[...] = s0_ref[0]

    nl = len(INV_LEVELS)
    ri = lax.broadcasted_iota(jnp.int32, (PK, 1), 0)
    top = (ri < GDN_CHUNK).astype(F32)
    bot = 1.0 - top
    rr = lax.broadcasted_iota(jnp.int32, (PK, PK), 0)
    cc = lax.broadcasted_iota(jnp.int32, (PK, PK), 1)
    eye = (rr == cc).astype(F32)
    dot = functools.partial(jnp.dot, preferred_element_type=F32)
    nt = (((1,), (1,)), ((), ()))
    tn = (((0,), (0,)), ((), ()))

    q_refs, k_refs, v_refs, o_refs = (qf_ref, qb_ref), (kf_ref, kb_ref), (vf_ref, vb_ref), (of_ref, ob_ref)
    cps = (cpf_ref[0, 0], cpb_ref[0, 0])
    rps = (rpf_ref[0, 0], rpb_ref[0, 0])
    aps = (apf_ref[0, 0], apb_ref[0, 0])
    units = [(d, p) for p in range(N_PAIRS) for d in range(2)]
    per_unit = lambda f: [f(i, d, p) for i, (d, p) in enumerate(units)]
    cols = lambda p: (pl.ds(2 * p * HEAD_DIM, HEAD_DIM), pl.ds((2 * p + 1) * HEAD_DIM, HEAD_DIM))
    pack = lambda ref, p: jnp.concatenate([ref[:, cols(p)[0]], ref[:, cols(p)[1]]], axis=0)
    mask = lambda d, n: mk_ref[d, n]

    gcol = per_unit(lambda i, d, p: cps[d][:, p:p + 1])
    bcol = per_unit(lambda i, d, p: cps[d][:, N_PAIRS + p:N_PAIRS + p + 1])
    glcol = per_unit(lambda i, d, p: cps[d][:, 2 * N_PAIRS + p:2 * N_PAIRS + p + 1])
    kp = per_unit(lambda i, d, p: pack(k_refs[d], p))
    qp = per_unit(lambda i, d, p: pack(q_refs[d], p))
    kk = per_unit(lambda i, d, p: lax.dot_general(kp[i], kp[i], nt, preferred_element_type=F32))
    qk = per_unit(lambda i, d, p: lax.dot_general(qp[i], kp[i], nt, preferred_element_type=F32))
    dec = per_unit(lambda i, d, p: jnp.exp((gcol[i] - rps[d][p:p + 1, :]) * mask(d, nl) + mask(d, nl + 1)))
    a = per_unit(lambda i, d, p: kk[i] * dec[i] * bcol[i])
    qkm = per_unit(lambda i, d, p: (qk[i] * dec[i]).astype(BF16))

    x = per_unit(lambda i, d, p: eye - a[i] * mask(d, 0))
    for lv in range(1, nl):
        xb = per_unit(lambda i, d, p: x[i].astype(BF16))
        po = per_unit(lambda i, d, p: dot(xb[i], (a[i] * mask(d, lv)).astype(BF16)))
        x = per_unit(lambda i, d, p: x[i] - dot(po[i].astype(BF16), xb[i]))
    tb = per_unit(lambda i, d, p: x[i].astype(BF16))

    egc = per_unit(lambda i, d, p: jnp.exp(gcol[i]))
    kf = per_unit(lambda i, d, p: kp[i].astype(F32))
    u = per_unit(lambda i, d, p: dot(tb[i], (pack(v_refs[d], p).astype(F32) * bcol[i]).astype(BF16)))
    w = per_unit(lambda i, d, p: dot(tb[i], (kf[i] * (bcol[i] * egc[i])).astype(BF16)))
    qd = per_unit(lambda i, d, p: qp[i].astype(F32) * egc[i])
    kd = per_unit(lambda i, d, p: kf[i] * jnp.exp(glcol[i] - gcol[i]))

    s = per_unit(lambda i, d, p: s_scr[d, p])
    lhs = per_unit(lambda i, d, p: jnp.concatenate(
        [jnp.concatenate([w[i] * top, w[i] * bot], axis=1),
         jnp.concatenate([qd[i] * top, qd[i] * bot], axis=1)], axis=0).astype(BF16))
    ws = per_unit(lambda i, d, p: dot(lhs[i], s[i].astype(BF16)))
    vnb = per_unit(lambda i, d, p: (u[i] - ws[i][:PK]).astype(BF16))
    o = per_unit(lambda i, d, p: ws[i][PK:] + dot(qkm[i], vnb[i]))
    kbd = per_unit(lambda i, d, p: jnp.concatenate([kd[i] * top, kd[i] * bot], axis=1).astype(BF16))
    kv = per_unit(lambda i, d, p: lax.dot_general(kbd[i], vnb[i], tn, preferred_element_type=F32))
    for i, (d, p) in enumerate(units):
        s_scr[d, p] = jnp.exp(aps[d][:, p:p + 1]) * s[i] + kv[i]
    for i, (d, p) in enumerate(units):
        o_refs[d][:, cols(p)[0]] = o[i][:GDN_CHUNK].astype(o_refs[d].dtype)
        o_refs[d][:, cols(p)[1]] = o[i][GDN_CHUNK:].astype(o_refs[d].dtype)

    @pl.when(c == pl.num_programs(1) - 1)
    def _():
        sfin_ref[0] = s_scr[...]


def _gdn_out_kernel(of_ref, ob_ref, z_ref, gon_ref, o_ref):
    for h in range(GDN_HEADS):
        c = pl.ds(h * HEAD_DIM, HEAD_DIM)
        o = of_ref[:, c].astype(F32) + ob_ref[:, c].astype(F32)
        z = z_ref[:, c]
        y = o * lax.rsqrt(jnp.mean(o * o, axis=-1, keepdims=True) + EPS) * gon_ref[...]
        o_ref[:, c] = (y * (z * jax.nn.sigmoid(z))).astype(o_ref.dtype)


def _gdn_out(o_f, o_b, y_main, g_on, tm):
    t = o_f.shape[0]
    blk = lambda col: pl.BlockSpec((tm, GDN_WIDTH), lambda i: (i, col))
    return pl.pallas_call(
        _gdn_out_kernel,
        grid=(t // tm,),
        in_specs=[blk(0), blk(0), blk(N_QKV // GDN_WIDTH), pl.BlockSpec((1, HEAD_DIM), lambda i: (0, 0))],
        out_specs=blk(0),
        out_shape=jax.ShapeDtypeStruct((t, GDN_WIDTH), BF16),
        compiler_params=_cparams(("arbitrary",)),
        name="gdn_out",
    )(o_f, o_b, y_main, g_on.reshape(1, HEAD_DIM))


def _gdn_packs(gates, b_, l, reverse):
    nc = l // GDN_CHUNK
    nh = GDN_HEADS
    g4 = gates.reshape(b_, nc, GDN_CHUNK, N_SMALL)
    d = 1 if reverse else 0
    beta = g4[..., d * nh:(d + 1) * nh]
    gc = g4[..., (2 + d) * nh:(3 + d) * nh]
    gl = jnp.broadcast_to(gc[:, :, 0:1] if reverse else gc[:, :, GDN_CHUNK - 1:GDN_CHUNK], gc.shape)

    def rowpack(t):
        return jnp.transpose(t.reshape(b_, nc, GDN_CHUNK, N_PAIRS, 2), (0, 1, 3, 4, 2)).reshape(b_, nc, N_PAIRS, PK)

    rp = rowpack(gc)
    cp = jnp.concatenate([jnp.swapaxes(rowpack(t), 2, 3) for t in (gc, beta, gl)]
                         + [jnp.zeros((b_, nc, PK, N_PAIRS), F32)], axis=-1)
    glh = gl[:, :, 0].reshape(b_, nc, N_PAIRS, 2)
    ap = jnp.swapaxes(jnp.repeat(glh, HEAD_DIM, axis=-1), 2, 3)
    return cp, rp, ap


def _gdn_scan(qkv, gates, s0, b_, l):
    nc = l // GDN_CHUNK
    packs = [_gdn_packs(gates, b_, l, reverse) for reverse in (False, True)]
    masks = jnp.stack([_gdn_masks(False), _gdn_masks(True)])
    chunk_of = (lambda c: c, lambda c: nc - 1 - c)
    tok = lambda d, col: pl.BlockSpec((GDN_CHUNK, GDN_WIDTH), lambda b, c: (b * nc + chunk_of[d](c), col))
    per_chunk = lambda d, shp: pl.BlockSpec((1, 1) + shp, lambda b, c: (b, chunk_of[d](c), 0, 0))
    gate_specs = lambda d: [per_chunk(d, (PK, 4 * N_PAIRS)), per_chunk(d, (N_PAIRS, PK)),
                            per_chunk(d, (2 * HEAD_DIM, N_PAIRS))]
    state = pl.BlockSpec((1, 2, N_PAIRS, 2 * HEAD_DIM, HEAD_DIM), lambda b, c: (b, 0, 0, 0, 0))
    return pl.pallas_call(
        _gdn_scan_kernel,
        grid=(b_, nc),
        in_specs=[tok(0, 0), tok(0, 1), tok(0, 2), tok(1, 0), tok(1, 1), tok(1, 2)] + gate_specs(0) + gate_specs(1)
        + [state, pl.BlockSpec(masks.shape, lambda b, c: (0, 0, 0, 0))],
        out_specs=[tok(0, 0), tok(1, 0), state],
        out_shape=[jax.ShapeDtypeStruct((b_ * l, GDN_WIDTH), BF16),
                   jax.ShapeDtypeStruct((b_ * l, GDN_WIDTH), BF16),
                   jax.ShapeDtypeStruct(s0.shape, F32)],
        scratch_shapes=[pltpu.VMEM((2, N_PAIRS, 2 * HEAD_DIM, HEAD_DIM), F32)],
        compiler_params=_cparams(("arbitrary", "arbitrary")),
        name="gdn_scan",
    )(qkv, qkv, qkv, qkv, qkv, qkv, *packs[0], *packs[1], s0, masks)


def _gdn_mixer(yx, sx, yc, sc, conv_w, a_log, dt_bias, g_on, b_, l, n_ctx):
    nh = GDN_HEADS
    conv_w8 = jnp.pad(conv_w, ((0, HALO - CONV_K), (0, 0)))
    ea_row = jnp.zeros((1, N_SMALL), F32).at[0, 2 * nh:4 * nh].set(jnp.exp(a_log.reshape(-1)))
    dt_row = jnp.zeros((1, N_SMALL), F32).at[0, 2 * nh:4 * nh].set(dt_bias.reshape(-1))
    qkv_c, gates_c = _gdn_prep(yc, sc, conv_w8, ea_row, dt_row, b_, n_ctx)
    qkv_x, gates_x = _gdn_prep(yx, sx, conv_w8, ea_row, dt_row, b_, l)
    zero = jnp.zeros((b_, 2, N_PAIRS, 2 * HEAD_DIM, HEAD_DIM), F32)
    _, _, s_ctx = _gdn_scan(qkv_c, gates_c, zero, b_, n_ctx)
    o_f, o_b, _ = _gdn_scan(qkv_x, gates_x, s_ctx, b_, l)
    return _gdn_out(o_f, o_b, yx, g_on, 256)


SWA_GROUP = SWA_HEADS // SWA_KV_HEADS
ROT = HEAD_DIM // 4


def _rope_tables(l):
    half = HEAD_DIM // 2
    inv = ROPE_BASE ** (-jnp.arange(0, half, 2, dtype=F32) / half)
    pos = jnp.arange(l, dtype=jnp.int32)
    ang_r = (pos // GRID_W).astype(F32)[:, None] * inv
    ang_c = (pos % GRID_W).astype(F32)[:, None] * inv
    zero = jnp.zeros_like(ang_r)
    cos = jnp.concatenate([jnp.cos(ang_r), jnp.cos(ang_r), jnp.cos(ang_c), jnp.cos(ang_c)], axis=1)
    sin_up = jnp.concatenate([-jnp.sin(ang_r), zero, -jnp.sin(ang_c), zero], axis=1)
    sin_dn = jnp.concatenate([zero, jnp.sin(ang_r), zero, jnp.sin(ang_c)], axis=1)
    return cos, sin_up, sin_dn


def _swa_prep_kernel(q_ref, k_ref, v_ref, cos_ref, su_ref, sd_ref, gq_ref, gk_ref, qo_ref, ko_ref, vo_ref):
    cos, su, sd = cos_ref[...], su_ref[...], sd_ref[...]

    def norm_rope(t, g, scale):
        y = t * lax.rsqrt(jnp.mean(t * t, axis=-1, keepdims=True) + EPS) * g
        y = y * cos + pltpu.roll(y, HEAD_DIM - ROT, 1) * su + pltpu.roll(y, ROT, 1) * sd
        return y * scale if scale != 1.0 else y

    for h in range(SWA_HEADS):
        c = pl.ds(h * HEAD_DIM, HEAD_DIM)
        qo_ref[:, c] = norm_rope(q_ref[:, c], gq_ref[...], HEAD_DIM ** -0.5).astype(qo_ref.dtype)
    for h in range(SWA_KV_HEADS):
        c = pl.ds(h * HEAD_DIM, HEAD_DIM)
        ko_ref[:, c] = norm_rope(k_ref[:, c], gk_ref[...], 1.0).astype(ko_ref.dtype)
    vo_ref[...] = v_ref[...].astype(vo_ref.dtype)


def _swa_prep(y_main, tables, g_q, g_k, rows, tm):
    q_blk = (N_QKV + GDN_WIDTH) // SWA_WIDTH
    k_blk = (N_QKV + GDN_WIDTH + SWA_WIDTH) // SWA_KV_WIDTH
    tpb = tables[0].shape[0] // tm
    tab = pl.BlockSpec((tm, HEAD_DIM), lambda i: (i % tpb, 0))
    vec = pl.BlockSpec((1, HEAD_DIM), lambda i: (0, 0))
    return pl.pallas_call(
        _swa_prep_kernel,
        grid=(rows // tm,),
        in_specs=[pl.BlockSpec((tm, SWA_WIDTH), lambda i: (i, q_blk)),
                  pl.BlockSpec((tm, SWA_KV_WIDTH), lambda i: (i, k_blk)),
                  pl.BlockSpec((tm, SWA_KV_WIDTH), lambda i: (i, k_blk + 1)),
                  tab, tab, tab, vec, vec],
        out_specs=[pl.BlockSpec((tm, SWA_WIDTH), lambda i: (i, 0)),
                   pl.BlockSpec((tm, SWA_KV_WIDTH), lambda i: (i, 0)),
                   pl.BlockSpec((tm, SWA_KV_WIDTH), lambda i: (i, 0))],
        out_shape=[jax.ShapeDtypeStruct((rows, SWA_WIDTH), BF16),
                   jax.ShapeDtypeStruct((rows, SWA_KV_WIDTH), BF16),
                   jax.ShapeDtypeStruct((rows, SWA_KV_WIDTH), BF16)],
        compiler_params=_cparams(("arbitrary",)),
        name="swa_prep",
    )(y_main, y_main, y_main, *tables, g_q.reshape(1, HEAD_DIM), g_k.reshape(1, HEAD_DIM))


def _swa_attn_kernel(q_ref, kp_ref, kc_ref, kn_ref, vp_ref, vc_ref, vn_ref, kx_ref, vx_ref, sink_ref, o_ref, *, n_ctx):
    n = pl.program_id(1)
    nb = pl.num_programs(1)
    rows = SWA_GROUP * Q_BLOCK
    nk = n_ctx + 3 * Q_BLOCK
    qi = lax.broadcasted_iota(jnp.int32, (rows, nk), 0) & (Q_BLOCK - 1)
    kj = lax.broadcasted_iota(jnp.int32, (rows, nk), 1) - n_ctx
    lo = jnp.where(n == 0, Q_BLOCK, 0)
    hi = jnp.where(n == nb - 1, 2 * Q_BLOCK, 3 * Q_BLOCK)
    valid = (kj < 0) | ((kj >= qi) & (kj <= qi + 2 * WINDOW) & (kj >= lo) & (kj < hi))
    hsel = lax.broadcasted_iota(jnp.int32, (rows, 1), 0) // Q_BLOCK
    nt = (((1,), (1,)), ((), ()))
    for j in range(SWA_KV_HEADS):
        c = pl.ds(j * HEAD_DIM, HEAD_DIM)
        heads = [j * SWA_GROUP + g for g in range(SWA_GROUP)]
        q = jnp.concatenate([q_ref[:, pl.ds(h * HEAD_DIM, HEAD_DIM)] for h in heads], axis=0)
        k = jnp.concatenate([kx_ref[:, c], kp_ref[:, c], kc_ref[:, c], kn_ref[:, c]], axis=0)
        v = jnp.concatenate([vx_ref[:, c], vp_ref[:, c], vc_ref[:, c], vn_ref[:, c]], axis=0)
        sink = jnp.zeros((rows, 1), F32)
        for g, h in enumerate(heads):
            sink = jnp.where(hsel == g, sink_ref[h:h + 1, 0:1], sink)
        s = jnp.where(valid, lax.dot_general(q, k, nt, preferred_element_type=F32), NEG_INF)
        m = jnp.maximum(jnp.max(s, axis=-1, keepdims=True), sink)
        p = jnp.exp(s - m)
        den = jnp.sum(p, axis=-1, keepdims=True) + jnp.exp(sink - m)
        o = jnp.dot(p.astype(BF16), v, preferred_element_type=F32) / den
        for g, h in enumerate(heads):
            o_ref[:, pl.ds(h * HEAD_DIM, HEAD_DIM)] = o[g * Q_BLOCK:(g + 1) * Q_BLOCK].astype(o_ref.dtype)


def _swa_attention(q, k, v, k_ctx, v_ctx, sink, b_, l, n_ctx):
    nb = l // Q_BLOCK
    blk = lambda w, off: pl.BlockSpec(
        (Q_BLOCK, w), lambda b, n: (b * nb + jnp.clip(n + off, 0, nb - 1), 0))
    ctx = pl.BlockSpec((n_ctx, SWA_KV_WIDTH), lambda b, n: (b, 0))
    kw = SWA_KV_WIDTH
    return pl.pallas_call(
        functools.partial(_swa_attn_kernel, n_ctx=n_ctx),
        grid=(b_, nb),
        in_specs=[blk(SWA_WIDTH, 0), blk(kw, -1), blk(kw, 0), blk(kw, 1), blk(kw, -1), blk(kw, 0), blk(kw, 1),
                  ctx, ctx, pl.BlockSpec((SWA_HEADS, HEAD_DIM), lambda b, n: (0, 0))],
        out_specs=blk(SWA_WIDTH, 0),
        out_shape=jax.ShapeDtypeStruct((b_ * l, SWA_WIDTH), BF16),
        compiler_params=_cparams(("arbitrary", "arbitrary")),
        name="swa_attention",
    )(q, k, k, k, v, v, v, k_ctx, v_ctx, jnp.broadcast_to(sink.astype(F32)[:, None], (SWA_HEADS, HEAD_DIM)))


def _swa_mixer(yx, yc, g_q, g_k, sink, b_, l, n_ctx):
    ones = jnp.ones((n_ctx, HEAD_DIM), F32)
    zeros = jnp.zeros((n_ctx, HEAD_DIM), F32)
    qx, kx, vx = _swa_prep(yx, _rope_tables(l), g_q, g_k, b_ * l, 256)
    _, kc, vc = _swa_prep(yc, (ones, zeros, zeros), g_q, g_k, b_ * n_ctx, n_ctx)
    return _swa_attention(qx, kx, vx, kc, vc, sink, b_, l, n_ctx)


def _rmsnorm(x, g):
    xf = x.astype(F32)
    y = xf * lax.rsqrt(jnp.mean(xf * xf, axis=-1, keepdims=True) + EPS)
    return (y * g.astype(F32)).astype(x.dtype)


def _l2norm(x):
    return x * lax.rsqrt(jnp.sum(x * x, axis=-1, keepdims=True) + EPS)


def _short_conv(x, w):
    c = x.shape[-1]
    y = lax.conv_general_dilated(x, w[:, None, :].astype(x.dtype), window_strides=(1,),
                                 padding=[(CONV_K // 2, CONV_K // 2)],
                                 dimension_numbers=('NWC', 'WIO', 'NWC'), feature_group_count=c)
    return jax.nn.silu(y)


def _axial_rope(t, rows, cols):
    half = HEAD_DIM // 2
    inv = ROPE_BASE ** (-jnp.arange(0, half, 2, dtype=F32) / half)

    def rot(u, pos):
        ang = pos[:, None] * inv
        cos, sin = jnp.cos(ang)[None, :, None, :], jnp.sin(ang)[None, :, None, :]
        u1, u2 = jnp.split(u.astype(F32), 2, axis=-1)
        return jnp.concatenate([u1 * cos - u2 * sin, u2 * cos + u1 * sin], axis=-1)

    return jnp.concatenate([rot(t[..., :half], rows), rot(t[..., half:], cols)], axis=-1).astype(t.dtype)


def _gated_delta_chunked(q, k, v, g, beta, s0):
    b_, l, h, _ = q.shape
    n = l // GDN_CHUNK

    def chunks(t):
        t = t.reshape((b_, n, GDN_CHUNK) + t.shape[2:])
        return jnp.moveaxis(jnp.swapaxes(t, 2, 3), 1, 0)

    qc, kc, vc, gc, bc = map(chunks, (q, k, v, g, beta))
    gcum = jnp.cumsum(gc, axis=-1)
    tri = jnp.tril(jnp.ones((GDN_CHUNK, GDN_CHUNK), bool))
    strict = jnp.tril(jnp.ones((GDN_CHUNK, GDN_CHUNK), F32), -1)
    diff = gcum[..., :, None] - gcum[..., None, :]
    decay = jnp.where(tri, jnp.exp(jnp.where(tri, diff, 0.0)), 0.0)
    kb = kc * bc[..., None]
    a_strict = jnp.einsum('nbhid,nbhjd->nbhij', kb, kc) * decay * strict
    eye = jnp.eye(GDN_CHUNK, dtype=F32)
    t_inv = lax.linalg.triangular_solve(eye + a_strict, jnp.broadcast_to(eye, a_strict.shape),
                                        left_side=True, lower=True, unit_diagonal=True)
    u = jnp.einsum('nbhij,nbhjd->nbhid', t_inv, vc * bc[..., None])
    w = jnp.einsum('nbhij,nbhjd->nbhid', t_inv, kb * jnp.exp(gcum)[..., None])
    qk = jnp.einsum('nbhid,nbhjd->nbhij', qc, kc) * decay
    q_dec = qc * jnp.exp(gcum)[..., None]
    k_dec = kc * jnp.exp(gcum[..., -1:] - gcum)[..., None]
    chunk_decay = jnp.exp(gcum[..., -1])

    def step(s, xs):
        u_n, w_n, qk_n, q_n, k_n, a_n = xs
        v_new = u_n - jnp.einsum('bhcd,bhde->bhce', w_n, s)
        o = jnp.einsum('bhcd,bhde->bhce', q_n, s) + jnp.einsum('bhij,bhje->bhie', qk_n, v_new)
        s = s * a_n[..., None, None] + jnp.einsum('bhcd,bhce->bhde', k_n, v_new)
        return s, o

    s_fin, o = lax.scan(step, s0, (u, w, qk, q_dec, k_dec, chunk_decay))
    o = jnp.swapaxes(jnp.moveaxis(o, 0, 1), 2, 3).reshape(b_, l, h, -1)
    return o, s_fin


def _gdn_inputs(qkv, b_logit, a_logit, conv_w, a_log, dt_bias):
    b_, l, _ = qkv.shape
    qkv = _short_conv(qkv, conv_w).astype(F32).reshape(b_, l, 3, GDN_HEADS, HEAD_DIM)
    q = _l2norm(qkv[:, :, 0]) * HEAD_DIM ** -0.5
    k = _l2norm(qkv[:, :, 1])
    v = qkv[:, :, 2]
    beta = jax.nn.sigmoid(b_logit.astype(F32).reshape(b_, l, 2, GDN_HEADS))
    g = -jnp.exp(a_log.astype(F32)) * jax.nn.softplus(
        a_logit.astype(F32).reshape(b_, l, 2, GDN_HEADS) + dt_bias.astype(F32))
    return q, k, v, g, beta


def _gdn_bidir(q, k, v, g, beta, s_fwd0, s_bwd0):
    o_f, s_f = _gated_delta_chunked(q, k, v, g[:, :, 0], beta[:, :, 0], s_fwd0)
    rev = lambda t: jnp.flip(t, axis=1)
    o_b, s_b = _gated_delta_chunked(rev(q), rev(k), rev(v), rev(g[:, :, 1]), rev(beta[:, :, 1]), s_bwd0)
    return o_f + rev(o_b), s_f, s_b


def _gdn_output(o, z, g_on):
    b_, l = o.shape[:2]
    y = _rmsnorm(o, g_on).reshape(b_, l, GDN_WIDTH)
    return y * jax.nn.silu(z.astype(F32))


def _swa_inputs(q, k, v, g_q, g_k):
    b_, l, _ = q.shape
    q = _rmsnorm(q.reshape(b_, l, SWA_HEADS, HEAD_DIM), g_q)
    k = _rmsnorm(k.reshape(b_, l, SWA_KV_HEADS, HEAD_DIM), g_k)
    return q, k, v.reshape(b_, l, SWA_KV_HEADS, HEAD_DIM)


def _window_attention(q, k, v, k_ctx, v_ctx, sink):
    b_, l, h, d = q.shape
    g = h // SWA_KV_HEADS
    nb = l // Q_BLOCK
    n_ctx = k_ctx.shape[1]
    scale = d ** -0.5
    qb = q.reshape(b_, nb, Q_BLOCK, SWA_KV_HEADS, g, d)

    def band_blocks(t):
        tp = jnp.pad(t, ((0, 0), (Q_BLOCK, Q_BLOCK), (0, 0), (0, 0))).reshape(b_, nb + 2, Q_BLOCK, SWA_KV_HEADS, d)
        return jnp.concatenate([tp[:, :-2], tp[:, 1:-1], tp[:, 2:]], axis=2)

    kw, vw = band_blocks(k), band_blocks(v)
    qi = jnp.arange(Q_BLOCK)[:, None]
    kj = jnp.arange(3 * Q_BLOCK)[None, :]
    band = jnp.abs(kj - Q_BLOCK - qi) <= WINDOW
    kpos = jnp.arange(nb)[:, None] * Q_BLOCK - Q_BLOCK + jnp.arange(3 * Q_BLOCK)[None, :]
    valid = band[None] & ((kpos >= 0) & (kpos < l))[:, None, :]
    s_win = jnp.einsum('bnqkgd,bnjkd->bnkgqj', qb, kw).astype(F32) * scale
    s_win = jnp.where(valid[None, :, None, None], s_win, NEG_INF)
    s_ctx = jnp.einsum('bnqkgd,bckd->bnkgqc', qb, k_ctx).astype(F32) * scale
    s_sink = jnp.broadcast_to(sink.astype(F32).reshape(SWA_KV_HEADS, g, 1, 1), s_ctx.shape[:-1] + (1,))
    p = jax.nn.softmax(jnp.concatenate([s_ctx, s_win, s_sink], axis=-1), axis=-1).astype(v.dtype)
    o = (jnp.einsum('bnkgqc,bckd->bnqkgd', p[..., :n_ctx], v_ctx)
         + jnp.einsum('bnkgqj,bnjkd->bnqkgd', p[..., n_ctx:n_ctx + 3 * Q_BLOCK], vw))
    return o.reshape(b_, l, h * d)


def _split_main(y):
    o = np.cumsum((0, 3 * GDN_WIDTH, GDN_WIDTH, SWA_WIDTH, SWA_KV_WIDTH, SWA_KV_WIDTH))
    return tuple(y[..., int(o[n]):int(o[n + 1])] for n in range(5))


def kernel(x, c, ctx, c_ctx, w_ada, b_ada, g_norm1, g_norm2, w_in, conv_qkv, a_log, dt_bias, g_onorm, g_qnorm,
           g_knorm, sink, w_out, w_router_grp, b_router_grp, w_router_exp, b_router_exp, w_gate, w_up, w_down):
    b_, l, d = x.shape
    n_ctx = ctx.shape[1]
    t = b_ * l
    assert w_ada.shape[0] == 1 and d == D_MODEL and b_ + 1 <= MOD_ROWS
    rows = jnp.repeat(jnp.arange(l // GRID_W, dtype=F32), GRID_W)
    cols = jnp.tile(jnp.arange(GRID_W, dtype=F32), l // GRID_W)

    wi = w_in[0]
    w_main = jnp.concatenate([wi[:, IN_OFFS[0]:IN_OFFS[2]], wi[:, IN_OFFS[4]:IN_OFFS[7]]], axis=1).astype(BF16)
    w_small = jnp.pad(wi[:, IN_OFFS[2]:IN_OFFS[4]], ((0, 0), (0, N_SMALL - 4 * GDN_HEADS))).astype(BF16)
    wo = w_out[0].astype(BF16)
    w_router = jnp.pad(jnp.concatenate([w_router_grp[0], w_router_exp[0]], axis=1),
                       ((0, 0), (0, N_ROUTER - N_GROUPS - N_EXPERTS))).astype(BF16)
    b_router = jnp.pad(jnp.concatenate([b_router_grp[0], b_router_exp[0]]),
                       (0, N_ROUTER - N_GROUPS - N_EXPERTS)).reshape(1, N_ROUTER)

    c_rows = jnp.zeros((MOD_ROWS, d), F32).at[:b_].set(c).at[b_].set(c_ctx)
    mod = _modulation(c_rows, w_ada[0], b_ada[0])
    mod3 = mod.reshape(MOD_ROWS * 6, 1, d)

    tm = 512
    tpb = l // tm
    yx, sx = _in_projection(x.reshape(t, d), g_norm1[0], mod3, lambda i: i // tpb, w_main, w_small, tm)
    yc, sc = _in_projection(ctx.reshape(b_ * n_ctx, d), g_norm1[0], mod3, lambda i: b_, w_main, w_small, n_ctx)
    ya_x = _gdn_mixer(yx, sx, yc, sc, conv_qkv[0], a_log[0], dt_bias[0], g_onorm[0], b_, l, n_ctx)
    yb_x = _swa_mixer(yx, yc, g_qnorm[0], g_knorm[0], sink[0], b_, l, n_ctx)

    x1 = _out_projection(ya_x, yb_x, wo[:GDN_WIDTH], wo[GDN_WIDTH:], x.reshape(t, d), mod3, tpb, tm)

    tm2 = 256
    h2c, ids, gates = _norm2_router(x1, g_norm2[0], mod3, l // tm2, w_router, b_router, tm2)
    tok_sorted, blk_first, blk_rows, slot_of, blk_expert, n_used = _slots(ids[:, :TOP_K], t)
    yc_moe = _moe_experts(h2c, tok_sorted, blk_first, blk_rows, blk_expert, n_used, w_gate[0], w_up[0], w_down[0])
    return _moe_combine(yc_moe, slot_of, x1, gates, mod3, l // COMBINE_TM).reshape(b_, l, d)
```

```python
import functools
import math

import jax
import jax.numpy as jnp
import numpy as np
from jax import lax
from jax.experimental import pallas as pl
from jax.experimental.pallas import tpu as pltpu

F32 = jnp.float32
BF16 = jnp.bfloat16

D_MODEL = 4096
CTX_LEN = 256
GRID_W = 64
HEAD_DIM = 128
GDN_HEADS = 16
GDN_WIDTH = GDN_HEADS * HEAD_DIM
GDN_CHUNK = 64
CONV_K = 5
SWA_HEADS = 16
SWA_KV_HEADS = 4
SWA_WIDTH = SWA_HEADS * HEAD_DIM
SWA_KV_WIDTH = SWA_KV_HEADS * HEAD_DIM
WINDOW = 128
Q_BLOCK = 128
ROPE_BASE = 10000.0
N_GROUPS = 8
EXPERTS_PER_GROUP = 8
N_EXPERTS = N_GROUPS * EXPERTS_PER_GROUP
TOP_K = 2
D_EXPERT = D_MODEL // 8
EPS = 1e-6
NEG_INF = -1e30

IN_SIZES = (3 * GDN_WIDTH, GDN_WIDTH, 2 * GDN_HEADS, 2 * GDN_HEADS, SWA_WIDTH, SWA_KV_WIDTH, SWA_KV_WIDTH)
IN_OFFS = tuple(int(v) for v in np.cumsum((0,) + IN_SIZES))
N_MAIN = 3 * GDN_WIDTH + GDN_WIDTH + SWA_WIDTH + 2 * SWA_KV_WIDTH
N_SMALL = 128
N_ROUTER = 128
MOD_ROWS = 8

MOE_BM = 256
VMEM_LIMIT = 56 * 1024 * 1024


def _cparams(sem):
    return pltpu.CompilerParams(dimension_semantics=sem, vmem_limit_bytes=VMEM_LIMIT)


def _mod_kernel(c_ref, w_ref, b_ref, o_ref):
    c = c_ref[...]
    a = (c * jax.nn.sigmoid(c)).astype(BF16)
    o_ref[...] = jnp.dot(a, w_ref[...].astype(BF16), preferred_element_type=F32) + b_ref[...]


def _modulation(c_rows, w_ada, b_ada):
    d, n = w_ada.shape
    tn = 512
    return pl.pallas_call(
        _mod_kernel,
        grid=(n // tn,),
        in_specs=[pl.BlockSpec((MOD_ROWS, d), lambda j: (0, 0)),
                  pl.BlockSpec((d, tn), lambda j: (0, j)),
                  pl.BlockSpec((1, tn), lambda j: (0, j))],
        out_specs=pl.BlockSpec((MOD_ROWS, tn), lambda j: (0, j)),
        out_shape=jax.ShapeDtypeStruct((MOD_ROWS, n), F32),
        compiler_params=_cparams(("arbitrary",)),
        name="modulation",
    )(c_rows, w_ada, b_ada.reshape(1, n))


NORM_ROWS = 64


def _norm_mod_rows(x_ref, g_ref, sh_ref, sc_ref, h_ref, tm):
    g = g_ref[...]
    sc = 1.0 + sc_ref[0]
    sh = sh_ref[0]

    def body(r, carry):
        rows = pl.ds(pl.multiple_of(r * NORM_ROWS, NORM_ROWS), NORM_ROWS)
        xf = x_ref[rows, :]
        ms = jnp.mean(xf * xf, axis=-1, keepdims=True)
        y = xf * lax.rsqrt(ms + EPS) * g
        h_ref[rows, :] = (y * sc + sh).astype(h_ref.dtype)
        return carry

    lax.fori_loop(0, tm // NORM_ROWS, body, 0)


def _inproj_kernel(x_ref, g_ref, sh_ref, sc_ref, w_ref, ws_ref, o_ref, os_ref, h_ref, *, tm):
    @pl.when(pl.program_id(1) == 0)
    def _():
        _norm_mod_rows(x_ref, g_ref, sh_ref, sc_ref, h_ref, tm)
        os_ref[...] = jnp.dot(h_ref[...], ws_ref[...], preferred_element_type=F32)

    o_ref[...] = jnp.dot(h_ref[...], w_ref[...], preferred_element_type=F32)


def _in_projection(x2d, g_norm, mod3, mod_row_of_tile, w_main, w_small, tm):
    t, d = x2d.shape
    tn = 1024
    return pl.pallas_call(
        functools.partial(_inproj_kernel, tm=tm),
        grid=(t // tm, N_MAIN // tn),
        in_specs=[pl.BlockSpec((tm, d), lambda i, j: (i, 0)),
                  pl.BlockSpec((1, d), lambda i, j: (0, 0)),
                  pl.BlockSpec((1, 1, d), lambda i, j: (mod_row_of_tile(i) * 6 + 0, 0, 0)),
                  pl.BlockSpec((1, 1, d), lambda i, j: (mod_row_of_tile(i) * 6 + 1, 0, 0)),
                  pl.BlockSpec((d, tn), lambda i, j: (0, j)),
                  pl.BlockSpec((d, N_SMALL), lambda i, j: (0, 0))],
        out_specs=[pl.BlockSpec((tm, tn), lambda i, j: (i, j)),
                   pl.BlockSpec((tm, N_SMALL), lambda i, j: (i, 0))],
        out_shape=[jax.ShapeDtypeStruct((t, N_MAIN), F32),
                   jax.ShapeDtypeStruct((t, N_SMALL), F32)],
        scratch_shapes=[pltpu.VMEM((tm, d), BF16)],
        compiler_params=_cparams(("arbitrary", "arbitrary")),
        name="in_projection",
    )(x2d, g_norm.reshape(1, d), mod3, mod3, w_main, w_small)


def _outproj_kernel(ya_ref, yb_ref, wa_ref, wb_ref, x_ref, gate_ref, o_ref):
    acc = jnp.dot(ya_ref[...], wa_ref[...], preferred_element_type=F32)
    acc = acc + jnp.dot(yb_ref[...], wb_ref[...], preferred_element_type=F32)
    o_ref[...] = x_ref[...] + gate_ref[0] * acc


def _out_projection(ya, yb, wa, wb, x2d, mod3, tiles_per_batch, tm):
    t, d = x2d.shape
    tn = 1024
    nj = d // tn
    ka, kb = ya.shape[1], yb.shape[1]
    return pl.pallas_call(
        _outproj_kernel,
        grid=(t // tm, nj),
        in_specs=[pl.BlockSpec((tm, ka), lambda i, j: (i, 0)),
                  pl.BlockSpec((tm, kb), lambda i, j: (i, 0)),
                  pl.BlockSpec((ka, tn), lambda i, j: (0, j)),
                  pl.BlockSpec((kb, tn), lambda i, j: (0, j)),
                  pl.BlockSpec((tm, tn), lambda i, j: (i, j)),
                  pl.BlockSpec((1, 1, tn), lambda i, j: (((i // tiles_per_batch) * 6 + 2) * nj + j, 0, 0))],
        out_specs=pl.BlockSpec((tm, tn), lambda i, j: (i, j)),
        out_shape=jax.ShapeDtypeStruct((t, d), F32),
        compiler_params=_cparams(("arbitrary", "arbitrary")),
        name="out_projection",
    )(ya, yb, wa, wb, x2d, mod3.reshape(-1, 1, tn))


def _route_rows(lg):
    col = lax.broadcasted_iota(jnp.int32, lg.shape, 1)
    first = lambda hit: jnp.min(jnp.where(hit, col, N_ROUTER), axis=-1, keepdims=True)
    gm = col < N_GROUPS
    mg = jnp.max(jnp.where(gm, lg, NEG_INF), axis=-1, keepdims=True)
    grp = first(gm & (lg == mg))
    p_grp = 1.0 / jnp.sum(jnp.where(gm, jnp.exp(lg - mg), 0.0), axis=-1, keepdims=True)
    lo = N_GROUPS + grp * EXPERTS_PER_GROUP
    em = (col >= lo) & (col < lo + EXPERTS_PER_GROUP)
    m1 = jnp.max(jnp.where(em, lg, NEG_INF), axis=-1, keepdims=True)
    i1 = first(em & (lg == m1))
    em2 = em & (col != i1)
    m2 = jnp.max(jnp.where(em2, lg, NEG_INF), axis=-1, keepdims=True)
    i2 = first(em2 & (lg == m2))
    e2 = jnp.exp(m2 - m1)
    g1 = p_grp / (1.0 + e2)
    ids = jnp.where(col == 0, i1 - N_GROUPS, jnp.where(col == 1, i2 - N_GROUPS, 0))
    gates = jnp.where(col == 0, g1, jnp.where(col == 1, g1 * e2, 0.0))
    return ids, gates


LANES = 128
HALF_D = D_MODEL // 2
ROW_CH = HALF_D // LANES
ROW_PITCH = 24
U32 = jnp.uint32
HI_MASK = 0xFFFF0000
DMA_UNROLL = 8


def _pack_pair(lo, hi):
    bits = lambda v: lax.bitcast_convert_type(v.astype(BF16).astype(F32), U32)
    return (bits(hi) & U32(HI_MASK)) | (bits(lo) >> 16)


def _unpack_pair(w):
    return lax.bitcast_convert_type(w << 16, F32), lax.bitcast_convert_type(w & U32(HI_MASK), F32)


def _store_chunked(dst_ref, row0, vals):
    n = vals.shape[0]
    for j in range(ROW_PITCH):
        if j < ROW_CH:
            piece = _pack_pair(vals[:, j * LANES:(j + 1) * LANES], vals[:, HALF_D + j * LANES:HALF_D + (j + 1) * LANES])
        else:
            piece = jnp.zeros((n, LANES), U32)
        dst_ref[pl.ds(row0 * ROW_PITCH + j, n, stride=ROW_PITCH), :] = piece


def _load_chunk(src_ref, row0, n, j):
    return _unpack_pair(src_ref[pl.ds(row0 * ROW_PITCH + j, n, stride=ROW_PITCH), :])


def _row_copy(src_ref, dst_ref, src_row, dst_row, sem):
    return pltpu.make_async_copy(src_ref.at[pl.ds(pl.multiple_of(src_row * ROW_PITCH, 8), ROW_CH), :],
                                 dst_ref.at[pl.ds(pl.multiple_of(dst_row * ROW_PITCH, 8), ROW_CH), :], sem)


def _gather_rows(src_hbm, dst_ref, src_row_of, n, sem, dst_row0=0):
    def body(r, carry):
        _row_copy(src_hbm, dst_ref, src_row_of(r), dst_row0 + r, sem).start()
        return carry

    lax.fori_loop(0, n, body, 0, unroll=DMA_UNROLL)


def _wait_rows(src_hbm, dst_ref, n, sem):
    def body(r, carry):
        _row_copy(src_hbm, dst_ref, 0, r, sem).wait()
        return carry

    lax.fori_loop(0, n, body, 0, unroll=DMA_UNROLL)


def _norm2_kernel(x_ref, g_ref, sh_ref, sc_ref, wr_ref, br_ref, hc_ref, id_ref, gt_ref, hb_ref, *, tm):
    g = g_ref[...]
    sc = 1.0 + sc_ref[0]
    sh = sh_ref[0]

    def body(r, carry):
        row0 = pl.multiple_of(r * NORM_ROWS, NORM_ROWS)
        xf = x_ref[pl.ds(row0, NORM_ROWS), :]
        ms = jnp.mean(xf * xf, axis=-1, keepdims=True)
        h = xf * lax.rsqrt(ms + EPS) * g * sc + sh
        hb_ref[pl.ds(row0, NORM_ROWS), :] = h.astype(BF16)
        _store_chunked(hc_ref, row0, h)
        return carry

    lax.fori_loop(0, tm // NORM_ROWS, body, 0)
    lg = jnp.dot(hb_ref[...], wr_ref[...], preferred_element_type=F32) + br_ref[...]
    id_ref[...], gt_ref[...] = _route_rows(lg)


def _norm2_router(x2d, g_norm, mod3, tiles_per_batch, w_router, b_router, tm):
    t, d = x2d.shape
    return pl.pallas_call(
        functools.partial(_norm2_kernel, tm=tm),
        grid=(t // tm,),
        in_specs=[pl.BlockSpec((tm, d), lambda i: (i, 0)),
                  pl.BlockSpec((1, d), lambda i: (0, 0)),
                  pl.BlockSpec((1, 1, d), lambda i: ((i // tiles_per_batch) * 6 + 3, 0, 0)),
                  pl.BlockSpec((1, 1, d), lambda i: ((i // tiles_per_batch) * 6 + 4, 0, 0)),
                  pl.BlockSpec((d, N_ROUTER), lambda i: (0, 0)),
                  pl.BlockSpec((1, N_ROUTER), lambda i: (0, 0))],
        out_specs=[pl.BlockSpec((tm * ROW_PITCH, LANES), lambda i: (i, 0)),
                   pl.BlockSpec((tm, N_ROUTER), lambda i: (i, 0)),
                   pl.BlockSpec((tm, N_ROUTER), lambda i: (i, 0))],
        out_shape=[jax.ShapeDtypeStruct((t * ROW_PITCH, LANES), U32),
                   jax.ShapeDtypeStruct((t, N_ROUTER), jnp.int32),
                   jax.ShapeDtypeStruct((t, N_ROUTER), F32)],
        scratch_shapes=[pltpu.VMEM((tm, d), BF16)],
        compiler_params=_cparams(("arbitrary",)),
        name="norm2_router",
    )(x2d, g_norm.reshape(1, d), mod3, mod3, w_router, b_router)


CAST_ROWS = 128
WEIGHT_DMA_PRIORITY = 0


def _cast_rows(src_ref, dst_ref):
    n = src_ref.shape[0]

    def body(r, carry):
        rows = pl.ds(pl.multiple_of(r * CAST_ROWS, CAST_ROWS), CAST_ROWS)
        dst_ref[rows, :] = src_ref[rows, :].astype(dst_ref.dtype)
        return carry

    lax.fori_loop(0, n // CAST_ROWS, body, 0)


def _expert_weights(i, nu, sched, w_hbms, w_bufs, w_bf16s, sem):
    be_ref, sl_ref, nx_ref, nv_ref = sched

    def copies(e, slot):
        return [pltpu.make_async_copy(w.at[e], buf.at[slot], sem.at[slot, k])
                for k, (w, buf) in enumerate(zip(w_hbms, w_bufs))]

    @pl.when(i == 0)
    def _():
        for cp in copies(be_ref[0], 0):
            cp.start(priority=WEIGHT_DMA_PRIORITY)

    first_block = (i == 0) | (be_ref[i] != be_ref[jnp.maximum(i - 1, 0)])

    @pl.when(first_block & (i < nu))
    def _():
        slot = sl_ref[i]
        for cp in copies(be_ref[i], slot):
            cp.wait()
        for buf, dst in zip(w_bufs, w_bf16s):
            _cast_rows(buf.at[slot], dst)

        @pl.when(nv_ref[i] == 1)
        def _():
            for cp in copies(nx_ref[i], 1 - slot):
                cp.start(priority=WEIGHT_DMA_PRIORITY)


def _moe_up_kernel(be_ref, nu_ref, sl_ref, nx_ref, nv_ref, tk_ref, bf_ref, br_ref, h_hbm, wg_hbm, wu_hbm, o_ref,
                   wg_buf, wu_buf, wgb_ref, wub_ref, xg_ref, xb_ref, sem, wsem):
    i = pl.program_id(0)
    nu = nu_ref[0]
    pad_entry = tk_ref.shape[0] - 8

    def start_gather(blk, slot):
        first, rows = bf_ref[blk], br_ref[blk]
        token_of = lambda r: tk_ref[jnp.where(r < rows, first + r, pad_entry)]
        _gather_rows(h_hbm, xg_ref.at[slot], token_of, MOE_BM, sem.at[slot])

    @pl.when(i == 0)
    def _():
        start_gather(0, 0)

    @pl.when(i + 1 < nu)
    def _():
        start_gather(i + 1, (i + 1) % 2)

    _expert_weights(i, nu, (be_ref, sl_ref, nx_ref, nv_ref), (wg_hbm, wu_hbm), (wg_buf, wu_buf), (wgb_ref, wub_ref),
                    wsem)

    def unpack_rows(slot):
        _wait_rows(h_hbm, xg_ref.at[slot], MOE_BM, sem.at[slot])
        for j in range(ROW_CH):
            lo, hi = _load_chunk(xg_ref.at[slot], 0, MOE_BM, j)
            xb_ref[:, pl.ds(j * LANES, LANES)] = lo.astype(BF16)
            xb_ref[:, pl.ds(HALF_D + j * LANES, LANES)] = hi.astype(BF16)

    @pl.when(i < nu)
    def _():
        for slot in range(2):
            pl.when(i % 2 == slot)(functools.partial(unpack_rows, slot))
        xb = xb_ref[...]
        g = jnp.dot(xb, wgb_ref[...], preferred_element_type=F32)
        u = jnp.dot(xb, wub_ref[...], preferred_element_type=F32)
        o_ref[...] = (g * jax.nn.sigmoid(g) * u).astype(o_ref.dtype)

    @pl.when(i >= nu)
    def _():
        o_ref[...] = jnp.zeros_like(o_ref)


def _moe_down_kernel(be_ref, nu_ref, sl_ref, nx_ref, nv_ref, h_ref, wd_hbm, o_ref, wd_buf, wdb_ref, wsem):
    i = pl.program_id(0)
    _expert_weights(i, nu_ref[0], (be_ref, sl_ref, nx_ref, nv_ref), (wd_hbm,), (wd_buf,), (wdb_ref,), wsem)

    @pl.when(i < nu_ref[0])
    def _():
        _store_chunked(o_ref, 0, jnp.dot(h_ref[...], wdb_ref[...], preferred_element_type=F32))

    @pl.when(i >= nu_ref[0])
    def _():
        o_ref[...] = jnp.zeros_like(o_ref)


def _moe_experts(h_chunked, tok_sorted, blk_first, blk_rows, blk_expert, n_used, w_gate, w_up, w_down):
    n_blocks = blk_expert.shape[0]
    p = n_blocks * MOE_BM
    _, d, de = w_gate.shape
    run = jnp.cumsum(jnp.concatenate([jnp.zeros((1,), jnp.int32),
                                      (blk_expert[1:] != blk_expert[:-1]).astype(jnp.int32)]))
    run_end = jnp.searchsorted(blk_expert, blk_expert, side='right').astype(jnp.int32)
    sched = (blk_expert, n_used, run % 2, blk_expert[jnp.minimum(run_end, n_blocks - 1)],
             (run_end < n_used[0]).astype(jnp.int32))
    any_spec = pl.BlockSpec(memory_space=pl.ANY)
    hmid = pl.pallas_call(
        _moe_up_kernel,
        grid_spec=pltpu.PrefetchScalarGridSpec(
            num_scalar_prefetch=8,
            grid=(n_blocks,),
            in_specs=[any_spec, any_spec, any_spec],
            out_specs=pl.BlockSpec((MOE_BM, de), lambda i, *_: (i, 0)),
            scratch_shapes=[pltpu.VMEM((2, d, de), F32), pltpu.VMEM((2, d, de), F32),
                            pltpu.VMEM((d, de), BF16), pltpu.VMEM((d, de), BF16),
                            pltpu.VMEM((2, MOE_BM * ROW_PITCH, LANES), U32), pltpu.VMEM((MOE_BM, d), BF16),
                            pltpu.SemaphoreType.DMA((2,)), pltpu.SemaphoreType.DMA((2, 2))]),
        out_shape=jax.ShapeDtypeStruct((p, de), BF16),
        compiler_params=_cparams(("arbitrary",)),
        name="moe_gate_up",
    )(*sched, tok_sorted, blk_first, blk_rows, h_chunked, w_gate, w_up)
    return pl.pallas_call(
        _moe_down_kernel,
        grid_spec=pltpu.PrefetchScalarGridSpec(
            num_scalar_prefetch=5,
            grid=(n_blocks,),
            in_specs=[pl.BlockSpec((MOE_BM, de), lambda i, *_: (i, 0)), any_spec],
            out_specs=pl.BlockSpec((MOE_BM * ROW_PITCH, LANES), lambda i, *_: (i, 0)),
            scratch_shapes=[pltpu.VMEM((2, de, d), F32), pltpu.VMEM((de, d), BF16),
                            pltpu.SemaphoreType.DMA((2, 1))]),
        out_shape=jax.ShapeDtypeStruct((p * ROW_PITCH, LANES), U32),
        compiler_params=_cparams(("arbitrary",)),
        name="moe_down",
    )(*sched, hmid, w_down)


COMBINE_TM = 256


def _combine_kernel(so_ref, y_hbm, x_ref, gt_ref, g2_ref, o_ref, yb_ref, sem, *, tm):
    i = pl.program_id(0)
    n = pl.num_programs(0)

    def start_gather(blk, slot):
        for k in range(TOP_K):
            _gather_rows(y_hbm, yb_ref.at[slot], lambda r: so_ref[(blk * tm + r) * TOP_K + k], tm, sem.at[slot],
                         dst_row0=k * tm)

    @pl.when(i == 0)
    def _():
        start_gather(0, 0)

    @pl.when(i + 1 < n)
    def _():
        start_gather(i + 1, (i + 1) % 2)

    def combine(slot):
        _wait_rows(y_hbm, yb_ref.at[slot], TOP_K * tm, sem.at[slot])
        g0 = gt_ref[:, 0:1]
        g1 = gt_ref[:, 1:2]
        for j in range(ROW_CH):
            lo0, hi0 = _load_chunk(yb_ref.at[slot], 0, tm, j)
            lo1, hi1 = _load_chunk(yb_ref.at[slot], tm, tm, j)
            for off, y in ((j * LANES, g0 * lo0 + g1 * lo1), (HALF_D + j * LANES, g0 * hi0 + g1 * hi1)):
                o_ref[:, pl.ds(off, LANES)] = x_ref[:, pl.ds(off, LANES)] + g2_ref[0][:, off:off + LANES] * y

    for slot in range(2):
        pl.when(i % 2 == slot)(functools.partial(combine, slot))


def _moe_combine(y_chunked, slot_of, x2d, gates, mod3, tiles_per_batch):
    t, d = x2d.shape
    tm = COMBINE_TM
    return pl.pallas_call(
        functools.partial(_combine_kernel, tm=tm),
        grid_spec=pltpu.PrefetchScalarGridSpec(
            num_scalar_prefetch=1,
            grid=(t // tm,),
            in_specs=[pl.BlockSpec(memory_space=pl.ANY),
                      pl.BlockSpec((tm, d), lambda i, so: (i, 0)),
                      pl.BlockSpec((tm, N_ROUTER), lambda i, so: (i, 0)),
                      pl.BlockSpec((1, 1, d), lambda i, so: ((i // tiles_per_batch) * 6 + 5, 0, 0))],
            out_specs=pl.BlockSpec((tm, d), lambda i, so: (i, 0)),
            scratch_shapes=[pltpu.VMEM((2, TOP_K * tm * ROW_PITCH, LANES), U32),
                            pltpu.SemaphoreType.DMA((2,))]),
        out_shape=jax.ShapeDtypeStruct((t, d), F32),
        compiler_params=_cparams(("arbitrary",)),
        name="moe_combine",
    )(slot_of, y_chunked, x2d, gates, mod3)


def _slots(eid, t):
    a = t * TOP_K
    iota = jnp.arange(a, dtype=jnp.int32)
    e_sorted, order = lax.sort((eid.reshape(a), iota), num_keys=1)
    experts = jnp.arange(N_EXPERTS, dtype=jnp.int32)
    start = jnp.searchsorted(e_sorted, experts, side='left').astype(jnp.int32)
    counts = jnp.searchsorted(e_sorted, experts, side='right').astype(jnp.int32) - start
    padded = (counts + MOE_BM - 1) // MOE_BM * MOE_BM
    pend = jnp.cumsum(padded)
    pstart = pend - padded
    dest = pstart[e_sorted] + iota - start[e_sorted]
    slot_of = lax.sort((order, dest), num_keys=1)[1]
    n_blocks = (a + MOE_BM - 1) // MOE_BM + N_EXPERTS
    blk = jnp.arange(n_blocks, dtype=jnp.int32) * MOE_BM
    blk_expert = jnp.minimum(jnp.searchsorted(pend, blk, side='right'), N_EXPERTS - 1).astype(jnp.int32)
    off = blk - pstart[blk_expert]
    blk_first = start[blk_expert] + off
    blk_rows = jnp.clip(counts[blk_expert] - off, 0, MOE_BM)
    tok_sorted = jnp.concatenate([order // TOP_K, jnp.zeros((8,), jnp.int32)])
    n_used = (pend[-1] // MOE_BM).astype(jnp.int32).reshape(1)
    return tok_sorted, blk_first, blk_rows, slot_of, blk_expert, n_used


GDN_TB = 256
HALO = 8
PREP_SLABS = 2
N_QKV = 3 * GDN_WIDTH


def _softplus(v):
    return jnp.maximum(v, 0.0) + jnp.log(1.0 + jnp.exp(-jnp.abs(v)))


def _split3_bf16(v):
    hi = v.astype(BF16)
    r1 = v - hi.astype(F32)
    mid = r1.astype(BF16)
    lo = (r1 - mid.astype(F32)).astype(BF16)
    return hi, mid, lo


def _gdn_prep_kernel(cur_ref, prev_ref, next_ref, sm_ref, cw_ref, ea_ref, dt_ref, o_ref, g_ref, ext_ref, *, tb):
    i = pl.program_id(1)
    nblk = pl.num_programs(1)
    def conv_head(kind, hh, slab):
        cols = pl.ds(pl.multiple_of((kind * GDN_HEADS + hh) * HEAD_DIM, HEAD_DIM), HEAD_DIM)
        cw = cw_ref[:, cols]
        ext = ext_ref.at[slab]
        ext[pl.ds(0, HALO), :] = jnp.where(i > 0, prev_ref[:, cols], 0.0)
        ext[pl.ds(HALO, tb), :] = cur_ref[:, cols]
        ext[pl.ds(HALO + tb, HALO), :] = jnp.where(i < nblk - 1, next_ref[:, cols], 0.0)
        for r0 in range(0, tb, 64):
            acc = None
            for s in range(CONV_K):
                term = ext[pl.ds(HALO - CONV_K // 2 + s + r0, 64), :] * cw[s:s + 1, :]
                acc = term if acc is None else acc + term
            y = acc * jax.nn.sigmoid(acc)
            if kind < 2:
                y = y * lax.rsqrt(jnp.sum(y * y, axis=-1, keepdims=True) + EPS)
            if kind == 0:
                y = y * HEAD_DIM ** -0.5
            o_ref[pl.ds(r0, 64), cols] = y.astype(o_ref.dtype)

    def conv_cols(kind):
        def body(it, carry):
            for slab in range(PREP_SLABS):
                conv_head(kind, it * PREP_SLABS + slab, slab)
            return carry

        lax.fori_loop(0, GDN_HEADS // PREP_SLABS, body, 0)

    conv_cols(0)
    conv_cols(1)
    conv_cols(2)

    s = sm_ref[...]
    beta = jax.nn.sigmoid(s)
    g = -ea_ref[...] * _softplus(s + dt_ref[...])
    r = lax.broadcasted_iota(jnp.int32, (tb, tb), 0)
    c = lax.broadcasted_iota(jnp.int32, (tb, tb), 1)
    same = (r // GDN_CHUNK) == (c // GDN_CHUNK)
    lower = (same & (c <= r)).astype(BF16)
    upper = (same & (c >= r)).astype(BF16)
    parts = _split3_bf16(g)
    cf = sum(jnp.dot(lower, pt, preferred_element_type=F32) for pt in parts)
    cb = sum(jnp.dot(upper, pt, preferred_element_type=F32) for pt in parts)
    col = lax.broadcasted_iota(jnp.int32, s.shape, 1)
    nh = GDN_HEADS
    g_ref[...] = jnp.where(col < 2 * nh, beta, jnp.where(col < 3 * nh, cf, jnp.where(col < 4 * nh, cb, 0.0)))


def _gdn_prep(y_main, small, conv_w8, ea_row, dt_row, b_, l):
    tb = min(GDN_TB, l)
    nblk = l // tb
    hb = tb // HALO
    last = b_ * l // HALO - 1
    return pl.pallas_call(
        functools.partial(_gdn_prep_kernel, tb=tb),
        grid=(b_, nblk),
        in_specs=[pl.BlockSpec((tb, N_QKV), lambda b, i: (b * nblk + i, 0)),
                  pl.BlockSpec((HALO, N_QKV), lambda b, i: (jnp.maximum((b * nblk + i) * hb - 1, 0), 0)),
                  pl.BlockSpec((HALO, N_QKV), lambda b, i: (jnp.minimum((b * nblk + i + 1) * hb, last), 0)),
                  pl.BlockSpec((tb, N_SMALL), lambda b, i: (b * nblk + i, 0)),
                  pl.BlockSpec((HALO, N_QKV), lambda b, i: (0, 0)),
                  pl.BlockSpec((1, N_SMALL), lambda b, i: (0, 0)),
                  pl.BlockSpec((1, N_SMALL), lambda b, i: (0, 0))],
        out_specs=[pl.BlockSpec((tb, N_QKV), lambda b, i: (b * nblk + i, 0)),
                   pl.BlockSpec((tb, N_SMALL), lambda b, i: (b * nblk + i, 0))],
        out_shape=[jax.ShapeDtypeStruct((b_ * l, N_QKV), BF16),
                   jax.ShapeDtypeStruct((b_ * l, N_SMALL), F32)],
        scratch_shapes=[pltpu.VMEM((PREP_SLABS, tb + 2 * HALO, HEAD_DIM), F32)],
        compiler_params=_cparams(("arbitrary", "arbitrary")),
        name="gdn_prep",
    )(y_main, y_main, y_main, small, conv_w8, ea_row, dt_row)


N_PAIRS = GDN_HEADS // 2
PK = 2 * GDN_CHUNK
INV_LEVELS = (2, 4, 8, 16, 32, 64)


def _gdn_masks(reverse):
    i = np.arange(PK)[:, None]
    j = np.arange(PK)[None, :]
    same = (i // GDN_CHUNK) == (j // GDN_CHUNK)
    strict = same & ((j > i) if reverse else (j < i))
    out = []
    for bs in INV_LEVELS:
        out.append(strict & (i // bs == j // bs) & (i // (bs // 2) != j // (bs // 2)))
    incl = same & ((j >= i) if reverse else (j <= i))
    out.append(incl)
    m = np.stack(out).astype(np.float32)
    neg = ((incl.astype(np.float32) - 1.0) * 1e30)[None]
    return jnp.asarray(np.concatenate([m, neg], axis=0))


def _gdn_scan_kernel(qf_ref, kf_ref, vf_ref, qb_ref, kb_ref, vb_ref, cpf_ref, rpf_ref, apf_ref, cpb_ref, rpb_ref,
                     apb_ref, s0_ref, mk_ref, of_ref, ob_ref, sfin_ref, s_scr):
    c = pl.program_id(1)

    @pl.when(c == 0)
    def _():
        s_scr[...] = s0_ref[0]

    nl = len(INV_LEVELS)
    ri = lax.broadcasted_iota(jnp.int32, (PK, 1), 0)
    top = (ri < GDN_CHUNK).astype(F32)
    bot = 1.0 - top
    rr = lax.broadcasted_iota(jnp.int32, (PK, PK), 0)
    cc = lax.broadcasted_iota(jnp.int32, (PK, PK), 1)
    eye = (rr == cc).astype(F32)
    dot = functools.partial(jnp.dot, preferred_element_type=F32)
    nt = (((1,), (1,)), ((), ()))
    tn = (((0,), (0,)), ((), ()))

    q_refs, k_refs, v_refs, o_refs = (qf_ref, qb_ref), (kf_ref, kb_ref), (vf_ref, vb_ref), (of_ref, ob_ref)
    cps = (cpf_ref[0, 0], cpb_ref[0, 0])
    rps = (rpf_ref[0, 0], rpb_ref[0, 0])
    aps = (apf_ref[0, 0], apb_ref[0, 0])
    units = [(d, p) for p in range(N_PAIRS) for d in range(2)]
    per_unit = lambda f: [f(i, d, p) for i, (d, p) in enumerate(units)]
    cols = lambda p: (pl.ds(2 * p * HEAD_DIM, HEAD_DIM), pl.ds((2 * p + 1) * HEAD_DIM, HEAD_DIM))
    pack = lambda ref, p: jnp.concatenate([ref[:, cols(p)[0]], ref[:, cols(p)[1]]], axis=0)
    mask = lambda d, n: mk_ref[d, n]

    gcol = per_unit(lambda i, d, p: cps[d][:, p:p + 1])
    bcol = per_unit(lambda i, d, p: cps[d][:, N_PAIRS + p:N_PAIRS + p + 1])
    glcol = per_unit(lambda i, d, p: cps[d][:, 2 * N_PAIRS + p:2 * N_PAIRS + p + 1])
    kp = per_unit(lambda i, d, p: pack(k_refs[d], p))
    qp = per_unit(lambda i, d, p: pack(q_refs[d], p))
    kk = per_unit(lambda i, d, p: lax.dot_general(kp[i], kp[i], nt, preferred_element_type=F32))
    qk = per_unit(lambda i, d, p: lax.dot_general(qp[i], kp[i], nt, preferred_element_type=F32))
    dec = per_unit(lambda i, d, p: jnp.exp((gcol[i] - rps[d][p:p + 1, :]) * mask(d, nl) + mask(d, nl + 1)))
    a = per_unit(lambda i, d, p: kk[i] * dec[i] * bcol[i])
    qkm = per_unit(lambda i, d, p: (qk[i] * dec[i]).astype(BF16))

    x = per_unit(lambda i, d, p: eye - a[i] * mask(d, 0))
    for lv in range(1, nl):
        xb = per_unit(lambda i, d, p: x[i].astype(BF16))
        po = per_unit(lambda i, d, p: dot(xb[i], (a[i] * mask(d, lv)).astype(BF16)))
        x = per_unit(lambda i, d, p: x[i] - dot(po[i].astype(BF16), xb[i]))
    tb = per_unit(lambda i, d, p: x[i].astype(BF16))

    egc = per_unit(lambda i, d, p: jnp.exp(gcol[i]))
    kf = per_unit(lambda i, d, p: kp[i].astype(F32))
    u = per_unit(lambda i, d, p: dot(tb[i], (pack(v_refs[d], p).astype(F32) * bcol[i]).astype(BF16)))
    w = per_unit(lambda i, d, p: dot(tb[i], (kf[i] * (bcol[i] * egc[i])).astype(BF16)))
    qd = per_unit(lambda i, d, p: qp[i].astype(F32) * egc[i])
    kd = per_unit(lambda i, d, p: kf[i] * jnp.exp(glcol[i] - gcol[i]))

    s = per_unit(lambda i, d, p: s_scr[d, p])
    lhs = per_unit(lambda i, d, p: jnp.concatenate(
        [jnp.concatenate([w[i] * top, w[i] * bot], axis=1),
         jnp.concatenate([qd[i] * top, qd[i] * bot], axis=1)], axis=0).astype(BF16))
    ws = per_unit(lambda i, d, p: dot(lhs[i], s[i].astype(BF16)))
    vnb = per_unit(lambda i, d, p: (u[i] - ws[i][:PK]).astype(BF16))
    o = per_unit(lambda i, d, p: ws[i][PK:] + dot(qkm[i], vnb[i]))
    kbd = per_unit(lambda i, d, p: jnp.concatenate([kd[i] * top, kd[i] * bot], axis=1).astype(BF16))
    kv = per_unit(lambda i, d, p: lax.dot_general(kbd[i], vnb[i], tn, preferred_element_type=F32))
    for i, (d, p) in enumerate(units):
        s_scr[d, p] = jnp.exp(aps[d][:, p:p + 1]) * s[i] + kv[i]
    for i, (d, p) in enumerate(units):
        o_refs[d][:, cols(p)[0]] = o[i][:GDN_CHUNK].astype(o_refs[d].dtype)
        o_refs[d][:, cols(p)[1]] = o[i][GDN_CHUNK:].astype(o_refs[d].dtype)

    @pl.when(c == pl.num_programs(1) - 1)
    def _():
        sfin_ref[0] = s_scr[...]


def _gdn_out_kernel(of_ref, ob_ref, z_ref, gon_ref, o_ref):
    for h in range(GDN_HEADS):
        c = pl.ds(h * HEAD_DIM, HEAD_DIM)
        o = of_ref[:, c].astype(F32) + ob_ref[:, c].astype(F32)
        z = z_ref[:, c]
        y = o * lax.rsqrt(jnp.mean(o * o, axis=-1, keepdims=True) + EPS) * gon_ref[...]
        o_ref[:, c] = (y * (z * jax.nn.sigmoid(z))).astype(o_ref.dtype)


def _gdn_out(o_f, o_b, y_main, g_on, tm):
    t = o_f.shape[0]
    blk = lambda col: pl.BlockSpec((tm, GDN_WIDTH), lambda i: (i, col))
    return pl.pallas_call(
        _gdn_out_kernel,
        grid=(t // tm,),
        in_specs=[blk(0), blk(0), blk(N_QKV // GDN_WIDTH), pl.BlockSpec((1, HEAD_DIM), lambda i: (0, 0))],
        out_specs=blk(0),
        out_shape=jax.ShapeDtypeStruct((t, GDN_WIDTH), BF16),
        compiler_params=_cparams(("arbitrary",)),
        name="gdn_out",
    )(o_f, o_b, y_main, g_on.reshape(1, HEAD_DIM))


def _gdn_packs(gates, b_, l, reverse):
    nc = l // GDN_CHUNK
    nh = GDN_HEADS
    g4 = gates.reshape(b_, nc, GDN_CHUNK, N_SMALL)
    d = 1 if reverse else 0
    beta = g4[..., d * nh:(d + 1) * nh]
    gc = g4[..., (2 + d) * nh:(3 + d) * nh]
    gl = jnp.broadcast_to(gc[:, :, 0:1] if reverse else gc[:, :, GDN_CHUNK - 1:GDN_CHUNK], gc.shape)

    def rowpack(t):
        return jnp.transpose(t.reshape(b_, nc, GDN_CHUNK, N_PAIRS, 2), (0, 1, 3, 4, 2)).reshape(b_, nc, N_PAIRS, PK)

    rp = rowpack(gc)
    cp = jnp.concatenate([jnp.swapaxes(rowpack(t), 2, 3) for t in (gc, beta, gl)]
                         + [jnp.zeros((b_, nc, PK, N_PAIRS), F32)], axis=-1)
    glh = gl[:, :, 0].reshape(b_, nc, N_PAIRS, 2)
    ap = jnp.swapaxes(jnp.repeat(glh, HEAD_DIM, axis=-1), 2, 3)
    return cp, rp, ap


def _gdn_scan(qkv, gates, s0, b_, l):
    nc = l // GDN_CHUNK
    packs = [_gdn_packs(gates, b_, l, reverse) for reverse in (False, True)]
    masks = jnp.stack([_gdn_masks(False), _gdn_masks(True)])
    chunk_of = (lambda c: c, lambda c: nc - 1 - c)
    tok = lambda d, col: pl.BlockSpec((GDN_CHUNK, GDN_WIDTH), lambda b, c: (b * nc + chunk_of[d](c), col))
    per_chunk = lambda d, shp: pl.BlockSpec((1, 1) + shp, lambda b, c: (b, chunk_of[d](c), 0, 0))
    gate_specs = lambda d: [per_chunk(d, (PK, 4 * N_PAIRS)), per_chunk(d, (N_PAIRS, PK)),
                            per_chunk(d, (2 * HEAD_DIM, N_PAIRS))]
    state = pl.BlockSpec((1, 2, N_PAIRS, 2 * HEAD_DIM, HEAD_DIM), lambda b, c: (b, 0, 0, 0, 0))
    return pl.pallas_call(
        _gdn_scan_kernel,
        grid=(b_, nc),
        in_specs=[tok(0, 0), tok(0, 1), tok(0, 2), tok(1, 0), tok(1, 1), tok(1, 2)] + gate_specs(0) + gate_specs(1)
        + [state, pl.BlockSpec(masks.shape, lambda b, c: (0, 0, 0, 0))],
        out_specs=[tok(0, 0), tok(1, 0), state],
        out_shape=[jax.ShapeDtypeStruct((b_ * l, GDN_WIDTH), BF16),
                   jax.ShapeDtypeStruct((b_ * l, GDN_WIDTH), BF16),
                   jax.ShapeDtypeStruct(s0.shape, F32)],
        scratch_shapes=[pltpu.VMEM((2, N_PAIRS, 2 * HEAD_DIM, HEAD_DIM), F32)],
        compiler_params=_cparams(("arbitrary", "arbitrary")),
        name="gdn_scan",
    )(qkv, qkv, qkv, qkv, qkv, qkv, *packs[0], *packs[1], s0, masks)


def _gdn_mixer(yx, sx, yc, sc, conv_w, a_log, dt_bias, g_on, b_, l, n_ctx):
    nh = GDN_HEADS
    conv_w8 = jnp.pad(conv_w, ((0, HALO - CONV_K), (0, 0)))
    ea_row = jnp.zeros((1, N_SMALL), F32).at[0, 2 * nh:4 * nh].set(jnp.exp(a_log.reshape(-1)))
    dt_row = jnp.zeros((1, N_SMALL), F32).at[0, 2 * nh:4 * nh].set(dt_bias.reshape(-1))
    qkv_c, gates_c = _gdn_prep(yc, sc, conv_w8, ea_row, dt_row, b_, n_ctx)
    qkv_x, gates_x = _gdn_prep(yx, sx, conv_w8, ea_row, dt_row, b_, l)
    zero = jnp.zeros((b_, 2, N_PAIRS, 2 * HEAD_DIM, HEAD_DIM), F32)
    _, _, s_ctx = _gdn_scan(qkv_c, gates_c, zero, b_, n_ctx)
    o_f, o_b, _ = _gdn_scan(qkv_x, gates_x, s_ctx, b_, l)
    return _gdn_out(o_f, o_b, yx, g_on, 256)


SWA_GROUP = SWA_HEADS // SWA_KV_HEADS
ROT = HEAD_DIM // 4


def _rope_tables(l):
    half = HEAD_DIM // 2
    inv = ROPE_BASE ** (-jnp.arange(0, half, 2, dtype=F32) / half)
    pos = jnp.arange(l, dtype=jnp.int32)
    ang_r = (pos // GRID_W).astype(F32)[:, None] * inv
    ang_c = (pos % GRID_W).astype(F32)[:, None] * inv
    zero = jnp.zeros_like(ang_r)
    cos = jnp.concatenate([jnp.cos(ang_r), jnp.cos(ang_r), jnp.cos(ang_c), jnp.cos(ang_c)], axis=1)
    sin_up = jnp.concatenate([-jnp.sin(ang_r), zero, -jnp.sin(ang_c), zero], axis=1)
    sin_dn = jnp.concatenate([zero, jnp.sin(ang_r), zero, jnp.sin(ang_c)], axis=1)
    return cos, sin_up, sin_dn


def _swa_prep_kernel(q_ref, k_ref, v_ref, cos_ref, su_ref, sd_ref, gq_ref, gk_ref, qo_ref, ko_ref, vo_ref):
    cos, su, sd = cos_ref[...], su_ref[...], sd_ref[...]

    def norm_rope(t, g, scale):
        y = t * lax.rsqrt(jnp.mean(t * t, axis=-1, keepdims=True) + EPS) * g
        y = y * cos + pltpu.roll(y, HEAD_DIM - ROT, 1) * su + pltpu.roll(y, ROT, 1) * sd
        return y * scale if scale != 1.0 else y

    for h in range(SWA_HEADS):
        c = pl.ds(h * HEAD_DIM, HEAD_DIM)
        qo_ref[:, c] = norm_rope(q_ref[:, c], gq_ref[...], HEAD_DIM ** -0.5).astype(qo_ref.dtype)
    for h in range(SWA_KV_HEADS):
        c = pl.ds(h * HEAD_DIM, HEAD_DIM)
        ko_ref[:, c] = norm_rope(k_ref[:, c], gk_ref[...], 1.0).astype(ko_ref.dtype)
    vo_ref[...] = v_ref[...].astype(vo_ref.dtype)


def _swa_prep(y_main, tables, g_q, g_k, rows, tm):
    q_blk = (N_QKV + GDN_WIDTH) // SWA_WIDTH
    k_blk = (N_QKV + GDN_WIDTH + SWA_WIDTH) // SWA_KV_WIDTH
    tpb = tables[0].shape[0] // tm
    tab = pl.BlockSpec((tm, HEAD_DIM), lambda i: (i % tpb, 0))
    vec = pl.BlockSpec((1, HEAD_DIM), lambda i: (0, 0))
    return pl.pallas_call(
        _swa_prep_kernel,
        grid=(rows // tm,),
        in_specs=[pl.BlockSpec((tm, SWA_WIDTH), lambda i: (i, q_blk)),
                  pl.BlockSpec((tm, SWA_KV_WIDTH), lambda i: (i, k_blk)),
                  pl.BlockSpec((tm, SWA_KV_WIDTH), lambda i: (i, k_blk + 1)),
                  tab, tab, tab, vec, vec],
        out_specs=[pl.BlockSpec((tm, SWA_WIDTH), lambda i: (i, 0)),
                   pl.BlockSpec((tm, SWA_KV_WIDTH), lambda i: (i, 0)),
                   pl.BlockSpec((tm, SWA_KV_WIDTH), lambda i: (i, 0))],
        out_shape=[jax.ShapeDtypeStruct((rows, SWA_WIDTH), BF16),
                   jax.ShapeDtypeStruct((rows, SWA_KV_WIDTH), BF16),
                   jax.ShapeDtypeStruct((rows, SWA_KV_WIDTH), BF16)],
        compiler_params=_cparams(("arbitrary",)),
        name="swa_prep",
    )(y_main, y_main, y_main, *tables, g_q.reshape(1, HEAD_DIM), g_k.reshape(1, HEAD_DIM))


def _swa_attn_kernel(q_ref, kp_ref, kc_ref, kn_ref, vp_ref, vc_ref, vn_ref, kx_ref, vx_ref, sink_ref, o_ref, *, n_ctx):
    n = pl.program_id(1)
    nb = pl.num_programs(1)
    rows = SWA_GROUP * Q_BLOCK
    nk = n_ctx + 3 * Q_BLOCK
    qi = lax.broadcasted_iota(jnp.int32, (rows, nk), 0) & (Q_BLOCK - 1)
    kj = lax.broadcasted_iota(jnp.int32, (rows, nk), 1) - n_ctx
    lo = jnp.where(n == 0, Q_BLOCK, 0)
    hi = jnp.where(n == nb - 1, 2 * Q_BLOCK, 3 * Q_BLOCK)
    valid = (kj < 0) | ((kj >= qi) & (kj <= qi + 2 * WINDOW) & (kj >= lo) & (kj < hi))
    hsel = lax.broadcasted_iota(jnp.int32, (rows, 1), 0) // Q_BLOCK
    nt = (((1,), (1,)), ((), ()))
    kv_heads = range(SWA_KV_HEADS)
    col = lambda j: pl.ds(j * HEAD_DIM, HEAD_DIM)
    heads = [[j * SWA_GROUP + g for g in range(SWA_GROUP)] for j in kv_heads]
    s = []
    for j in kv_heads:
        q = jnp.concatenate([q_ref[:, col(h)] for h in heads[j]], axis=0)
        k = jnp.concatenate([kx_ref[:, col(j)], kp_ref[:, col(j)], kc_ref[:, col(j)], kn_ref[:, col(j)]], axis=0)
        s.append(jnp.where(valid, lax.dot_general(q, k, nt, preferred_element_type=F32), NEG_INF))
    p, den = [], []
    for j in kv_heads:
        sink = jnp.zeros((rows, 1), F32)
        for g, h in enumerate(heads[j]):
            sink = jnp.where(hsel == g, sink_ref[h:h + 1, 0:1], sink)
        m = jnp.maximum(jnp.max(s[j], axis=-1, keepdims=True), sink)
        e = jnp.exp(s[j] - m)
        den.append(jnp.sum(e, axis=-1, keepdims=True) + jnp.exp(sink - m))
        p.append(e.astype(BF16))
    for j in kv_heads:
        v = jnp.concatenate([vx_ref[:, col(j)], vp_ref[:, col(j)], vc_ref[:, col(j)], vn_ref[:, col(j)]], axis=0)
        o = jnp.dot(p[j], v, preferred_element_type=F32) / den[j]
        for g, h in enumerate(heads[j]):
            o_ref[:, col(h)] = o[g * Q_BLOCK:(g + 1) * Q_BLOCK].astype(o_ref.dtype)


def _swa_attention(q, k, v, k_ctx, v_ctx, sink, b_, l, n_ctx):
    nb = l // Q_BLOCK
    blk = lambda w, off: pl.BlockSpec(
        (Q_BLOCK, w), lambda b, n: (b * nb + jnp.clip(n + off, 0, nb - 1), 0))
    ctx = pl.BlockSpec((n_ctx, SWA_KV_WIDTH), lambda b, n: (b, 0))
    kw = SWA_KV_WIDTH
    return pl.pallas_call(
        functools.partial(_swa_attn_kernel, n_ctx=n_ctx),
        grid=(b_, nb),
        in_specs=[blk(SWA_WIDTH, 0), blk(kw, -1), blk(kw, 0), blk(kw, 1), blk(kw, -1), blk(kw, 0), blk(kw, 1),
                  ctx, ctx, pl.BlockSpec((SWA_HEADS, HEAD_DIM), lambda b, n: (0, 0))],
        out_specs=blk(SWA_WIDTH, 0),
        out_shape=jax.ShapeDtypeStruct((b_ * l, SWA_WIDTH), BF16),
        compiler_params=_cparams(("arbitrary", "arbitrary")),
        name="swa_attention",
    )(q, k, k, k, v, v, v, k_ctx, v_ctx, jnp.broadcast_to(sink.astype(F32)[:, None], (SWA_HEADS, HEAD_DIM)))


def _swa_mixer(yx, yc, g_q, g_k, sink, b_, l, n_ctx):
    ones = jnp.ones((n_ctx, HEAD_DIM), F32)
    zeros = jnp.zeros((n_ctx, HEAD_DIM), F32)
    qx, kx, vx = _swa_prep(yx, _rope_tables(l), g_q, g_k, b_ * l, 256)
    _, kc, vc = _swa_prep(yc, (ones, zeros, zeros), g_q, g_k, b_ * n_ctx, n_ctx)
    return _swa_attention(qx, kx, vx, kc, vc, sink, b_, l, n_ctx)


def _rmsnorm(x, g):
    xf = x.astype(F32)
    y = xf * lax.rsqrt(jnp.mean(xf * xf, axis=-1, keepdims=True) + EPS)
    return (y * g.astype(F32)).astype(x.dtype)


def _l2norm(x):
    return x * lax.rsqrt(jnp.sum(x * x, axis=-1, keepdims=True) + EPS)


def _short_conv(x, w):
    c = x.shape[-1]
    y = lax.conv_general_dilated(x, w[:, None, :].astype(x.dtype), window_strides=(1,),
                                 padding=[(CONV_K // 2, CONV_K // 2)],
                                 dimension_numbers=('NWC', 'WIO', 'NWC'), feature_group_count=c)
    return jax.nn.silu(y)


def _axial_rope(t, rows, cols):
    half = HEAD_DIM // 2
    inv = ROPE_BASE ** (-jnp.arange(0, half, 2, dtype=F32) / half)

    def rot(u, pos):
        ang = pos[:, None] * inv
        cos, sin = jnp.cos(ang)[None, :, None, :], jnp.sin(ang)[None, :, None, :]
        u1, u2 = jnp.split(u.astype(F32), 2, axis=-1)
        return jnp.concatenate([u1 * cos - u2 * sin, u2 * cos + u1 * sin], axis=-1)

    return jnp.concatenate([rot(t[..., :half], rows), rot(t[..., half:], cols)], axis=-1).astype(t.dtype)


def _gated_delta_chunked(q, k, v, g, beta, s0):
    b_, l, h, _ = q.shape
    n = l // GDN_CHUNK

    def chunks(t):
        t = t.reshape((b_, n, GDN_CHUNK) + t.shape[2:])
        return jnp.moveaxis(jnp.swapaxes(t, 2, 3), 1, 0)

    qc, kc, vc, gc, bc = map(chunks, (q, k, v, g, beta))
    gcum = jnp.cumsum(gc, axis=-1)
    tri = jnp.tril(jnp.ones((GDN_CHUNK, GDN_CHUNK), bool))
    strict = jnp.tril(jnp.ones((GDN_CHUNK, GDN_CHUNK), F32), -1)
    diff = gcum[..., :, None] - gcum[..., None, :]
    decay = jnp.where(tri, jnp.exp(jnp.where(tri, diff, 0.0)), 0.0)
    kb = kc * bc[..., None]
    a_strict = jnp.einsum('nbhid,nbhjd->nbhij', kb, kc) * decay * strict
    eye = jnp.eye(GDN_CHUNK, dtype=F32)
    t_inv = lax.linalg.triangular_solve(eye + a_strict, jnp.broadcast_to(eye, a_strict.shape),
                                        left_side=True, lower=True, unit_diagonal=True)
    u = jnp.einsum('nbhij,nbhjd->nbhid', t_inv, vc * bc[..., None])
    w = jnp.einsum('nbhij,nbhjd->nbhid', t_inv, kb * jnp.exp(gcum)[..., None])
    qk = jnp.einsum('nbhid,nbhjd->nbhij', qc, kc) * decay
    q_dec = qc * jnp.exp(gcum)[..., None]
    k_dec = kc * jnp.exp(gcum[..., -1:] - gcum)[..., None]
    chunk_decay = jnp.exp(gcum[..., -1])

    def step(s, xs):
        u_n, w_n, qk_n, q_n, k_n, a_n = xs
        v_new = u_n - jnp.einsum('bhcd,bhde->bhce', w_n, s)
        o = jnp.einsum('bhcd,bhde->bhce', q_n, s) + jnp.einsum('bhij,bhje->bhie', qk_n, v_new)
        s = s * a_n[..., None, None] + jnp.einsum('bhcd,bhce->bhde', k_n, v_new)
        return s, o

    s_fin, o = lax.scan(step, s0, (u, w, qk, q_dec, k_dec, chunk_decay))
    o = jnp.swapaxes(jnp.moveaxis(o, 0, 1), 2, 3).reshape(b_, l, h, -1)
    return o, s_fin


def _gdn_inputs(qkv, b_logit, a_logit, conv_w, a_log, dt_bias):
    b_, l, _ = qkv.shape
    qkv = _short_conv(qkv, conv_w).astype(F32).reshape(b_, l, 3, GDN_HEADS, HEAD_DIM)
    q = _l2norm(qkv[:, :, 0]) * HEAD_DIM ** -0.5
    k = _l2norm(qkv[:, :, 1])
    v = qkv[:, :, 2]
    beta = jax.nn.sigmoid(b_logit.astype(F32).reshape(b_, l, 2, GDN_HEADS))
    g = -jnp.exp(a_log.astype(F32)) * jax.nn.softplus(
        a_logit.astype(F32).reshape(b_, l, 2, GDN_HEADS) + dt_bias.astype(F32))
    return q, k, v, g, beta


def _gdn_bidir(q, k, v, g, beta, s_fwd0, s_bwd0):
    o_f, s_f = _gated_delta_chunked(q, k, v, g[:, :, 0], beta[:, :, 0], s_fwd0)
    rev = lambda t: jnp.flip(t, axis=1)
    o_b, s_b = _gated_delta_chunked(rev(q), rev(k), rev(v), rev(g[:, :, 1]), rev(beta[:, :, 1]), s_bwd0)
    return o_f + rev(o_b), s_f, s_b


def _gdn_output(o, z, g_on):
    b_, l = o.shape[:2]
    y = _rmsnorm(o, g_on).reshape(b_, l, GDN_WIDTH)
    return y * jax.nn.silu(z.astype(F32))


def _swa_inputs(q, k, v, g_q, g_k):
    b_, l, _ = q.shape
    q = _rmsnorm(q.reshape(b_, l, SWA_HEADS, HEAD_DIM), g_q)
    k = _rmsnorm(k.reshape(b_, l, SWA_KV_HEADS, HEAD_DIM), g_k)
    return q, k, v.reshape(b_, l, SWA_KV_HEADS, HEAD_DIM)


def _window_attention(q, k, v, k_ctx, v_ctx, sink):
    b_, l, h, d = q.shape
    g = h // SWA_KV_HEADS
    nb = l // Q_BLOCK
    n_ctx = k_ctx.shape[1]
    scale = d ** -0.5
    qb = q.reshape(b_, nb, Q_BLOCK, SWA_KV_HEADS, g, d)

    def band_blocks(t):
        tp = jnp.pad(t, ((0, 0), (Q_BLOCK, Q_BLOCK), (0, 0), (0, 0))).reshape(b_, nb + 2, Q_BLOCK, SWA_KV_HEADS, d)
        return jnp.concatenate([tp[:, :-2], tp[:, 1:-1], tp[:, 2:]], axis=2)

    kw, vw = band_blocks(k), band_blocks(v)
    qi = jnp.arange(Q_BLOCK)[:, None]
    kj = jnp.arange(3 * Q_BLOCK)[None, :]
    band = jnp.abs(kj - Q_BLOCK - qi) <= WINDOW
    kpos = jnp.arange(nb)[:, None] * Q_BLOCK - Q_BLOCK + jnp.arange(3 * Q_BLOCK)[None, :]
    valid = band[None] & ((kpos >= 0) & (kpos < l))[:, None, :]
    s_win = jnp.einsum('bnqkgd,bnjkd->bnkgqj', qb, kw).astype(F32) * scale
    s_win = jnp.where(valid[None, :, None, None], s_win, NEG_INF)
    s_ctx = jnp.einsum('bnqkgd,bckd->bnkgqc', qb, k_ctx).astype(F32) * scale
    s_sink = jnp.broadcast_to(sink.astype(F32).reshape(SWA_KV_HEADS, g, 1, 1), s_ctx.shape[:-1] + (1,))
    p = jax.nn.softmax(jnp.concatenate([s_ctx, s_win, s_sink], axis=-1), axis=-1).astype(v.dtype)
    o = (jnp.einsum('bnkgqc,bckd->bnqkgd', p[..., :n_ctx], v_ctx)
         + jnp.einsum('bnkgqj,bnjkd->bnqkgd', p[..., n_ctx:n_ctx + 3 * Q_BLOCK], vw))
    return o.reshape(b_, l, h * d)


def _split_main(y):
    o = np.cumsum((0, 3 * GDN_WIDTH, GDN_WIDTH, SWA_WIDTH, SWA_KV_WIDTH, SWA_KV_WIDTH))
    return tuple(y[..., int(o[n]):int(o[n + 1])] for n in range(5))


def kernel(x, c, ctx, c_ctx, w_ada, b_ada, g_norm1, g_norm2, w_in, conv_qkv, a_log, dt_bias, g_onorm, g_qnorm,
           g_knorm, sink, w_out, w_router_grp, b_router_grp, w_router_exp, b_router_exp, w_gate, w_up, w_down):
    b_, l, d = x.shape
    n_ctx = ctx.shape[1]
    t = b_ * l
    assert w_ada.shape[0] == 1 and d == D_MODEL and b_ + 1 <= MOD_ROWS
    rows = jnp.repeat(jnp.arange(l // GRID_W, dtype=F32), GRID_W)
    cols = jnp.tile(jnp.arange(GRID_W, dtype=F32), l // GRID_W)

    wi = w_in[0]
    w_main = jnp.concatenate([wi[:, IN_OFFS[0]:IN_OFFS[2]], wi[:, IN_OFFS[4]:IN_OFFS[7]]], axis=1).astype(BF16)
    w_small = jnp.pad(wi[:, IN_OFFS[2]:IN_OFFS[4]], ((0, 0), (0, N_SMALL - 4 * GDN_HEADS))).astype(BF16)
    wo = w_out[0].astype(BF16)
    w_router = jnp.pad(jnp.concatenate([w_router_grp[0], w_router_exp[0]], axis=1),
                       ((0, 0), (0, N_ROUTER - N_GROUPS - N_EXPERTS))).astype(BF16)
    b_router = jnp.pad(jnp.concatenate([b_router_grp[0], b_router_exp[0]]),
                       (0, N_ROUTER - N_GROUPS - N_EXPERTS)).reshape(1, N_ROUTER)

    c_rows = jnp.zeros((MOD_ROWS, d), F32).at[:b_].set(c).at[b_].set(c_ctx)
    mod = _modulation(c_rows, w_ada[0], b_ada[0])
    mod3 = mod.reshape(MOD_ROWS * 6, 1, d)

    tm = 512
    tpb = l // tm
    yx, sx = _in_projection(x.reshape(t, d), g_norm1[0], mod3, lambda i: i // tpb, w_main, w_small, tm)
    yc, sc = _in_projection(ctx.reshape(b_ * n_ctx, d), g_norm1[0], mod3, lambda i: b_, w_main, w_small, n_ctx)
    ya_x = _gdn_mixer(yx, sx, yc, sc, conv_qkv[0], a_log[0], dt_bias[0], g_onorm[0], b_, l, n_ctx)
    yb_x = _swa_mixer(yx, yc, g_qnorm[0], g_knorm[0], sink[0], b_, l, n_ctx)

    x1 = _out_projection(ya_x, yb_x, wo[:GDN_WIDTH], wo[GDN_WIDTH:], x.reshape(t, d), mod3, tpb, tm)

    tm2 = 256
    h2c, ids, gates = _norm2_router(x1, g_norm2[0], mod3, l // tm2, w_router, b_router, tm2)
    tok_sorted, blk_first, blk_rows, slot_of, blk_expert, n_used = _slots(ids[:, :TOP_K], t)
    yc_moe = _moe_experts(h2c, tok_sorted, blk_first, blk_rows, blk_expert, n_used, w_gate[0], w_up[0], w_down[0])
    return _moe_combine(yc_moe, slot_of, x1, gates, mod3, l // COMBINE_TM).reshape(b_, l, d)
```

```python
import functools
import math

import jax
import jax.numpy as jnp
import numpy as np
from jax import lax
from jax.experimental import pallas as pl
from jax.experimental.pallas import tpu as pltpu

F32 = jnp.float32
BF16 = jnp.bfloat16

D_MODEL = 4096
CTX_LEN = 256
GRID_W = 64
HEAD_DIM = 128
GDN_HEADS = 16
GDN_WIDTH = GDN_HEADS * HEAD_DIM
GDN_CHUNK = 64
CONV_K = 5
SWA_HEADS = 16
SWA_KV_HEADS = 4
SWA_WIDTH = SWA_HEADS * HEAD_DIM
SWA_KV_WIDTH = SWA_KV_HEADS * HEAD_DIM
WINDOW = 128
Q_BLOCK = 128
ROPE_BASE = 10000.0
N_GROUPS = 8
EXPERTS_PER_GROUP = 8
N_EXPERTS = N_GROUPS * EXPERTS_PER_GROUP
TOP_K = 2
D_EXPERT = D_MODEL // 8
EPS = 1e-6
NEG_INF = -1e30

IN_SIZES = (3 * GDN_WIDTH, GDN_WIDTH, 2 * GDN_HEADS, 2 * GDN_HEADS, SWA_WIDTH, SWA_KV_WIDTH, SWA_KV_WIDTH)
IN_OFFS = tuple(int(v) for v in np.cumsum((0,) + IN_SIZES))
N_MAIN = 3 * GDN_WIDTH + GDN_WIDTH + SWA_WIDTH + 2 * SWA_KV_WIDTH
N_SMALL = 128
N_ROUTER = 128
MOD_ROWS = 8

MOE_BM = 256
VMEM_LIMIT = 56 * 1024 * 1024


def _cparams(sem):
    return pltpu.CompilerParams(dimension_semantics=sem, vmem_limit_bytes=VMEM_LIMIT)


def _mod_kernel(c_ref, w_ref, b_ref, o_ref):
    c = c_ref[...]
    a = (c * jax.nn.sigmoid(c)).astype(BF16)
    o_ref[...] = jnp.dot(a, w_ref[...].astype(BF16), preferred_element_type=F32) + b_ref[...]


def _modulation(c_rows, w_ada, b_ada):
    d, n = w_ada.shape
    tn = 512
    return pl.pallas_call(
        _mod_kernel,
        grid=(n // tn,),
        in_specs=[pl.BlockSpec((MOD_ROWS, d), lambda j: (0, 0)),
                  pl.BlockSpec((d, tn), lambda j: (0, j)),
                  pl.BlockSpec((1, tn), lambda j: (0, j))],
        out_specs=pl.BlockSpec((MOD_ROWS, tn), lambda j: (0, j)),
        out_shape=jax.ShapeDtypeStruct((MOD_ROWS, n), F32),
        compiler_params=_cparams(("arbitrary",)),
        name="modulation",
    )(c_rows, w_ada, b_ada.reshape(1, n))


GATE_COLS = IN_OFFS[4] - IN_OFFS[2]
WPREP_TN = 1024


def _wprep_kernel(a_ref, b_ref, o_ref):
    j = pl.program_id(1)
    first_shifted = IN_OFFS[2] // WPREP_TN

    @pl.when(j < first_shifted)
    def _():
        o_ref[...] = a_ref[...].astype(o_ref.dtype)

    @pl.when(j >= first_shifted)
    def _():
        src = jnp.concatenate([a_ref[...], b_ref[...]], axis=1)
        lane = lax.broadcasted_iota(jnp.int32, (a_ref.shape[0], LANES), 1)
        for c in range(WPREP_TN // LANES):
            lo = pltpu.roll(src[:, c * LANES:(c + 1) * LANES], LANES - GATE_COLS, 1)
            hi = pltpu.roll(src[:, (c + 1) * LANES:(c + 2) * LANES], LANES - GATE_COLS, 1)
            o_ref[:, c * LANES:(c + 1) * LANES] = jnp.where(lane < LANES - GATE_COLS, lo, hi).astype(o_ref.dtype)


def _prep_in_weights(wi):
    d = wi.shape[0]
    tr = 512
    assert IN_OFFS[2] % WPREP_TN == 0 and N_MAIN % WPREP_TN == 0
    per = WPREP_TN // LANES
    return pl.pallas_call(
        _wprep_kernel,
        grid=(d // tr, N_MAIN // WPREP_TN),
        in_specs=[pl.BlockSpec((tr, WPREP_TN), lambda i, j: (i, j)),
                  pl.BlockSpec((tr, LANES), lambda i, j: (i, (j + 1) * per))],
        out_specs=pl.BlockSpec((tr, WPREP_TN), lambda i, j: (i, j)),
        out_shape=jax.ShapeDtypeStruct((d, N_MAIN), BF16),
        compiler_params=_cparams(("arbitrary", "arbitrary")),
        name="in_weight_prep",
    )(wi, wi)


NORM_ROWS = 64


def _norm_mod_rows(x_ref, g_ref, sh_ref, sc_ref, h_ref, tm):
    g = g_ref[...]
    sc = 1.0 + sc_ref[0]
    sh = sh_ref[0]

    def body(r, carry):
        rows = pl.ds(pl.multiple_of(r * NORM_ROWS, NORM_ROWS), NORM_ROWS)
        xf = x_ref[rows, :]
        ms = jnp.mean(xf * xf, axis=-1, keepdims=True)
        y = xf * lax.rsqrt(ms + EPS) * g
        h_ref[rows, :] = (y * sc + sh).astype(h_ref.dtype)
        return carry

    lax.fori_loop(0, tm // NORM_ROWS, body, 0)


def _inproj_kernel(x_ref, g_ref, sh_ref, sc_ref, w_ref, ws_ref, o_ref, os_ref, h_ref, *, tm):
    @pl.when(pl.program_id(1) == 0)
    def _():
        _norm_mod_rows(x_ref, g_ref, sh_ref, sc_ref, h_ref, tm)
        os_ref[...] = jnp.dot(h_ref[...], ws_ref[...], preferred_element_type=F32)

    o_ref[...] = jnp.dot(h_ref[...], w_ref[...], preferred_element_type=F32)


def _in_projection(x2d, g_norm, mod3, mod_row_of_tile, w_main, w_small, tm):
    t, d = x2d.shape
    tn = 1024
    return pl.pallas_call(
        functools.partial(_inproj_kernel, tm=tm),
        grid=(t // tm, N_MAIN // tn),
        in_specs=[pl.BlockSpec((tm, d), lambda i, j: (i, 0)),
                  pl.BlockSpec((1, d), lambda i, j: (0, 0)),
                  pl.BlockSpec((1, 1, d), lambda i, j: (mod_row_of_tile(i) * 6 + 0, 0, 0)),
                  pl.BlockSpec((1, 1, d), lambda i, j: (mod_row_of_tile(i) * 6 + 1, 0, 0)),
                  pl.BlockSpec((d, tn), lambda i, j: (0, j)),
                  pl.BlockSpec((d, N_SMALL), lambda i, j: (0, 0))],
        out_specs=[pl.BlockSpec((tm, tn), lambda i, j: (i, j)),
                   pl.BlockSpec((tm, N_SMALL), lambda i, j: (i, 0))],
        out_shape=[jax.ShapeDtypeStruct((t, N_MAIN), F32),
                   jax.ShapeDtypeStruct((t, N_SMALL), F32)],
        scratch_shapes=[pltpu.VMEM((tm, d), BF16)],
        compiler_params=_cparams(("arbitrary", "arbitrary")),
        name="in_projection",
    )(x2d, g_norm.reshape(1, d), mod3, mod3, w_main, w_small)


def _outproj_kernel(ya_ref, yb_ref, wa_ref, wb_ref, x_ref, gate_ref, o_ref):
    acc = jnp.dot(ya_ref[...], wa_ref[...], preferred_element_type=F32)
    acc = acc + jnp.dot(yb_ref[...], wb_ref[...], preferred_element_type=F32)
    o_ref[...] = x_ref[...] + gate_ref[0] * acc


def _out_projection(ya, yb, w, x2d, mod3, tiles_per_batch, tm):
    t, d = x2d.shape
    tn = 1024
    nj = d // tn
    ka, kb = ya.shape[1], yb.shape[1]
    assert ka == kb and w.shape[0] == ka + kb
    return pl.pallas_call(
        _outproj_kernel,
        grid=(t // tm, nj),
        in_specs=[pl.BlockSpec((tm, ka), lambda i, j: (i, 0)),
                  pl.BlockSpec((tm, kb), lambda i, j: (i, 0)),
                  pl.BlockSpec((ka, tn), lambda i, j: (0, j)),
                  pl.BlockSpec((kb, tn), lambda i, j: (1, j)),
                  pl.BlockSpec((tm, tn), lambda i, j: (i, j)),
                  pl.BlockSpec((1, 1, tn), lambda i, j: (((i // tiles_per_batch) * 6 + 2) * nj + j, 0, 0))],
        out_specs=pl.BlockSpec((tm, tn), lambda i, j: (i, j)),
        out_shape=jax.ShapeDtypeStruct((t, d), F32),
        compiler_params=_cparams(("arbitrary", "arbitrary")),
        name="out_projection",
    )(ya, yb, w, w, x2d, mod3.reshape(-1, 1, tn))


def _route_rows(lg):
    col = lax.broadcasted_iota(jnp.int32, lg.shape, 1)
    first = lambda hit: jnp.min(jnp.where(hit, col, N_ROUTER), axis=-1, keepdims=True)
    gm = col < N_GROUPS
    mg = jnp.max(jnp.where(gm, lg, NEG_INF), axis=-1, keepdims=True)
    grp = first(gm & (lg == mg))
    p_grp = 1.0 / jnp.sum(jnp.where(gm, jnp.exp(lg - mg), 0.0), axis=-1, keepdims=True)
    lo = N_GROUPS + grp * EXPERTS_PER_GROUP
    em = (col >= lo) & (col < lo + EXPERTS_PER_GROUP)
    m1 = jnp.max(jnp.where(em, lg, NEG_INF), axis=-1, keepdims=True)
    i1 = first(em & (lg == m1))
    em2 = em & (col != i1)
    m2 = jnp.max(jnp.where(em2, lg, NEG_INF), axis=-1, keepdims=True)
    i2 = first(em2 & (lg == m2))
    e2 = jnp.exp(m2 - m1)
    g1 = p_grp / (1.0 + e2)
    ids = jnp.where(col == 0, i1 - N_GROUPS, jnp.where(col == 1, i2 - N_GROUPS, 0))
    gates = jnp.where(col == 0, g1, jnp.where(col == 1, g1 * e2, 0.0))
    return ids, gates


LANES = 128
HALF_D = D_MODEL // 2
ROW_CH = HALF_D // LANES
ROW_PITCH = 24
U32 = jnp.uint32
HI_MASK = 0xFFFF0000
DMA_UNROLL = 8


def _pack_pair(lo, hi):
    bits = lambda v: lax.bitcast_convert_type(v.astype(BF16).astype(F32), U32)
    return (bits(hi) & U32(HI_MASK)) | (bits(lo) >> 16)


def _unpack_pair(w):
    return lax.bitcast_convert_type(w << 16, F32), lax.bitcast_convert_type(w & U32(HI_MASK), F32)


def _store_chunked(dst_ref, row0, vals):
    n = vals.shape[0]
    for j in range(ROW_PITCH):
        if j < ROW_CH:
            piece = _pack_pair(vals[:, j * LANES:(j + 1) * LANES], vals[:, HALF_D + j * LANES:HALF_D + (j + 1) * LANES])
        else:
            piece = jnp.zeros((n, LANES), U32)
        dst_ref[pl.ds(row0 * ROW_PITCH + j, n, stride=ROW_PITCH), :] = piece


def _load_chunk(src_ref, row0, n, j):
    return _unpack_pair(src_ref[pl.ds(row0 * ROW_PITCH + j, n, stride=ROW_PITCH), :])


def _row_copy(src_ref, dst_ref, src_row, dst_row, sem):
    return pltpu.make_async_copy(src_ref.at[pl.ds(pl.multiple_of(src_row * ROW_PITCH, 8), ROW_CH), :],
                                 dst_ref.at[pl.ds(pl.multiple_of(dst_row * ROW_PITCH, 8), ROW_CH), :], sem)


def _gather_rows(src_hbm, dst_ref, src_row_of, n, sem, dst_row0=0):
    def body(r, carry):
        _row_copy(src_hbm, dst_ref, src_row_of(r), dst_row0 + r, sem).start()
        return carry

    lax.fori_loop(0, n, body, 0, unroll=DMA_UNROLL)


def _wait_rows(src_hbm, dst_ref, n, sem):
    def body(r, carry):
        _row_copy(src_hbm, dst_ref, 0, r, sem).wait()
        return carry

    lax.fori_loop(0, n, body, 0, unroll=DMA_UNROLL)


def _norm2_kernel(x_ref, g_ref, sh_ref, sc_ref, wr_ref, br_ref, hc_ref, id_ref, gt_ref, hb_ref, *, tm):
    g = g_ref[...]
    sc = 1.0 + sc_ref[0]
    sh = sh_ref[0]

    def body(r, carry):
        row0 = pl.multiple_of(r * NORM_ROWS, NORM_ROWS)
        xf = x_ref[pl.ds(row0, NORM_ROWS), :]
        ms = jnp.mean(xf * xf, axis=-1, keepdims=True)
        h = xf * lax.rsqrt(ms + EPS) * g * sc + sh
        hb_ref[pl.ds(row0, NORM_ROWS), :] = h.astype(BF16)
        _store_chunked(hc_ref, row0, h)
        return carry

    lax.fori_loop(0, tm // NORM_ROWS, body, 0)
    lg = jnp.dot(hb_ref[...], wr_ref[...], preferred_element_type=F32) + br_ref[...]
    id_ref[...], gt_ref[...] = _route_rows(lg)


def _norm2_router(x2d, g_norm, mod3, tiles_per_batch, w_router, b_router, tm):
    t, d = x2d.shape
    return pl.pallas_call(
        functools.partial(_norm2_kernel, tm=tm),
        grid=(t // tm,),
        in_specs=[pl.BlockSpec((tm, d), lambda i: (i, 0)),
                  pl.BlockSpec((1, d), lambda i: (0, 0)),
                  pl.BlockSpec((1, 1, d), lambda i: ((i // tiles_per_batch) * 6 + 3, 0, 0)),
                  pl.BlockSpec((1, 1, d), lambda i: ((i // tiles_per_batch) * 6 + 4, 0, 0)),
                  pl.BlockSpec((d, N_ROUTER), lambda i: (0, 0)),
                  pl.BlockSpec((1, N_ROUTER), lambda i: (0, 0))],
        out_specs=[pl.BlockSpec((tm * ROW_PITCH, LANES), lambda i: (i, 0)),
                   pl.BlockSpec((tm, N_ROUTER), lambda i: (i, 0)),
                   pl.BlockSpec((tm, N_ROUTER), lambda i: (i, 0))],
        out_shape=[jax.ShapeDtypeStruct((t * ROW_PITCH, LANES), U32),
                   jax.ShapeDtypeStruct((t, N_ROUTER), jnp.int32),
                   jax.ShapeDtypeStruct((t, N_ROUTER), F32)],
        scratch_shapes=[pltpu.VMEM((tm, d), BF16)],
        compiler_params=_cparams(("arbitrary",)),
        name="norm2_router",
    )(x2d, g_norm.reshape(1, d), mod3, mod3, w_router, b_router)


CAST_ROWS = 128
WEIGHT_DMA_PRIORITY = 1


def _cast_rows(src_ref, dst_ref):
    n = src_ref.shape[0]

    def body(r, carry):
        rows = pl.ds(pl.multiple_of(r * CAST_ROWS, CAST_ROWS), CAST_ROWS)
        dst_ref[rows, :] = src_ref[rows, :].astype(dst_ref.dtype)
        return carry

    lax.fori_loop(0, n // CAST_ROWS, body, 0)


def _expert_weights(i, nu, sched, w_hbms, w_bufs, w_bf16s, sem):
    be_ref, sl_ref, nx_ref, nv_ref = sched

    def copies(e, slot):
        return [pltpu.make_async_copy(w.at[e], buf.at[slot], sem.at[slot, k])
                for k, (w, buf) in enumerate(zip(w_hbms, w_bufs))]

    @pl.when(i == 0)
    def _():
        for cp in copies(be_ref[0], 0):
            cp.start(priority=WEIGHT_DMA_PRIORITY)

    first_block = (i == 0) | (be_ref[i] != be_ref[jnp.maximum(i - 1, 0)])

    @pl.when(first_block & (i < nu))
    def _():
        slot = sl_ref[i]
        for cp in copies(be_ref[i], slot):
            cp.wait()
        for buf, dst in zip(w_bufs, w_bf16s):
            _cast_rows(buf.at[slot], dst)

        @pl.when(nv_ref[i] == 1)
        def _():
            for cp in copies(nx_ref[i], 1 - slot):
                cp.start(priority=WEIGHT_DMA_PRIORITY)


def _moe_up_kernel(be_ref, nu_ref, sl_ref, nx_ref, nv_ref, tk_ref, bf_ref, br_ref, h_hbm, wg_hbm, wu_hbm, o_ref,
                   wg_buf, wu_buf, wgb_ref, wub_ref, xg_ref, xb_ref, sem, wsem):
    i = pl.program_id(0)
    nu = nu_ref[0]
    pad_entry = tk_ref.shape[0] - 8

    def start_gather(blk, slot):
        first, rows = bf_ref[blk], br_ref[blk]
        token_of = lambda r: tk_ref[jnp.where(r < rows, first + r, pad_entry)]
        _gather_rows(h_hbm, xg_ref.at[slot], token_of, MOE_BM, sem.at[slot])

    @pl.when(i == 0)
    def _():
        start_gather(0, 0)

    @pl.when(i + 1 < nu)
    def _():
        start_gather(i + 1, (i + 1) % 2)

    _expert_weights(i, nu, (be_ref, sl_ref, nx_ref, nv_ref), (wg_hbm, wu_hbm), (wg_buf, wu_buf), (wgb_ref, wub_ref),
                    wsem)

    def unpack_rows(slot):
        _wait_rows(h_hbm, xg_ref.at[slot], MOE_BM, sem.at[slot])
        for j in range(ROW_CH):
            lo, hi = _load_chunk(xg_ref.at[slot], 0, MOE_BM, j)
            xb_ref[:, pl.ds(j * LANES, LANES)] = lo.astype(BF16)
            xb_ref[:, pl.ds(HALF_D + j * LANES, LANES)] = hi.astype(BF16)

    @pl.when(i < nu)
    def _():
        for slot in range(2):
            pl.when(i % 2 == slot)(functools.partial(unpack_rows, slot))
        xb = xb_ref[...]
        g = jnp.dot(xb, wgb_ref[...], preferred_element_type=F32)
        u = jnp.dot(xb, wub_ref[...], preferred_element_type=F32)
        o_ref[...] = (g * jax.nn.sigmoid(g) * u).astype(o_ref.dtype)

    @pl.when(i >= nu)
    def _():
        o_ref[...] = jnp.zeros_like(o_ref)


def _moe_down_kernel(be_ref, nu_ref, sl_ref, nx_ref, nv_ref, h_ref, wd_hbm, o_ref, wd_buf, wdb_ref, wsem):
    i = pl.program_id(0)
    _expert_weights(i, nu_ref[0], (be_ref, sl_ref, nx_ref, nv_ref), (wd_hbm,), (wd_buf,), (wdb_ref,), wsem)

    @pl.when(i < nu_ref[0])
    def _():
        _store_chunked(o_ref, 0, jnp.dot(h_ref[...], wdb_ref[...], preferred_element_type=F32))

    @pl.when(i >= nu_ref[0])
    def _():
        o_ref[...] = jnp.zeros_like(o_ref)


def _moe_experts(h_chunked, tok_sorted, blk_first, blk_rows, blk_expert, n_used, w_gate, w_up, w_down):
    n_blocks = blk_expert.shape[0]
    p = n_blocks * MOE_BM
    _, d, de = w_gate.shape
    run = jnp.cumsum(jnp.concatenate([jnp.zeros((1,), jnp.int32),
                                      (blk_expert[1:] != blk_expert[:-1]).astype(jnp.int32)]))
    run_end = jnp.searchsorted(blk_expert, blk_expert, side='right').astype(jnp.int32)
    sched = (blk_expert, n_used, run % 2, blk_expert[jnp.minimum(run_end, n_blocks - 1)],
             (run_end < n_used[0]).astype(jnp.int32))
    any_spec = pl.BlockSpec(memory_space=pl.ANY)
    hmid = pl.pallas_call(
        _moe_up_kernel,
        grid_spec=pltpu.PrefetchScalarGridSpec(
            num_scalar_prefetch=8,
            grid=(n_blocks,),
            in_specs=[any_spec, any_spec, any_spec],
            out_specs=pl.BlockSpec((MOE_BM, de), lambda i, *_: (i, 0)),
            scratch_shapes=[pltpu.VMEM((2, d, de), F32), pltpu.VMEM((2, d, de), F32),
                            pltpu.VMEM((d, de), BF16), pltpu.VMEM((d, de), BF16),
                            pltpu.VMEM((2, MOE_BM * ROW_PITCH, LANES), U32), pltpu.VMEM((MOE_BM, d), BF16),
                            pltpu.SemaphoreType.DMA((2,)), pltpu.SemaphoreType.DMA((2, 2))]),
        out_shape=jax.ShapeDtypeStruct((p, de), BF16),
        compiler_params=_cparams(("arbitrary",)),
        name="moe_gate_up",
    )(*sched, tok_sorted, blk_first, blk_rows, h_chunked, w_gate, w_up)
    return pl.pallas_call(
        _moe_down_kernel,
        grid_spec=pltpu.PrefetchScalarGridSpec(
            num_scalar_prefetch=5,
            grid=(n_blocks,),
            in_specs=[pl.BlockSpec((MOE_BM, de), lambda i, *_: (i, 0)), any_spec],
            out_specs=pl.BlockSpec((MOE_BM * ROW_PITCH, LANES), lambda i, *_: (i, 0)),
            scratch_shapes=[pltpu.VMEM((2, de, d), F32), pltpu.VMEM((de, d), BF16),
                            pltpu.SemaphoreType.DMA((2, 1))]),
        out_shape=jax.ShapeDtypeStruct((p * ROW_PITCH, LANES), U32),
        compiler_params=_cparams(("arbitrary",)),
        name="moe_down",
    )(*sched, hmid, w_down)


COMBINE_TM = 256


def _combine_kernel(so_ref, y_hbm, x_ref, gt_ref, g2_ref, o_ref, yb_ref, sem, *, tm):
    i = pl.program_id(0)
    n = pl.num_programs(0)

    def start_gather(blk, slot):
        for k in range(TOP_K):
            _gather_rows(y_hbm, yb_ref.at[slot], lambda r: so_ref[(blk * tm + r) * TOP_K + k], tm, sem.at[slot],
                         dst_row0=k * tm)

    @pl.when(i == 0)
    def _():
        start_gather(0, 0)

    @pl.when(i + 1 < n)
    def _():
        start_gather(i + 1, (i + 1) % 2)

    def combine(slot):
        _wait_rows(y_hbm, yb_ref.at[slot], TOP_K * tm, sem.at[slot])
        g0 = gt_ref[:, 0:1]
        g1 = gt_ref[:, 1:2]
        for j in range(ROW_CH):
            lo0, hi0 = _load_chunk(yb_ref.at[slot], 0, tm, j)
            lo1, hi1 = _load_chunk(yb_ref.at[slot], tm, tm, j)
            for off, y in ((j * LANES, g0 * lo0 + g1 * lo1), (HALF_D + j * LANES, g0 * hi0 + g1 * hi1)):
                o_ref[:, pl.ds(off, LANES)] = x_ref[:, pl.ds(off, LANES)] + g2_ref[0][:, off:off + LANES] * y

    for slot in range(2):
        pl.when(i % 2 == slot)(functools.partial(combine, slot))


def _moe_combine(y_chunked, slot_of, x2d, gates, mod3, tiles_per_batch):
    t, d = x2d.shape
    tm = COMBINE_TM
    return pl.pallas_call(
        functools.partial(_combine_kernel, tm=tm),
        grid_spec=pltpu.PrefetchScalarGridSpec(
            num_scalar_prefetch=1,
            grid=(t // tm,),
            in_specs=[pl.BlockSpec(memory_space=pl.ANY),
                      pl.BlockSpec((tm, d), lambda i, so: (i, 0)),
                      pl.BlockSpec((tm, N_ROUTER), lambda i, so: (i, 0)),
                      pl.BlockSpec((1, 1, d), lambda i, so: ((i // tiles_per_batch) * 6 + 5, 0, 0))],
            out_specs=pl.BlockSpec((tm, d), lambda i, so: (i, 0)),
            scratch_shapes=[pltpu.VMEM((2, TOP_K * tm * ROW_PITCH, LANES), U32),
                            pltpu.SemaphoreType.DMA((2,))]),
        out_shape=jax.ShapeDtypeStruct((t, d), F32),
        compiler_params=_cparams(("arbitrary",)),
        name="moe_combine",
    )(slot_of, y_chunked, x2d, gates, mod3)


def _slots(eid, t):
    a = t * TOP_K
    iota = jnp.arange(a, dtype=jnp.int32)
    e_sorted, order = lax.sort((eid.reshape(a), iota), num_keys=1)
    experts = jnp.arange(N_EXPERTS, dtype=jnp.int32)
    start = jnp.searchsorted(e_sorted, experts, side='left').astype(jnp.int32)
    counts = jnp.searchsorted(e_sorted, experts, side='right').astype(jnp.int32) - start
    padded = (counts + MOE_BM - 1) // MOE_BM * MOE_BM
    pend = jnp.cumsum(padded)
    pstart = pend - padded
    dest = pstart[e_sorted] + iota - start[e_sorted]
    slot_of = lax.sort((order, dest), num_keys=1)[1]
    n_blocks = (a + MOE_BM - 1) // MOE_BM + N_EXPERTS
    blk = jnp.arange(n_blocks, dtype=jnp.int32) * MOE_BM
    blk_expert = jnp.minimum(jnp.searchsorted(pend, blk, side='right'), N_EXPERTS - 1).astype(jnp.int32)
    off = blk - pstart[blk_expert]
    blk_first = start[blk_expert] + off
    blk_rows = jnp.clip(counts[blk_expert] - off, 0, MOE_BM)
    tok_sorted = jnp.concatenate([order // TOP_K, jnp.zeros((8,), jnp.int32)])
    n_used = (pend[-1] // MOE_BM).astype(jnp.int32).reshape(1)
    return tok_sorted, blk_first, blk_rows, slot_of, blk_expert, n_used


GDN_TB = 256
HALO = 8
PREP_SLABS = 2
N_QKV = 3 * GDN_WIDTH


def _softplus(v):
    return jnp.maximum(v, 0.0) + jnp.log(1.0 + jnp.exp(-jnp.abs(v)))


def _split3_bf16(v):
    hi = v.astype(BF16)
    r1 = v - hi.astype(F32)
    mid = r1.astype(BF16)
    lo = (r1 - mid.astype(F32)).astype(BF16)
    return hi, mid, lo


def _gdn_prep_kernel(cur_ref, prev_ref, next_ref, sm_ref, cw_ref, ea_ref, dt_ref, o_ref, g_ref, ext_ref, *, tb):
    i = pl.program_id(1)
    nblk = pl.num_programs(1)
    def conv_head(kind, hh, slab):
        cols = pl.ds(pl.multiple_of((kind * GDN_HEADS + hh) * HEAD_DIM, HEAD_DIM), HEAD_DIM)
        cw = cw_ref[:, cols]
        ext = ext_ref.at[slab]
        ext[pl.ds(0, HALO), :] = jnp.where(i > 0, prev_ref[:, cols], 0.0)
        ext[pl.ds(HALO, tb), :] = cur_ref[:, cols]
        ext[pl.ds(HALO + tb, HALO), :] = jnp.where(i < nblk - 1, next_ref[:, cols], 0.0)
        for r0 in range(0, tb, 64):
            acc = None
            for s in range(CONV_K):
                term = ext[pl.ds(HALO - CONV_K // 2 + s + r0, 64), :] * cw[s:s + 1, :]
                acc = term if acc is None else acc + term
            y = acc * jax.nn.sigmoid(acc)
            if kind < 2:
                y = y * lax.rsqrt(jnp.sum(y * y, axis=-1, keepdims=True) + EPS)
            if kind == 0:
                y = y * HEAD_DIM ** -0.5
            o_ref[pl.ds(r0, 64), cols] = y.astype(o_ref.dtype)

    def conv_cols(kind):
        def body(it, carry):
            for slab in range(PREP_SLABS):
                conv_head(kind, it * PREP_SLABS + slab, slab)
            return carry

        lax.fori_loop(0, GDN_HEADS // PREP_SLABS, body, 0)

    conv_cols(0)
    conv_cols(1)
    conv_cols(2)

    s = sm_ref[...]
    beta = jax.nn.sigmoid(s)
    g = -ea_ref[...] * _softplus(s + dt_ref[...])
    r = lax.broadcasted_iota(jnp.int32, (tb, tb), 0)
    c = lax.broadcasted_iota(jnp.int32, (tb, tb), 1)
    same = (r // GDN_CHUNK) == (c // GDN_CHUNK)
    lower = (same & (c <= r)).astype(BF16)
    upper = (same & (c >= r)).astype(BF16)
    parts = _split3_bf16(g)
    cf = sum(jnp.dot(lower, pt, preferred_element_type=F32) for pt in parts)
    cb = sum(jnp.dot(upper, pt, preferred_element_type=F32) for pt in parts)
    col = lax.broadcasted_iota(jnp.int32, s.shape, 1)
    nh = GDN_HEADS
    g_ref[...] = jnp.where(col < 2 * nh, beta, jnp.where(col < 3 * nh, cf, jnp.where(col < 4 * nh, cb, 0.0)))


def _gdn_prep(y_main, small, conv_w8, ea_row, dt_row, b_, l):
    tb = min(GDN_TB, l)
    nblk = l // tb
    hb = tb // HALO
    last = b_ * l // HALO - 1
    return pl.pallas_call(
        functools.partial(_gdn_prep_kernel, tb=tb),
        grid=(b_, nblk),
        in_specs=[pl.BlockSpec((tb, N_QKV), lambda b, i: (b * nblk + i, 0)),
                  pl.BlockSpec((HALO, N_QKV), lambda b, i: (jnp.maximum((b * nblk + i) * hb - 1, 0), 0)),
                  pl.BlockSpec((HALO, N_QKV), lambda b, i: (jnp.minimum((b * nblk + i + 1) * hb, last), 0)),
                  pl.BlockSpec((tb, N_SMALL), lambda b, i: (b * nblk + i, 0)),
                  pl.BlockSpec((HALO, N_QKV), lambda b, i: (0, 0)),
                  pl.BlockSpec((1, N_SMALL), lambda b, i: (0, 0)),
                  pl.BlockSpec((1, N_SMALL), lambda b, i: (0, 0))],
        out_specs=[pl.BlockSpec((tb, N_QKV), lambda b, i: (b * nblk + i, 0)),
                   pl.BlockSpec((tb, N_SMALL), lambda b, i: (b * nblk + i, 0))],
        out_shape=[jax.ShapeDtypeStruct((b_ * l, N_QKV), BF16),
                   jax.ShapeDtypeStruct((b_ * l, N_SMALL), F32)],
        scratch_shapes=[pltpu.VMEM((PREP_SLABS, tb + 2 * HALO, HEAD_DIM), F32)],
        compiler_params=_cparams(("arbitrary", "arbitrary")),
        name="gdn_prep",
    )(y_main, y_main, y_main, small, conv_w8, ea_row, dt_row)


N_PAIRS = GDN_HEADS // 2
PK = 2 * GDN_CHUNK
INV_LEVELS = (2, 4, 8, 16, 32, 64)


def _gdn_masks(reverse):
    i = np.arange(PK)[:, None]
    j = np.arange(PK)[None, :]
    same = (i // GDN_CHUNK) == (j // GDN_CHUNK)
    strict = same & ((j > i) if reverse else (j < i))
    out = []
    for bs in INV_LEVELS:
        out.append(strict & (i // bs == j // bs) & (i // (bs // 2) != j // (bs // 2)))
    incl = same & ((j >= i) if reverse else (j <= i))
    out.append(incl)
    m = np.stack(out).astype(np.float32)
    neg = ((incl.astype(np.float32) - 1.0) * 1e30)[None]
    return jnp.asarray(np.concatenate([m, neg], axis=0))


def _gdn_scan_kernel(qf_ref, kf_ref, vf_ref, qb_ref, kb_ref, vb_ref, cpf_ref, rpf_ref, apf_ref, cpb_ref, rpb_ref,
                     apb_ref, s0_ref, mk_ref, mkb_ref, of_ref, ob_ref, sfin_ref, s_scr):
    c = pl.program_id(1)

    @pl.when(c == 0)
    def _():
        s_scr[...] = s0_ref[0]

    nl = len(INV_LEVELS)
    ri = lax.broadcasted_iota(jnp.int32, (PK, 1), 0)
    top = (ri < GDN_CHUNK).astype(F32)
    bot = 1.0 - top
    rr = lax.broadcasted_iota(jnp.int32, (PK, PK), 0)
    cc = lax.broadcasted_iota(jnp.int32, (PK, PK), 1)
    eye = (rr == cc).astype(F32)
    dot = functools.partial(jnp.dot, preferred_element_type=F32)
    nt = (((1,), (1,)), ((), ()))
    tn = (((0,), (0,)), ((), ()))

    q_refs, k_refs, v_refs, o_refs = (qf_ref, qb_ref), (kf_ref, kb_ref), (vf_ref, vb_ref), (of_ref, ob_ref)
    cps = (cpf_ref[0, 0], cpb_ref[0, 0])
    rps = (rpf_ref[0, 0], rpb_ref[0, 0])
    aps = (apf_ref[0, 0], apb_ref[0, 0])
    units = [(d, p) for p in range(N_PAIRS) for d in range(2)]
    per_unit = lambda f: [f(i, d, p) for i, (d, p) in enumerate(units)]
    cols = lambda p: (pl.ds(2 * p * HEAD_DIM, HEAD_DIM), pl.ds((2 * p + 1) * HEAD_DIM, HEAD_DIM))
    pack = lambda ref, p: jnp.concatenate([ref[:, cols(p)[0]], ref[:, cols(p)[1]]], axis=0)
    mask = lambda d, n: mk_ref[d, n]

    gcol = per_unit(lambda i, d, p: cps[d][:, p:p + 1])
    bcol = per_unit(lambda i, d, p: cps[d][:, N_PAIRS + p:N_PAIRS + p + 1])
    glcol = per_unit(lambda i, d, p: cps[d][:, 2 * N_PAIRS + p:2 * N_PAIRS + p + 1])
    kp = per_unit(lambda i, d, p: pack(k_refs[d], p))
    qp = per_unit(lambda i, d, p: pack(q_refs[d], p))
    kk = per_unit(lambda i, d, p: lax.dot_general(kp[i], kp[i], nt, preferred_element_type=F32))
    qk = per_unit(lambda i, d, p: lax.dot_general(qp[i], kp[i], nt, preferred_element_type=F32))
    dec = per_unit(lambda i, d, p: jnp.exp((gcol[i] - rps[d][p:p + 1, :]) * mask(d, nl) + mask(d, nl + 1)))
    a = per_unit(lambda i, d, p: (kk[i] * dec[i] * bcol[i]).astype(BF16))
    qkm = per_unit(lambda i, d, p: (qk[i] * dec[i]).astype(BF16))

    x = per_unit(lambda i, d, p: eye - (a[i] * mkb_ref[d, 0]).astype(F32))
    for lv in range(1, nl):
        xb = per_unit(lambda i, d, p: x[i].astype(BF16))
        po = per_unit(lambda i, d, p: dot(xb[i], a[i] * mkb_ref[d, lv]))
        x = per_unit(lambda i, d, p: x[i] - dot(po[i].astype(BF16), xb[i]))
    tb = per_unit(lambda i, d, p: x[i].astype(BF16))

    egc = per_unit(lambda i, d, p: jnp.exp(gcol[i]))
    kf = per_unit(lambda i, d, p: kp[i].astype(F32))
    u = per_unit(lambda i, d, p: dot(tb[i], (pack(v_refs[d], p).astype(F32) * bcol[i]).astype(BF16)))
    w = per_unit(lambda i, d, p: dot(tb[i], (kf[i] * (bcol[i] * egc[i])).astype(BF16)))
    qd = per_unit(lambda i, d, p: qp[i].astype(F32) * egc[i])
    kd = per_unit(lambda i, d, p: kf[i] * jnp.exp(glcol[i] - gcol[i]))

    s = per_unit(lambda i, d, p: s_scr[d, p])
    lhs = per_unit(lambda i, d, p: jnp.concatenate(
        [jnp.concatenate([w[i] * top, w[i] * bot], axis=1),
         jnp.concatenate([qd[i] * top, qd[i] * bot], axis=1)], axis=0).astype(BF16))
    ws = per_unit(lambda i, d, p: dot(lhs[i], s[i].astype(BF16)))
    vnb = per_unit(lambda i, d, p: (u[i] - ws[i][:PK]).astype(BF16))
    o = per_unit(lambda i, d, p: ws[i][PK:] + dot(qkm[i], vnb[i]))
    kbd = per_unit(lambda i, d, p: jnp.concatenate([kd[i] * top, kd[i] * bot], axis=1).astype(BF16))
    kv = per_unit(lambda i, d, p: lax.dot_general(kbd[i], vnb[i], tn, preferred_element_type=F32))
    for i, (d, p) in enumerate(units):
        s_scr[d, p] = jnp.exp(aps[d][:, p:p + 1]) * s[i] + kv[i]
    for i, (d, p) in enumerate(units):
        o_refs[d][:, cols(p)[0]] = o[i][:GDN_CHUNK].astype(o_refs[d].dtype)
        o_refs[d][:, cols(p)[1]] = o[i][GDN_CHUNK:].astype(o_refs[d].dtype)

    @pl.when(c == pl.num_programs(1) - 1)
    def _():
        sfin_ref[0] = s_scr[...]


def _gdn_out_kernel(of_ref, ob_ref, z_ref, gon_ref, o_ref):
    for h in range(GDN_HEADS):
        c = pl.ds(h * HEAD_DIM, HEAD_DIM)
        o = of_ref[:, c].astype(F32) + ob_ref[:, c].astype(F32)
        z = z_ref[:, c]
        y = o * lax.rsqrt(jnp.mean(o * o, axis=-1, keepdims=True) + EPS) * gon_ref[...]
        o_ref[:, c] = (y * (z * jax.nn.sigmoid(z))).astype(o_ref.dtype)


def _gdn_out(o_f, o_b, y_main, g_on, tm):
    t = o_f.shape[0]
    blk = lambda col: pl.BlockSpec((tm, GDN_WIDTH), lambda i: (i, col))
    return pl.pallas_call(
        _gdn_out_kernel,
        grid=(t // tm,),
        in_specs=[blk(0), blk(0), blk(N_QKV // GDN_WIDTH), pl.BlockSpec((1, HEAD_DIM), lambda i: (0, 0))],
        out_specs=blk(0),
        out_shape=jax.ShapeDtypeStruct((t, GDN_WIDTH), BF16),
        compiler_params=_cparams(("arbitrary",)),
        name="gdn_out",
    )(o_f, o_b, y_main, g_on.reshape(1, HEAD_DIM))


def _gdn_packs(gates, b_, l, reverse):
    nc = l // GDN_CHUNK
    nh = GDN_HEADS
    g4 = gates.reshape(b_, nc, GDN_CHUNK, N_SMALL)
    d = 1 if reverse else 0
    beta = g4[..., d * nh:(d + 1) * nh]
    gc = g4[..., (2 + d) * nh:(3 + d) * nh]
    gl = jnp.broadcast_to(gc[:, :, 0:1] if reverse else gc[:, :, GDN_CHUNK - 1:GDN_CHUNK], gc.shape)

    def rowpack(t):
        return jnp.transpose(t.reshape(b_, nc, GDN_CHUNK, N_PAIRS, 2), (0, 1, 3, 4, 2)).reshape(b_, nc, N_PAIRS, PK)

    rp = rowpack(gc)
    cp = jnp.concatenate([jnp.swapaxes(rowpack(t), 2, 3) for t in (gc, beta, gl)]
                         + [jnp.zeros((b_, nc, PK, N_PAIRS), F32)], axis=-1)
    glh = gl[:, :, 0].reshape(b_, nc, N_PAIRS, 2)
    ap = jnp.swapaxes(jnp.repeat(glh, HEAD_DIM, axis=-1), 2, 3)
    return cp, rp, ap


def _gdn_scan(qkv, gates, s0, b_, l):
    nc = l // GDN_CHUNK
    packs = [_gdn_packs(gates, b_, l, reverse) for reverse in (False, True)]
    masks = jnp.stack([_gdn_masks(False), _gdn_masks(True)])
    level_masks = masks[:, :len(INV_LEVELS)].astype(BF16)
    chunk_of = (lambda c: c, lambda c: nc - 1 - c)
    tok = lambda d, col: pl.BlockSpec((GDN_CHUNK, GDN_WIDTH), lambda b, c: (b * nc + chunk_of[d](c), col))
    per_chunk = lambda d, shp: pl.BlockSpec((1, 1) + shp, lambda b, c: (b, chunk_of[d](c), 0, 0))
    gate_specs = lambda d: [per_chunk(d, (PK, 4 * N_PAIRS)), per_chunk(d, (N_PAIRS, PK)),
                            per_chunk(d, (2 * HEAD_DIM, N_PAIRS))]
    state = pl.BlockSpec((1, 2, N_PAIRS, 2 * HEAD_DIM, HEAD_DIM), lambda b, c: (b, 0, 0, 0, 0))
    return pl.pallas_call(
        _gdn_scan_kernel,
        grid=(b_, nc),
        in_specs=[tok(0, 0), tok(0, 1), tok(0, 2), tok(1, 0), tok(1, 1), tok(1, 2)] + gate_specs(0) + gate_specs(1)
        + [state, pl.BlockSpec(masks.shape, lambda b, c: (0, 0, 0, 0)),
           pl.BlockSpec(level_masks.shape, lambda b, c: (0, 0, 0, 0))],
        out_specs=[tok(0, 0), tok(1, 0), state],
        out_shape=[jax.ShapeDtypeStruct((b_ * l, GDN_WIDTH), BF16),
                   jax.ShapeDtypeStruct((b_ * l, GDN_WIDTH), BF16),
                   jax.ShapeDtypeStruct(s0.shape, F32)],
        scratch_shapes=[pltpu.VMEM((2, N_PAIRS, 2 * HEAD_DIM, HEAD_DIM), F32)],
        compiler_params=_cparams(("arbitrary", "arbitrary")),
        name="gdn_scan",
    )(qkv, qkv, qkv, qkv, qkv, qkv, *packs[0], *packs[1], s0, masks, level_masks)


def _gdn_mixer(yx, sx, yc, sc, conv_w, a_log, dt_bias, g_on, b_, l, n_ctx):
    nh = GDN_HEADS
    conv_w8 = jnp.pad(conv_w, ((0, HALO - CONV_K), (0, 0)))
    ea_row = jnp.zeros((1, N_SMALL), F32).at[0, 2 * nh:4 * nh].set(jnp.exp(a_log.reshape(-1)))
    dt_row = jnp.zeros((1, N_SMALL), F32).at[0, 2 * nh:4 * nh].set(dt_bias.reshape(-1))
    qkv_c, gates_c = _gdn_prep(yc, sc, conv_w8, ea_row, dt_row, b_, n_ctx)
    qkv_x, gates_x = _gdn_prep(yx, sx, conv_w8, ea_row, dt_row, b_, l)
    zero = jnp.zeros((b_, 2, N_PAIRS, 2 * HEAD_DIM, HEAD_DIM), F32)
    _, _, s_ctx = _gdn_scan(qkv_c, gates_c, zero, b_, n_ctx)
    o_f, o_b, _ = _gdn_scan(qkv_x, gates_x, s_ctx, b_, l)
    return _gdn_out(o_f, o_b, yx, g_on, 256)


SWA_GROUP = SWA_HEADS // SWA_KV_HEADS
ROT = HEAD_DIM // 4


def _rope_tables(l):
    half = HEAD_DIM // 2
    inv = ROPE_BASE ** (-jnp.arange(0, half, 2, dtype=F32) / half)
    pos = jnp.arange(l, dtype=jnp.int32)
    ang_r = (pos // GRID_W).astype(F32)[:, None] * inv
    ang_c = (pos % GRID_W).astype(F32)[:, None] * inv
    zero = jnp.zeros_like(ang_r)
    cos = jnp.concatenate([jnp.cos(ang_r), jnp.cos(ang_r), jnp.cos(ang_c), jnp.cos(ang_c)], axis=1)
    sin_up = jnp.concatenate([-jnp.sin(ang_r), zero, -jnp.sin(ang_c), zero], axis=1)
    sin_dn = jnp.concatenate([zero, jnp.sin(ang_r), zero, jnp.sin(ang_c)], axis=1)
    return cos, sin_up, sin_dn


def _swa_prep_kernel(q_ref, k_ref, v_ref, cos_ref, su_ref, sd_ref, gq_ref, gk_ref, qo_ref, ko_ref, vo_ref):
    cos, su, sd = cos_ref[...], su_ref[...], sd_ref[...]

    def norm_rope(t, g, scale):
        y = t * lax.rsqrt(jnp.mean(t * t, axis=-1, keepdims=True) + EPS) * g
        y = y * cos + pltpu.roll(y, HEAD_DIM - ROT, 1) * su + pltpu.roll(y, ROT, 1) * sd
        return y * scale if scale != 1.0 else y

    for h in range(SWA_HEADS):
        c = pl.ds(h * HEAD_DIM, HEAD_DIM)
        qo_ref[:, c] = norm_rope(q_ref[:, c], gq_ref[...], HEAD_DIM ** -0.5).astype(qo_ref.dtype)
    for h in range(SWA_KV_HEADS):
        c = pl.ds(h * HEAD_DIM, HEAD_DIM)
        ko_ref[:, c] = norm_rope(k_ref[:, c], gk_ref[...], 1.0).astype(ko_ref.dtype)
    vo_ref[...] = v_ref[...].astype(vo_ref.dtype)


def _swa_prep(y_main, tables, g_q, g_k, rows, tm):
    q_blk = (N_QKV + GDN_WIDTH) // SWA_WIDTH
    k_blk = (N_QKV + GDN_WIDTH + SWA_WIDTH) // SWA_KV_WIDTH
    tpb = tables[0].shape[0] // tm
    tab = pl.BlockSpec((tm, HEAD_DIM), lambda i: (i % tpb, 0))
    vec = pl.BlockSpec((1, HEAD_DIM), lambda i: (0, 0))
    return pl.pallas_call(
        _swa_prep_kernel,
        grid=(rows // tm,),
        in_specs=[pl.BlockSpec((tm, SWA_WIDTH), lambda i: (i, q_blk)),
                  pl.BlockSpec((tm, SWA_KV_WIDTH), lambda i: (i, k_blk)),
                  pl.BlockSpec((tm, SWA_KV_WIDTH), lambda i: (i, k_blk + 1)),
                  tab, tab, tab, vec, vec],
        out_specs=[pl.BlockSpec((tm, SWA_WIDTH), lambda i: (i, 0)),
                   pl.BlockSpec((tm, SWA_KV_WIDTH), lambda i: (i, 0)),
                   pl.BlockSpec((tm, SWA_KV_WIDTH), lambda i: (i, 0))],
        out_shape=[jax.ShapeDtypeStruct((rows, SWA_WIDTH), BF16),
                   jax.ShapeDtypeStruct((rows, SWA_KV_WIDTH), BF16),
                   jax.ShapeDtypeStruct((rows, SWA_KV_WIDTH), BF16)],
        compiler_params=_cparams(("arbitrary",)),
        name="swa_prep",
    )(y_main, y_main, y_main, *tables, g_q.reshape(1, HEAD_DIM), g_k.reshape(1, HEAD_DIM))


def _swa_attn_kernel(q_ref, kp_ref, kc_ref, kn_ref, vp_ref, vc_ref, vn_ref, kx_ref, vx_ref, sink_ref, o_ref, *, n_ctx):
    n = pl.program_id(1)
    nb = pl.num_programs(1)
    rows = SWA_GROUP * Q_BLOCK
    nk = n_ctx + 3 * Q_BLOCK
    qi = lax.broadcasted_iota(jnp.int32, (rows, nk), 0) & (Q_BLOCK - 1)
    kj = lax.broadcasted_iota(jnp.int32, (rows, nk), 1) - n_ctx
    lo = jnp.where(n == 0, Q_BLOCK, 0)
    hi = jnp.where(n == nb - 1, 2 * Q_BLOCK, 3 * Q_BLOCK)
    valid = (kj < 0) | ((kj >= qi) & (kj <= qi + 2 * WINDOW) & (kj >= lo) & (kj < hi))
    hsel = lax.broadcasted_iota(jnp.int32, (rows, 1), 0) // Q_BLOCK
    nt = (((1,), (1,)), ((), ()))
    kv_heads = range(SWA_KV_HEADS)
    col = lambda j: pl.ds(j * HEAD_DIM, HEAD_DIM)
    heads = [[j * SWA_GROUP + g for g in range(SWA_GROUP)] for j in kv_heads]
    s = []
    for j in kv_heads:
        q = jnp.concatenate([q_ref[:, col(h)] for h in heads[j]], axis=0)
        k = jnp.concatenate([kx_ref[:, col(j)], kp_ref[:, col(j)], kc_ref[:, col(j)], kn_ref[:, col(j)]], axis=0)
        s.append(jnp.where(valid, lax.dot_general(q, k, nt, preferred_element_type=F32), NEG_INF))
    p, den = [], []
    for j in kv_heads:
        sink = jnp.zeros((rows, 1), F32)
        for g, h in enumerate(heads[j]):
            sink = jnp.where(hsel == g, sink_ref[h:h + 1, 0:1], sink)
        m = jnp.maximum(jnp.max(s[j], axis=-1, keepdims=True), sink)
        e = jnp.exp(s[j] - m)
        den.append(jnp.sum(e, axis=-1, keepdims=True) + jnp.exp(sink - m))
        p.append(e.astype(BF16))
    for j in kv_heads:
        v = jnp.concatenate([vx_ref[:, col(j)], vp_ref[:, col(j)], vc_ref[:, col(j)], vn_ref[:, col(j)]], axis=0)
        o = jnp.dot(p[j], v, preferred_element_type=F32) / den[j]
        for g, h in enumerate(heads[j]):
            o_ref[:, col(h)] = o[g * Q_BLOCK:(g + 1) * Q_BLOCK].astype(o_ref.dtype)


def _swa_attention(q, k, v, k_ctx, v_ctx, sink, b_, l, n_ctx):
    nb = l // Q_BLOCK
    blk = lambda w, off: pl.BlockSpec(
        (Q_BLOCK, w), lambda b, n: (b * nb + jnp.clip(n + off, 0, nb - 1), 0))
    ctx = pl.BlockSpec((n_ctx, SWA_KV_WIDTH), lambda b, n: (b, 0))
    kw = SWA_KV_WIDTH
    return pl.pallas_call(
        functools.partial(_swa_attn_kernel, n_ctx=n_ctx),
        grid=(b_, nb),
        in_specs=[blk(SWA_WIDTH, 0), blk(kw, -1), blk(kw, 0), blk(kw, 1), blk(kw, -1), blk(kw, 0), blk(kw, 1),
                  ctx, ctx, pl.BlockSpec((SWA_HEADS, HEAD_DIM), lambda b, n: (0, 0))],
        out_specs=blk(SWA_WIDTH, 0),
        out_shape=jax.ShapeDtypeStruct((b_ * l, SWA_WIDTH), BF16),
        compiler_params=_cparams(("arbitrary", "arbitrary")),
        name="swa_attention",
    )(q, k, k, k, v, v, v, k_ctx, v_ctx, jnp.broadcast_to(sink.astype(F32)[:, None], (SWA_HEADS, HEAD_DIM)))


def _swa_mixer(yx, yc, g_q, g_k, sink, b_, l, n_ctx):
    ones = jnp.ones((n_ctx, HEAD_DIM), F32)
    zeros = jnp.zeros((n_ctx, HEAD_DIM), F32)
    qx, kx, vx = _swa_prep(yx, _rope_tables(l), g_q, g_k, b_ * l, 256)
    _, kc, vc = _swa_prep(yc, (ones, zeros, zeros), g_q, g_k, b_ * n_ctx, n_ctx)
    return _swa_attention(qx, kx, vx, kc, vc, sink, b_, l, n_ctx)


def _rmsnorm(x, g):
    xf = x.astype(F32)
    y = xf * lax.rsqrt(jnp.mean(xf * xf, axis=-1, keepdims=True) + EPS)
    return (y * g.astype(F32)).astype(x.dtype)


def _l2norm(x):
    return x * lax.rsqrt(jnp.sum(x * x, axis=-1, keepdims=True) + EPS)


def _short_conv(x, w):
    c = x.shape[-1]
    y = lax.conv_general_dilated(x, w[:, None, :].astype(x.dtype), window_strides=(1,),
                                 padding=[(CONV_K // 2, CONV_K // 2)],
                                 dimension_numbers=('NWC', 'WIO', 'NWC'), feature_group_count=c)
    return jax.nn.silu(y)


def _axial_rope(t, rows, cols):
    half = HEAD_DIM // 2
    inv = ROPE_BASE ** (-jnp.arange(0, half, 2, dtype=F32) / half)

    def rot(u, pos):
        ang = pos[:, None] * inv
        cos, sin = jnp.cos(ang)[None, :, None, :], jnp.sin(ang)[None, :, None, :]
        u1, u2 = jnp.split(u.astype(F32), 2, axis=-1)
        return jnp.concatenate([u1 * cos - u2 * sin, u2 * cos + u1 * sin], axis=-1)

    return jnp.concatenate([rot(t[..., :half], rows), rot(t[..., half:], cols)], axis=-1).astype(t.dtype)


def _gated_delta_chunked(q, k, v, g, beta, s0):
    b_, l, h, _ = q.shape
    n = l // GDN_CHUNK

    def chunks(t):
        t = t.reshape((b_, n, GDN_CHUNK) + t.shape[2:])
        return jnp.moveaxis(jnp.swapaxes(t, 2, 3), 1, 0)

    qc, kc, vc, gc, bc = map(chunks, (q, k, v, g, beta))
    gcum = jnp.cumsum(gc, axis=-1)
    tri = jnp.tril(jnp.ones((GDN_CHUNK, GDN_CHUNK), bool))
    strict = jnp.tril(jnp.ones((GDN_CHUNK, GDN_CHUNK), F32), -1)
    diff = gcum[..., :, None] - gcum[..., None, :]
    decay = jnp.where(tri, jnp.exp(jnp.where(tri, diff, 0.0)), 0.0)
    kb = kc * bc[..., None]
    a_strict = jnp.einsum('nbhid,nbhjd->nbhij', kb, kc) * decay * strict
    eye = jnp.eye(GDN_CHUNK, dtype=F32)
    t_inv = lax.linalg.triangular_solve(eye + a_strict, jnp.broadcast_to(eye, a_strict.shape),
                                        left_side=True, lower=True, unit_diagonal=True)
    u = jnp.einsum('nbhij,nbhjd->nbhid', t_inv, vc * bc[..., None])
    w = jnp.einsum('nbhij,nbhjd->nbhid', t_inv, kb * jnp.exp(gcum)[..., None])
    qk = jnp.einsum('nbhid,nbhjd->nbhij', qc, kc) * decay
    q_dec = qc * jnp.exp(gcum)[..., None]
    k_dec = kc * jnp.exp(gcum[..., -1:] - gcum)[..., None]
    chunk_decay = jnp.exp(gcum[..., -1])

    def step(s, xs):
        u_n, w_n, qk_n, q_n, k_n, a_n = xs
        v_new = u_n - jnp.einsum('bhcd,bhde->bhce', w_n, s)
        o = jnp.einsum('bhcd,bhde->bhce', q_n, s) + jnp.einsum('bhij,bhje->bhie', qk_n, v_new)
        s = s * a_n[..., None, None] + jnp.einsum('bhcd,bhce->bhde', k_n, v_new)
        return s, o

    s_fin, o = lax.scan(step, s0, (u, w, qk, q_dec, k_dec, chunk_decay))
    o = jnp.swapaxes(jnp.moveaxis(o, 0, 1), 2, 3).reshape(b_, l, h, -1)
    return o, s_fin


def _gdn_inputs(qkv, b_logit, a_logit, conv_w, a_log, dt_bias):
    b_, l, _ = qkv.shape
    qkv = _short_conv(qkv, conv_w).astype(F32).reshape(b_, l, 3, GDN_HEADS, HEAD_DIM)
    q = _l2norm(qkv[:, :, 0]) * HEAD_DIM ** -0.5
    k = _l2norm(qkv[:, :, 1])
    v = qkv[:, :, 2]
    beta = jax.nn.sigmoid(b_logit.astype(F32).reshape(b_, l, 2, GDN_HEADS))
    g = -jnp.exp(a_log.astype(F32)) * jax.nn.softplus(
        a_logit.astype(F32).reshape(b_, l, 2, GDN_HEADS) + dt_bias.astype(F32))
    return q, k, v, g, beta


def _gdn_bidir(q, k, v, g, beta, s_fwd0, s_bwd0):
    o_f, s_f = _gated_delta_chunked(q, k, v, g[:, :, 0], beta[:, :, 0], s_fwd0)
    rev = lambda t: jnp.flip(t, axis=1)
    o_b, s_b = _gated_delta_chunked(rev(q), rev(k), rev(v), rev(g[:, :, 1]), rev(beta[:, :, 1]), s_bwd0)
    return o_f + rev(o_b), s_f, s_b


def _gdn_output(o, z, g_on):
    b_, l = o.shape[:2]
    y = _rmsnorm(o, g_on).reshape(b_, l, GDN_WIDTH)
    return y * jax.nn.silu(z.astype(F32))


def _swa_inputs(q, k, v, g_q, g_k):
    b_, l, _ = q.shape
    q = _rmsnorm(q.reshape(b_, l, SWA_HEADS, HEAD_DIM), g_q)
    k = _rmsnorm(k.reshape(b_, l, SWA_KV_HEADS, HEAD_DIM), g_k)
    return q, k, v.reshape(b_, l, SWA_KV_HEADS, HEAD_DIM)


def _window_attention(q, k, v, k_ctx, v_ctx, sink):
    b_, l, h, d = q.shape
    g = h // SWA_KV_HEADS
    nb = l // Q_BLOCK
    n_ctx = k_ctx.shape[1]
    scale = d ** -0.5
    qb = q.reshape(b_, nb, Q_BLOCK, SWA_KV_HEADS, g, d)

    def band_blocks(t):
        tp = jnp.pad(t, ((0, 0), (Q_BLOCK, Q_BLOCK), (0, 0), (0, 0))).reshape(b_, nb + 2, Q_BLOCK, SWA_KV_HEADS, d)
        return jnp.concatenate([tp[:, :-2], tp[:, 1:-1], tp[:, 2:]], axis=2)

    kw, vw = band_blocks(k), band_blocks(v)
    qi = jnp.arange(Q_BLOCK)[:, None]
    kj = jnp.arange(3 * Q_BLOCK)[None, :]
    band = jnp.abs(kj - Q_BLOCK - qi) <= WINDOW
    kpos = jnp.arange(nb)[:, None] * Q_BLOCK - Q_BLOCK + jnp.arange(3 * Q_BLOCK)[None, :]
    valid = band[None] & ((kpos >= 0) & (kpos < l))[:, None, :]
    s_win = jnp.einsum('bnqkgd,bnjkd->bnkgqj', qb, kw).astype(F32) * scale
    s_win = jnp.where(valid[None, :, None, None], s_win, NEG_INF)
    s_ctx = jnp.einsum('bnqkgd,bckd->bnkgqc', qb, k_ctx).astype(F32) * scale
    s_sink = jnp.broadcast_to(sink.astype(F32).reshape(SWA_KV_HEADS, g, 1, 1), s_ctx.shape[:-1] + (1,))
    p = jax.nn.softmax(jnp.concatenate([s_ctx, s_win, s_sink], axis=-1), axis=-1).astype(v.dtype)
    o = (jnp.einsum('bnkgqc,bckd->bnqkgd', p[..., :n_ctx], v_ctx)
         + jnp.einsum('bnkgqj,bnjkd->bnqkgd', p[..., n_ctx:n_ctx + 3 * Q_BLOCK], vw))
    return o.reshape(b_, l, h * d)


def _split_main(y):
    o = np.cumsum((0, 3 * GDN_WIDTH, GDN_WIDTH, SWA_WIDTH, SWA_KV_WIDTH, SWA_KV_WIDTH))
    return tuple(y[..., int(o[n]):int(o[n + 1])] for n in range(5))


def kernel(x, c, ctx, c_ctx, w_ada, b_ada, g_norm1, g_norm2, w_in, conv_qkv, a_log, dt_bias, g_onorm, g_qnorm,
           g_knorm, sink, w_out, w_router_grp, b_router_grp, w_router_exp, b_router_exp, w_gate, w_up, w_down):
    b_, l, d = x.shape
    n_ctx = ctx.shape[1]
    t = b_ * l
    assert w_ada.shape[0] == 1 and d == D_MODEL and b_ + 1 <= MOD_ROWS
    rows = jnp.repeat(jnp.arange(l // GRID_W, dtype=F32), GRID_W)
    cols = jnp.tile(jnp.arange(GRID_W, dtype=F32), l // GRID_W)

    wi = w_in[0]
    w_main = _prep_in_weights(wi)
    w_small = jnp.pad(wi[:, IN_OFFS[2]:IN_OFFS[4]], ((0, 0), (0, N_SMALL - 4 * GDN_HEADS))).astype(BF16)
    wo = w_out[0].astype(BF16)
    w_router = jnp.pad(jnp.concatenate([w_router_grp[0], w_router_exp[0]], axis=1),
                       ((0, 0), (0, N_ROUTER - N_GROUPS - N_EXPERTS))).astype(BF16)
    b_router = jnp.pad(jnp.concatenate([b_router_grp[0], b_router_exp[0]]),
                       (0, N_ROUTER - N_GROUPS - N_EXPERTS)).reshape(1, N_ROUTER)

    c_rows = jnp.zeros((MOD_ROWS, d), F32).at[:b_].set(c).at[b_].set(c_ctx)
    mod = _modulation(c_rows, w_ada[0], b_ada[0])
    mod3 = mod.reshape(MOD_ROWS * 6, 1, d)

    tm = 512
    tpb = l // tm
    yx, sx = _in_projection(x.reshape(t, d), g_norm1[0], mod3, lambda i: i // tpb, w_main, w_small, tm)
    yc, sc = _in_projection(ctx.reshape(b_ * n_ctx, d), g_norm1[0], mod3, lambda i: b_, w_main, w_small, n_ctx)
    ya_x = _gdn_mixer(yx, sx, yc, sc, conv_qkv[0], a_log[0], dt_bias[0], g_onorm[0], b_, l, n_ctx)
    yb_x = _swa_mixer(yx, yc, g_qnorm[0], g_knorm[0], sink[0], b_, l, n_ctx)

    x1 = _out_projection(ya_x, yb_x, wo, x.reshape(t, d), mod3, tpb, tm)

    tm2 = 256
    h2c, ids, gates = _norm2_router(x1, g_norm2[0], mod3, l // tm2, w_router, b_router, tm2)
    tok_sorted, blk_first, blk_rows, slot_of, blk_expert, n_used = _slots(ids[:, :TOP_K], t)
    yc_moe = _moe_experts(h2c, tok_sorted, blk_first, blk_rows, blk_expert, n_used, w_gate[0], w_up[0], w_down[0])
    return _moe_combine(yc_moe, slot_of, x1, gates, mod3, l // COMBINE_TM).reshape(b_, l, d)
```

```python
import functools
import math

import jax
import jax.numpy as jnp
import numpy as np
from jax import lax
from jax.experimental import pallas as pl
from jax.experimental.pallas import tpu as pltpu

F32 = jnp.float32
BF16 = jnp.bfloat16

D_MODEL = 4096
CTX_LEN = 256
GRID_W = 64
HEAD_DIM = 128
GDN_HEADS = 16
GDN_WIDTH = GDN_HEADS * HEAD_DIM
GDN_CHUNK = 64
CONV_K = 5
SWA_HEADS = 16
SWA_KV_HEADS = 4
SWA_WIDTH = SWA_HEADS * HEAD_DIM
SWA_KV_WIDTH = SWA_KV_HEADS * HEAD_DIM
WINDOW = 128
Q_BLOCK = 128
ROPE_BASE = 10000.0
N_GROUPS = 8
EXPERTS_PER_GROUP = 8
N_EXPERTS = N_GROUPS * EXPERTS_PER_GROUP
TOP_K = 2
D_EXPERT = D_MODEL // 8
EPS = 1e-6
NEG_INF = -1e30

IN_SIZES = (3 * GDN_WIDTH, GDN_WIDTH, 2 * GDN_HEADS, 2 * GDN_HEADS, SWA_WIDTH, SWA_KV_WIDTH, SWA_KV_WIDTH)
IN_OFFS = tuple(int(v) for v in np.cumsum((0,) + IN_SIZES))
N_MAIN = 3 * GDN_WIDTH + GDN_WIDTH + SWA_WIDTH + 2 * SWA_KV_WIDTH
N_SMALL = 128
N_ROUTER = 128
MOD_ROWS = 8

MOE_BM = 256
VMEM_LIMIT = 56 * 1024 * 1024


def _cparams(sem):
    return pltpu.CompilerParams(dimension_semantics=sem, vmem_limit_bytes=VMEM_LIMIT)


def _mod_kernel(c_ref, w_ref, b_ref, o_ref):
    c = c_ref[...]
    a = (c * jax.nn.sigmoid(c)).astype(BF16)
    o_ref[...] = jnp.dot(a, w_ref[...].astype(BF16), preferred_element_type=F32) + b_ref[...]


def _modulation(c_rows, w_ada, b_ada):
    d, n = w_ada.shape
    tn = 512
    return pl.pallas_call(
        _mod_kernel,
        grid=(n // tn,),
        in_specs=[pl.BlockSpec((MOD_ROWS, d), lambda j: (0, 0)),
                  pl.BlockSpec((d, tn), lambda j: (0, j)),
                  pl.BlockSpec((1, tn), lambda j: (0, j))],
        out_specs=pl.BlockSpec((MOD_ROWS, tn), lambda j: (0, j)),
        out_shape=jax.ShapeDtypeStruct((MOD_ROWS, n), F32),
        compiler_params=_cparams(("arbitrary",)),
        name="modulation",
    )(c_rows, w_ada, b_ada.reshape(1, n))


NORM_ROWS = 64


def _norm_mod_rows(x_ref, g_ref, sh_ref, sc_ref, h_ref, tm):
    g = g_ref[...]
    sc = 1.0 + sc_ref[0]
    sh = sh_ref[0]

    def body(r, carry):
        rows = pl.ds(pl.multiple_of(r * NORM_ROWS, NORM_ROWS), NORM_ROWS)
        xf = x_ref[rows, :]
        ms = jnp.mean(xf * xf, axis=-1, keepdims=True)
        y = xf * lax.rsqrt(ms + EPS) * g
        h_ref[rows, :] = (y * sc + sh).astype(h_ref.dtype)
        return carry

    lax.fori_loop(0, tm // NORM_ROWS, body, 0)


def _inproj_kernel(x_ref, g_ref, sh_ref, sc_ref, w_ref, ws_ref, o_ref, os_ref, h_ref, *, tm):
    nt = (((1,), (1,)), ((), ()))

    @pl.when(pl.program_id(1) == 0)
    def _():
        _norm_mod_rows(x_ref, g_ref, sh_ref, sc_ref, h_ref, tm)
        os_ref[...] = lax.dot_general(h_ref[...], ws_ref[...], nt, preferred_element_type=F32)

    o_ref[...] = lax.dot_general(h_ref[...], w_ref[...], nt, preferred_element_type=F32)


def _in_projection(x2d, g_norm, mod3, mod_row_of_tile, w_main, w_small, tm):
    t, d = x2d.shape
    tn = 1024
    return pl.pallas_call(
        functools.partial(_inproj_kernel, tm=tm),
        grid=(t // tm, N_MAIN // tn),
        in_specs=[pl.BlockSpec((tm, d), lambda i, j: (i, 0)),
                  pl.BlockSpec((1, d), lambda i, j: (0, 0)),
                  pl.BlockSpec((1, 1, d), lambda i, j: (mod_row_of_tile(i) * 6 + 0, 0, 0)),
                  pl.BlockSpec((1, 1, d), lambda i, j: (mod_row_of_tile(i) * 6 + 1, 0, 0)),
                  pl.BlockSpec((tn, d), lambda i, j: (j, 0)),
                  pl.BlockSpec((N_SMALL, d), lambda i, j: (0, 0))],
        out_specs=[pl.BlockSpec((tm, tn), lambda i, j: (i, j)),
                   pl.BlockSpec((tm, N_SMALL), lambda i, j: (i, 0))],
        out_shape=[jax.ShapeDtypeStruct((t, N_MAIN), F32),
                   jax.ShapeDtypeStruct((t, N_SMALL), F32)],
        scratch_shapes=[pltpu.VMEM((tm, d), BF16)],
        compiler_params=_cparams(("arbitrary", "arbitrary")),
        name="in_projection",
    )(x2d, g_norm.reshape(1, d), mod3, mod3, w_main, w_small)


def _outproj_kernel(ya_ref, yb_ref, wa_ref, wb_ref, x_ref, gate_ref, o_ref):
    acc = jnp.dot(ya_ref[...], wa_ref[...], preferred_element_type=F32)
    acc = acc + jnp.dot(yb_ref[...], wb_ref[...], preferred_element_type=F32)
    o_ref[...] = x_ref[...] + gate_ref[0] * acc


def _out_projection(ya, yb, w, x2d, mod3, tiles_per_batch, tm):
    t, d = x2d.shape
    tn = 1024
    nj = d // tn
    ka, kb = ya.shape[1], yb.shape[1]
    assert ka == kb and w.shape[0] == ka + kb
    return pl.pallas_call(
        _outproj_kernel,
        grid=(t // tm, nj),
        in_specs=[pl.BlockSpec((tm, ka), lambda i, j: (i, 0)),
                  pl.BlockSpec((tm, kb), lambda i, j: (i, 0)),
                  pl.BlockSpec((ka, tn), lambda i, j: (0, j)),
                  pl.BlockSpec((kb, tn), lambda i, j: (1, j)),
                  pl.BlockSpec((tm, tn), lambda i, j: (i, j)),
                  pl.BlockSpec((1, 1, tn), lambda i, j: (((i // tiles_per_batch) * 6 + 2) * nj + j, 0, 0))],
        out_specs=pl.BlockSpec((tm, tn), lambda i, j: (i, j)),
        out_shape=jax.ShapeDtypeStruct((t, d), F32),
        compiler_params=_cparams(("arbitrary", "arbitrary")),
        name="out_projection",
    )(ya, yb, w, w, x2d, mod3.reshape(-1, 1, tn))


def _route_rows(lg):
    col = lax.broadcasted_iota(jnp.int32, lg.shape, 1)
    first = lambda hit: jnp.min(jnp.where(hit, col, N_ROUTER), axis=-1, keepdims=True)
    gm = col < N_GROUPS
    mg = jnp.max(jnp.where(gm, lg, NEG_INF), axis=-1, keepdims=True)
    grp = first(gm & (lg == mg))
    p_grp = 1.0 / jnp.sum(jnp.where(gm, jnp.exp(lg - mg), 0.0), axis=-1, keepdims=True)
    lo = N_GROUPS + grp * EXPERTS_PER_GROUP
    em = (col >= lo) & (col < lo + EXPERTS_PER_GROUP)
    m1 = jnp.max(jnp.where(em, lg, NEG_INF), axis=-1, keepdims=True)
    i1 = first(em & (lg == m1))
    em2 = em & (col != i1)
    m2 = jnp.max(jnp.where(em2, lg, NEG_INF), axis=-1, keepdims=True)
    i2 = first(em2 & (lg == m2))
    e2 = jnp.exp(m2 - m1)
    g1 = p_grp / (1.0 + e2)
    ids = jnp.where(col == 0, i1 - N_GROUPS, jnp.where(col == 1, i2 - N_GROUPS, 0))
    gates = jnp.where(col == 0, g1, jnp.where(col == 1, g1 * e2, 0.0))
    return ids, gates


LANES = 128
HALF_D = D_MODEL // 2
ROW_CH = HALF_D // LANES
ROW_PITCH = 24
U32 = jnp.uint32
HI_MASK = 0xFFFF0000
DMA_UNROLL = 8


def _pack_pair(lo, hi):
    bits = lambda v: lax.bitcast_convert_type(v.astype(BF16).astype(F32), U32)
    return (bits(hi) & U32(HI_MASK)) | (bits(lo) >> 16)


def _unpack_pair(w):
    return lax.bitcast_convert_type(w << 16, F32), lax.bitcast_convert_type(w & U32(HI_MASK), F32)


def _store_chunked(dst_ref, row0, vals):
    n = vals.shape[0]
    for j in range(ROW_PITCH):
        if j < ROW_CH:
            piece = _pack_pair(vals[:, j * LANES:(j + 1) * LANES], vals[:, HALF_D + j * LANES:HALF_D + (j + 1) * LANES])
        else:
            piece = jnp.zeros((n, LANES), U32)
        dst_ref[pl.ds(row0 * ROW_PITCH + j, n, stride=ROW_PITCH), :] = piece


def _load_chunk(src_ref, row0, n, j):
    return _unpack_pair(src_ref[pl.ds(row0 * ROW_PITCH + j, n, stride=ROW_PITCH), :])


def _row_copy(src_ref, dst_ref, src_row, dst_row, sem):
    return pltpu.make_async_copy(src_ref.at[pl.ds(pl.multiple_of(src_row * ROW_PITCH, 8), ROW_CH), :],
                                 dst_ref.at[pl.ds(pl.multiple_of(dst_row * ROW_PITCH, 8), ROW_CH), :], sem)


def _gather_rows(src_hbm, dst_ref, src_row_of, n, sem, dst_row0=0):
    def body(r, carry):
        _row_copy(src_hbm, dst_ref, src_row_of(r), dst_row0 + r, sem).start()
        return carry

    lax.fori_loop(0, n, body, 0, unroll=DMA_UNROLL)


def _wait_rows(src_hbm, dst_ref, n, sem):
    def body(r, carry):
        _row_copy(src_hbm, dst_ref, 0, r, sem).wait()
        return carry

    lax.fori_loop(0, n, body, 0, unroll=DMA_UNROLL)


def _norm2_kernel(x_ref, g_ref, sh_ref, sc_ref, wr_ref, br_ref, hc_ref, id_ref, gt_ref, hb_ref, *, tm):
    g = g_ref[...]
    sc = 1.0 + sc_ref[0]
    sh = sh_ref[0]

    def body(r, carry):
        row0 = pl.multiple_of(r * NORM_ROWS, NORM_ROWS)
        xf = x_ref[pl.ds(row0, NORM_ROWS), :]
        ms = jnp.mean(xf * xf, axis=-1, keepdims=True)
        h = xf * lax.rsqrt(ms + EPS) * g * sc + sh
        hb_ref[pl.ds(row0, NORM_ROWS), :] = h.astype(BF16)
        _store_chunked(hc_ref, row0, h)
        return carry

    lax.fori_loop(0, tm // NORM_ROWS, body, 0)
    lg = jnp.dot(hb_ref[...], wr_ref[...], preferred_element_type=F32) + br_ref[...]
    id_ref[...], gt_ref[...] = _route_rows(lg)


def _norm2_router(x2d, g_norm, mod3, tiles_per_batch, w_router, b_router, tm):
    t, d = x2d.shape
    return pl.pallas_call(
        functools.partial(_norm2_kernel, tm=tm),
        grid=(t // tm,),
        in_specs=[pl.BlockSpec((tm, d), lambda i: (i, 0)),
                  pl.BlockSpec((1, d), lambda i: (0, 0)),
                  pl.BlockSpec((1, 1, d), lambda i: ((i // tiles_per_batch) * 6 + 3, 0, 0)),
                  pl.BlockSpec((1, 1, d), lambda i: ((i // tiles_per_batch) * 6 + 4, 0, 0)),
                  pl.BlockSpec((d, N_ROUTER), lambda i: (0, 0)),
                  pl.BlockSpec((1, N_ROUTER), lambda i: (0, 0))],
        out_specs=[pl.BlockSpec((tm * ROW_PITCH, LANES), lambda i: (i, 0)),
                   pl.BlockSpec((tm, N_ROUTER), lambda i: (i, 0)),
                   pl.BlockSpec((tm, N_ROUTER), lambda i: (i, 0))],
        out_shape=[jax.ShapeDtypeStruct((t * ROW_PITCH, LANES), U32),
                   jax.ShapeDtypeStruct((t, N_ROUTER), jnp.int32),
                   jax.ShapeDtypeStruct((t, N_ROUTER), F32)],
        scratch_shapes=[pltpu.VMEM((tm, d), BF16)],
        compiler_params=_cparams(("arbitrary",)),
        name="norm2_router",
    )(x2d, g_norm.reshape(1, d), mod3, mod3, w_router, b_router)


CAST_ROWS = 128
UP_WEIGHT_DMA_PRIORITY = 0
DOWN_WEIGHT_DMA_PRIORITY = 1


def _cast_rows(src_ref, dst_ref):
    n = src_ref.shape[0]

    def body(r, carry):
        rows = pl.ds(pl.multiple_of(r * CAST_ROWS, CAST_ROWS), CAST_ROWS)
        dst_ref[rows, :] = src_ref[rows, :].astype(dst_ref.dtype)
        return carry

    lax.fori_loop(0, n // CAST_ROWS, body, 0)


def _expert_weights(i, nu, sched, w_hbms, w_bufs, w_bf16s, sem, priority):
    be_ref, sl_ref, nx_ref, nv_ref = sched

    def copies(e, slot):
        return [pltpu.make_async_copy(w.at[e], buf.at[slot], sem.at[slot, k])
                for k, (w, buf) in enumerate(zip(w_hbms, w_bufs))]

    @pl.when(i == 0)
    def _():
        for cp in copies(be_ref[0], 0):
            cp.start(priority=priority)

    first_block = (i == 0) | (be_ref[i] != be_ref[jnp.maximum(i - 1, 0)])

    @pl.when(first_block & (i < nu))
    def _():
        slot = sl_ref[i]
        for cp in copies(be_ref[i], slot):
            cp.wait()
        for buf, dst in zip(w_bufs, w_bf16s):
            _cast_rows(buf.at[slot], dst)

        @pl.when(nv_ref[i] == 1)
        def _():
            for cp in copies(nx_ref[i], 1 - slot):
                cp.start(priority=priority)


def _moe_up_kernel(be_ref, nu_ref, sl_ref, nx_ref, nv_ref, tk_ref, bf_ref, br_ref, h_hbm, wg_hbm, wu_hbm, o_ref,
                   wg_buf, wu_buf, wgb_ref, wub_ref, xg_ref, xb_ref, sem, wsem):
    i = pl.program_id(0)
    nu = nu_ref[0]
    pad_entry = tk_ref.shape[0] - 8

    def start_gather(blk, slot):
        first, rows = bf_ref[blk], br_ref[blk]
        token_of = lambda r: tk_ref[jnp.where(r < rows, first + r, pad_entry)]
        _gather_rows(h_hbm, xg_ref.at[slot], token_of, MOE_BM, sem.at[slot])

    @pl.when(i == 0)
    def _():
        start_gather(0, 0)

    @pl.when(i + 1 < nu)
    def _():
        start_gather(i + 1, (i + 1) % 2)

    _expert_weights(i, nu, (be_ref, sl_ref, nx_ref, nv_ref), (wg_hbm, wu_hbm), (wg_buf, wu_buf), (wgb_ref, wub_ref),
                    wsem, UP_WEIGHT_DMA_PRIORITY)

    def unpack_rows(slot):
        _wait_rows(h_hbm, xg_ref.at[slot], MOE_BM, sem.at[slot])
        for j in range(ROW_CH):
            lo, hi = _load_chunk(xg_ref.at[slot], 0, MOE_BM, j)
            xb_ref[:, pl.ds(j * LANES, LANES)] = lo.astype(BF16)
            xb_ref[:, pl.ds(HALF_D + j * LANES, LANES)] = hi.astype(BF16)

    @pl.when(i < nu)
    def _():
        for slot in range(2):
            pl.when(i % 2 == slot)(functools.partial(unpack_rows, slot))
        xb = xb_ref[...]
        g = jnp.dot(xb, wgb_ref[...], preferred_element_type=F32)
        u = jnp.dot(xb, wub_ref[...], preferred_element_type=F32)
        o_ref[...] = (g * jax.nn.sigmoid(g) * u).astype(o_ref.dtype)

    @pl.when(i >= nu)
    def _():
        o_ref[...] = jnp.zeros_like(o_ref)


def _moe_down_kernel(be_ref, nu_ref, sl_ref, nx_ref, nv_ref, h_ref, wd_hbm, o_ref, wd_buf, wdb_ref, wsem):
    i = pl.program_id(0)
    _expert_weights(i, nu_ref[0], (be_ref, sl_ref, nx_ref, nv_ref), (wd_hbm,), (wd_buf,), (wdb_ref,), wsem,
                    DOWN_WEIGHT_DMA_PRIORITY)

    @pl.when(i < nu_ref[0])
    def _():
        _store_chunked(o_ref, 0, jnp.dot(h_ref[...], wdb_ref[...], preferred_element_type=F32))

    @pl.when(i >= nu_ref[0])
    def _():
        o_ref[...] = jnp.zeros_like(o_ref)


def _moe_experts(h_chunked, tok_sorted, blk_first, blk_rows, blk_expert, n_used, w_gate, w_up, w_down):
    n_blocks = blk_expert.shape[0]
    p = n_blocks * MOE_BM
    _, d, de = w_gate.shape
    run = jnp.cumsum(jnp.concatenate([jnp.zeros((1,), jnp.int32),
                                      (blk_expert[1:] != blk_expert[:-1]).astype(jnp.int32)]))
    run_end = jnp.searchsorted(blk_expert, blk_expert, side='right').astype(jnp.int32)
    sched = (blk_expert, n_used, run % 2, blk_expert[jnp.minimum(run_end, n_blocks - 1)],
             (run_end < n_used[0]).astype(jnp.int32))
    any_spec = pl.BlockSpec(memory_space=pl.ANY)
    hmid = pl.pallas_call(
        _moe_up_kernel,
        grid_spec=pltpu.PrefetchScalarGridSpec(
            num_scalar_prefetch=8,
            grid=(n_blocks,),
            in_specs=[any_spec, any_spec, any_spec],
            out_specs=pl.BlockSpec((MOE_BM, de), lambda i, *_: (i, 0)),
            scratch_shapes=[pltpu.VMEM((2, d, de), F32), pltpu.VMEM((2, d, de), F32),
                            pltpu.VMEM((d, de), BF16), pltpu.VMEM((d, de), BF16),
                            pltpu.VMEM((2, MOE_BM * ROW_PITCH, LANES), U32), pltpu.VMEM((MOE_BM, d), BF16),
                            pltpu.SemaphoreType.DMA((2,)), pltpu.SemaphoreType.DMA((2, 2))]),
        out_shape=jax.ShapeDtypeStruct((p, de), BF16),
        compiler_params=_cparams(("arbitrary",)),
        name="moe_gate_up",
    )(*sched, tok_sorted, blk_first, blk_rows, h_chunked, w_gate, w_up)
    return pl.pallas_call(
        _moe_down_kernel,
        grid_spec=pltpu.PrefetchScalarGridSpec(
            num_scalar_prefetch=5,
            grid=(n_blocks,),
            in_specs=[pl.BlockSpec((MOE_BM, de), lambda i, *_: (i, 0)), any_spec],
            out_specs=pl.BlockSpec((MOE_BM * ROW_PITCH, LANES), lambda i, *_: (i, 0)),
            scratch_shapes=[pltpu.VMEM((2, de, d), F32), pltpu.VMEM((de, d), BF16),
                            pltpu.SemaphoreType.DMA((2, 1))]),
        out_shape=jax.ShapeDtypeStruct((p * ROW_PITCH, LANES), U32),
        compiler_params=_cparams(("arbitrary",)),
        name="moe_down",
    )(*sched, hmid, w_down)


COMBINE_TM = 256


def _combine_kernel(so_ref, y_hbm, x_ref, gt_ref, g2_ref, o_ref, yb_ref, sem, *, tm):
    i = pl.program_id(0)
    n = pl.num_programs(0)

    def start_gather(blk, slot):
        for k in range(TOP_K):
            _gather_rows(y_hbm, yb_ref.at[slot], lambda r: so_ref[(blk * tm + r) * TOP_K + k], tm, sem.at[slot],
                         dst_row0=k * tm)

    @pl.when(i == 0)
    def _():
        start_gather(0, 0)

    @pl.when(i + 1 < n)
    def _():
        start_gather(i + 1, (i + 1) % 2)

    def combine(slot):
        _wait_rows(y_hbm, yb_ref.at[slot], TOP_K * tm, sem.at[slot])
        g0 = gt_ref[:, 0:1]
        g1 = gt_ref[:, 1:2]
        for j in range(ROW_CH):
            lo0, hi0 = _load_chunk(yb_ref.at[slot], 0, tm, j)
            lo1, hi1 = _load_chunk(yb_ref.at[slot], tm, tm, j)
            for off, y in ((j * LANES, g0 * lo0 + g1 * lo1), (HALF_D + j * LANES, g0 * hi0 + g1 * hi1)):
                o_ref[:, pl.ds(off, LANES)] = x_ref[:, pl.ds(off, LANES)] + g2_ref[0][:, off:off + LANES] * y

    for slot in range(2):
        pl.when(i % 2 == slot)(functools.partial(combine, slot))


def _moe_combine(y_chunked, slot_of, x2d, gates, mod3, tiles_per_batch):
    t, d = x2d.shape
    tm = COMBINE_TM
    return pl.pallas_call(
        functools.partial(_combine_kernel, tm=tm),
        grid_spec=pltpu.PrefetchScalarGridSpec(
            num_scalar_prefetch=1,
            grid=(t // tm,),
            in_specs=[pl.BlockSpec(memory_space=pl.ANY),
                      pl.BlockSpec((tm, d), lambda i, so: (i, 0)),
                      pl.BlockSpec((tm, N_ROUTER), lambda i, so: (i, 0)),
                      pl.BlockSpec((1, 1, d), lambda i, so: ((i // tiles_per_batch) * 6 + 5, 0, 0))],
            out_specs=pl.BlockSpec((tm, d), lambda i, so: (i, 0)),
            scratch_shapes=[pltpu.VMEM((2, TOP_K * tm * ROW_PITCH, LANES), U32),
                            pltpu.SemaphoreType.DMA((2,))]),
        out_shape=jax.ShapeDtypeStruct((t, d), F32),
        compiler_params=_cparams(("arbitrary",)),
        name="moe_combine",
    )(slot_of, y_chunked, x2d, gates, mod3)


def _slots(eid, t):
    a = t * TOP_K
    iota = jnp.arange(a, dtype=jnp.int32)
    e_sorted, order = lax.sort((eid.reshape(a), iota), num_keys=1)
    experts = jnp.arange(N_EXPERTS, dtype=jnp.int32)
    start = jnp.searchsorted(e_sorted, experts, side='left').astype(jnp.int32)
    counts = jnp.searchsorted(e_sorted, experts, side='right').astype(jnp.int32) - start
    padded = (counts + MOE_BM - 1) // MOE_BM * MOE_BM
    pend = jnp.cumsum(padded)
    pstart = pend - padded
    dest = pstart[e_sorted] + iota - start[e_sorted]
    slot_of = lax.sort((order, dest), num_keys=1)[1]
    n_blocks = (a + MOE_BM - 1) // MOE_BM + N_EXPERTS
    blk = jnp.arange(n_blocks, dtype=jnp.int32) * MOE_BM
    blk_expert = jnp.minimum(jnp.searchsorted(pend, blk, side='right'), N_EXPERTS - 1).astype(jnp.int32)
    off = blk - pstart[blk_expert]
    blk_first = start[blk_expert] + off
    blk_rows = jnp.clip(counts[blk_expert] - off, 0, MOE_BM)
    tok_sorted = jnp.concatenate([order // TOP_K, jnp.zeros((8,), jnp.int32)])
    n_used = (pend[-1] // MOE_BM).astype(jnp.int32).reshape(1)
    return tok_sorted, blk_first, blk_rows, slot_of, blk_expert, n_used


GDN_TB = 256
HALO = 8
PREP_SLABS = 2
N_QKV = 3 * GDN_WIDTH


def _softplus(v):
    return jnp.maximum(v, 0.0) + jnp.log(1.0 + jnp.exp(-jnp.abs(v)))


def _split3_bf16(v):
    hi = v.astype(BF16)
    r1 = v - hi.astype(F32)
    mid = r1.astype(BF16)
    lo = (r1 - mid.astype(F32)).astype(BF16)
    return hi, mid, lo


def _gdn_prep_kernel(cur_ref, prev_ref, next_ref, sm_ref, cw_ref, ea_ref, dt_ref, o_ref, g_ref, ext_ref, *, tb):
    i = pl.program_id(1)
    nblk = pl.num_programs(1)
    def conv_head(kind, hh, slab):
        cols = pl.ds(pl.multiple_of((kind * GDN_HEADS + hh) * HEAD_DIM, HEAD_DIM), HEAD_DIM)
        cw = cw_ref[:, cols]
        ext = ext_ref.at[slab]
        ext[pl.ds(0, HALO), :] = jnp.where(i > 0, prev_ref[:, cols], 0.0)
        ext[pl.ds(HALO, tb), :] = cur_ref[:, cols]
        ext[pl.ds(HALO + tb, HALO), :] = jnp.where(i < nblk - 1, next_ref[:, cols], 0.0)
        for r0 in range(0, tb, 64):
            acc = None
            for s in range(CONV_K):
                term = ext[pl.ds(HALO - CONV_K // 2 + s + r0, 64), :] * cw[s:s + 1, :]
                acc = term if acc is None else acc + term
            y = acc * jax.nn.sigmoid(acc)
            if kind < 2:
                y = y * lax.rsqrt(jnp.sum(y * y, axis=-1, keepdims=True) + EPS)
            if kind == 0:
                y = y * HEAD_DIM ** -0.5
            o_ref[pl.ds(r0, 64), cols] = y.astype(o_ref.dtype)

    def conv_cols(kind):
        def body(it, carry):
            for slab in range(PREP_SLABS):
                conv_head(kind, it * PREP_SLABS + slab, slab)
            return carry

        lax.fori_loop(0, GDN_HEADS // PREP_SLABS, body, 0)

    conv_cols(0)
    conv_cols(1)
    conv_cols(2)

    s = sm_ref[...]
    beta = jax.nn.sigmoid(s)
    g = -ea_ref[...] * _softplus(s + dt_ref[...])
    r = lax.broadcasted_iota(jnp.int32, (tb, tb), 0)
    c = lax.broadcasted_iota(jnp.int32, (tb, tb), 1)
    same = (r // GDN_CHUNK) == (c // GDN_CHUNK)
    lower = (same & (c <= r)).astype(BF16)
    upper = (same & (c >= r)).astype(BF16)
    parts = _split3_bf16(g)
    cf = sum(jnp.dot(lower, pt, preferred_element_type=F32) for pt in parts)
    cb = sum(jnp.dot(upper, pt, preferred_element_type=F32) for pt in parts)
    col = lax.broadcasted_iota(jnp.int32, s.shape, 1)
    nh = GDN_HEADS
    g_ref[...] = jnp.where(col < 2 * nh, beta, jnp.where(col < 3 * nh, cf, jnp.where(col < 4 * nh, cb, 0.0)))


def _gdn_prep(y_main, small, conv_w8, ea_row, dt_row, b_, l):
    tb = min(GDN_TB, l)
    nblk = l // tb
    hb = tb // HALO
    last = b_ * l // HALO - 1
    return pl.pallas_call(
        functools.partial(_gdn_prep_kernel, tb=tb),
        grid=(b_, nblk),
        in_specs=[pl.BlockSpec((tb, N_QKV), lambda b, i: (b * nblk + i, 0)),
                  pl.BlockSpec((HALO, N_QKV), lambda b, i: (jnp.maximum((b * nblk + i) * hb - 1, 0), 0)),
                  pl.BlockSpec((HALO, N_QKV), lambda b, i: (jnp.minimum((b * nblk + i + 1) * hb, last), 0)),
                  pl.BlockSpec((tb, N_SMALL), lambda b, i: (b * nblk + i, 0)),
                  pl.BlockSpec((HALO, N_QKV), lambda b, i: (0, 0)),
                  pl.BlockSpec((1, N_SMALL), lambda b, i: (0, 0)),
                  pl.BlockSpec((1, N_SMALL), lambda b, i: (0, 0))],
        out_specs=[pl.BlockSpec((tb, N_QKV), lambda b, i: (b * nblk + i, 0)),
                   pl.BlockSpec((tb, N_SMALL), lambda b, i: (b * nblk + i, 0))],
        out_shape=[jax.ShapeDtypeStruct((b_ * l, N_QKV), BF16),
                   jax.ShapeDtypeStruct((b_ * l, N_SMALL), F32)],
        scratch_shapes=[pltpu.VMEM((PREP_SLABS, tb + 2 * HALO, HEAD_DIM), F32)],
        compiler_params=_cparams(("arbitrary", "arbitrary")),
        name="gdn_prep",
    )(y_main, y_main, y_main, small, conv_w8, ea_row, dt_row)


N_PAIRS = GDN_HEADS // 2
PK = 2 * GDN_CHUNK
INV_LEVELS = (2, 4, 8, 16, 32, 64)


def _gdn_masks(reverse):
    i = np.arange(PK)[:, None]
    j = np.arange(PK)[None, :]
    same = (i // GDN_CHUNK) == (j // GDN_CHUNK)
    strict = same & ((j > i) if reverse else (j < i))
    out = []
    for bs in INV_LEVELS:
        out.append(strict & (i // bs == j // bs) & (i // (bs // 2) != j // (bs // 2)))
    incl = same & ((j >= i) if reverse else (j <= i))
    out.append(incl)
    m = np.stack(out).astype(np.float32)
    neg = ((incl.astype(np.float32) - 1.0) * 1e30)[None]
    return jnp.asarray(np.concatenate([m, neg], axis=0))


def _gdn_scan_kernel(qf_ref, kf_ref, vf_ref, qb_ref, kb_ref, vb_ref, cpf_ref, rpf_ref, apf_ref, cpb_ref, rpb_ref,
                     apb_ref, s0_ref, mk_ref, mkb_ref, of_ref, ob_ref, sfin_ref, s_scr):
    c = pl.program_id(1)

    @pl.when(c == 0)
    def _():
        s_scr[...] = s0_ref[0]

    nl = len(INV_LEVELS)
    ri = lax.broadcasted_iota(jnp.int32, (PK, 1), 0)
    top = (ri < GDN_CHUNK).astype(F32)
    bot = 1.0 - top
    rr = lax.broadcasted_iota(jnp.int32, (PK, PK), 0)
    cc = lax.broadcasted_iota(jnp.int32, (PK, PK), 1)
    eye = (rr == cc).astype(F32)
    dot = functools.partial(jnp.dot, preferred_element_type=F32)
    nt = (((1,), (1,)), ((), ()))
    tn = (((0,), (0,)), ((), ()))

    q_refs, k_refs, v_refs, o_refs = (qf_ref, qb_ref), (kf_ref, kb_ref), (vf_ref, vb_ref), (of_ref, ob_ref)
    cps = (cpf_ref[0, 0], cpb_ref[0, 0])
    rps = (rpf_ref[0, 0], rpb_ref[0, 0])
    aps = (apf_ref[0, 0], apb_ref[0, 0])
    units = [(d, p) for p in range(N_PAIRS) for d in range(2)]
    per_unit = lambda f: [f(i, d, p) for i, (d, p) in enumerate(units)]
    cols = lambda p: (pl.ds(2 * p * HEAD_DIM, HEAD_DIM), pl.ds((2 * p + 1) * HEAD_DIM, HEAD_DIM))
    pack = lambda ref, p: jnp.concatenate([ref[:, cols(p)[0]], ref[:, cols(p)[1]]], axis=0)
    mask = lambda d, n: mk_ref[d, n]

    gcol = per_unit(lambda i, d, p: cps[d][:, p:p + 1])
    bcol = per_unit(lambda i, d, p: cps[d][:, N_PAIRS + p:N_PAIRS + p + 1])
    glcol = per_unit(lambda i, d, p: cps[d][:, 2 * N_PAIRS + p:2 * N_PAIRS + p + 1])
    kp = per_unit(lambda i, d, p: pack(k_refs[d], p))
    qp = per_unit(lambda i, d, p: pack(q_refs[d], p))
    kk = per_unit(lambda i, d, p: lax.dot_general(kp[i], kp[i], nt, preferred_element_type=F32))
    qk = per_unit(lambda i, d, p: lax.dot_general(qp[i], kp[i], nt, preferred_element_type=F32))
    dec = per_unit(lambda i, d, p: jnp.exp((gcol[i] - rps[d][p:p + 1, :]) * mask(d, nl) + mask(d, nl + 1)))
    a = per_unit(lambda i, d, p: (kk[i] * dec[i] * bcol[i]).astype(BF16))
    qkm = per_unit(lambda i, d, p: (qk[i] * dec[i]).astype(BF16))

    x = per_unit(lambda i, d, p: eye - (a[i] * mkb_ref[d, 0]).astype(F32))
    for lv in range(1, nl):
        xb = per_unit(lambda i, d, p: x[i].astype(BF16))
        po = per_unit(lambda i, d, p: dot(xb[i], a[i] * mkb_ref[d, lv]))
        x = per_unit(lambda i, d, p: x[i] - dot(po[i].astype(BF16), xb[i]))
    tb = per_unit(lambda i, d, p: x[i].astype(BF16))

    egc = per_unit(lambda i, d, p: jnp.exp(gcol[i]))
    kf = per_unit(lambda i, d, p: kp[i].astype(F32))
    u = per_unit(lambda i, d, p: dot(tb[i], (pack(v_refs[d], p).astype(F32) * bcol[i]).astype(BF16)))
    w = per_unit(lambda i, d, p: dot(tb[i], (kf[i] * (bcol[i] * egc[i])).astype(BF16)))
    qd = per_unit(lambda i, d, p: qp[i].astype(F32) * egc[i])
    kd = per_unit(lambda i, d, p: kf[i] * jnp.exp(glcol[i] - gcol[i]))

    s = per_unit(lambda i, d, p: s_scr[d, p])
    lhs = per_unit(lambda i, d, p: jnp.concatenate(
        [jnp.concatenate([w[i] * top, w[i] * bot], axis=1),
         jnp.concatenate([qd[i] * top, qd[i] * bot], axis=1)], axis=0).astype(BF16))
    ws = per_unit(lambda i, d, p: dot(lhs[i], s[i].astype(BF16)))
    vnb = per_unit(lambda i, d, p: (u[i] - ws[i][:PK]).astype(BF16))
    o = per_unit(lambda i, d, p: ws[i][PK:] + dot(qkm[i], vnb[i]))
    kbd = per_unit(lambda i, d, p: jnp.concatenate([kd[i] * top, kd[i] * bot], axis=1).astype(BF16))
    kv = per_unit(lambda i, d, p: lax.dot_general(kbd[i], vnb[i], tn, preferred_element_type=F32))
    for i, (d, p) in enumerate(units):
        s_scr[d, p] = jnp.exp(aps[d][:, p:p + 1]) * s[i] + kv[i]
    for i, (d, p) in enumerate(units):
        o_refs[d][:, cols(p)[0]] = o[i][:GDN_CHUNK].astype(o_refs[d].dtype)
        o_refs[d][:, cols(p)[1]] = o[i][GDN_CHUNK:].astype(o_refs[d].dtype)

    @pl.when(c == pl.num_programs(1) - 1)
    def _():
        sfin_ref[0] = s_scr[...]


def _gdn_out_kernel(of_ref, ob_ref, z_ref, gon_ref, o_ref):
    for h in range(GDN_HEADS):
        c = pl.ds(h * HEAD_DIM, HEAD_DIM)
        o = of_ref[:, c].astype(F32) + ob_ref[:, c].astype(F32)
        z = z_ref[:, c]
        y = o * lax.rsqrt(jnp.mean(o * o, axis=-1, keepdims=True) + EPS) * gon_ref[...]
        o_ref[:, c] = (y * (z * jax.nn.sigmoid(z))).astype(o_ref.dtype)


def _gdn_out(o_f, o_b, y_main, g_on, tm):
    t = o_f.shape[0]
    blk = lambda col: pl.BlockSpec((tm, GDN_WIDTH), lambda i: (i, col))
    return pl.pallas_call(
        _gdn_out_kernel,
        grid=(t // tm,),
        in_specs=[blk(0), blk(0), blk(N_QKV // GDN_WIDTH), pl.BlockSpec((1, HEAD_DIM), lambda i: (0, 0))],
        out_specs=blk(0),
        out_shape=jax.ShapeDtypeStruct((t, GDN_WIDTH), BF16),
        compiler_params=_cparams(("arbitrary",)),
        name="gdn_out",
    )(o_f, o_b, y_main, g_on.reshape(1, HEAD_DIM))


def _gdn_packs(gates, b_, l, reverse):
    nc = l // GDN_CHUNK
    nh = GDN_HEADS
    g4 = gates.reshape(b_, nc, GDN_CHUNK, N_SMALL)
    d = 1 if reverse else 0
    beta = g4[..., d * nh:(d + 1) * nh]
    gc = g4[..., (2 + d) * nh:(3 + d) * nh]
    gl = jnp.broadcast_to(gc[:, :, 0:1] if reverse else gc[:, :, GDN_CHUNK - 1:GDN_CHUNK], gc.shape)

    def rowpack(t):
        return jnp.transpose(t.reshape(b_, nc, GDN_CHUNK, N_PAIRS, 2), (0, 1, 3, 4, 2)).reshape(b_, nc, N_PAIRS, PK)

    rp = rowpack(gc)
    cp = jnp.concatenate([jnp.swapaxes(rowpack(t), 2, 3) for t in (gc, beta, gl)]
                         + [jnp.zeros((b_, nc, PK, N_PAIRS), F32)], axis=-1)
    glh = gl[:, :, 0].reshape(b_, nc, N_PAIRS, 2)
    ap = jnp.swapaxes(jnp.repeat(glh, HEAD_DIM, axis=-1), 2, 3)
    return cp, rp, ap


def _gdn_scan(qkv, gates, s0, b_, l):
    nc = l // GDN_CHUNK
    packs = [_gdn_packs(gates, b_, l, reverse) for reverse in (False, True)]
    masks = jnp.stack([_gdn_masks(False), _gdn_masks(True)])
    level_masks = masks[:, :len(INV_LEVELS)].astype(BF16)
    chunk_of = (lambda c: c, lambda c: nc - 1 - c)
    tok = lambda d, col: pl.BlockSpec((GDN_CHUNK, GDN_WIDTH), lambda b, c: (b * nc + chunk_of[d](c), col))
    per_chunk = lambda d, shp: pl.BlockSpec((1, 1) + shp, lambda b, c: (b, chunk_of[d](c), 0, 0))
    gate_specs = lambda d: [per_chunk(d, (PK, 4 * N_PAIRS)), per_chunk(d, (N_PAIRS, PK)),
                            per_chunk(d, (2 * HEAD_DIM, N_PAIRS))]
    state = pl.BlockSpec((1, 2, N_PAIRS, 2 * HEAD_DIM, HEAD_DIM), lambda b, c: (b, 0, 0, 0, 0))
    return pl.pallas_call(
        _gdn_scan_kernel,
        grid=(b_, nc),
        in_specs=[tok(0, 0), tok(0, 1), tok(0, 2), tok(1, 0), tok(1, 1), tok(1, 2)] + gate_specs(0) + gate_specs(1)
        + [state, pl.BlockSpec(masks.shape, lambda b, c: (0, 0, 0, 0)),
           pl.BlockSpec(level_masks.shape, lambda b, c: (0, 0, 0, 0))],
        out_specs=[tok(0, 0), tok(1, 0), state],
        out_shape=[jax.ShapeDtypeStruct((b_ * l, GDN_WIDTH), BF16),
                   jax.ShapeDtypeStruct((b_ * l, GDN_WIDTH), BF16),
                   jax.ShapeDtypeStruct(s0.shape, F32)],
        scratch_shapes=[pltpu.VMEM((2, N_PAIRS, 2 * HEAD_DIM, HEAD_DIM), F32)],
        compiler_params=_cparams(("arbitrary", "arbitrary")),
        name="gdn_scan",
    )(qkv, qkv, qkv, qkv, qkv, qkv, *packs[0], *packs[1], s0, masks, level_masks)


def _gdn_mixer(yx, sx, yc, sc, conv_w, a_log, dt_bias, g_on, b_, l, n_ctx):
    nh = GDN_HEADS
    conv_w8 = jnp.pad(conv_w, ((0, HALO - CONV_K), (0, 0)))
    ea_row = jnp.zeros((1, N_SMALL), F32).at[0, 2 * nh:4 * nh].set(jnp.exp(a_log.reshape(-1)))
    dt_row = jnp.zeros((1, N_SMALL), F32).at[0, 2 * nh:4 * nh].set(dt_bias.reshape(-1))
    qkv_c, gates_c = _gdn_prep(yc, sc, conv_w8, ea_row, dt_row, b_, n_ctx)
    qkv_x, gates_x = _gdn_prep(yx, sx, conv_w8, ea_row, dt_row, b_, l)
    zero = jnp.zeros((b_, 2, N_PAIRS, 2 * HEAD_DIM, HEAD_DIM), F32)
    _, _, s_ctx = _gdn_scan(qkv_c, gates_c, zero, b_, n_ctx)
    o_f, o_b, _ = _gdn_scan(qkv_x, gates_x, s_ctx, b_, l)
    return _gdn_out(o_f, o_b, yx, g_on, 256)


SWA_GROUP = SWA_HEADS // SWA_KV_HEADS
ROT = HEAD_DIM // 4


def _rope_tables(l):
    half = HEAD_DIM // 2
    inv = ROPE_BASE ** (-jnp.arange(0, half, 2, dtype=F32) / half)
    pos = jnp.arange(l, dtype=jnp.int32)
    ang_r = (pos // GRID_W).astype(F32)[:, None] * inv
    ang_c = (pos % GRID_W).astype(F32)[:, None] * inv
    zero = jnp.zeros_like(ang_r)
    cos = jnp.concatenate([jnp.cos(ang_r), jnp.cos(ang_r), jnp.cos(ang_c), jnp.cos(ang_c)], axis=1)
    sin_up = jnp.concatenate([-jnp.sin(ang_r), zero, -jnp.sin(ang_c), zero], axis=1)
    sin_dn = jnp.concatenate([zero, jnp.sin(ang_r), zero, jnp.sin(ang_c)], axis=1)
    return cos, sin_up, sin_dn


def _swa_prep_kernel(q_ref, k_ref, v_ref, cos_ref, su_ref, sd_ref, gq_ref, gk_ref, qo_ref, ko_ref, vo_ref):
    cos, su, sd = cos_ref[...], su_ref[...], sd_ref[...]

    def norm_rope(t, g, scale):
        y = t * lax.rsqrt(jnp.mean(t * t, axis=-1, keepdims=True) + EPS) * g
        y = y * cos + pltpu.roll(y, HEAD_DIM - ROT, 1) * su + pltpu.roll(y, ROT, 1) * sd
        return y * scale if scale != 1.0 else y

    for h in range(SWA_HEADS):
        c = pl.ds(h * HEAD_DIM, HEAD_DIM)
        qo_ref[:, c] = norm_rope(q_ref[:, c], gq_ref[...], HEAD_DIM ** -0.5).astype(qo_ref.dtype)
    for h in range(SWA_KV_HEADS):
        c = pl.ds(h * HEAD_DIM, HEAD_DIM)
        ko_ref[:, c] = norm_rope(k_ref[:, c], gk_ref[...], 1.0).astype(ko_ref.dtype)
    vo_ref[...] = v_ref[...].astype(vo_ref.dtype)


def _swa_prep(y_main, tables, g_q, g_k, rows, tm):
    q_blk = (N_QKV + GDN_WIDTH) // SWA_WIDTH
    k_blk = (N_QKV + GDN_WIDTH + SWA_WIDTH) // SWA_KV_WIDTH
    tpb = tables[0].shape[0] // tm
    tab = pl.BlockSpec((tm, HEAD_DIM), lambda i: (i % tpb, 0))
    vec = pl.BlockSpec((1, HEAD_DIM), lambda i: (0, 0))
    return pl.pallas_call(
        _swa_prep_kernel,
        grid=(rows // tm,),
        in_specs=[pl.BlockSpec((tm, SWA_WIDTH), lambda i: (i, q_blk)),
                  pl.BlockSpec((tm, SWA_KV_WIDTH), lambda i: (i, k_blk)),
                  pl.BlockSpec((tm, SWA_KV_WIDTH), lambda i: (i, k_blk + 1)),
                  tab, tab, tab, vec, vec],
        out_specs=[pl.BlockSpec((tm, SWA_WIDTH), lambda i: (i, 0)),
                   pl.BlockSpec((tm, SWA_KV_WIDTH), lambda i: (i, 0)),
                   pl.BlockSpec((tm, SWA_KV_WIDTH), lambda i: (i, 0))],
        out_shape=[jax.ShapeDtypeStruct((rows, SWA_WIDTH), BF16),
                   jax.ShapeDtypeStruct((rows, SWA_KV_WIDTH), BF16),
                   jax.ShapeDtypeStruct((rows, SWA_KV_WIDTH), BF16)],
        compiler_params=_cparams(("arbitrary",)),
        name="swa_prep",
    )(y_main, y_main, y_main, *tables, g_q.reshape(1, HEAD_DIM), g_k.reshape(1, HEAD_DIM))


def _swa_attn_kernel(q_ref, kp_ref, kc_ref, kn_ref, vp_ref, vc_ref, vn_ref, kx_ref, vx_ref, sink_ref, o_ref, *, n_ctx):
    n = pl.program_id(1)
    nb = pl.num_programs(1)
    rows = SWA_GROUP * Q_BLOCK
    nk = n_ctx + 3 * Q_BLOCK
    qi = lax.broadcasted_iota(jnp.int32, (rows, nk), 0) & (Q_BLOCK - 1)
    kj = lax.broadcasted_iota(jnp.int32, (rows, nk), 1) - n_ctx
    lo = jnp.where(n == 0, Q_BLOCK, 0)
    hi = jnp.where(n == nb - 1, 2 * Q_BLOCK, 3 * Q_BLOCK)
    valid = (kj < 0) | ((kj >= qi) & (kj <= qi + 2 * WINDOW) & (kj >= lo) & (kj < hi))
    hsel = lax.broadcasted_iota(jnp.int32, (rows, 1), 0) // Q_BLOCK
    nt = (((1,), (1,)), ((), ()))
    kv_heads = range(SWA_KV_HEADS)
    col = lambda j: pl.ds(j * HEAD_DIM, HEAD_DIM)
    heads = [[j * SWA_GROUP + g for g in range(SWA_GROUP)] for j in kv_heads]
    s = []
    for j in kv_heads:
        q = jnp.concatenate([q_ref[:, col(h)] for h in heads[j]], axis=0)
        k = jnp.concatenate([kx_ref[:, col(j)], kp_ref[:, col(j)], kc_ref[:, col(j)], kn_ref[:, col(j)]], axis=0)
        s.append(jnp.where(valid, lax.dot_general(q, k, nt, preferred_element_type=F32), NEG_INF))
    p, den = [], []
    for j in kv_heads:
        sink = jnp.zeros((rows, 1), F32)
        for g, h in enumerate(heads[j]):
            sink = jnp.where(hsel == g, sink_ref[h:h + 1, 0:1], sink)
        m = jnp.maximum(jnp.max(s[j], axis=-1, keepdims=True), sink)
        e = jnp.exp(s[j] - m)
        den.append(jnp.sum(e, axis=-1, keepdims=True) + jnp.exp(sink - m))
        p.append(e.astype(BF16))
    for j in kv_heads:
        v = jnp.concatenate([vx_ref[:, col(j)], vp_ref[:, col(j)], vc_ref[:, col(j)], vn_ref[:, col(j)]], axis=0)
        o = jnp.dot(p[j], v, preferred_element_type=F32) / den[j]
        for g, h in enumerate(heads[j]):
            o_ref[:, col(h)] = o[g * Q_BLOCK:(g + 1) * Q_BLOCK].astype(o_ref.dtype)


def _swa_attention(q, k, v, k_ctx, v_ctx, sink, b_, l, n_ctx):
    nb = l // Q_BLOCK
    blk = lambda w, off: pl.BlockSpec(
        (Q_BLOCK, w), lambda b, n: (b * nb + jnp.clip(n + off, 0, nb - 1), 0))
    ctx = pl.BlockSpec((n_ctx, SWA_KV_WIDTH), lambda b, n: (b, 0))
    kw = SWA_KV_WIDTH
    return pl.pallas_call(
        functools.partial(_swa_attn_kernel, n_ctx=n_ctx),
        grid=(b_, nb),
        in_specs=[blk(SWA_WIDTH, 0), blk(kw, -1), blk(kw, 0), blk(kw, 1), blk(kw, -1), blk(kw, 0), blk(kw, 1),
                  ctx, ctx, pl.BlockSpec((SWA_HEADS, HEAD_DIM), lambda b, n: (0, 0))],
        out_specs=blk(SWA_WIDTH, 0),
        out_shape=jax.ShapeDtypeStruct((b_ * l, SWA_WIDTH), BF16),
        compiler_params=_cparams(("arbitrary", "arbitrary")),
        name="swa_attention",
    )(q, k, k, k, v, v, v, k_ctx, v_ctx, jnp.broadcast_to(sink.astype(F32)[:, None], (SWA_HEADS, HEAD_DIM)))


def _swa_mixer(yx, yc, g_q, g_k, sink, b_, l, n_ctx):
    ones = jnp.ones((n_ctx, HEAD_DIM), F32)
    zeros = jnp.zeros((n_ctx, HEAD_DIM), F32)
    qx, kx, vx = _swa_prep(yx, _rope_tables(l), g_q, g_k, b_ * l, 256)
    _, kc, vc = _swa_prep(yc, (ones, zeros, zeros), g_q, g_k, b_ * n_ctx, n_ctx)
    return _swa_attention(qx, kx, vx, kc, vc, sink, b_, l, n_ctx)


def kernel(x, c, ctx, c_ctx, w_ada, b_ada, g_norm1, g_norm2, w_in, conv_qkv, a_log, dt_bias, g_onorm, g_qnorm,
           g_knorm, sink, w_out, w_router_grp, b_router_grp, w_router_exp, b_router_exp, w_gate, w_up, w_down):
    b_, l, d = x.shape
    n_ctx = ctx.shape[1]
    t = b_ * l
    assert w_ada.shape[0] == 1 and d == D_MODEL and b_ + 1 <= MOD_ROWS

    wi_t = jnp.swapaxes(w_in[0], 0, 1)
    w_main = jnp.concatenate([wi_t[IN_OFFS[0]:IN_OFFS[2]], wi_t[IN_OFFS[4]:IN_OFFS[7]]], axis=0).astype(BF16)
    w_small = jnp.pad(wi_t[IN_OFFS[2]:IN_OFFS[4]], ((0, N_SMALL - 4 * GDN_HEADS), (0, 0))).astype(BF16)
    wo = w_out[0].astype(BF16)
    w_router = jnp.pad(jnp.concatenate([w_router_grp[0], w_router_exp[0]], axis=1),
                       ((0, 0), (0, N_ROUTER - N_GROUPS - N_EXPERTS))).astype(BF16)
    b_router = jnp.pad(jnp.concatenate([b_router_grp[0], b_router_exp[0]]),
                       (0, N_ROUTER - N_GROUPS - N_EXPERTS)).reshape(1, N_ROUTER)

    c_rows = jnp.zeros((MOD_ROWS, d), F32).at[:b_].set(c).at[b_].set(c_ctx)
    mod = _modulation(c_rows, w_ada[0], b_ada[0])
    mod3 = mod.reshape(MOD_ROWS * 6, 1, d)

    tm = 512
    tpb = l // tm
    yx, sx = _in_projection(x.reshape(t, d), g_norm1[0], mod3, lambda i: i // tpb, w_main, w_small, tm)
    yc, sc = _in_projection(ctx.reshape(b_ * n_ctx, d), g_norm1[0], mod3, lambda i: b_, w_main, w_small, n_ctx)
    ya_x = _gdn_mixer(yx, sx, yc, sc, conv_qkv[0], a_log[0], dt_bias[0], g_onorm[0], b_, l, n_ctx)
    yb_x = _swa_mixer(yx, yc, g_qnorm[0], g_knorm[0], sink[0], b_, l, n_ctx)

    x1 = _out_projection(ya_x, yb_x, wo, x.reshape(t, d), mod3, tpb, tm)

    tm2 = 256
    h2c, ids, gates = _norm2_router(x1, g_norm2[0], mod3, l // tm2, w_router, b_router, tm2)
    tok_sorted, blk_first, blk_rows, slot_of, blk_expert, n_used = _slots(ids[:, :TOP_K], t)
    yc_moe = _moe_experts(h2c, tok_sorted, blk_first, blk_rows, blk_expert, n_used, w_gate[0], w_up[0], w_down[0])
    return _moe_combine(yc_moe, slot_of, x1, gates, mod3, l // COMBINE_TM).reshape(b_, l, d)
```

```python
import functools
import math

import jax
import jax.numpy as jnp
import numpy as np
from jax import lax
from jax.experimental import pallas as pl
from jax.experimental.pallas import tpu as pltpu

F32 = jnp.float32
BF16 = jnp.bfloat16

D_MODEL = 4096
CTX_LEN = 256
GRID_W = 64
HEAD_DIM = 128
GDN_HEADS = 16
GDN_WIDTH = GDN_HEADS * HEAD_DIM
GDN_CHUNK = 64
CONV_K = 5
SWA_HEADS = 16
SWA_KV_HEADS = 4
SWA_WIDTH = SWA_HEADS * HEAD_DIM
SWA_KV_WIDTH = SWA_KV_HEADS * HEAD_DIM
WINDOW = 128
Q_BLOCK = 128
ROPE_BASE = 10000.0
N_GROUPS = 8
EXPERTS_PER_GROUP = 8
N_EXPERTS = N_GROUPS * EXPERTS_PER_GROUP
TOP_K = 2
D_EXPERT = D_MODEL // 8
EPS = 1e-6
NEG_INF = -1e30

IN_SIZES = (3 * GDN_WIDTH, GDN_WIDTH, 2 * GDN_HEADS, 2 * GDN_HEADS, SWA_WIDTH, SWA_KV_WIDTH, SWA_KV_WIDTH)
IN_OFFS = tuple(int(v) for v in np.cumsum((0,) + IN_SIZES))
N_MAIN = 3 * GDN_WIDTH + GDN_WIDTH + SWA_WIDTH + 2 * SWA_KV_WIDTH
N_SMALL = 128
N_ROUTER = 128
MOD_ROWS = 8

MOE_BM = 256
VMEM_LIMIT = 56 * 1024 * 1024


def _cparams(sem):
    return pltpu.CompilerParams(dimension_semantics=sem, vmem_limit_bytes=VMEM_LIMIT)


def _mod_kernel(c_ref, w_ref, b_ref, o_ref):
    c = c_ref[...]
    a = (c * jax.nn.sigmoid(c)).astype(BF16)
    o_ref[...] = jnp.dot(a, w_ref[...].astype(BF16), preferred_element_type=F32) + b_ref[...]


def _modulation(c_rows, w_ada, b_ada):
    d, n = w_ada.shape
    tn = 512
    return pl.pallas_call(
        _mod_kernel,
        grid=(n // tn,),
        in_specs=[pl.BlockSpec((MOD_ROWS, d), lambda j: (0, 0)),
                  pl.BlockSpec((d, tn), lambda j: (0, j)),
                  pl.BlockSpec((1, tn), lambda j: (0, j))],
        out_specs=pl.BlockSpec((MOD_ROWS, tn), lambda j: (0, j)),
        out_shape=jax.ShapeDtypeStruct((MOD_ROWS, n), F32),
        compiler_params=_cparams(("arbitrary",)),
        name="modulation",
    )(c_rows, w_ada, b_ada.reshape(1, n))


NORM_ROWS = 64


def _norm_mod_rows(x_ref, g_ref, sh_ref, sc_ref, h_ref, tm):
    g = g_ref[...]
    sc = 1.0 + sc_ref[0]
    sh = sh_ref[0]

    def body(r, carry):
        rows = pl.ds(pl.multiple_of(r * NORM_ROWS, NORM_ROWS), NORM_ROWS)
        xf = x_ref[rows, :]
        ms = jnp.mean(xf * xf, axis=-1, keepdims=True)
        y = xf * lax.rsqrt(ms + EPS) * g
        h_ref[rows, :] = (y * sc + sh).astype(h_ref.dtype)
        return carry

    lax.fori_loop(0, tm // NORM_ROWS, body, 0)


def _inproj_kernel(x_ref, g_ref, sh_ref, sc_ref, w_ref, ws_ref, o_ref, os_ref, h_ref, *, tm):
    nt = (((1,), (1,)), ((), ()))

    @pl.when(pl.program_id(1) == 0)
    def _():
        _norm_mod_rows(x_ref, g_ref, sh_ref, sc_ref, h_ref, tm)
        os_ref[...] = lax.dot_general(h_ref[...], ws_ref[...], nt, preferred_element_type=F32)

    o_ref[...] = lax.dot_general(h_ref[...], w_ref[...], nt, preferred_element_type=F32)


def _in_projection(x2d, g_norm, mod3, mod_row_of_tile, w_main, w_small, tm):
    t, d = x2d.shape
    tn = 1024
    return pl.pallas_call(
        functools.partial(_inproj_kernel, tm=tm),
        grid=(t // tm, N_MAIN // tn),
        in_specs=[pl.BlockSpec((tm, d), lambda i, j: (i, 0)),
                  pl.BlockSpec((1, d), lambda i, j: (0, 0)),
                  pl.BlockSpec((1, 1, d), lambda i, j: (mod_row_of_tile(i) * 6 + 0, 0, 0)),
                  pl.BlockSpec((1, 1, d), lambda i, j: (mod_row_of_tile(i) * 6 + 1, 0, 0)),
                  pl.BlockSpec((tn, d), lambda i, j: (j, 0)),
                  pl.BlockSpec((N_SMALL, d), lambda i, j: (0, 0))],
        out_specs=[pl.BlockSpec((tm, tn), lambda i, j: (i, j)),
                   pl.BlockSpec((tm, N_SMALL), lambda i, j: (i, 0))],
        out_shape=[jax.ShapeDtypeStruct((t, N_MAIN), F32),
                   jax.ShapeDtypeStruct((t, N_SMALL), F32)],
        scratch_shapes=[pltpu.VMEM((tm, d), BF16)],
        compiler_params=_cparams(("arbitrary", "arbitrary")),
        name="in_projection",
    )(x2d, g_norm.reshape(1, d), mod3, mod3, w_main, w_small)


def _outproj_kernel(ya_ref, yb_ref, wa_ref, wb_ref, x_ref, gate_ref, o_ref):
    acc = jnp.dot(ya_ref[...], wa_ref[...], preferred_element_type=F32)
    acc = acc + jnp.dot(yb_ref[...], wb_ref[...], preferred_element_type=F32)
    o_ref[...] = x_ref[...] + gate_ref[0] * acc


def _out_projection(ya, yb, w, x2d, mod3, tiles_per_batch, tm):
    t, d = x2d.shape
    tn = 1024
    nj = d // tn
    ka, kb = ya.shape[1], yb.shape[1]
    assert ka == kb and w.shape[0] == ka + kb
    return pl.pallas_call(
        _outproj_kernel,
        grid=(t // tm, nj),
        in_specs=[pl.BlockSpec((tm, ka), lambda i, j: (i, 0)),
                  pl.BlockSpec((tm, kb), lambda i, j: (i, 0)),
                  pl.BlockSpec((ka, tn), lambda i, j: (0, j)),
                  pl.BlockSpec((kb, tn), lambda i, j: (1, j)),
                  pl.BlockSpec((tm, tn), lambda i, j: (i, j)),
                  pl.BlockSpec((1, 1, tn), lambda i, j: (((i // tiles_per_batch) * 6 + 2) * nj + j, 0, 0))],
        out_specs=pl.BlockSpec((tm, tn), lambda i, j: (i, j)),
        out_shape=jax.ShapeDtypeStruct((t, d), F32),
        compiler_params=_cparams(("arbitrary", "arbitrary")),
        name="out_projection",
    )(ya, yb, w, w, x2d, mod3.reshape(-1, 1, tn))


def _route_rows(lg):
    col = lax.broadcasted_iota(jnp.int32, lg.shape, 1)
    first = lambda hit: jnp.min(jnp.where(hit, col, N_ROUTER), axis=-1, keepdims=True)
    gm = col < N_GROUPS
    mg = jnp.max(jnp.where(gm, lg, NEG_INF), axis=-1, keepdims=True)
    grp = first(gm & (lg == mg))
    p_grp = 1.0 / jnp.sum(jnp.where(gm, jnp.exp(lg - mg), 0.0), axis=-1, keepdims=True)
    lo = N_GROUPS + grp * EXPERTS_PER_GROUP
    em = (col >= lo) & (col < lo + EXPERTS_PER_GROUP)
    m1 = jnp.max(jnp.where(em, lg, NEG_INF), axis=-1, keepdims=True)
    i1 = first(em & (lg == m1))
    em2 = em & (col != i1)
    m2 = jnp.max(jnp.where(em2, lg, NEG_INF), axis=-1, keepdims=True)
    i2 = first(em2 & (lg == m2))
    e2 = jnp.exp(m2 - m1)
    g1 = p_grp / (1.0 + e2)
    ids = jnp.where(col == 0, i1 - N_GROUPS, jnp.where(col == 1, i2 - N_GROUPS, 0))
    gates = jnp.where(col == 0, g1, jnp.where(col == 1, g1 * e2, 0.0))
    return ids, gates


LANES = 128
HALF_D = D_MODEL // 2
ROW_CH = HALF_D // LANES
ROW_PITCH = 24
U32 = jnp.uint32
HI_MASK = 0xFFFF0000
DMA_UNROLL = 8


def _pack_pair(lo, hi):
    bits = lambda v: lax.bitcast_convert_type(v.astype(BF16).astype(F32), U32)
    return (bits(hi) & U32(HI_MASK)) | (bits(lo) >> 16)


def _unpack_pair(w):
    return lax.bitcast_convert_type(w << 16, F32), lax.bitcast_convert_type(w & U32(HI_MASK), F32)


def _store_chunked(dst_ref, row0, vals):
    n = vals.shape[0]
    for j in range(ROW_PITCH):
        if j < ROW_CH:
            piece = _pack_pair(vals[:, j * LANES:(j + 1) * LANES], vals[:, HALF_D + j * LANES:HALF_D + (j + 1) * LANES])
        else:
            piece = jnp.zeros((n, LANES), U32)
        dst_ref[pl.ds(row0 * ROW_PITCH + j, n, stride=ROW_PITCH), :] = piece


def _load_chunk(src_ref, row0, n, j):
    return _unpack_pair(src_ref[pl.ds(row0 * ROW_PITCH + j, n, stride=ROW_PITCH), :])


def _row_copy(src_ref, dst_ref, src_row, dst_row, sem):
    return pltpu.make_async_copy(src_ref.at[pl.ds(pl.multiple_of(src_row * ROW_PITCH, 8), ROW_CH), :],
                                 dst_ref.at[pl.ds(pl.multiple_of(dst_row * ROW_PITCH, 8), ROW_CH), :], sem)


def _gather_rows(src_hbm, dst_ref, src_row_of, n, sem, dst_row0=0):
    def body(r, carry):
        _row_copy(src_hbm, dst_ref, src_row_of(r), dst_row0 + r, sem).start()
        return carry

    lax.fori_loop(0, n, body, 0, unroll=DMA_UNROLL)


def _wait_rows(src_hbm, dst_ref, n, sem):
    def body(r, carry):
        _row_copy(src_hbm, dst_ref, 0, r, sem).wait()
        return carry

    lax.fori_loop(0, n, body, 0, unroll=DMA_UNROLL)


def _norm2_kernel(x_ref, g_ref, sh_ref, sc_ref, wr_ref, br_ref, hc_ref, id_ref, gt_ref, hb_ref, *, tm):
    g = g_ref[...]
    sc = 1.0 + sc_ref[0]
    sh = sh_ref[0]

    def body(r, carry):
        row0 = pl.multiple_of(r * NORM_ROWS, NORM_ROWS)
        xf = x_ref[pl.ds(row0, NORM_ROWS), :]
        ms = jnp.mean(xf * xf, axis=-1, keepdims=True)
        h = xf * lax.rsqrt(ms + EPS) * g * sc + sh
        hb_ref[pl.ds(row0, NORM_ROWS), :] = h.astype(BF16)
        _store_chunked(hc_ref, row0, h)
        return carry

    lax.fori_loop(0, tm // NORM_ROWS, body, 0)
    lg = jnp.dot(hb_ref[...], wr_ref[...], preferred_element_type=F32) + br_ref[...]
    id_ref[...], gt_ref[...] = _route_rows(lg)


def _norm2_router(x2d, g_norm, mod3, tiles_per_batch, w_router, b_router, tm):
    t, d = x2d.shape
    return pl.pallas_call(
        functools.partial(_norm2_kernel, tm=tm),
        grid=(t // tm,),
        in_specs=[pl.BlockSpec((tm, d), lambda i: (i, 0)),
                  pl.BlockSpec((1, d), lambda i: (0, 0)),
                  pl.BlockSpec((1, 1, d), lambda i: ((i // tiles_per_batch) * 6 + 3, 0, 0)),
                  pl.BlockSpec((1, 1, d), lambda i: ((i // tiles_per_batch) * 6 + 4, 0, 0)),
                  pl.BlockSpec((d, N_ROUTER), lambda i: (0, 0)),
                  pl.BlockSpec((1, N_ROUTER), lambda i: (0, 0))],
        out_specs=[pl.BlockSpec((tm * ROW_PITCH, LANES), lambda i: (i, 0)),
                   pl.BlockSpec((tm, N_ROUTER), lambda i: (i, 0)),
                   pl.BlockSpec((tm, N_ROUTER), lambda i: (i, 0))],
        out_shape=[jax.ShapeDtypeStruct((t * ROW_PITCH, LANES), U32),
                   jax.ShapeDtypeStruct((t, N_ROUTER), jnp.int32),
                   jax.ShapeDtypeStruct((t, N_ROUTER), F32)],
        scratch_shapes=[pltpu.VMEM((tm, d), BF16)],
        compiler_params=_cparams(("arbitrary",)),
        name="norm2_router",
    )(x2d, g_norm.reshape(1, d), mod3, mod3, w_router, b_router)


CAST_ROWS = 128
UP_WEIGHT_DMA_PRIORITY = 0
DOWN_WEIGHT_DMA_PRIORITY = 1


def _cast_rows(src_ref, dst_ref):
    n = src_ref.shape[0]

    def body(r, carry):
        rows = pl.ds(pl.multiple_of(r * CAST_ROWS, CAST_ROWS), CAST_ROWS)
        dst_ref[rows, :] = src_ref[rows, :].astype(dst_ref.dtype)
        return carry

    lax.fori_loop(0, n // CAST_ROWS, body, 0)


def _expert_weights(i, nu, sched, w_hbms, w_bufs, w_bf16s, sem, priority):
    be_ref, sl_ref, nx_ref, nv_ref = sched

    def copies(e, slot):
        return [pltpu.make_async_copy(w.at[e], buf.at[slot], sem.at[slot, k])
                for k, (w, buf) in enumerate(zip(w_hbms, w_bufs))]

    @pl.when(i == 0)
    def _():
        for cp in copies(be_ref[0], 0):
            cp.start(priority=priority)

    first_block = (i == 0) | (be_ref[i] != be_ref[jnp.maximum(i - 1, 0)])

    @pl.when(first_block & (i < nu))
    def _():
        slot = sl_ref[i]
        for cp in copies(be_ref[i], slot):
            cp.wait()
        for buf, dst in zip(w_bufs, w_bf16s):
            _cast_rows(buf.at[slot], dst)

        @pl.when(nv_ref[i] == 1)
        def _():
            for cp in copies(nx_ref[i], 1 - slot):
                cp.start(priority=priority)


def _moe_up_kernel(be_ref, nu_ref, sl_ref, nx_ref, nv_ref, tk_ref, bf_ref, br_ref, h_hbm, wg_hbm, wu_hbm, o_ref,
                   wg_buf, wu_buf, wgb_ref, wub_ref, xg_ref, xb_ref, sem, wsem):
    i = pl.program_id(0)
    nu = nu_ref[0]
    pad_entry = tk_ref.shape[0] - 8

    def start_gather(blk, slot):
        first, rows = bf_ref[blk], br_ref[blk]
        token_of = lambda r: tk_ref[jnp.where(r < rows, first + r, pad_entry)]
        _gather_rows(h_hbm, xg_ref.at[slot], token_of, MOE_BM, sem.at[slot])

    @pl.when(i == 0)
    def _():
        start_gather(0, 0)

    @pl.when(i + 1 < nu)
    def _():
        start_gather(i + 1, (i + 1) % 2)

    _expert_weights(i, nu, (be_ref, sl_ref, nx_ref, nv_ref), (wg_hbm, wu_hbm), (wg_buf, wu_buf), (wgb_ref, wub_ref),
                    wsem, UP_WEIGHT_DMA_PRIORITY)

    def unpack_rows(slot):
        _wait_rows(h_hbm, xg_ref.at[slot], MOE_BM, sem.at[slot])
        for j in range(ROW_CH):
            lo, hi = _load_chunk(xg_ref.at[slot], 0, MOE_BM, j)
            xb_ref[:, pl.ds(j * LANES, LANES)] = lo.astype(BF16)
            xb_ref[:, pl.ds(HALF_D + j * LANES, LANES)] = hi.astype(BF16)

    @pl.when(i < nu)
    def _():
        for slot in range(2):
            pl.when(i % 2 == slot)(functools.partial(unpack_rows, slot))
        xb = xb_ref[...]
        g = jnp.dot(xb, wgb_ref[...], preferred_element_type=F32)
        u = jnp.dot(xb, wub_ref[...], preferred_element_type=F32)
        o_ref[...] = (g * jax.nn.sigmoid(g) * u).astype(o_ref.dtype)

    @pl.when(i >= nu)
    def _():
        o_ref[...] = jnp.zeros_like(o_ref)


def _moe_down_kernel(be_ref, nu_ref, sl_ref, nx_ref, nv_ref, h_ref, wd_hbm, o_ref, wd_buf, wdb_ref, wsem):
    i = pl.program_id(0)
    _expert_weights(i, nu_ref[0], (be_ref, sl_ref, nx_ref, nv_ref), (wd_hbm,), (wd_buf,), (wdb_ref,), wsem,
                    DOWN_WEIGHT_DMA_PRIORITY)

    @pl.when(i < nu_ref[0])
    def _():
        _store_chunked(o_ref, 0, jnp.dot(h_ref[...], wdb_ref[...], preferred_element_type=F32))

    @pl.when(i >= nu_ref[0])
    def _():
        o_ref[...] = jnp.zeros_like(o_ref)


def _moe_experts(h_chunked, tok_sorted, blk_first, blk_rows, blk_expert, n_used, w_gate, w_up, w_down):
    n_blocks = blk_expert.shape[0]
    p = n_blocks * MOE_BM
    _, d, de = w_gate.shape
    run = jnp.cumsum(jnp.concatenate([jnp.zeros((1,), jnp.int32),
                                      (blk_expert[1:] != blk_expert[:-1]).astype(jnp.int32)]))
    run_end = jnp.searchsorted(blk_expert, blk_expert, side='right').astype(jnp.int32)
    sched = (blk_expert, n_used, run % 2, blk_expert[jnp.minimum(run_end, n_blocks - 1)],
             (run_end < n_used[0]).astype(jnp.int32))
    any_spec = pl.BlockSpec(memory_space=pl.ANY)
    hmid = pl.pallas_call(
        _moe_up_kernel,
        grid_spec=pltpu.PrefetchScalarGridSpec(
            num_scalar_prefetch=8,
            grid=(n_blocks,),
            in_specs=[any_spec, any_spec, any_spec],
            out_specs=pl.BlockSpec((MOE_BM, de), lambda i, *_: (i, 0)),
            scratch_shapes=[pltpu.VMEM((2, d, de), F32), pltpu.VMEM((2, d, de), F32),
                            pltpu.VMEM((d, de), BF16), pltpu.VMEM((d, de), BF16),
                            pltpu.VMEM((2, MOE_BM * ROW_PITCH, LANES), U32), pltpu.VMEM((MOE_BM, d), BF16),
                            pltpu.SemaphoreType.DMA((2,)), pltpu.SemaphoreType.DMA((2, 2))]),
        out_shape=jax.ShapeDtypeStruct((p, de), BF16),
        compiler_params=_cparams(("arbitrary",)),
        name="moe_gate_up",
    )(*sched, tok_sorted, blk_first, blk_rows, h_chunked, w_gate, w_up)
    return pl.pallas_call(
        _moe_down_kernel,
        grid_spec=pltpu.PrefetchScalarGridSpec(
            num_scalar_prefetch=5,
            grid=(n_blocks,),
            in_specs=[pl.BlockSpec((MOE_BM, de), lambda i, *_: (i, 0)), any_spec],
            out_specs=pl.BlockSpec((MOE_BM * ROW_PITCH, LANES), lambda i, *_: (i, 0)),
            scratch_shapes=[pltpu.VMEM((2, de, d), F32), pltpu.VMEM((de, d), BF16),
                            pltpu.SemaphoreType.DMA((2, 1))]),
        out_shape=jax.ShapeDtypeStruct((p * ROW_PITCH, LANES), U32),
        compiler_params=_cparams(("arbitrary",)),
        name="moe_down",
    )(*sched, hmid, w_down)


COMBINE_TM = 256


def _combine_kernel(so_ref, y_hbm, x_ref, gt_ref, g2_ref, o_ref, yb_ref, sem, *, tm):
    i = pl.program_id(0)
    n = pl.num_programs(0)

    def start_gather(blk, slot):
        for k in range(TOP_K):
            _gather_rows(y_hbm, yb_ref.at[slot], lambda r: so_ref[(blk * tm + r) * TOP_K + k], tm, sem.at[slot],
                         dst_row0=k * tm)

    @pl.when(i == 0)
    def _():
        start_gather(0, 0)

    @pl.when(i + 1 < n)
    def _():
        start_gather(i + 1, (i + 1) % 2)

    def combine(slot):
        _wait_rows(y_hbm, yb_ref.at[slot], TOP_K * tm, sem.at[slot])
        g0 = gt_ref[:, 0:1]
        g1 = gt_ref[:, 1:2]
        for j in range(ROW_CH):
            lo0, hi0 = _load_chunk(yb_ref.at[slot], 0, tm, j)
            lo1, hi1 = _load_chunk(yb_ref.at[slot], tm, tm, j)
            for off, y in ((j * LANES, g0 * lo0 + g1 * lo1), (HALF_D + j * LANES, g0 * hi0 + g1 * hi1)):
                o_ref[:, pl.ds(off, LANES)] = x_ref[:, pl.ds(off, LANES)] + g2_ref[0][:, off:off + LANES] * y

    for slot in range(2):
        pl.when(i % 2 == slot)(functools.partial(combine, slot))


def _moe_combine(y_chunked, slot_of, x2d, gates, mod3, tiles_per_batch):
    t, d = x2d.shape
    tm = COMBINE_TM
    return pl.pallas_call(
        functools.partial(_combine_kernel, tm=tm),
        grid_spec=pltpu.PrefetchScalarGridSpec(
            num_scalar_prefetch=1,
            grid=(t // tm,),
            in_specs=[pl.BlockSpec(memory_space=pl.ANY),
                      pl.BlockSpec((tm, d), lambda i, so: (i, 0)),
                      pl.BlockSpec((tm, N_ROUTER), lambda i, so: (i, 0)),
                      pl.BlockSpec((1, 1, d), lambda i, so: ((i // tiles_per_batch) * 6 + 5, 0, 0))],
            out_specs=pl.BlockSpec((tm, d), lambda i, so: (i, 0)),
            scratch_shapes=[pltpu.VMEM((2, TOP_K * tm * ROW_PITCH, LANES), U32),
                            pltpu.SemaphoreType.DMA((2,))]),
        out_shape=jax.ShapeDtypeStruct((t, d), F32),
        compiler_params=_cparams(("arbitrary",)),
        name="moe_combine",
    )(slot_of, y_chunked, x2d, gates, mod3)


def _slots(eid, t):
    a = t * TOP_K
    iota = jnp.arange(a, dtype=jnp.int32)
    e_sorted, order = lax.sort((eid.reshape(a), iota), num_keys=1)
    experts = jnp.arange(N_EXPERTS, dtype=jnp.int32)
    start = jnp.searchsorted(e_sorted, experts, side='left').astype(jnp.int32)
    counts = jnp.searchsorted(e_sorted, experts, side='right').astype(jnp.int32) - start
    padded = (counts + MOE_BM - 1) // MOE_BM * MOE_BM
    pend = jnp.cumsum(padded)
    pstart = pend - padded
    dest = iota + (pstart - start)[e_sorted]
    slot_of = lax.sort((order, dest), num_keys=1)[1]
    n_blocks = (a + MOE_BM - 1) // MOE_BM + N_EXPERTS
    blk = jnp.arange(n_blocks, dtype=jnp.int32) * MOE_BM
    blk_expert = jnp.minimum(jnp.searchsorted(pend, blk, side='right'), N_EXPERTS - 1).astype(jnp.int32)
    off = blk - pstart[blk_expert]
    blk_first = start[blk_expert] + off
    blk_rows = jnp.clip(counts[blk_expert] - off, 0, MOE_BM)
    tok_sorted = jnp.concatenate([order // TOP_K, jnp.zeros((8,), jnp.int32)])
    n_used = (pend[-1] // MOE_BM).astype(jnp.int32).reshape(1)
    return tok_sorted, blk_first, blk_rows, slot_of, blk_expert, n_used


GDN_TB = 256
HALO = 8
PREP_SLABS = 4
N_QKV = 3 * GDN_WIDTH


def _softplus(v):
    return jnp.maximum(v, 0.0) + jnp.log(1.0 + jnp.exp(-jnp.abs(v)))


def _split3_bf16(v):
    hi = v.astype(BF16)
    r1 = v - hi.astype(F32)
    mid = r1.astype(BF16)
    lo = (r1 - mid.astype(F32)).astype(BF16)
    return hi, mid, lo


def _gdn_prep_kernel(cur_ref, prev_ref, next_ref, sm_ref, cw_ref, ea_ref, dt_ref, o_ref, g_ref, ext_ref, *, tb):
    i = pl.program_id(1)
    nblk = pl.num_programs(1)
    def conv_head(kind, hh, slab):
        cols = pl.ds(pl.multiple_of((kind * GDN_HEADS + hh) * HEAD_DIM, HEAD_DIM), HEAD_DIM)
        cw = cw_ref[:, cols]
        ext = ext_ref.at[slab]
        ext[pl.ds(0, HALO), :] = jnp.where(i > 0, prev_ref[:, cols], 0.0)
        ext[pl.ds(HALO, tb), :] = cur_ref[:, cols]
        ext[pl.ds(HALO + tb, HALO), :] = jnp.where(i < nblk - 1, next_ref[:, cols], 0.0)
        for r0 in range(0, tb, 64):
            acc = None
            for s in range(CONV_K):
                term = ext[pl.ds(HALO - CONV_K // 2 + s + r0, 64), :] * cw[s:s + 1, :]
                acc = term if acc is None else acc + term
            y = acc * jax.nn.sigmoid(acc)
            if kind < 2:
                y = y * lax.rsqrt(jnp.sum(y * y, axis=-1, keepdims=True) + EPS)
            if kind == 0:
                y = y * HEAD_DIM ** -0.5
            o_ref[pl.ds(r0, 64), cols] = y.astype(o_ref.dtype)

    def conv_cols(kind):
        def body(it, carry):
            for slab in range(PREP_SLABS):
                conv_head(kind, it * PREP_SLABS + slab, slab)
            return carry

        lax.fori_loop(0, GDN_HEADS // PREP_SLABS, body, 0)

    conv_cols(0)
    conv_cols(1)
    conv_cols(2)

    s = sm_ref[...]
    beta = jax.nn.sigmoid(s)
    g = -ea_ref[...] * _softplus(s + dt_ref[...])
    r = lax.broadcasted_iota(jnp.int32, (tb, tb), 0)
    c = lax.broadcasted_iota(jnp.int32, (tb, tb), 1)
    same = (r // GDN_CHUNK) == (c // GDN_CHUNK)
    lower = (same & (c <= r)).astype(BF16)
    upper = (same & (c >= r)).astype(BF16)
    parts = _split3_bf16(g)
    cf = sum(jnp.dot(lower, pt, preferred_element_type=F32) for pt in parts)
    cb = sum(jnp.dot(upper, pt, preferred_element_type=F32) for pt in parts)
    col = lax.broadcasted_iota(jnp.int32, s.shape, 1)
    nh = GDN_HEADS
    g_ref[...] = jnp.where(col < 2 * nh, beta, jnp.where(col < 3 * nh, cf, jnp.where(col < 4 * nh, cb, 0.0)))


def _gdn_prep(y_main, small, conv_w8, ea_row, dt_row, b_, l):
    tb = min(GDN_TB, l)
    nblk = l // tb
    hb = tb // HALO
    last = b_ * l // HALO - 1
    return pl.pallas_call(
        functools.partial(_gdn_prep_kernel, tb=tb),
        grid=(b_, nblk),
        in_specs=[pl.BlockSpec((tb, N_QKV), lambda b, i: (b * nblk + i, 0)),
                  pl.BlockSpec((HALO, N_QKV), lambda b, i: (jnp.maximum((b * nblk + i) * hb - 1, 0), 0)),
                  pl.BlockSpec((HALO, N_QKV), lambda b, i: (jnp.minimum((b * nblk + i + 1) * hb, last), 0)),
                  pl.BlockSpec((tb, N_SMALL), lambda b, i: (b * nblk + i, 0)),
                  pl.BlockSpec((HALO, N_QKV), lambda b, i: (0, 0)),
                  pl.BlockSpec((1, N_SMALL), lambda b, i: (0, 0)),
                  pl.BlockSpec((1, N_SMALL), lambda b, i: (0, 0))],
        out_specs=[pl.BlockSpec((tb, N_QKV), lambda b, i: (b * nblk + i, 0)),
                   pl.BlockSpec((tb, N_SMALL), lambda b, i: (b * nblk + i, 0))],
        out_shape=[jax.ShapeDtypeStruct((b_ * l, N_QKV), BF16),
                   jax.ShapeDtypeStruct((b_ * l, N_SMALL), F32)],
        scratch_shapes=[pltpu.VMEM((PREP_SLABS, tb + 2 * HALO, HEAD_DIM), F32)],
        compiler_params=_cparams(("arbitrary", "arbitrary")),
        name="gdn_prep",
    )(y_main, y_main, y_main, small, conv_w8, ea_row, dt_row)


N_PAIRS = GDN_HEADS // 2
PK = 2 * GDN_CHUNK
INV_LEVELS = (2, 4, 8, 16, 32, 64)


def _gdn_masks(reverse):
    i = np.arange(PK)[:, None]
    j = np.arange(PK)[None, :]
    same = (i // GDN_CHUNK) == (j // GDN_CHUNK)
    strict = same & ((j > i) if reverse else (j < i))
    out = []
    for bs in INV_LEVELS:
        out.append(strict & (i // bs == j // bs) & (i // (bs // 2) != j // (bs // 2)))
    incl = same & ((j >= i) if reverse else (j <= i))
    out.append(incl)
    m = np.stack(out).astype(np.float32)
    neg = ((incl.astype(np.float32) - 1.0) * 1e30)[None]
    return jnp.asarray(np.concatenate([m, neg], axis=0))


def _gdn_scan_kernel(qf_ref, kf_ref, vf_ref, qb_ref, kb_ref, vb_ref, cpf_ref, rpf_ref, apf_ref, cpb_ref, rpb_ref,
                     apb_ref, s0_ref, mk_ref, mkb_ref, of_ref, ob_ref, sfin_ref, s_scr):
    c = pl.program_id(1)

    @pl.when(c == 0)
    def _():
        s_scr[...] = s0_ref[0]

    nl = len(INV_LEVELS)
    ri = lax.broadcasted_iota(jnp.int32, (PK, 1), 0)
    top = (ri < GDN_CHUNK).astype(F32)
    bot = 1.0 - top
    rr = lax.broadcasted_iota(jnp.int32, (PK, PK), 0)
    cc = lax.broadcasted_iota(jnp.int32, (PK, PK), 1)
    eye = (rr == cc).astype(F32)
    dot = functools.partial(jnp.dot, preferred_element_type=F32)
    nt = (((1,), (1,)), ((), ()))
    tn = (((0,), (0,)), ((), ()))

    q_refs, k_refs, v_refs, o_refs = (qf_ref, qb_ref), (kf_ref, kb_ref), (vf_ref, vb_ref), (of_ref, ob_ref)
    cps = (cpf_ref[0, 0], cpb_ref[0, 0])
    rps = (rpf_ref[0, 0], rpb_ref[0, 0])
    aps = (apf_ref[0, 0], apb_ref[0, 0])
    units = [(d, p) for p in range(N_PAIRS) for d in range(2)]
    per_unit = lambda f: [f(i, d, p) for i, (d, p) in enumerate(units)]
    cols = lambda p: (pl.ds(2 * p * HEAD_DIM, HEAD_DIM), pl.ds((2 * p + 1) * HEAD_DIM, HEAD_DIM))
    pack = lambda ref, p: jnp.concatenate([ref[:, cols(p)[0]], ref[:, cols(p)[1]]], axis=0)
    mask = lambda d, n: mk_ref[d, n]

    gcol = per_unit(lambda i, d, p: cps[d][:, p:p + 1])
    bcol = per_unit(lambda i, d, p: cps[d][:, N_PAIRS + p:N_PAIRS + p + 1])
    glcol = per_unit(lambda i, d, p: cps[d][:, 2 * N_PAIRS + p:2 * N_PAIRS + p + 1])
    kp = per_unit(lambda i, d, p: pack(k_refs[d], p))
    qp = per_unit(lambda i, d, p: pack(q_refs[d], p))
    kk = per_unit(lambda i, d, p: lax.dot_general(kp[i], kp[i], nt, preferred_element_type=F32))
    qk = per_unit(lambda i, d, p: lax.dot_general(qp[i], kp[i], nt, preferred_element_type=F32))
    dec = per_unit(lambda i, d, p: jnp.exp((gcol[i] - rps[d][p:p + 1, :]) * mask(d, nl) + mask(d, nl + 1)))
    a = per_unit(lambda i, d, p: (kk[i] * dec[i] * bcol[i]).astype(BF16))
    qkm = per_unit(lambda i, d, p: (qk[i] * dec[i]).astype(BF16))

    x = per_unit(lambda i, d, p: eye - (a[i] * mkb_ref[d, 0]).astype(F32))
    for lv in range(1, nl):
        xb = per_unit(lambda i, d, p: x[i].astype(BF16))
        po = per_unit(lambda i, d, p: dot(xb[i], a[i] * mkb_ref[d, lv]))
        x = per_unit(lambda i, d, p: x[i] - dot(po[i].astype(BF16), xb[i]))
    tb = per_unit(lambda i, d, p: x[i].astype(BF16))

    egc = per_unit(lambda i, d, p: jnp.exp(gcol[i]))
    kf = per_unit(lambda i, d, p: kp[i].astype(F32))
    u = per_unit(lambda i, d, p: dot(tb[i], (pack(v_refs[d], p).astype(F32) * bcol[i]).astype(BF16)))
    w = per_unit(lambda i, d, p: dot(tb[i], (kf[i] * (bcol[i] * egc[i])).astype(BF16)))
    qd = per_unit(lambda i, d, p: qp[i].astype(F32) * egc[i])
    kd = per_unit(lambda i, d, p: kf[i] * jnp.exp(glcol[i] - gcol[i]))

    s = per_unit(lambda i, d, p: s_scr[d, p])
    lhs = per_unit(lambda i, d, p: jnp.concatenate(
        [jnp.concatenate([w[i] * top, w[i] * bot], axis=1),
         jnp.concatenate([qd[i] * top, qd[i] * bot], axis=1)], axis=0).astype(BF16))
    ws = per_unit(lambda i, d, p: dot(lhs[i], s[i].astype(BF16)))
    vnb = per_unit(lambda i, d, p: (u[i] - ws[i][:PK]).astype(BF16))
    o = per_unit(lambda i, d, p: ws[i][PK:] + dot(qkm[i], vnb[i]))
    kbd = per_unit(lambda i, d, p: jnp.concatenate([kd[i] * top, kd[i] * bot], axis=1).astype(BF16))
    kv = per_unit(lambda i, d, p: lax.dot_general(kbd[i], vnb[i], tn, preferred_element_type=F32))
    for i, (d, p) in enumerate(units):
        s_scr[d, p] = jnp.exp(aps[d][:, p:p + 1]) * s[i] + kv[i]
    for i, (d, p) in enumerate(units):
        o_refs[d][:, cols(p)[0]] = o[i][:GDN_CHUNK].astype(o_refs[d].dtype)
        o_refs[d][:, cols(p)[1]] = o[i][GDN_CHUNK:].astype(o_refs[d].dtype)

    @pl.when(c == pl.num_programs(1) - 1)
    def _():
        sfin_ref[0] = s_scr[...]


def _gdn_out_kernel(of_ref, ob_ref, z_ref, gon_ref, o_ref):
    for h in range(GDN_HEADS):
        c = pl.ds(h * HEAD_DIM, HEAD_DIM)
        o = of_ref[:, c].astype(F32) + ob_ref[:, c].astype(F32)
        z = z_ref[:, c]
        y = o * lax.rsqrt(jnp.mean(o * o, axis=-1, keepdims=True) + EPS) * gon_ref[...]
        o_ref[:, c] = (y * (z * jax.nn.sigmoid(z))).astype(o_ref.dtype)


def _gdn_out(o_f, o_b, y_main, g_on, tm):
    t = o_f.shape[0]
    blk = lambda col: pl.BlockSpec((tm, GDN_WIDTH), lambda i: (i, col))
    return pl.pallas_call(
        _gdn_out_kernel,
        grid=(t // tm,),
        in_specs=[blk(0), blk(0), blk(N_QKV // GDN_WIDTH), pl.BlockSpec((1, HEAD_DIM), lambda i: (0, 0))],
        out_specs=blk(0),
        out_shape=jax.ShapeDtypeStruct((t, GDN_WIDTH), BF16),
        compiler_params=_cparams(("arbitrary",)),
        name="gdn_out",
    )(o_f, o_b, y_main, g_on.reshape(1, HEAD_DIM))


def _gdn_packs(gates, b_, l, reverse):
    nc = l // GDN_CHUNK
    nh = GDN_HEADS
    g4 = gates.reshape(b_, nc, GDN_CHUNK, N_SMALL)
    d = 1 if reverse else 0
    beta = g4[..., d * nh:(d + 1) * nh]
    gc = g4[..., (2 + d) * nh:(3 + d) * nh]
    gl = jnp.broadcast_to(gc[:, :, 0:1] if reverse else gc[:, :, GDN_CHUNK - 1:GDN_CHUNK], gc.shape)

    def rowpack(t):
        return jnp.transpose(t.reshape(b_, nc, GDN_CHUNK, N_PAIRS, 2), (0, 1, 3, 4, 2)).reshape(b_, nc, N_PAIRS, PK)

    rp = rowpack(gc)
    cp = jnp.concatenate([jnp.swapaxes(rowpack(t), 2, 3) for t in (gc, beta, gl)]
                         + [jnp.zeros((b_, nc, PK, N_PAIRS), F32)], axis=-1)
    glh = gl[:, :, 0].reshape(b_, nc, N_PAIRS, 2)
    ap = jnp.swapaxes(jnp.repeat(glh, HEAD_DIM, axis=-1), 2, 3)
    return cp, rp, ap


def _gdn_scan(qkv, gates, s0, b_, l):
    nc = l // GDN_CHUNK
    packs = [_gdn_packs(gates, b_, l, reverse) for reverse in (False, True)]
    masks = jnp.stack([_gdn_masks(False), _gdn_masks(True)])
    level_masks = masks[:, :len(INV_LEVELS)].astype(BF16)
    chunk_of = (lambda c: c, lambda c: nc - 1 - c)
    tok = lambda d, col: pl.BlockSpec((GDN_CHUNK, GDN_WIDTH), lambda b, c: (b * nc + chunk_of[d](c), col))
    per_chunk = lambda d, shp: pl.BlockSpec((1, 1) + shp, lambda b, c: (b, chunk_of[d](c), 0, 0))
    gate_specs = lambda d: [per_chunk(d, (PK, 4 * N_PAIRS)), per_chunk(d, (N_PAIRS, PK)),
                            per_chunk(d, (2 * HEAD_DIM, N_PAIRS))]
    state = pl.BlockSpec((1, 2, N_PAIRS, 2 * HEAD_DIM, HEAD_DIM), lambda b, c: (b, 0, 0, 0, 0))
    return pl.pallas_call(
        _gdn_scan_kernel,
        grid=(b_, nc),
        in_specs=[tok(0, 0), tok(0, 1), tok(0, 2), tok(1, 0), tok(1, 1), tok(1, 2)] + gate_specs(0) + gate_specs(1)
        + [state, pl.BlockSpec(masks.shape, lambda b, c: (0, 0, 0, 0)),
           pl.BlockSpec(level_masks.shape, lambda b, c: (0, 0, 0, 0))],
        out_specs=[tok(0, 0), tok(1, 0), state],
        out_shape=[jax.ShapeDtypeStruct((b_ * l, GDN_WIDTH), BF16),
                   jax.ShapeDtypeStruct((b_ * l, GDN_WIDTH), BF16),
                   jax.ShapeDtypeStruct(s0.shape, F32)],
        scratch_shapes=[pltpu.VMEM((2, N_PAIRS, 2 * HEAD_DIM, HEAD_DIM), F32)],
        compiler_params=_cparams(("arbitrary", "arbitrary")),
        name="gdn_scan",
    )(qkv, qkv, qkv, qkv, qkv, qkv, *packs[0], *packs[1], s0, masks, level_masks)


def _gdn_mixer(yx, sx, yc, sc, conv_w, a_log, dt_bias, g_on, b_, l, n_ctx):
    nh = GDN_HEADS
    conv_w8 = jnp.pad(conv_w, ((0, HALO - CONV_K), (0, 0)))
    ea_row = jnp.zeros((1, N_SMALL), F32).at[0, 2 * nh:4 * nh].set(jnp.exp(a_log.reshape(-1)))
    dt_row = jnp.zeros((1, N_SMALL), F32).at[0, 2 * nh:4 * nh].set(dt_bias.reshape(-1))
    qkv_c, gates_c = _gdn_prep(yc, sc, conv_w8, ea_row, dt_row, b_, n_ctx)
    qkv_x, gates_x = _gdn_prep(yx, sx, conv_w8, ea_row, dt_row, b_, l)
    zero = jnp.zeros((b_, 2, N_PAIRS, 2 * HEAD_DIM, HEAD_DIM), F32)
    _, _, s_ctx = _gdn_scan(qkv_c, gates_c, zero, b_, n_ctx)
    o_f, o_b, _ = _gdn_scan(qkv_x, gates_x, s_ctx, b_, l)
    return _gdn_out(o_f, o_b, yx, g_on, 256)


SWA_GROUP = SWA_HEADS // SWA_KV_HEADS
ROT = HEAD_DIM // 4


def _rope_tables(l):
    half = HEAD_DIM // 2
    inv = ROPE_BASE ** (-jnp.arange(0, half, 2, dtype=F32) / half)
    pos = jnp.arange(l, dtype=jnp.int32)
    ang_r = (pos // GRID_W).astype(F32)[:, None] * inv
    ang_c = (pos % GRID_W).astype(F32)[:, None] * inv
    zero = jnp.zeros_like(ang_r)
    cos = jnp.concatenate([jnp.cos(ang_r), jnp.cos(ang_r), jnp.cos(ang_c), jnp.cos(ang_c)], axis=1)
    sin_up = jnp.concatenate([-jnp.sin(ang_r), zero, -jnp.sin(ang_c), zero], axis=1)
    sin_dn = jnp.concatenate([zero, jnp.sin(ang_r), zero, jnp.sin(ang_c)], axis=1)
    return cos, sin_up, sin_dn


def _swa_prep_kernel(q_ref, k_ref, v_ref, cos_ref, su_ref, sd_ref, gq_ref, gk_ref, qo_ref, ko_ref, vo_ref):
    cos, su, sd = cos_ref[...], su_ref[...], sd_ref[...]

    def norm_rope(t, g, scale):
        y = t * lax.rsqrt(jnp.mean(t * t, axis=-1, keepdims=True) + EPS) * g
        y = y * cos + pltpu.roll(y, HEAD_DIM - ROT, 1) * su + pltpu.roll(y, ROT, 1) * sd
        return y * scale if scale != 1.0 else y

    for h in range(SWA_HEADS):
        c = pl.ds(h * HEAD_DIM, HEAD_DIM)
        qo_ref[:, c] = norm_rope(q_ref[:, c], gq_ref[...], HEAD_DIM ** -0.5).astype(qo_ref.dtype)
    for h in range(SWA_KV_HEADS):
        c = pl.ds(h * HEAD_DIM, HEAD_DIM)
        ko_ref[:, c] = norm_rope(k_ref[:, c], gk_ref[...], 1.0).astype(ko_ref.dtype)
    vo_ref[...] = v_ref[...].astype(vo_ref.dtype)


def _swa_prep(y_main, tables, g_q, g_k, rows, tm):
    q_blk = (N_QKV + GDN_WIDTH) // SWA_WIDTH
    k_blk = (N_QKV + GDN_WIDTH + SWA_WIDTH) // SWA_KV_WIDTH
    tpb = tables[0].shape[0] // tm
    tab = pl.BlockSpec((tm, HEAD_DIM), lambda i: (i % tpb, 0))
    vec = pl.BlockSpec((1, HEAD_DIM), lambda i: (0, 0))
    return pl.pallas_call(
        _swa_prep_kernel,
        grid=(rows // tm,),
        in_specs=[pl.BlockSpec((tm, SWA_WIDTH), lambda i: (i, q_blk)),
                  pl.BlockSpec((tm, SWA_KV_WIDTH), lambda i: (i, k_blk)),
                  pl.BlockSpec((tm, SWA_KV_WIDTH), lambda i: (i, k_blk + 1)),
                  tab, tab, tab, vec, vec],
        out_specs=[pl.BlockSpec((tm, SWA_WIDTH), lambda i: (i, 0)),
                   pl.BlockSpec((tm, SWA_KV_WIDTH), lambda i: (i, 0)),
                   pl.BlockSpec((tm, SWA_KV_WIDTH), lambda i: (i, 0))],
        out_shape=[jax.ShapeDtypeStruct((rows, SWA_WIDTH), BF16),
                   jax.ShapeDtypeStruct((rows, SWA_KV_WIDTH), BF16),
                   jax.ShapeDtypeStruct((rows, SWA_KV_WIDTH), BF16)],
        compiler_params=_cparams(("arbitrary",)),
        name="swa_prep",
    )(y_main, y_main, y_main, *tables, g_q.reshape(1, HEAD_DIM), g_k.reshape(1, HEAD_DIM))


def _swa_bias(n_ctx):
    rows = SWA_GROUP * Q_BLOCK
    qi = (np.arange(rows) % Q_BLOCK)[:, None]
    kj = np.arange(n_ctx + 3 * Q_BLOCK)[None, :] - n_ctx
    band = (kj >= qi) & (kj <= qi + 2 * WINDOW)
    out = [(kj < 0) | (band & (kj >= lo) & (kj < hi))
           for lo, hi in ((Q_BLOCK, 3 * Q_BLOCK), (0, 3 * Q_BLOCK), (0, 2 * Q_BLOCK))]
    return jnp.asarray(np.where(np.stack(out), 0.0, NEG_INF).astype(np.float32))


def _swa_attn_kernel(q_ref, kp_ref, kc_ref, kn_ref, vp_ref, vc_ref, vn_ref, kx_ref, vx_ref, sink_ref, bias_ref, o_ref):
    rows = SWA_GROUP * Q_BLOCK
    bias = bias_ref[0]
    hsel = lax.broadcasted_iota(jnp.int32, (rows, 1), 0) // Q_BLOCK
    nt = (((1,), (1,)), ((), ()))
    kv_heads = range(SWA_KV_HEADS)
    col = lambda j: pl.ds(j * HEAD_DIM, HEAD_DIM)
    heads = [[j * SWA_GROUP + g for g in range(SWA_GROUP)] for j in kv_heads]
    s = []
    for j in kv_heads:
        q = jnp.concatenate([q_ref[:, col(h)] for h in heads[j]], axis=0)
        k = jnp.concatenate([kx_ref[:, col(j)], kp_ref[:, col(j)], kc_ref[:, col(j)], kn_ref[:, col(j)]], axis=0)
        s.append(lax.dot_general(q, k, nt, preferred_element_type=F32) + bias)
    p, den = [], []
    for j in kv_heads:
        sink = jnp.zeros((rows, 1), F32)
        for g, h in enumerate(heads[j]):
            sink = jnp.where(hsel == g, sink_ref[h:h + 1, 0:1], sink)
        m = jnp.maximum(jnp.max(s[j], axis=-1, keepdims=True), sink)
        e = jnp.exp(s[j] - m)
        den.append(jnp.sum(e, axis=-1, keepdims=True) + jnp.exp(sink - m))
        p.append(e.astype(BF16))
    for j in kv_heads:
        v = jnp.concatenate([vx_ref[:, col(j)], vp_ref[:, col(j)], vc_ref[:, col(j)], vn_ref[:, col(j)]], axis=0)
        o = jnp.dot(p[j], v, preferred_element_type=F32) / den[j]
        for g, h in enumerate(heads[j]):
            o_ref[:, col(h)] = o[g * Q_BLOCK:(g + 1) * Q_BLOCK].astype(o_ref.dtype)


def _swa_attention(q, k, v, k_ctx, v_ctx, sink, b_, l, n_ctx):
    nb = l // Q_BLOCK
    blk = lambda w, off: pl.BlockSpec(
        (Q_BLOCK, w), lambda b, n: (b * nb + jnp.clip(n + off, 0, nb - 1), 0))
    ctx = pl.BlockSpec((n_ctx, SWA_KV_WIDTH), lambda b, n: (b, 0))
    kw = SWA_KV_WIDTH
    assert nb >= 2
    bias = _swa_bias(n_ctx)
    which = lambda n: jnp.where(n == 0, 0, jnp.where(n == nb - 1, 2, 1))
    return pl.pallas_call(
        _swa_attn_kernel,
        grid=(b_, nb),
        in_specs=[blk(SWA_WIDTH, 0), blk(kw, -1), blk(kw, 0), blk(kw, 1), blk(kw, -1), blk(kw, 0), blk(kw, 1),
                  ctx, ctx, pl.BlockSpec((SWA_HEADS, HEAD_DIM), lambda b, n: (0, 0)),
                  pl.BlockSpec((1,) + bias.shape[1:], lambda b, n: (which(n), 0, 0))],
        out_specs=blk(SWA_WIDTH, 0),
        out_shape=jax.ShapeDtypeStruct((b_ * l, SWA_WIDTH), BF16),
        compiler_params=_cparams(("arbitrary", "arbitrary")),
        name="swa_attention",
    )(q, k, k, k, v, v, v, k_ctx, v_ctx, jnp.broadcast_to(sink.astype(F32)[:, None], (SWA_HEADS, HEAD_DIM)), bias)


def _swa_mixer(yx, yc, g_q, g_k, sink, b_, l, n_ctx):
    ones = jnp.ones((n_ctx, HEAD_DIM), F32)
    zeros = jnp.zeros((n_ctx, HEAD_DIM), F32)
    qx, kx, vx = _swa_prep(yx, _rope_tables(l), g_q, g_k, b_ * l, 256)
    _, kc, vc = _swa_prep(yc, (ones, zeros, zeros), g_q, g_k, b_ * n_ctx, n_ctx)
    return _swa_attention(qx, kx, vx, kc, vc, sink, b_, l, n_ctx)


def kernel(x, c, ctx, c_ctx, w_ada, b_ada, g_norm1, g_norm2, w_in, conv_qkv, a_log, dt_bias, g_onorm, g_qnorm,
           g_knorm, sink, w_out, w_router_grp, b_router_grp, w_router_exp, b_router_exp, w_gate, w_up, w_down):
    b_, l, d = x.shape
    n_ctx = ctx.shape[1]
    t = b_ * l
    assert w_ada.shape[0] == 1 and d == D_MODEL and b_ + 1 <= MOD_ROWS

    wi_t = jnp.swapaxes(w_in[0], 0, 1)
    w_main = jnp.concatenate([wi_t[IN_OFFS[0]:IN_OFFS[2]], wi_t[IN_OFFS[4]:IN_OFFS[7]]], axis=0).astype(BF16)
    w_small = jnp.pad(wi_t[IN_OFFS[2]:IN_OFFS[4]], ((0, N_SMALL - 4 * GDN_HEADS), (0, 0))).astype(BF16)
    wo = w_out[0].astype(BF16)
    w_router = jnp.pad(jnp.concatenate([w_router_grp[0], w_router_exp[0]], axis=1),
                       ((0, 0), (0, N_ROUTER - N_GROUPS - N_EXPERTS))).astype(BF16)
    b_router = jnp.pad(jnp.concatenate([b_router_grp[0], b_router_exp[0]]),
                       (0, N_ROUTER - N_GROUPS - N_EXPERTS)).reshape(1, N_ROUTER)

    c_rows = jnp.zeros((MOD_ROWS, d), F32).at[:b_].set(c).at[b_].set(c_ctx)
    mod = _modulation(c_rows, w_ada[0], b_ada[0])
    mod3 = mod.reshape(MOD_ROWS * 6, 1, d)

    tm = 512
    tpb = l // tm
    yx, sx = _in_projection(x.reshape(t, d), g_norm1[0], mod3, lambda i: i // tpb, w_main, w_small, tm)
    yc, sc = _in_projection(ctx.reshape(b_ * n_ctx, d), g_norm1[0], mod3, lambda i: b_, w_main, w_small, n_ctx)
    ya_x = _gdn_mixer(yx, sx, yc, sc, conv_qkv[0], a_log[0], dt_bias[0], g_onorm[0], b_, l, n_ctx)
    yb_x = _swa_mixer(yx, yc, g_qnorm[0], g_knorm[0], sink[0], b_, l, n_ctx)

    x1 = _out_projection(ya_x, yb_x, wo, x.reshape(t, d), mod3, tpb, tm)

    tm2 = 256
    h2c, ids, gates = _norm2_router(x1, g_norm2[0], mod3, l // tm2, w_router, b_router, tm2)
    tok_sorted, blk_first, blk_rows, slot_of, blk_expert, n_used = _slots(ids[:, :TOP_K], t)
    yc_moe = _moe_experts(h2c, tok_sorted, blk_first, blk_rows, blk_expert, n_used, w_gate[0], w_up[0], w_down[0])
    return _moe_combine(yc_moe, slot_of, x1, gates, mod3, l // COMBINE_TM).reshape(b_, l, d)
```
